```python
import math
import jax, jax.numpy as jnp
from jax import lax
import numpy as np

D_MODEL = 1024
BATCH = 32
SEQ = 2048
DEPTH = 2

HEAD_DIM = 128
HEADS_PER_GROUP = 4
ATTN_PATTERNS = ((128, 1), (512, 4), (2048, 16))
N_ATTN_GROUPS = len(ATTN_PATTERNS)
ATTN_WIDTH = N_ATTN_GROUPS * HEADS_PER_GROUP * HEAD_DIM
ATTN_OUT_WIDTH = HEADS_PER_GROUP * HEAD_DIM
ROPE_DIM = HEAD_DIM // 4
ROPE_THETA = 500000.0
BLOCK = 128
NEG_INF = -1e30
POOL_WINDOWS = (2, 4, 8, 16)
POOL_GROUP_WIDTH = D_MODEL // 4
POOL_WIDTH = len(POOL_WINDOWS) * POOL_GROUP_WIDTH
IN_WIDTH = 3 * ATTN_WIDTH + POOL_WIDTH + 2 * D_MODEL
D_FF = 2816
CONV_WIDTH = 3
PLE_DIM = 256
RMS_EPS = 1e-6

kernel_name = 'hybrid_dilated_attn_pool_gated_merge'


def rmsnorm(x, g):
    x32 = x.astype(jnp.float32)
    y = x32 * lax.rsqrt(jnp.mean(x32 * x32, axis=-1, keepdims=True) + RMS_EPS)
    return (y * g.astype(jnp.float32)).astype(x.dtype)


def partial_rotary(t, cos, sin):
    half = ROPE_DIM // 2
    t1 = t[..., :half].astype(jnp.float32)
    t2 = t[..., half:ROPE_DIM].astype(jnp.float32)
    c = cos[:, None, None, :]
    s = sin[:, None, None, :]
    rot = jnp.concatenate([t1 * c - t2 * s, t2 * c + t1 * s], axis=-1).astype(t.dtype)
    return jnp.concatenate([rot, t[..., ROPE_DIM:]], axis=-1)


def dilated_window_attention(q, k, v, window, dilation):
    B, S, H, hd = q.shape
    span = BLOCK * dilation
    s_pad = -(-S // span) * span
    L = s_pad // dilation
    nb = L // BLOCK
    w_sub = window // dilation

    def to_blocks(t):
        t = jnp.pad(t, ((0, 0), (0, s_pad - S), (0, 0), (0, 0)))
        t = t.reshape(B, L, dilation, H, hd).transpose(0, 2, 1, 3, 4)
        return t.reshape(B, dilation, nb, BLOCK, H, hd)

    def with_prev(t):
        prev = jnp.pad(t, ((0, 0), (0, 0), (1, 0), (0, 0), (0, 0), (0, 0)))[:, :, :-1]
        return jnp.concatenate([prev, t], axis=3)

    qb = to_blocks(q)
    kk = with_prev(to_blocks(k))
    vv = with_prev(to_blocks(v))
    scores = jnp.einsum('brnqhd,brnkhd->brnhqk', qb, kk,
                        preferred_element_type=jnp.float32) * (hd ** -0.5)
    qi = jnp.arange(BLOCK)[:, None]
    ki = jnp.arange(2 * BLOCK)[None, :]
    diff = BLOCK + qi - ki
    band = (diff >= 0) & (diff <= w_sub)
    blk = jnp.arange(nb)[:, None, None]
    mask = band[None] & ((blk > 0) | (ki[None] >= BLOCK))
    scores = jnp.where(mask[None, None, :, None], scores, NEG_INF)
    lse = jax.nn.logsumexp(scores, axis=-1)
    probs = jnp.exp(scores - lse[..., None])
    out = jnp.einsum('brnhqk,brnkhd->brnqhd', probs.astype(v.dtype), vv,
                     preferred_element_type=jnp.float32)
    out = out.reshape(B, dilation, L, H, hd).transpose(0, 2, 1, 3, 4)
    out = out.reshape(B, s_pad, H, hd)[:, :S]
    lse = lse.transpose(0, 1, 2, 4, 3).reshape(B, dilation, L, H).transpose(0, 2, 1, 3)
    lse = lse.reshape(B, s_pad, H)[:, :S]
    return out, lse


def multiscale_pool_mixer(u, pool_w, pool_scale):
    B, S, _ = u.shape
    u32 = u.astype(jnp.float32)
    csum = jnp.cumsum(u32, axis=1)
    t = jnp.arange(S)
    groups = []
    for g, w in enumerate(POOL_WINDOWS):
        sl = slice(g * POOL_GROUP_WIDTH, (g + 1) * POOL_GROUP_WIDTH)
        cg = csum[..., sl]
        shifted = jnp.pad(cg, ((0, 0), (w, 0), (0, 0)))[:, :S]
        count = jnp.minimum(t + 1, w).astype(jnp.float32)
        groups.append((cg - shifted) / count[None, :, None] - u32[..., sl])
    pooled = jnp.stack(groups, axis=2).astype(u.dtype)
    mixed = jnp.einsum('bsgc,gcd->bsgd', pooled, pool_w).reshape(B, S, POOL_WIDTH)
    return mixed * pool_scale


def conv_gated_mlp(h, w_up, conv_w, conv_b, w_down):
    S = h.shape[1]
    u = h @ w_up
    y = conv_b
    for tap in range(CONV_WIDTH):
        shift = CONV_WIDTH - 1 - tap
        y = y + conv_w[tap] * jnp.pad(u, ((0, 0), (shift, 0), (0, 0)))[:, :S]
    gate, val = jnp.split(y, 2, axis=-1)
    return (jax.nn.silu(gate) * val) @ w_down


def _fwd_setup_inputs(seed: int = 0) -> dict:
    key = jax.random.key(seed)
    ks = jax.random.split(key, 20)
    f32 = jnp.float32

    def nrm(k, shape, fan_in):
        return jax.random.normal(k, shape, f32) * (fan_in ** -0.5)

    def gain(k, shape):
        return 1.0 + 0.02 * jax.random.normal(k, shape, f32)

    return {
        'x': jax.random.normal(ks[0], (BATCH, SEQ, D_MODEL), f32),
        'p': jax.random.normal(ks[1], (DEPTH, BATCH, SEQ, PLE_DIM), f32),
        'g_mix': gain(ks[2], (DEPTH, D_MODEL)),
        'w_in': nrm(ks[3], (DEPTH, D_MODEL, IN_WIDTH), D_MODEL),
        'w_ya': nrm(ks[4], (DEPTH, ATTN_OUT_WIDTH, D_MODEL), ATTN_OUT_WIDTH),
        'w_yb': nrm(ks[5], (DEPTH, POOL_WIDTH, D_MODEL), POOL_WIDTH),
        'pool_w': nrm(ks[6], (DEPTH, len(POOL_WINDOWS), POOL_GROUP_WIDTH, POOL_GROUP_WIDTH), POOL_GROUP_WIDTH),
        'pool_scale': gain(ks[7], (DEPTH, POOL_WIDTH)),
        'w_o': nrm(ks[8], (DEPTH, D_MODEL, D_MODEL), D_MODEL),
        'g_ffn': gain(ks[9], (DEPTH, D_MODEL)),
        'w_up': nrm(ks[10], (DEPTH, D_MODEL, 2 * D_FF), D_MODEL),
        'conv_w': nrm(ks[11], (DEPTH, CONV_WIDTH, 2 * D_FF), CONV_WIDTH),
        'conv_b': 0.01 * jax.random.normal(ks[12], (DEPTH, 2 * D_FF), f32),
        'w_down': nrm(ks[13], (DEPTH, D_FF, D_MODEL), D_FF),
        'g_ple': gain(ks[14], (DEPTH, D_MODEL)),
        'w_ple': nrm(ks[15], (DEPTH, PLE_DIM, D_MODEL), PLE_DIM),
        'w_ple_gate': nrm(ks[16], (DEPTH, D_MODEL, D_MODEL), D_MODEL),
        'g_final': gain(ks[17], (D_MODEL,)),
    }


def _fwd_reference(x, p, g_mix, w_in, w_ya, w_yb, pool_w, pool_scale, w_o, g_ffn,
              w_up, conv_w, conv_b, w_down, g_ple, w_ple, w_ple_gate, g_final):
    B, S, _ = x.shape
    pos = jnp.arange(S, dtype=jnp.float32)
    inv_freq = jnp.exp(jnp.arange(0, ROPE_DIM, 2, dtype=jnp.float32)
                       * (-math.log(ROPE_THETA) / ROPE_DIM))
    ang = pos[:, None] * inv_freq[None, :]
    cos, sin = jnp.cos(ang), jnp.sin(ang)
    split_at = [ATTN_WIDTH, 2 * ATTN_WIDTH, 3 * ATTN_WIDTH,
                3 * ATTN_WIDTH + POOL_WIDTH, 3 * ATTN_WIDTH + POOL_WIDTH + D_MODEL]
    head_shape = (B, S, N_ATTN_GROUPS, HEADS_PER_GROUP, HEAD_DIM)

    for i in range(DEPTH):
        h = rmsnorm(x, g_mix[i])
        z = h @ w_in[i]
        q, k, v, u_pool, gate_a, gate_b = jnp.split(z, split_at, axis=-1)
        q = partial_rotary(q.reshape(head_shape), cos, sin)
        k = partial_rotary(k.reshape(head_shape), cos, sin)
        v = v.reshape(head_shape)

        outs, lses = [], []
        for g, (window, dilation) in enumerate(ATTN_PATTERNS):
            o_g, lse_g = dilated_window_attention(q[:, :, g], k[:, :, g], v[:, :, g],
                                                  window, dilation)
            outs.append(o_g)
            lses.append(lse_g)
        weights = jax.nn.softmax(jnp.stack(lses, axis=0), axis=0)
        attn = jnp.sum(weights[..., None] * jnp.stack(outs, axis=0), axis=0)
        y_a = attn.reshape(B, S, ATTN_OUT_WIDTH).astype(x.dtype) @ w_ya[i]

        y_b = multiscale_pool_mixer(u_pool, pool_w[i], pool_scale[i]) @ w_yb[i]

        merged = jax.nn.sigmoid(gate_a) * y_a + jax.nn.sigmoid(gate_b) * y_b
        x = x + merged @ w_o[i]

        x = x + conv_gated_mlp(rmsnorm(x, g_ffn[i]), w_up[i], conv_w[i], conv_b[i], w_down[i])

        ple_gate = jax.nn.sigmoid(rmsnorm(x, g_ple[i]) @ w_ple_gate[i])
        x = x + (p[i] @ w_ple[i]) * ple_gate

    return rmsnorm(x, g_final)


import jax as _jax
import jax.numpy as _jnp

TWIN_FORMAT = 'train_step'
FWD_PARAMS = ['x', 'p', 'g_mix', 'w_in', 'w_ya', 'w_yb', 'pool_w', 'pool_scale', 'w_o', 'g_ffn', 'w_up', 'conv_w', 'conv_b', 'w_down', 'g_ple', 'w_ple', 'w_ple_gate', 'g_final']
TWIN_WEIGHTS = ['g_mix', 'w_in', 'w_ya', 'w_yb', 'pool_w', 'pool_scale', 'w_o', 'g_ffn', 'w_up', 'conv_w', 'conv_b', 'w_down', 'g_ple', 'w_ple', 'w_ple_gate', 'g_final']
TWIN_DIFF_INPUT = 'x'
TWIN_INPUTS = ['x', 'p', 'g_mix', 'w_in', 'w_ya', 'w_yb', 'pool_w', 'pool_scale', 'w_o', 'g_ffn', 'w_up', 'conv_w', 'conv_b', 'w_down', 'g_ple', 'w_ple', 'w_ple_gate', 'g_final', 'loss_target', 'm_g_mix', 'm_w_in', 'm_w_ya', 'm_w_yb', 'm_pool_w', 'm_pool_scale', 'm_w_o', 'm_g_ffn', 'm_w_up', 'm_conv_w', 'm_conv_b', 'm_w_down', 'm_g_ple', 'm_w_ple', 'm_w_ple_gate', 'm_g_final', 'v_g_mix', 'v_w_in', 'v_w_ya', 'v_w_yb', 'v_pool_w', 'v_pool_scale', 'v_w_o', 'v_g_ffn', 'v_w_up', 'v_conv_w', 'v_conv_b', 'v_w_down', 'v_g_ple', 'v_w_ple', 'v_w_ple_gate', 'v_g_final']
TWIN_OUTPUTS = ['loss', 'grad_x', 'grad_g_mix', 'grad_w_in', 'grad_w_ya', 'grad_w_yb', 'grad_pool_w', 'grad_pool_scale', 'grad_w_o', 'grad_g_ffn', 'grad_w_up', 'grad_conv_w', 'grad_conv_b', 'grad_w_down', 'grad_g_ple', 'grad_w_ple', 'grad_w_ple_gate', 'grad_g_final', 'delta_g_mix', 'delta_w_in', 'delta_w_ya', 'delta_w_yb', 'delta_pool_w', 'delta_pool_scale', 'delta_w_o', 'delta_g_ffn', 'delta_w_up', 'delta_conv_w', 'delta_conv_b', 'delta_w_down', 'delta_g_ple', 'delta_w_ple', 'delta_w_ple_gate', 'delta_g_final', 'new_m_g_mix', 'new_m_w_in', 'new_m_w_ya', 'new_m_w_yb', 'new_m_pool_w', 'new_m_pool_scale', 'new_m_w_o', 'new_m_g_ffn', 'new_m_w_up', 'new_m_conv_w', 'new_m_conv_b', 'new_m_w_down', 'new_m_g_ple', 'new_m_w_ple', 'new_m_w_ple_gate', 'new_m_g_final', 'new_v_g_mix', 'new_v_w_in', 'new_v_w_ya', 'new_v_w_yb', 'new_v_pool_w', 'new_v_pool_scale', 'new_v_w_o', 'new_v_g_ffn', 'new_v_w_up', 'new_v_conv_w', 'new_v_conv_b', 'new_v_w_down', 'new_v_g_ple', 'new_v_w_ple', 'new_v_w_ple_gate', 'new_v_g_final']
TWIN_LEAF_KINDS = {'loss': 'loss', 'grad_x': 'grad_x', 'grad_g_mix': 'grad_w', 'grad_w_in': 'grad_w', 'grad_w_ya': 'grad_w', 'grad_w_yb': 'grad_w', 'grad_pool_w': 'grad_w', 'grad_pool_scale': 'grad_w', 'grad_w_o': 'grad_w', 'grad_g_ffn': 'grad_w', 'grad_w_up': 'grad_w', 'grad_conv_w': 'grad_w', 'grad_conv_b': 'grad_w', 'grad_w_down': 'grad_w', 'grad_g_ple': 'grad_w', 'grad_w_ple': 'grad_w', 'grad_w_ple_gate': 'grad_w', 'grad_g_final': 'grad_w', 'delta_g_mix': 'delta_w', 'delta_w_in': 'delta_w', 'delta_w_ya': 'delta_w', 'delta_w_yb': 'delta_w', 'delta_pool_w': 'delta_w', 'delta_pool_scale': 'delta_w', 'delta_w_o': 'delta_w', 'delta_g_ffn': 'delta_w', 'delta_w_up': 'delta_w', 'delta_conv_w': 'delta_w', 'delta_conv_b': 'delta_w', 'delta_w_down': 'delta_w', 'delta_g_ple': 'delta_w', 'delta_w_ple': 'delta_w', 'delta_w_ple_gate': 'delta_w', 'delta_g_final': 'delta_w', 'new_m_g_mix': 'new_m', 'new_m_w_in': 'new_m', 'new_m_w_ya': 'new_m', 'new_m_w_yb': 'new_m', 'new_m_pool_w': 'new_m', 'new_m_pool_scale': 'new_m', 'new_m_w_o': 'new_m', 'new_m_g_ffn': 'new_m', 'new_m_w_up': 'new_m', 'new_m_conv_w': 'new_m', 'new_m_conv_b': 'new_m', 'new_m_w_down': 'new_m', 'new_m_g_ple': 'new_m', 'new_m_w_ple': 'new_m', 'new_m_w_ple_gate': 'new_m', 'new_m_g_final': 'new_m', 'new_v_g_mix': 'new_v', 'new_v_w_in': 'new_v', 'new_v_w_ya': 'new_v', 'new_v_w_yb': 'new_v', 'new_v_pool_w': 'new_v', 'new_v_pool_scale': 'new_v', 'new_v_w_o': 'new_v', 'new_v_g_ffn': 'new_v', 'new_v_w_up': 'new_v', 'new_v_conv_w': 'new_v', 'new_v_conv_b': 'new_v', 'new_v_w_down': 'new_v', 'new_v_g_ple': 'new_v', 'new_v_w_ple': 'new_v', 'new_v_w_ple_gate': 'new_v', 'new_v_g_final': 'new_v'}


def _forward(args):
    return _fwd_reference(*[args[k] for k in FWD_PARAMS])


def _output_shape():
    out = _jax.eval_shape(lambda: _forward(_fwd_setup_inputs(0)))
    return out.shape, out.dtype

N_MICROBATCH = 1
ADAM_LR = 0.001
ADAM_B1 = 0.9
ADAM_B2 = 0.999
ADAM_EPS = 1e-08
ADAM_WD = 0.01
ADAM_STEP = 10
PER_EXAMPLE_BATCH_AXIS = {'x': 0, 'p': 1, 'loss_target': 0}
SHARED_INPUTS = []
_WEIGHT_DTYPES = {'g_mix': _jnp.float32, 'w_in': _jnp.float32, 'w_ya': _jnp.float32, 'w_yb': _jnp.float32, 'pool_w': _jnp.float32, 'pool_scale': _jnp.float32, 'w_o': _jnp.float32, 'g_ffn': _jnp.float32, 'w_up': _jnp.float32, 'conv_w': _jnp.float32, 'conv_b': _jnp.float32, 'w_down': _jnp.float32, 'g_ple': _jnp.float32, 'w_ple': _jnp.float32, 'w_ple_gate': _jnp.float32, 'g_final': _jnp.float32}
MOMENT_SCALE = {'g_mix': 1.110089e-01, 'w_in': 4.166509e-02, 'w_ya': 2.188727e-02, 'w_yb': 1.021710e-01, 'pool_w': 1.028922e-01, 'pool_scale': 1.051310e-01, 'w_o': 1.031048e-01, 'g_ffn': 1.564491e-01, 'w_up': 6.592274e-02, 'conv_w': 6.466619e-02, 'conv_b': 6.379307e-02, 'w_down': 1.069928e-01, 'g_ple': 3.692876e-02, 'w_ple': 9.216379e-02, 'w_ple_gate': 3.600379e-02, 'g_final': 6.400813e+01}


def _to_microbatches(a, axis):
    t = _jnp.moveaxis(a, axis, 0)
    t = t.reshape((N_MICROBATCH, t.shape[0] // N_MICROBATCH) + t.shape[1:])
    return _jnp.moveaxis(t, 1, axis + 1)


def setup_inputs(seed: int = 0) -> dict:
    inp = _fwd_setup_inputs(seed)
    key = _jax.random.fold_in(_jax.random.key(seed), 7919)
    shape, _ = _output_shape()
    out = dict(inp)
    out["loss_target"] = _jax.random.normal(_jax.random.fold_in(key, 0), shape, _jnp.float32)
    for i, name in enumerate(TWIN_WEIGHTS):
        w = inp[name].astype(_jnp.float32)
        if MOMENT_SCALE is None:
            s = _jnp.sqrt(_jnp.mean(_jnp.square(w)) + 1e-30)
        else:
            s = MOMENT_SCALE[name]
        km, kv = _jax.random.split(_jax.random.fold_in(key, i + 1))
        out[name] = w
        out["m_" + name] = s * _jax.random.normal(km, w.shape, _jnp.float32)
        out["v_" + name] = (s * s) * _jax.random.uniform(kv, w.shape, _jnp.float32, 0.5, 1.5)
    if N_MICROBATCH > 1:
        for name, axis in PER_EXAMPLE_BATCH_AXIS.items():
            out[name] = _to_microbatches(out[name], axis)
    return {'x': out['x'], 'p': out['p'], 'g_mix': out['g_mix'], 'w_in': out['w_in'], 'w_ya': out['w_ya'], 'w_yb': out['w_yb'], 'pool_w': out['pool_w'], 'pool_scale': out['pool_scale'], 'w_o': out['w_o'], 'g_ffn': out['g_ffn'], 'w_up': out['w_up'], 'conv_w': out['conv_w'], 'conv_b': out['conv_b'], 'w_down': out['w_down'], 'g_ple': out['g_ple'], 'w_ple': out['w_ple'], 'w_ple_gate': out['w_ple_gate'], 'g_final': out['g_final'], 'loss_target': out['loss_target'], 'm_g_mix': out['m_g_mix'], 'm_w_in': out['m_w_in'], 'm_w_ya': out['m_w_ya'], 'm_w_yb': out['m_w_yb'], 'm_pool_w': out['m_pool_w'], 'm_pool_scale': out['m_pool_scale'], 'm_w_o': out['m_w_o'], 'm_g_ffn': out['m_g_ffn'], 'm_w_up': out['m_w_up'], 'm_conv_w': out['m_conv_w'], 'm_conv_b': out['m_conv_b'], 'm_w_down': out['m_w_down'], 'm_g_ple': out['m_g_ple'], 'm_w_ple': out['m_w_ple'], 'm_w_ple_gate': out['m_w_ple_gate'], 'm_g_final': out['m_g_final'], 'v_g_mix': out['v_g_mix'], 'v_w_in': out['v_w_in'], 'v_w_ya': out['v_w_ya'], 'v_w_yb': out['v_w_yb'], 'v_pool_w': out['v_pool_w'], 'v_pool_scale': out['v_pool_scale'], 'v_w_o': out['v_w_o'], 'v_g_ffn': out['v_g_ffn'], 'v_w_up': out['v_w_up'], 'v_conv_w': out['v_conv_w'], 'v_conv_b': out['v_conv_b'], 'v_w_down': out['v_w_down'], 'v_g_ple': out['v_g_ple'], 'v_w_ple': out['v_w_ple'], 'v_w_ple_gate': out['v_w_ple_gate'], 'v_g_final': out['v_g_final']}


def _loss(weights, diff, rest, loss_target):
    with _jax.named_scope("forward"):
        args = {**rest, TWIN_DIFF_INPUT: diff, **{k: w.astype(_WEIGHT_DTYPES[k]) for k, w in weights.items()}}
        y = _forward(args)
    with _jax.named_scope("loss_head"):
        err = _jnp.square(y.astype(_jnp.float32) - loss_target)
        return 0.5 * _jnp.sum(_jnp.mean(err, axis=-1)) if err.ndim else 0.5 * err


def _adamw(w, g, m, v):
    m = ADAM_B1 * m + (1.0 - ADAM_B1) * g
    v = ADAM_B2 * v + (1.0 - ADAM_B2) * _jnp.square(g)
    m_hat = m / (1.0 - ADAM_B1 ** ADAM_STEP)
    v_hat = v / (1.0 - ADAM_B2 ** ADAM_STEP)
    delta = -ADAM_LR * (m_hat / (_jnp.sqrt(v_hat) + ADAM_EPS) + ADAM_WD * w)
    return delta, m, v


def reference(x, p, g_mix, w_in, w_ya, w_yb, pool_w, pool_scale, w_o, g_ffn, w_up, conv_w, conv_b, w_down, g_ple, w_ple, w_ple_gate, g_final, loss_target, m_g_mix, m_w_in, m_w_ya, m_w_yb, m_pool_w, m_pool_scale, m_w_o, m_g_ffn, m_w_up, m_conv_w, m_conv_b, m_w_down, m_g_ple, m_w_ple, m_w_ple_gate, m_g_final, v_g_mix, v_w_in, v_w_ya, v_w_yb, v_pool_w, v_pool_scale, v_w_o, v_g_ffn, v_w_up, v_conv_w, v_conv_b, v_w_down, v_g_ple, v_w_ple, v_w_ple_gate, v_g_final):
    given = dict(x=x, p=p, g_mix=g_mix, w_in=w_in, w_ya=w_ya, w_yb=w_yb, pool_w=pool_w, pool_scale=pool_scale, w_o=w_o, g_ffn=g_ffn, w_up=w_up, conv_w=conv_w, conv_b=conv_b, w_down=w_down, g_ple=g_ple, w_ple=w_ple, w_ple_gate=w_ple_gate, g_final=g_final, loss_target=loss_target, m_g_mix=m_g_mix, m_w_in=m_w_in, m_w_ya=m_w_ya, m_w_yb=m_w_yb, m_pool_w=m_pool_w, m_pool_scale=m_pool_scale, m_w_o=m_w_o, m_g_ffn=m_g_ffn, m_w_up=m_w_up, m_conv_w=m_conv_w, m_conv_b=m_conv_b, m_w_down=m_w_down, m_g_ple=m_g_ple, m_w_ple=m_w_ple, m_w_ple_gate=m_w_ple_gate, m_g_final=m_g_final, v_g_mix=v_g_mix, v_w_in=v_w_in, v_w_ya=v_w_ya, v_w_yb=v_w_yb, v_pool_w=v_pool_w, v_pool_scale=v_pool_scale, v_w_o=v_w_o, v_g_ffn=v_g_ffn, v_w_up=v_w_up, v_conv_w=v_conv_w, v_conv_b=v_conv_b, v_w_down=v_w_down, v_g_ple=v_g_ple, v_w_ple=v_w_ple, v_w_ple_gate=v_w_ple_gate, v_g_final=v_g_final)
    weights = {n: given[n] for n in TWIN_WEIGHTS}
    shared = {n: given[n] for n in SHARED_INPUTS}
    per_example = {n: given[n] for n in ['x', 'p']}
    grad_fn = _jax.value_and_grad(_loss, argnums=(0, 1))

    def one_microbatch(ex, loss_target):
        ex = dict(ex)
        diff = ex.pop(TWIN_DIFF_INPUT)
        return grad_fn(weights, diff, {**shared, **ex}, loss_target)

    if N_MICROBATCH == 1:
        loss, (grad_w, grad_x) = one_microbatch(per_example, given["loss_target"])
    else:
        def body(carry, xs):
            loss_sum, grad_sum = carry
            l_k, (gw_k, gx_k) = one_microbatch(xs[0], xs[1])
            with _jax.named_scope("update"):
                return (loss_sum + l_k, _jax.tree.map(_jnp.add, grad_sum, gw_k)), gx_k

        init = (_jnp.zeros((), _jnp.float32), _jax.tree.map(_jnp.zeros_like, weights))
        (loss, grad_w), grad_x = _jax.lax.scan(body, init, (per_example, given["loss_target"]))
    with _jax.named_scope("update"):
        delta_w, new_m, new_v = {}, {}, {}
        for n in TWIN_WEIGHTS:
            delta_w[n], new_m[n], new_v[n] = _adamw(weights[n], grad_w[n], given["m_" + n], given["v_" + n])
    return (loss, grad_x, *[grad_w[n] for n in TWIN_WEIGHTS], *[delta_w[n] for n in TWIN_WEIGHTS],
            *[new_m[n] for n in TWIN_WEIGHTS], *[new_v[n] for n in TWIN_WEIGHTS])
```

```python
import math

import jax
import jax.numpy as jnp
from jax import lax
from jax.experimental import pallas as pl
from jax.experimental.pallas import tpu as pltpu

F32 = jnp.float32
BF16 = jnp.bfloat16

N_DEV = 8
HEAD_DIM = 128
HEADS = 4
GROUP_W = HEADS * HEAD_DIM
DILATIONS = (1, 4, 16)
N_GROUPS = len(DILATIONS)
QKV_W = 3 * N_GROUPS * GROUP_W
GROUP_QKV_W = 3 * GROUP_W
BLOCK = 128
ROPE_DIM = HEAD_DIM // 4
ROPE_HALF = ROPE_DIM // 2
ROPE_THETA = 500000.0
NEG_INF = -1e30
POOL_WINDOWS = (2, 4, 8, 16)
POOL_HALO = 16
CONV_HALO = 8
RMS_EPS = 1e-6
ADAM_LR = 0.001
ADAM_B1 = 0.9
ADAM_B2 = 0.999
ADAM_EPS = 1e-08
ADAM_WD = 0.01
ADAM_STEP = 10
LANE = 128
PACK_COLS = 1024
PACK_ROWS = 16
MESH_ID = pl.DeviceIdType.MESH
MESH_AXES = ("x", "y", "c")

NT_DIMS = (((1,), (1,)), ((), ()))
TN_DIMS = (((0,), (0,)), ((), ()))
NN_DIMS = (((1,), (0,)), ((), ()))

SHARDED = (("w_in", 1), ("w_ya", 1), ("w_yb", 0), ("pool_w", 1), ("w_o", 0), ("w_up", 1), ("conv_w", 1),
           ("w_down", 0), ("w_ple", 1), ("w_ple_gate", 0))
EXACT_F32 = ("conv_w",)
REPLICATED = ("g_mix", "pool_scale", "g_ffn", "conv_b", "g_ple", "g_final")
WEIGHT_ORDER = ("g_mix", "w_in", "w_ya", "w_yb", "pool_w", "pool_scale", "w_o", "g_ffn", "w_up", "conv_w", "conv_b",
                "w_down", "g_ple", "w_ple", "w_ple_gate", "g_final")


def _tile(n, pref, mult=LANE):
    if n <= pref:
        return n
    t = (pref // mult) * mult
    while t >= mult:
        if n % t == 0:
            return t
        t -= mult
    return n


def _sigmoid(x):
    return 1.0 / (1.0 + jnp.exp(-x))


def _dot(a, b, dims=NN_DIMS):
    return lax.dot_general(a.astype(BF16), b.astype(BF16), dims, preferred_element_type=F32)


def _rstd(x):
    return lax.rsqrt(jnp.mean(x * x, axis=-1, keepdims=True) + RMS_EPS)


def _rms_bwd(x, g, dh):
    r = _rstd(x)
    u = dh * g
    dx = r * u - x * (r * r * r) * jnp.mean(x * u, axis=-1, keepdims=True)
    return dx, dh * x * r


def _full(a):
    return pl.BlockSpec(a.shape, lambda *_: (0,) * a.ndim)


def _mm(name, a, b, *, grid, a_block, a_map, b_block, b_map, dims, acc_shape, outs, extras=(), epi=None):
    nk = grid[2]
    n_ex = len(extras)
    n_out = len(outs)

    def body(*refs):
        a_ref, b_ref = refs[0], refs[1]
        ex = refs[2:2 + n_ex]
        o = refs[2 + n_ex:2 + n_ex + n_out]
        acc = refs[2 + n_ex + n_out]
        i, j, k = pl.program_id(0), pl.program_id(1), pl.program_id(2)

        @pl.when(k == 0)
        def _():
            acc[...] = jnp.zeros_like(acc)

        acc[...] += _dot(a_ref[...], b_ref[...], dims)

        @pl.when(k == nk - 1)
        def _():
            if epi is None:
                o[0][...] = acc[...].astype(o[0].dtype)
            else:
                epi(acc[...], ex, o, i, j)

    in_specs = [pl.BlockSpec(a_block, a_map), pl.BlockSpec(b_block, b_map)]
    in_specs += [pl.BlockSpec(blk, mp) for (_, blk, mp) in extras]
    out_specs = [pl.BlockSpec(blk, mp) for (_, _, blk, mp) in outs]
    out_shape = [jax.ShapeDtypeStruct(s, d) for (s, d, _, _) in outs]
    return pl.pallas_call(
        body, name=name, grid=grid, in_specs=in_specs, out_specs=out_specs, out_shape=out_shape,
        scratch_shapes=[pltpu.VMEM(acc_shape, F32)],
    )(a, b, *[e[0] for e in extras])


def _mm_nn(name, a, b, *, out_dtype=F32, tm=1024, tn=1024, tk=512, b_col_off=0, n_cols=None, add=None):
    M, K = a.shape
    N = n_cols if n_cols is not None else b.shape[1]
    tm, tn, tk = _tile(M, tm, 8), _tile(N, tn), _tile(K, tk)
    assert b_col_off % tn == 0
    joff = b_col_off // tn
    extras, epi = (), None
    if add is not None:
        extras = ((add, (tm, tn), lambda i, j, k: (i, j)),)

        def epi(acc, ex, o, i, j):
            o[0][...] = (acc + ex[0][...]).astype(o[0].dtype)

    return _mm(name, a, b, grid=(M // tm, N // tn, K // tk),
               a_block=(tm, tk), a_map=lambda i, j, k: (i, k),
               b_block=(tk, tn), b_map=lambda i, j, k: (k, j + joff),
               dims=NN_DIMS, acc_shape=(tm, tn),
               outs=[((M, N), out_dtype, (tm, tn), lambda i, j, k: (i, j))], extras=extras, epi=epi)[0]


def _mm_nt(name, a, b, *, out_dtype=F32, tm=1024, tn=1024, tk=512):
    M, K = a.shape
    N = b.shape[0]
    tm, tn, tk = _tile(M, tm, 8), _tile(N, tn), _tile(K, tk)
    return _mm(name, a, b, grid=(M // tm, N // tn, K // tk),
               a_block=(tm, tk), a_map=lambda i, j, k: (i, k),
               b_block=(tn, tk), b_map=lambda i, j, k: (j, k),
               dims=NT_DIMS, acc_shape=(tm, tn),
               outs=[((M, N), out_dtype, (tm, tn), lambda i, j, k: (i, j))])[0]


def _mm_tn(name, a, b, *, tm=1024, tn=1024, tk=1024):
    K, M = a.shape
    N = b.shape[1]
    tm, tn, tk = _tile(M, tm), _tile(N, tn), _tile(K, tk, 8)
    return _mm(name, a, b, grid=(M // tm, N // tn, K // tk),
               a_block=(tk, tm), a_map=lambda i, j, k: (k, i),
               b_block=(tk, tn), b_map=lambda i, j, k: (k, j),
               dims=TN_DIMS, acc_shape=(tm, tn),
               outs=[((M, N), F32, (tm, tn), lambda i, j, k: (i, j))])[0]


def _mm_nt_rmsbwd(name, a, a_block, a_map, nk, tk, w, x, g, dres):
    T, D = x.shape
    tm = a_block[-2]

    def epi(acc, ex, o, i, j):
        @pl.when(i == 0)
        def _():
            o[1][...] = jnp.zeros_like(o[1])

        dx, dgr = _rms_bwd(ex[0][...], ex[1][...], acc)
        o[0][...] = ex[2][...] + dx
        o[1][...] += jnp.sum(dgr, axis=0, keepdims=True)

    row = lambda i, j, k: (i, 0)
    vec = lambda i, j, k: (0, 0)
    return _mm(name, a, w, grid=(T // tm, 1, nk),
               a_block=a_block, a_map=a_map,
               b_block=(D, tk), b_map=lambda i, j, k: (0, k),
               dims=NT_DIMS, acc_shape=(tm, D),
               outs=[((T, D), F32, (tm, D), row), ((1, D), F32, (1, D), vec)],
               extras=[(x, (tm, D), row), (g.reshape(1, D), (1, D), vec), (dres, (tm, D), row)], epi=epi)


def _in_bwd(name, segs, w_perm, x, g, dres):
    T, D = x.shape
    tm = _tile(T, 512, 8)
    tk = GROUP_QKV_W
    steps = [s.shape[1] // tk for s in segs]
    starts = [sum(steps[:s]) for s in range(len(segs))]
    nk = sum(steps)
    ns = len(segs)
    assert all(s.shape[1] % tk == 0 for s in segs) and nk * tk == w_perm.shape[1]

    def body(*refs):
        a_refs = refs[:ns]
        w_ref, x_ref, g_ref, dres_ref, dx_ref, dg_ref, acc = refs[ns:]
        i, k = pl.program_id(0), pl.program_id(1)

        @pl.when(k == 0)
        def _():
            acc[...] = jnp.zeros_like(acc)

        for s in range(ns):
            @pl.when((k >= starts[s]) & (k < starts[s] + steps[s]))
            def _():
                acc[...] += _dot(a_refs[s][...], w_ref[...], NT_DIMS)

        @pl.when(k == nk - 1)
        def _():
            @pl.when(i == 0)
            def _():
                dg_ref[...] = jnp.zeros_like(dg_ref)

            dx, dgr = _rms_bwd(x_ref[...], g_ref[...], acc[...])
            dx_ref[...] = dres_ref[...] + dx
            dg_ref[...] += jnp.sum(dgr, axis=0, keepdims=True)

    seg_spec = lambda s: pl.BlockSpec((tm, tk), lambda i, k: (i, jnp.clip(k - starts[s], 0, steps[s] - 1)))
    row = pl.BlockSpec((tm, D), lambda i, k: (i, 0))
    vec = pl.BlockSpec((1, D), lambda i, k: (0, 0))
    return pl.pallas_call(
        body, name=name, grid=(T // tm, nk),
        in_specs=[seg_spec(s) for s in range(ns)] + [pl.BlockSpec((D, tk), lambda i, k: (0, k)), row, vec, row],
        out_specs=[row, vec],
        out_shape=[jax.ShapeDtypeStruct((T, D), F32), jax.ShapeDtypeStruct((1, D), F32)],
        scratch_shapes=[pltpu.VMEM((tm, D), F32)],
    )(*segs, w_perm, x, g.reshape(1, D), dres)


def _rope_tables(S):
    pos = jnp.arange(S, dtype=F32)
    inv_freq = jnp.exp(jnp.arange(0, ROPE_DIM, 2, dtype=F32) * (-math.log(ROPE_THETA) / ROPE_DIM))
    ang = pos[:, None] * inv_freq[None, :]
    cos, sin = jnp.cos(ang), jnp.sin(ang)
    ones = jnp.ones((S, HEAD_DIM - ROPE_DIM), F32)
    zeros_h = jnp.zeros((S, ROPE_HALF), F32)
    zeros_r = jnp.zeros((S, HEAD_DIM - ROPE_DIM), F32)
    c = jnp.concatenate([cos, cos, ones], axis=1)
    sa = jnp.concatenate([-sin, zeros_h, zeros_r], axis=1)
    sb = jnp.concatenate([zeros_h, sin, zeros_r], axis=1)
    return c, sa, sb


def _rope(t, c, sa, sb):
    return t * c + pltpu.roll(t, HEAD_DIM - ROPE_HALF, 1) * sa + pltpu.roll(t, ROPE_HALF, 1) * sb


def _rms_fwd(name, x, g):
    T, D = x.shape
    tm = _tile(T, 512, 8)

    def body(x_ref, g_ref, h_ref):
        xv = x_ref[...]
        h_ref[...] = (xv * _rstd(xv) * g_ref[...]).astype(BF16)

    return pl.pallas_call(
        body, name=name, grid=(T // tm,),
        in_specs=[pl.BlockSpec((tm, D), lambda i: (i, 0)), pl.BlockSpec((1, D), lambda i: (0, 0))],
        out_specs=pl.BlockSpec((tm, D), lambda i: (i, 0)),
        out_shape=jax.ShapeDtypeStruct((T, D), BF16))(x, g.reshape(1, D))


def _qkv_proj(name, h, w_perm, rope, S, col_off):
    T, D = h.shape
    tm = _tile(S, 1024, 8)
    tn = GROUP_W
    tk = _tile(D, 1024)
    c_t, sa_t, sb_t = rope
    n_seq_tiles = S // tm
    joff = col_off // tn
    tmap = lambda i, j, k: (i % n_seq_tiles, 0)

    def epi(acc, ex, o, i, j):
        is_rot = j < 2

        @pl.when(is_rot)
        def _():
            c, sa, sb = ex[0][...], ex[1][...], ex[2][...]
            for hh in range(HEADS):
                sl = slice(hh * HEAD_DIM, (hh + 1) * HEAD_DIM)
                o[0][:, sl] = _rope(acc[:, sl], c, sa, sb).astype(BF16)

        @pl.when(jnp.logical_not(is_rot))
        def _():
            o[0][...] = acc.astype(BF16)

    return _mm(name, h, w_perm, grid=(T // tm, 3, D // tk),
               a_block=(tm, tk), a_map=lambda i, j, k: (i, k),
               b_block=(tk, tn), b_map=lambda i, j, k: (k, j + joff),
               dims=NN_DIMS, acc_shape=(tm, tn),
               outs=[((T, GROUP_QKV_W), BF16, (tm, tn), lambda i, j, k: (i, j))],
               extras=[(c_t, (tm, HEAD_DIM), tmap), (sa_t, (tm, HEAD_DIM), tmap), (sb_t, (tm, HEAD_DIM), tmap)],
               epi=epi)[0]


def _attn_mask(n):
    qi = lax.broadcasted_iota(jnp.int32, (BLOCK, 2 * BLOCK), 0)
    ki = lax.broadcasted_iota(jnp.int32, (BLOCK, 2 * BLOCK), 1)
    diff = BLOCK + qi - ki
    return (diff >= 0) & (diff <= BLOCK) & ((n > 0) | (ki >= BLOCK))


def _attn_fwd(name, qkv, g, Bl, S):
    d = DILATIONS[g]
    L = S // d
    nb = L // BLOCK
    ncb = 3
    qv = qkv.reshape(Bl, L, d * GROUP_QKV_W)
    scale = HEAD_DIM ** -0.5

    def body(q_ref, kc_ref, vc_ref, kp_ref, vp_ref, o_ref, l_ref):
        valid = _attn_mask(pl.program_id(2))
        for hh in range(HEADS):
            sl = slice(hh * HEAD_DIM, (hh + 1) * HEAD_DIM)
            kk = jnp.concatenate([kp_ref[:, sl], kc_ref[:, sl]], axis=0)
            vv = jnp.concatenate([vp_ref[:, sl], vc_ref[:, sl]], axis=0)
            s = jnp.where(valid, _dot(q_ref[:, sl], kk, NT_DIMS) * scale, NEG_INF)
            m = jnp.max(s, axis=-1, keepdims=True)
            p = jnp.exp(s - m)
            l = jnp.sum(p, axis=-1, keepdims=True)
            o_ref[:, sl] = _dot(p, vv) / l
            l_ref[:, sl] = jnp.broadcast_to(m + jnp.log(l), (BLOCK, HEAD_DIM))

    blk = (None, BLOCK, GROUP_W)
    cur = lambda off: (lambda b, r, n: (b, n, r * ncb + off))
    prev = lambda off: (lambda b, r, n: (b, jnp.maximum(n - 1, 0), r * ncb + off))
    omap = lambda b, r, n: (b, n, r)
    o, lse = pl.pallas_call(
        body, name=name, grid=(Bl, d, nb),
        in_specs=[pl.BlockSpec(blk, cur(0)), pl.BlockSpec(blk, cur(1)), pl.BlockSpec(blk, cur(2)),
                  pl.BlockSpec(blk, prev(1)), pl.BlockSpec(blk, prev(2))],
        out_specs=[pl.BlockSpec(blk, omap), pl.BlockSpec(blk, omap)],
        out_shape=[jax.ShapeDtypeStruct((Bl, L, d * GROUP_W), F32)] * 2)(qv, qv, qv, qv, qv)
    return o.reshape(Bl * S, GROUP_W), lse.reshape(Bl * S, GROUP_W), lse


def _merge_weights(l0, l1, l2):
    mx = jnp.maximum(jnp.maximum(l0, l1), l2)
    e0, e1, e2 = jnp.exp(l0 - mx), jnp.exp(l1 - mx), jnp.exp(l2 - mx)
    inv = 1.0 / (e0 + e1 + e2)
    return e0 * inv, e1 * inv, e2 * inv


def _pool_inv_count(tseq, w):
    return 1.0 / jnp.minimum(tseq + 1, w).astype(F32)


def _mix_out_fwd(name, x, zr, o_l, lse_l, w_ya, pool_w, pool_scale, w_yb, w_o, S):
    T, D = x.shape
    gw = D // len(POOL_WINDOWS)
    tm = _tile(S, 256, POOL_HALO)
    nst = S // tm
    hpt = tm // POOL_HALO

    def body(x_ref, u_ref, uh_ref, ga_ref, gb_ref, o0, o1, o2, l0, l1, l2, wya_ref, pw_ref, ps_ref, wyb_ref, wo_ref,
             x1_ref, attn_ref, pooled_ref, mixed_ref, ya_ref, yb_ref, merged_ref):
        it = pl.program_id(0) % nst
        w0, w1, w2 = _merge_weights(l0[...], l1[...], l2[...])
        attn = w0 * o0[...] + w1 * o1[...] + w2 * o2[...]
        attn_ref[...] = attn.astype(BF16)
        y_a = _dot(attn, wya_ref[...])

        u = u_ref[...]
        halo = uh_ref[...] * jnp.where(it == 0, 0.0, 1.0)
        ext = jnp.concatenate([halo, u], axis=0)
        tseq = it * tm + lax.broadcasted_iota(jnp.int32, (tm, 1), 0)
        pm_parts = []
        for gi, w in enumerate(POOL_WINDOWS):
            cs = slice(gi * gw, (gi + 1) * gw)
            s = ext[:, cs]
            step = 1
            while step < w:
                s = s + pltpu.roll(s, step, 0)
                step *= 2
            pooled_g = (s[POOL_HALO:, :] * _pool_inv_count(tseq, w) - u[:, cs]).astype(BF16)
            pooled_ref[:, cs] = pooled_g
            pm_parts.append(_dot(pooled_g, pw_ref[gi]))
        mixed = (jnp.concatenate(pm_parts, axis=1) * ps_ref[...]).astype(BF16)
        mixed_ref[...] = mixed
        y_b = _dot(mixed, wyb_ref[...])
        merged = (_sigmoid(ga_ref[...]) * y_a + _sigmoid(gb_ref[...]) * y_b).astype(BF16)
        ya_ref[...] = y_a
        yb_ref[...] = y_b
        merged_ref[...] = merged
        x1_ref[...] = x_ref[...] + _dot(merged, wo_ref[...])

    row = lambda c: pl.BlockSpec((tm, D), lambda i: (i, c))
    row512 = pl.BlockSpec((tm, GROUP_W), lambda i: (i, 0))
    ps = pool_scale.reshape(1, D)
    halo_spec = pl.BlockSpec((POOL_HALO, D), lambda i: (jnp.maximum(i * hpt - 1, 0), 0))
    return pl.pallas_call(
        body, name=name, grid=(T // tm,),
        in_specs=[row(0), row(0), halo_spec, row(1), row(2)] + [row512] * 6
        + [_full(w_ya), _full(pool_w), _full(ps), _full(w_yb), _full(w_o)],
        out_specs=[row(0), row512, row(0), row(0), row(0), row(0), row(0)],
        out_shape=[jax.ShapeDtypeStruct((T, D), F32), jax.ShapeDtypeStruct((T, GROUP_W), BF16),
                   jax.ShapeDtypeStruct((T, D), BF16), jax.ShapeDtypeStruct((T, D), BF16),
                   jax.ShapeDtypeStruct((T, D), F32), jax.ShapeDtypeStruct((T, D), F32),
                   jax.ShapeDtypeStruct((T, D), BF16)],
    )(x, zr, zr, zr, zr, *o_l, *lse_l, w_ya, pool_w, ps, w_yb, w_o)


def _up_proj(name, h2, w_up, F):
    T, D = h2.shape
    tm, tn, tk = _tile(T, 1024, 8), _tile(F, 1408), _tile(D, 512)
    njh = F // tn
    return _mm(name, h2, w_up, grid=(T // tm, 2 * njh, D // tk),
               a_block=(tm, tk), a_map=lambda i, j, k: (i, k),
               b_block=(tk, tn), b_map=lambda i, j, k: (k, j),
               dims=NN_DIMS, acc_shape=(tm, tn),
               outs=[((2, T, F), F32, (None, tm, tn), lambda i, j, k: (j // njh, i, j % njh))])[0]


def _conv_y(ext, w_ref, b_ref):
    return (b_ref[...] + w_ref[2:3, :] * ext + w_ref[1:2, :] * pltpu.roll(ext, 1, 0)
            + w_ref[0:1, :] * pltpu.roll(ext, 2, 0))


def _conv_params(conv_w, conv_b, F):
    cw = conv_w.reshape(3, 2, F).transpose(1, 0, 2)
    return cw, conv_b.reshape(2, 1, F)


def _ffn_act_fwd(name, u, conv_w, conv_b, S):
    _, T, F = u.shape
    tm = _tile(S, 512, CONV_HALO)
    tf = _tile(F, 1408)
    nst = S // tm
    hpt = tm // CONV_HALO
    cw, cb = _conv_params(conv_w, conv_b, F)

    def body(ug_ref, uv_ref, hg_ref, hv_ref, wg_ref, wv_ref, bg_ref, bv_ref, a_ref):
        keep = jnp.where(pl.program_id(0) % nst == 0, 0.0, 1.0)
        yg = _conv_y(jnp.concatenate([hg_ref[...] * keep, ug_ref[...]], axis=0), wg_ref, bg_ref)[CONV_HALO:, :]
        yv = _conv_y(jnp.concatenate([hv_ref[...] * keep, uv_ref[...]], axis=0), wv_ref, bv_ref)[CONV_HALO:, :]
        a_ref[...] = (yg * _sigmoid(yg) * yv).astype(BF16)

    main = lambda h: pl.BlockSpec((None, tm, tf), lambda i, j: (h, i, j))
    halo = lambda h: pl.BlockSpec((None, CONV_HALO, tf), lambda i, j: (h, jnp.maximum(i * hpt - 1, 0), j))
    wsp = lambda h: pl.BlockSpec((None, 3, tf), lambda i, j: (h, 0, j))
    bsp = lambda h: pl.BlockSpec((None, 1, tf), lambda i, j: (h, 0, j))
    return pl.pallas_call(
        body, name=name, grid=(T // tm, F // tf),
        in_specs=[main(0), main(1), halo(0), halo(1), wsp(0), wsp(1), bsp(0), bsp(1)],
        out_specs=pl.BlockSpec((tm, tf), lambda i, j: (i, j)),
        out_shape=jax.ShapeDtypeStruct((T, F), BF16))(u, u, u, u, cw, cw, cb, cb)


def _ple_fwd(name, x2, p, g_ple, w_gate, w_ple):
    T, D = x2.shape
    P = p.shape[1]
    tm = _tile(T, 512, 8)

    def body(x_ref, p_ref, g_ref, wg_ref, wp_ref, x3_ref, e_ref, pg_ref, pbf_ref):
        xv = x_ref[...]
        h3 = xv * _rstd(xv) * g_ref[...]
        pg = _sigmoid(_dot(h3, wg_ref[...]))
        pb = p_ref[...].astype(BF16)
        e = _dot(pb, wp_ref[...])
        x3_ref[...] = xv + e * pg
        e_ref[...] = e
        pg_ref[...] = pg
        pbf_ref[...] = pb

    row = pl.BlockSpec((tm, D), lambda i: (i, 0))
    prow = pl.BlockSpec((tm, P), lambda i: (i, 0))
    g2 = g_ple.reshape(1, D)
    return pl.pallas_call(
        body, name=name, grid=(T // tm,),
        in_specs=[row, prow, _full(g2), _full(w_gate), _full(w_ple)],
        out_specs=[row, row, row, prow],
        out_shape=[jax.ShapeDtypeStruct((T, D), F32)] * 3 + [jax.ShapeDtypeStruct((T, P), BF16)],
    )(x2, p, g2, w_gate, w_ple)


def _loss_bwd(name, xf, target, g_final):
    T, D = xf.shape
    tm = _tile(T, 512, 8)
    nt = T // tm

    def body(x_ref, t_ref, g_ref, dx_ref, loss_ref, dg_ref, lacc):
        i = pl.program_id(0)

        @pl.when(i == 0)
        def _():
            lacc[...] = jnp.zeros_like(lacc)
            dg_ref[...] = jnp.zeros_like(dg_ref)
            loss_ref[...] = jnp.zeros_like(loss_ref)

        xv = x_ref[...]
        g = g_ref[...]
        diff = xv * _rstd(xv) * g - t_ref[...]
        lacc[...] += jnp.sum(diff * diff, axis=0, keepdims=True)
        dx, dgr = _rms_bwd(xv, g, diff * (1.0 / D))
        dx_ref[...] = dx
        dg_ref[...] += jnp.sum(dgr, axis=0, keepdims=True)

        @pl.when(i == nt - 1)
        def _():
            tot = jnp.sum(lacc[...], axis=-1, keepdims=True) * (0.5 / D)
            loss_ref[...] = jnp.broadcast_to(tot, (1, LANE))

    row = pl.BlockSpec((tm, D), lambda i: (i, 0))
    vec = pl.BlockSpec((1, D), lambda i: (0, 0))
    return pl.pallas_call(
        body, name=name, grid=(nt,),
        in_specs=[row, row, vec],
        out_specs=[row, pl.BlockSpec((1, LANE), lambda i: (0, 0)), vec],
        out_shape=[jax.ShapeDtypeStruct((T, D), F32), jax.ShapeDtypeStruct((1, LANE), F32),
                   jax.ShapeDtypeStruct((1, D), F32)],
        scratch_shapes=[pltpu.VMEM((1, D), F32)])(xf, target, g_final.reshape(1, D))


def _ple_bwd(name, dx3, x2, e, pg, g_ple, w_gate):
    T, D = x2.shape
    tm = _tile(T, 512, 8)

    def body(dx3_ref, x_ref, e_ref, pg_ref, g_ref, wg_ref, dx2_ref, de_ref, ds_ref, h3_ref, dg_ref):
        @pl.when(pl.program_id(0) == 0)
        def _():
            dg_ref[...] = jnp.zeros_like(dg_ref)

        dx3v, xv, pgv, g = dx3_ref[...], x_ref[...], pg_ref[...], g_ref[...]
        de_ref[...] = (dx3v * pgv).astype(BF16)
        ds = (dx3v * e_ref[...] * pgv * (1.0 - pgv)).astype(BF16)
        ds_ref[...] = ds
        dh3 = _dot(ds, wg_ref[...], NT_DIMS)
        h3_ref[...] = (xv * _rstd(xv) * g).astype(BF16)
        dx, dgr = _rms_bwd(xv, g, dh3)
        dx2_ref[...] = dx3v + dx
        dg_ref[...] += jnp.sum(dgr, axis=0, keepdims=True)

    row = pl.BlockSpec((tm, D), lambda i: (i, 0))
    vec = pl.BlockSpec((1, D), lambda i: (0, 0))
    return pl.pallas_call(
        body, name=name, grid=(T // tm,),
        in_specs=[row, row, row, row, vec, _full(w_gate)],
        out_specs=[row, row, row, row, vec],
        out_shape=[jax.ShapeDtypeStruct((T, D), F32)] + [jax.ShapeDtypeStruct((T, D), BF16)] * 3
        + [jax.ShapeDtypeStruct((1, D), F32)])(dx3, x2, e, pg, g_ple.reshape(1, D), w_gate)


def _ffn_act_bwd(name, u, d_a, conv_w, conv_b, S):
    _, T, F = u.shape
    H = CONV_HALO
    tm = _tile(S, 512, H)
    tf = _tile(F, 1408)
    nst = S // tm
    hpt = tm // H
    last_halo = T // H - 1
    n_ext = tm + H
    cw, cb = _conv_params(conv_w, conv_b, F)

    def body(ug_ref, uv_ref, pg_ref, pv_ref, ng_ref, nv_ref, da_ref, dan_ref, wg_ref, wv_ref, bg_ref, bv_ref,
             du_ref, dw_ref, db_ref):
        i = pl.program_id(1)
        it = i % nst

        @pl.when(i == 0)
        def _():
            dw_ref[...] = jnp.zeros_like(dw_ref)
            db_ref[...] = jnp.zeros_like(db_ref)

        keep_prev = jnp.where(it == 0, 0.0, 1.0)
        keep_next = jnp.where(it == nst - 1, 0.0, 1.0)
        ext_g = jnp.concatenate([pg_ref[...] * keep_prev, ug_ref[...], ng_ref[...]], axis=0)
        ext_v = jnp.concatenate([pv_ref[...] * keep_prev, uv_ref[...], nv_ref[...]], axis=0)
        yg = _conv_y(ext_g, wg_ref, bg_ref)[H:, :]
        yv = _conv_y(ext_v, wv_ref, bv_ref)[H:, :]
        rows = lax.broadcasted_iota(jnp.int32, (n_ext, 1), 0)
        live = jnp.where(rows < tm, 1.0, keep_next)
        da = jnp.concatenate([da_ref[...], dan_ref[...]], axis=0) * live
        sg = _sigmoid(yg)
        dyv = da * (yg * sg)
        dyg = da * yv * (sg * (1.0 + yg * (1.0 - sg)))
        for half, (dy, ext, w_ref) in enumerate(((dyg, ext_g, wg_ref), (dyv, ext_v, wv_ref))):
            du = (w_ref[2:3, :] * dy + w_ref[1:2, :] * pltpu.roll(dy, n_ext - 1, 0)
                  + w_ref[0:1, :] * pltpu.roll(dy, n_ext - 2, 0))
            du_ref[half] = du[:tm, :].astype(BF16)
            dym = dy[:tm, :]
            db_ref[half] += jnp.sum(dym, axis=0, keepdims=True)
            dw_ref[half, 2:3, :] += jnp.sum(dym * ext[H:H + tm, :], axis=0, keepdims=True)
            dw_ref[half, 1:2, :] += jnp.sum(dym * pltpu.roll(ext, 1, 0)[H:H + tm, :], axis=0, keepdims=True)
            dw_ref[half, 0:1, :] += jnp.sum(dym * pltpu.roll(ext, 2, 0)[H:H + tm, :], axis=0, keepdims=True)

    main = lambda h: pl.BlockSpec((None, tm, tf), lambda j, i: (h, i, j))
    prev = lambda h: pl.BlockSpec((None, H, tf), lambda j, i: (h, jnp.maximum(i * hpt - 1, 0), j))
    nxt = lambda h: pl.BlockSpec((None, H, tf), lambda j, i: (h, jnp.minimum((i + 1) * hpt, last_halo), j))
    wsp = lambda h: pl.BlockSpec((None, 3, tf), lambda j, i: (h, 0, j))
    bsp = lambda h: pl.BlockSpec((None, 1, tf), lambda j, i: (h, 0, j))
    return pl.pallas_call(
        body, name=name, grid=(F // tf, T // tm),
        in_specs=[main(0), main(1), prev(0), prev(1), nxt(0), nxt(1),
                  pl.BlockSpec((tm, tf), lambda j, i: (i, j)),
                  pl.BlockSpec((H, tf), lambda j, i: (jnp.minimum((i + 1) * hpt, last_halo), j)),
                  wsp(0), wsp(1), bsp(0), bsp(1)],
        out_specs=[pl.BlockSpec((2, tm, tf), lambda j, i: (0, i, j)),
                   pl.BlockSpec((2, 3, tf), lambda j, i: (0, 0, j)),
                   pl.BlockSpec((2, 1, tf), lambda j, i: (0, 0, j))],
        out_shape=[jax.ShapeDtypeStruct((2, T, F), BF16), jax.ShapeDtypeStruct((2, 3, F), F32),
                   jax.ShapeDtypeStruct((2, 1, F), F32)],
    )(u, u, u, u, u, u, d_a, d_a, cw, cw, cb, cb)


def _mix_out_bwd(name, dx1, zr, y_a, y_b, o_l, lse_l, pooled, w_o, w_ya, w_yb, pool_w, pool_scale, S):
    T, D = dx1.shape
    gw = D // len(POOL_WINDOWS)
    H = POOL_HALO
    tm = _tile(S, 256, H)
    nst = S // tm
    hpt = tm // H
    last_halo = T // H - 1
    n_ext = tm + H

    def body(dx_ref, dxn_ref, ga_ref, gb_ref, gbn_ref, ya_ref, yb_ref, o0, o1, o2, l0, l1, l2, pooled_ref,
             wo_ref, wya_ref, wyb_ref, pw_ref, ps_ref,
             dz_ref, dya_ref, dyb_ref, dpm_ref, do0, do1, do2, c0, c1, c2, dps_ref):
        i = pl.program_id(0)
        it = i % nst

        @pl.when(i == 0)
        def _():
            dps_ref[...] = jnp.zeros_like(dps_ref)

        keep_next = jnp.where(it == nst - 1, 0.0, 1.0)
        dm_e = _dot(jnp.concatenate([dx_ref[...], dxn_ref[...]], axis=0), wo_ref[...], NT_DIMS)
        sgb_e = _sigmoid(jnp.concatenate([gb_ref[...], gbn_ref[...]], axis=0))
        dyb_e = dm_e * sgb_e
        dm = dm_e[:tm, :]
        sga = _sigmoid(ga_ref[...])
        sgb = sgb_e[:tm, :]
        d_ga = dm * ya_ref[...] * (sga * (1.0 - sga))
        d_gb = dm * yb_ref[...] * (sgb * (1.0 - sgb))
        dya = (dm * sga).astype(BF16)
        dya_ref[...] = dya
        dyb_ref[...] = dyb_e[:tm, :].astype(BF16)
        dmixed_e = _dot(dyb_e, wyb_ref[...], NT_DIMS)

        rows = lax.broadcasted_iota(jnp.int32, (n_ext, 1), 0)
        tseq = it * tm + rows
        live = jnp.where(rows < tm, 1.0, keep_next)
        ps = ps_ref[...]
        du_parts = []
        for gi, w in enumerate(POOL_WINDOWS):
            cs = slice(gi * gw, (gi + 1) * gw)
            pm_g = _dot(pooled_ref[:, cs], pw_ref[gi])
            dps_ref[:, cs] += jnp.sum(dmixed_e[:tm, cs] * pm_g, axis=0, keepdims=True)
            dpm_e = (dmixed_e[:, cs] * ps[:, cs]).astype(BF16)
            dpm_ref[:, cs] = dpm_e[:tm, :]
            dpooled_e = _dot(dpm_e, pw_ref[gi], NT_DIMS)
            s = dpooled_e * (_pool_inv_count(tseq, w) * live)
            step = 1
            while step < w:
                s = s + pltpu.roll(s, n_ext - step, 0)
                step *= 2
            du_parts.append(s[:tm, :] - dpooled_e[:tm, :])
        dz_ref[...] = jnp.concatenate(du_parts + [d_ga, d_gb], axis=1).astype(BF16)

        d_attn = _dot(dya, wya_ref[...], NT_DIMS)
        w0, w1, w2 = _merge_weights(l0[...], l1[...], l2[...])
        prod = d_attn * (w0 * o0[...] + w1 * o1[...] + w2 * o2[...])
        rs = jnp.concatenate(
            [jnp.broadcast_to(jnp.sum(prod[:, hh * HEAD_DIM:(hh + 1) * HEAD_DIM], axis=-1, keepdims=True),
                              (tm, HEAD_DIM)) for hh in range(HEADS)], axis=1)
        for wg, do_ref, c_ref in ((w0, do0, c0), (w1, do1, c1), (w2, do2, c2)):
            do_ref[...] = (wg * d_attn).astype(BF16)
            c_ref[...] = -wg * rs

    row = lambda c: pl.BlockSpec((tm, D), lambda i: (i, c))
    nxt = lambda c: pl.BlockSpec((H, D), lambda i: (jnp.minimum((i + 1) * hpt, last_halo), c))
    row512 = pl.BlockSpec((tm, GROUP_W), lambda i: (i, 0))
    ps2 = pool_scale.reshape(1, D)
    bf = lambda w: jax.ShapeDtypeStruct((T, w), BF16)
    return pl.pallas_call(
        body, name=name, grid=(T // tm,),
        in_specs=[row(0), nxt(0), row(1), row(2), nxt(2), row(0), row(0)] + [row512] * 6 + [row(0)]
        + [_full(w_o), _full(w_ya), _full(w_yb), _full(pool_w), _full(ps2)],
        out_specs=[pl.BlockSpec((tm, 3 * D), lambda i: (i, 0)), row(0), row(0), row(0)] + [row512] * 6
        + [pl.BlockSpec((1, D), lambda i: (0, 0))],
        out_shape=[bf(3 * D), bf(D), bf(D), bf(D)] + [bf(GROUP_W)] * 3
        + [jax.ShapeDtypeStruct((T, GROUP_W), F32)] * 3 + [jax.ShapeDtypeStruct((1, D), F32)],
    )(dx1, dx1, zr, zr, zr, y_a, y_b, *o_l, *lse_l, pooled, w_o, w_ya, w_yb, pool_w, ps2)


def _attn_bwd(name, qkv, d_o, lse_view, cst, rope, g, Bl, S):
    d = DILATIONS[g]
    L = S // d
    nb = L // BLOCK
    ncb = 3
    qv = qkv.reshape(Bl, L, d * GROUP_QKV_W)
    dov = d_o.reshape(Bl, L, d * GROUP_W)
    lv = lse_view
    cv = cst.reshape(Bl, L, d * GROUP_W)
    tabs = [t.reshape(L, d * HEAD_DIM) for t in rope]
    scale = HEAD_DIM ** -0.5

    def body(q_ref, qn_ref, kp_ref, kc_ref, vp_ref, vc_ref, do_ref, don_ref, l_ref, ln_ref, c_ref, cn_ref,
             cos_ref, sa_ref, sb_ref, out_ref):
        n = pl.program_id(2)
        valid = _attn_mask(n)
        qi = lax.broadcasted_iota(jnp.int32, (BLOCK, BLOCK), 0)
        ki = lax.broadcasted_iota(jnp.int32, (BLOCK, BLOCK), 1)
        valid_n = (ki >= qi) & (n + 1 < nb)
        cos, sa, sb = cos_ref[...], -sa_ref[...], -sb_ref[...]
        for hh in range(HEADS):
            sl = slice(hh * HEAD_DIM, (hh + 1) * HEAD_DIM)
            q, qn, kc, vc, do, don = q_ref[:, sl], qn_ref[:, sl], kc_ref[:, sl], vc_ref[:, sl], do_ref[:, sl], don_ref[:, sl]
            kk = jnp.concatenate([kp_ref[:, sl], kc], axis=0)
            vv = jnp.concatenate([vp_ref[:, sl], vc], axis=0)
            col = slice(hh * HEAD_DIM, hh * HEAD_DIM + 1)
            s = jnp.where(valid, _dot(q, kk, NT_DIMS) * scale, NEG_INF)
            p = jnp.exp(s - l_ref[:, col])
            ds = p * (_dot(do, vv, NT_DIMS) + c_ref[:, col])
            dq = _dot(ds, kk) * scale
            s2 = jnp.where(valid_n, _dot(qn, kc, NT_DIMS) * scale, NEG_INF)
            p2 = jnp.exp(s2 - ln_ref[:, col])
            ds2 = p2 * (_dot(don, vc, NT_DIMS) + cn_ref[:, col])
            dk = (_dot(ds[:, BLOCK:], q, TN_DIMS) + _dot(ds2, qn, TN_DIMS)) * scale
            dv = _dot(p[:, BLOCK:], do, TN_DIMS) + _dot(p2, don, TN_DIMS)
            out_ref[:, sl] = _rope(dq, cos, sa, sb).astype(BF16)
            out_ref[:, GROUP_W + hh * HEAD_DIM:GROUP_W + (hh + 1) * HEAD_DIM] = _rope(dk, cos, sa, sb).astype(BF16)
            out_ref[:, 2 * GROUP_W + hh * HEAD_DIM:2 * GROUP_W + (hh + 1) * HEAD_DIM] = dv.astype(BF16)

    blk = (None, BLOCK, GROUP_W)
    at = lambda f, off: pl.BlockSpec(blk, lambda b, r, n: (b, f(n), r * ncb + off))
    cur = lambda n: n
    prv = lambda n: jnp.maximum(n - 1, 0)
    nxt = lambda n: jnp.minimum(n + 1, nb - 1)
    tok = lambda f: pl.BlockSpec(blk, lambda b, r, n: (b, f(n), r))
    tab = pl.BlockSpec((BLOCK, HEAD_DIM), lambda b, r, n: (n, r))
    out = pl.pallas_call(
        body, name=name, grid=(Bl, d, nb),
        in_specs=[at(cur, 0), at(nxt, 0), at(prv, 1), at(cur, 1), at(prv, 2), at(cur, 2),
                  tok(cur), tok(nxt), tok(cur), tok(nxt), tok(cur), tok(nxt), tab, tab, tab],
        out_specs=pl.BlockSpec((None, BLOCK, GROUP_QKV_W), lambda b, r, n: (b, n, r)),
        out_shape=jax.ShapeDtypeStruct((Bl, L, d * GROUP_QKV_W), BF16),
    )(qv, qv, qv, qv, qv, qv, dov, dov, lv, lv, cv, cv, *tabs)
    return out.reshape(Bl * S, GROUP_QKV_W)


def _pool_w_grad(name, pooled, d_pm, gw):
    T = pooled.shape[0]
    ng = len(POOL_WINDOWS)
    tk = _tile(T, 1024, 8)
    return _mm(name, pooled, d_pm, grid=(ng, 1, T // tk),
               a_block=(tk, gw), a_map=lambda i, j, k: (k, i),
               b_block=(tk, gw), b_map=lambda i, j, k: (k, i),
               dims=TN_DIMS, acc_shape=(gw, gw),
               outs=[((ng, gw, gw), F32, (None, gw, gw), lambda i, j, k: (i, 0, 0))])[0]


def _up_w_grad(name, h2, du):
    T, D = h2.shape
    F = du.shape[2]
    tm, tn, tk = _tile(D, 1024), _tile(F, 1408), _tile(T, 1024, 8)
    njh = F // tn
    return _mm(name, h2, du, grid=(D // tm, 2 * njh, T // tk),
               a_block=(tk, tm), a_map=lambda i, j, k: (k, i),
               b_block=(None, tk, tn), b_map=lambda i, j, k: (j // njh, k, j % njh),
               dims=TN_DIMS, acc_shape=(tm, tn),
               outs=[((D, 2 * F), F32, (tm, tn), lambda i, j, k: (i, j))])[0]


def _adamw(name, w, m, v, pieces):
    R, C = w.shape
    tr = _tile(R, max(PACK_ROWS, (1 << 18) // C // PACK_ROWS * PACK_ROWS), PACK_ROWS)
    c1 = 1.0 - ADAM_B1 ** ADAM_STEP
    c2 = 1.0 - ADAM_B2 ** ADAM_STEP

    def body(w_ref, m_ref, v_ref, p_ref, g_ref, d_ref, mo_ref, vo_ref):
        g = p_ref[0].astype(F32)
        for dev in range(1, N_DEV):
            g = g + p_ref[dev].astype(F32)
        mn = ADAM_B1 * m_ref[...] + (1.0 - ADAM_B1) * g
        vn = ADAM_B2 * v_ref[...] + (1.0 - ADAM_B2) * (g * g)
        g_ref[...] = g
        mo_ref[...] = mn
        vo_ref[...] = vn
        d_ref[...] = -ADAM_LR * ((mn / c1) / (jnp.sqrt(vn / c2) + ADAM_EPS) + ADAM_WD * w_ref[...])

    row = pl.BlockSpec((tr, C), lambda i: (i, 0))
    return pl.pallas_call(
        body, name=name, grid=(R // tr,),
        in_specs=[row, row, row, pl.BlockSpec((N_DEV, tr, C), lambda i: (0, i, 0))],
        out_specs=[row] * 4,
        out_shape=[jax.ShapeDtypeStruct((R, C), F32)] * 4)(w, m, v, pieces)


def _my_index():
    return 4 * lax.axis_index("x") + 2 * lax.axis_index("y") + lax.axis_index("c")


def _all_gather(name, mine):
    na = len(mine)

    def body(*refs):
        x_refs, out_refs = refs[:na], refs[na:2 * na]
        send_sems, recv_sems, local_sems = refs[2 * na:]
        x, y, c = lax.axis_index("x"), lax.axis_index("y"), lax.axis_index("c")
        me, sibling = (x, y, c), (x, y, 1 - c)
        chips = [(1 - x, y), (x, 1 - y), (1 - x, 1 - y)]

        def slot(a, px, py, pc):
            return out_refs[a].at[4 * px + 2 * py + pc]

        def copy(a, k, block, to, src=None):
            return pltpu.make_async_remote_copy(
                src_ref=slot(a, *block) if src is None else src, dst_ref=slot(a, *block),
                send_sem=send_sems.at[7 * a + k], recv_sem=recv_sems.at[7 * a + k],
                device_id=to, device_id_type=MESH_ID)

        own = [pltpu.make_async_copy(x_refs[a], slot(a, *me), local_sems.at[a]) for a in range(na)]
        for cp in own:
            cp.start()
        first = []
        for a in range(na):
            first.append(copy(a, 0, me, sibling, src=x_refs[a]))
            first += [copy(a, 1 + j, me, (*chip, c), src=x_refs[a]) for j, chip in enumerate(chips)]
        for cp in first:
            cp.start()
        passed = []
        for j, chip in enumerate(chips):
            for a in range(na):
                copy(a, 1 + j, (*chip, c), me).wait_recv()
                fwd = copy(a, 4 + j, (*chip, c), sibling)
                fwd.start()
                passed.append(fwd)
        for a in range(na):
            copy(a, 0, sibling, me).wait_recv()
            for j, chip in enumerate(chips):
                copy(a, 4 + j, (*chip, 1 - c), me).wait_recv()
        for cp in first + passed:
            cp.wait_send()
        for cp in own:
            cp.wait()

    return pl.pallas_call(
        body, name=name,
        in_specs=[pl.BlockSpec(memory_space=pl.ANY)] * na, out_specs=[pl.BlockSpec(memory_space=pl.ANY)] * na,
        out_shape=[jax.ShapeDtypeStruct((N_DEV,) + m.shape, m.dtype) for m in mine],
        scratch_shapes=[pltpu.SemaphoreType.DMA((7 * na,)), pltpu.SemaphoreType.DMA((7 * na,)),
                        pltpu.SemaphoreType.DMA((na,))],
    )(*mine)


def _exchange(name, pieces, bcast):
    n_p, n_b = len(pieces), len(bcast)
    na = n_p + n_b

    def body(*refs):
        src_refs, dst_refs = refs[:na], refs[na:2 * na]
        send_sems, recv_sems, local_sems = refs[2 * na:]
        x, y, c = lax.axis_index("x"), lax.axis_index("y"), lax.axis_index("c")
        me = 4 * x + 2 * y + c

        def src(a, slot):
            return src_refs[a].at[slot] if a < n_p else src_refs[a]

        own = [pltpu.make_async_copy(src(a, me), dst_refs[a].at[me], local_sems.at[a]) for a in range(na)]
        for cp in own:
            cp.start()

        def peer_of(k):
            px = 1 - x if k & 4 else x
            py = 1 - y if k & 2 else y
            pc = 1 - c if k & 1 else c
            return (px, py, pc), 4 * px + 2 * py + pc

        def copy(a, k, src_slot, dst_slot, to):
            return pltpu.make_async_remote_copy(
                src_ref=src(a, src_slot), dst_ref=dst_refs[a].at[dst_slot],
                send_sem=send_sems.at[7 * a + k - 1], recv_sem=recv_sems.at[7 * a + k - 1],
                device_id=to, device_id_type=MESH_ID)

        sent = []
        for k in range(1, N_DEV):
            to, pidx = peer_of(k)
            for a in range(na):
                cp = copy(a, k, pidx, me, to)
                cp.start()
                sent.append(cp)
        for k in range(1, N_DEV):
            to, pidx = peer_of(k)
            for a in range(na):
                copy(a, k, me, pidx, to).wait_recv()
        for cp in sent:
            cp.wait_send()
        for cp in own:
            cp.wait()

    arrays = list(pieces) + list(bcast)
    out_shape = [jax.ShapeDtypeStruct(p.shape, p.dtype) for p in pieces]
    out_shape += [jax.ShapeDtypeStruct((N_DEV,) + b.shape, b.dtype) for b in bcast]
    res = pl.pallas_call(
        body, name=name,
        in_specs=[pl.BlockSpec(memory_space=pl.ANY)] * na, out_specs=[pl.BlockSpec(memory_space=pl.ANY)] * na,
        out_shape=out_shape,
        scratch_shapes=[pltpu.SemaphoreType.DMA((7 * na,)), pltpu.SemaphoreType.DMA((7 * na,)),
                        pltpu.SemaphoreType.DMA((na,))],
    )(*arrays)
    return res[:n_p], res[n_p:]


def _pad_rows(flat, cols, row_mult):
    n = flat.shape[-1]
    unit = cols * row_mult
    padded = -(-n // unit) * unit
    pad = [(0, 0)] * (flat.ndim - 1) + [(0, padded - n)]
    return jnp.pad(flat, pad).reshape(flat.shape[:-1] + (padded // cols, cols))


def _perm_cols(w):
    aw = N_GROUPS * GROUP_W
    parts = [w[..., QKV_W:]]
    parts += [w[..., a * aw + g * GROUP_W:a * aw + (g + 1) * GROUP_W] for g in range(N_GROUPS) for a in range(3)]
    return jnp.concatenate(parts, axis=-1)


def _unperm_cols(wp, rest_w):
    qkv = wp[..., rest_w:]
    parts = [qkv[..., g * GROUP_QKV_W + a * GROUP_W:g * GROUP_QKV_W + (a + 1) * GROUP_W]
             for a in range(3) for g in range(N_GROUPS)]
    return jnp.concatenate(parts + [wp[..., :rest_w]], axis=-1)


def _gather_weights(shards):
    mine = [shards[n] if n in EXACT_F32 else shards[n].astype(BF16) for n, _ in SHARDED]
    got = _all_gather("all_gather_weights", mine)
    full = {}
    for (n, ax), seg in zip(SHARDED, got):
        shp = shards[n].shape
        seg = jnp.moveaxis(seg, 0, ax + 1)
        full[n] = seg.reshape(shp[:ax + 1] + (N_DEV * shp[ax + 1],) + shp[ax + 2:])
    return full


def _scatter_pieces(grads):
    out = []
    for n, ax in SHARDED:
        gr = grads[n]
        shp = gr.shape
        gr = gr.reshape(shp[:ax + 1] + (N_DEV, shp[ax + 1] // N_DEV) + shp[ax + 2:])
        out.append(jnp.moveaxis(gr, ax + 1, 0).astype(BF16))
    return out


def _pack_small(vals):
    flat = jnp.concatenate([vals[n].astype(F32).reshape(-1) for n in REPLICATED])
    return _pad_rows(flat, LANE, 8)


def _layer_fwd(li, x, p_l, W, G, rope, Bl, S, F):
    T, D = x.shape
    rest_w = 3 * D
    sv = {"x0": x}
    h = _rms_fwd(f"rms_mix_{li}", x, G["g_mix"])
    sv["h"] = h
    zr = _mm_nn(f"rest_proj_{li}", h, W["w_in"], n_cols=rest_w, tn=1024, tk=_tile(D, 1024))
    sv["zr"] = zr
    qkv_l, o_l, lse_l, lse_views = [], [], [], []
    for g in range(N_GROUPS):
        qkv = _qkv_proj(f"qkv_proj_{li}_{g}", h, W["w_in"], rope, S, rest_w + g * GROUP_QKV_W)
        o, lse, lse_view = _attn_fwd(f"attn_fwd_{li}_{g}", qkv, g, Bl, S)
        qkv_l.append(qkv)
        o_l.append(o)
        lse_l.append(lse)
        lse_views.append(lse_view)
    sv["qkv"], sv["o"], sv["lse"], sv["lse_view"] = qkv_l, o_l, lse_l, lse_views
    x1, attn, pooled, mixed, y_a, y_b, merged = _mix_out_fwd(
        f"mix_out_fwd_{li}", x, zr, o_l, lse_l, W["w_ya"], W["pool_w"], G["pool_scale"], W["w_yb"], W["w_o"], S)
    sv.update(x1=x1, attn=attn, pooled=pooled, mixed=mixed, y_a=y_a, y_b=y_b, merged=merged)
    h2 = _rms_fwd(f"rms_ffn_{li}", x1, G["g_ffn"])
    u = _up_proj(f"up_proj_{li}", h2, W["w_up"], F)
    a = _ffn_act_fwd(f"ffn_act_fwd_{li}", u, W["conv_w"], G["conv_b"], S)
    x2 = _mm_nn(f"down_proj_{li}", a, W["w_down"], add=x1, tk=_tile(F, 1408))
    sv.update(h2=h2, u=u, a=a, x2=x2)
    x3, e, pg, p_bf = _ple_fwd(f"ple_fwd_{li}", x2, p_l, G["g_ple"], W["w_ple_gate"], W["w_ple"])
    sv.update(e=e, pg=pg, p_bf=p_bf)
    return x3, sv


def _layer_bwd(li, dx3, sv, W, G, rope, Bl, S, F):
    T, D = dx3.shape
    rest_w = 3 * D
    z_w = rest_w + QKV_W
    gr = {}
    dx2, d_e, d_s, h3, dg = _ple_bwd(f"ple_bwd_{li}", dx3, sv["x2"], sv["e"], sv["pg"], G["g_ple"], W["w_ple_gate"])
    gr["g_ple"] = dg[0]
    gr["w_ple"] = _mm_tn(f"w_ple_grad_{li}", sv["p_bf"], d_e)
    gr["w_ple_gate"] = _mm_tn(f"w_ple_gate_grad_{li}", h3, d_s)

    d_a = _mm_nt(f"down_bwd_{li}", dx2, W["w_down"], tn=1408, tk=_tile(D, 1024))
    gr["w_down"] = _mm_tn(f"w_down_grad_{li}", sv["a"], dx2, tm=1408)
    du, d_cw, d_cb = _ffn_act_bwd(f"ffn_act_bwd_{li}", sv["u"], d_a, W["conv_w"], G["conv_b"], S)
    gr["conv_w"] = d_cw.transpose(1, 0, 2).reshape(3, 2 * F)
    gr["conv_b"] = d_cb.reshape(2 * F)
    tk_f = _tile(F, 1408)
    nkh = F // tk_f
    tm_r = _tile(T, 512, 8)
    dx1, dg = _mm_nt_rmsbwd(f"up_bwd_{li}", du, (None, tm_r, tk_f), lambda i, j, k: (k // nkh, i, k % nkh),
                            2 * nkh, tk_f, W["w_up"], sv["x1"], G["g_ffn"], dx2)
    gr["g_ffn"] = dg[0]
    gr["w_up"] = _up_w_grad(f"w_up_grad_{li}", sv["h2"], du)

    (dz_rest, d_ya, d_yb, d_pm, do0, do1, do2, c0, c1, c2, dps) = _mix_out_bwd(
        f"mix_out_bwd_{li}", dx1, sv["zr"], sv["y_a"], sv["y_b"], sv["o"], sv["lse"], sv["pooled"],
        W["w_o"], W["w_ya"], W["w_yb"], W["pool_w"], G["pool_scale"], S)
    gr["pool_scale"] = dps[0]
    gr["w_o"] = _mm_tn(f"w_o_grad_{li}", sv["merged"], dx1)
    gr["w_ya"] = _mm_tn(f"w_ya_grad_{li}", sv["attn"], d_ya)
    gr["w_yb"] = _mm_tn(f"w_yb_grad_{li}", sv["mixed"], d_yb)
    gr["pool_w"] = _pool_w_grad(f"pool_w_grad_{li}", sv["pooled"], d_pm, D // len(POOL_WINDOWS))
    segs = [dz_rest]
    for g, (do, cst) in enumerate(((do0, c0), (do1, c1), (do2, c2))):
        segs.append(_attn_bwd(f"attn_bwd_{li}_{g}", sv["qkv"][g], do, sv["lse_view"][g], cst, rope, g, Bl, S))

    dx0, dg = _in_bwd(f"in_bwd_{li}", segs, W["w_in"], sv["x0"], G["g_mix"], dx1)
    gr["g_mix"] = dg[0]
    w_in_parts = [_mm_tn(f"w_in_grad_{li}_{s}", sv["h"], seg, tn=1536) for s, seg in enumerate(segs)]
    gr["w_in"] = _unperm_cols(jnp.concatenate(w_in_parts, axis=1), rest_w)
    return dx0, gr


def kernel(x, p, g_mix, w_in, w_ya, w_yb, pool_w, pool_scale, w_o, g_ffn, w_up, conv_w, conv_b, w_down, g_ple, w_ple, w_ple_gate, g_final, loss_target, m_g_mix, m_w_in, m_w_ya, m_w_yb, m_pool_w, m_pool_scale, m_w_o, m_g_ffn, m_w_up, m_conv_w, m_conv_b, m_w_down, m_g_ple, m_w_ple, m_w_ple_gate, m_g_final, v_g_mix, v_w_in, v_w_ya, v_w_yb, v_pool_w, v_pool_scale, v_w_o, v_g_ffn, v_w_up, v_conv_w, v_conv_b, v_w_down, v_g_ple, v_w_ple, v_w_ple_gate, v_g_final):
    wts = dict(g_mix=g_mix, w_in=w_in, w_ya=w_ya, w_yb=w_yb, pool_w=pool_w, pool_scale=pool_scale, w_o=w_o,
               g_ffn=g_ffn, w_up=w_up, conv_w=conv_w, conv_b=conv_b, w_down=w_down, g_ple=g_ple, w_ple=w_ple,
               w_ple_gate=w_ple_gate, g_final=g_final)
    mom = dict(g_mix=m_g_mix, w_in=m_w_in, w_ya=m_w_ya, w_yb=m_w_yb, pool_w=m_pool_w, pool_scale=m_pool_scale,
               w_o=m_w_o, g_ffn=m_g_ffn, w_up=m_w_up, conv_w=m_conv_w, conv_b=m_conv_b, w_down=m_w_down,
               g_ple=m_g_ple, w_ple=m_w_ple, w_ple_gate=m_w_ple_gate, g_final=m_g_final)
    var = dict(g_mix=v_g_mix, w_in=v_w_in, w_ya=v_w_ya, w_yb=v_w_yb, pool_w=v_pool_w, pool_scale=v_pool_scale,
               w_o=v_w_o, g_ffn=v_g_ffn, w_up=v_w_up, conv_w=v_conv_w, conv_b=v_conv_b, w_down=v_w_down,
               g_ple=v_g_ple, w_ple=v_w_ple, w_ple_gate=v_w_ple_gate, g_final=v_g_final)
    Bl, S, D = x.shape
    depth = g_mix.shape[0]
    F = w_down.shape[1] * N_DEV
    T = Bl * S
    assert S % (BLOCK * DILATIONS[-1]) == 0 and D % GROUP_W == 0 and F % LANE == 0
    rope = _rope_tables(S)

    full = _gather_weights({n: wts[n] for n, _ in SHARDED})
    full["w_in"] = _perm_cols(full["w_in"])

    xs = x.reshape(T, D)
    saved = []
    for li in range(depth):
        W = {n: full[n][li] for n, _ in SHARDED}
        G = {n: wts[n][li] for n in REPLICATED if n != "g_final"}
        xs, sv = _layer_fwd(li, xs, p[li].reshape(T, -1), W, G, rope, Bl, S, F)
        saved.append((sv, W, G))

    dx, loss_row, dg_final = _loss_bwd("loss_bwd", xs, loss_target.reshape(T, D), g_final)
    layer_grads = [None] * depth
    for li in reversed(range(depth)):
        sv, W, G = saved[li]
        dx, layer_grads[li] = _layer_bwd(li, dx, sv, W, G, rope, Bl, S, F)

    grads = {n: jnp.stack([layer_grads[li][n] for li in range(depth)]) for n in WEIGHT_ORDER if n != "g_final"}
    grads["g_final"] = dg_final[0]
    recv, (small_all,) = _exchange("exchange_grads", _scatter_pieces(grads), [_pack_small(grads)])

    out_g, out_d, out_m, out_v = {}, {}, {}, {}
    for (n, _), pieces in zip(SHARDED, recv):
        shp = wts[n].shape
        two_d = (math.prod(shp[:-1]), shp[-1])
        res = _adamw(f"adamw_{n}", wts[n].reshape(two_d), mom[n].reshape(two_d), var[n].reshape(two_d),
                     pieces.reshape((N_DEV,) + two_d))
        out_g[n], out_d[n], out_m[n], out_v[n] = [r.reshape(shp) for r in res]
    res = _adamw("adamw_replicated", _pack_small(wts), _pack_small(mom), _pack_small(var), small_all)
    off = 0
    for n in REPLICATED:
        shp = wts[n].shape
        size = math.prod(shp)
        for dst, r in zip((out_g, out_d, out_m, out_v), res):
            dst[n] = r.reshape(-1)[off:off + size].reshape(shp)
        off += size

    loss = lax.psum(loss_row[0, 0], MESH_AXES)
    outs = [loss, dx.reshape(Bl, S, D)]
    for dct in (out_g, out_d, out_m, out_v):
        outs += [dct[n] for n in WEIGHT_ORDER]
    return tuple(outs)
```

```python
import math

import jax
import jax.numpy as jnp
from jax import lax
from jax.experimental import pallas as pl
from jax.experimental.pallas import tpu as pltpu

F32 = jnp.float32
BF16 = jnp.bfloat16

N_DEV = 8
HEAD_DIM = 128
HEADS = 4
GROUP_W = HEADS * HEAD_DIM
DILATIONS = (1, 4, 16)
N_GROUPS = len(DILATIONS)
QKV_W = 3 * N_GROUPS * GROUP_W
GROUP_QKV_W = 3 * GROUP_W
BLOCK = 128
ROPE_DIM = HEAD_DIM // 4
ROPE_HALF = ROPE_DIM // 2
ROPE_THETA = 500000.0
NEG_INF = -1e30
POOL_WINDOWS = (2, 4, 8, 16)
POOL_HALO = 16
CONV_HALO = 8
RMS_EPS = 1e-6
ADAM_LR = 0.001
ADAM_B1 = 0.9
ADAM_B2 = 0.999
ADAM_EPS = 1e-08
ADAM_WD = 0.01
ADAM_STEP = 10
LANE = 128
PACK_COLS = 1024
PACK_ROWS = 16
MESH_ID = pl.DeviceIdType.MESH
MESH_AXES = ("x", "y", "c")

NT_DIMS = (((1,), (1,)), ((), ()))
TN_DIMS = (((0,), (0,)), ((), ()))
NN_DIMS = (((1,), (0,)), ((), ()))

SHARDED = (("w_in", 1), ("w_ya", 1), ("w_yb", 0), ("pool_w", 1), ("w_o", 0), ("w_up", 1), ("conv_w", 1),
           ("w_down", 0), ("w_ple", 1), ("w_ple_gate", 0))
EXACT_F32 = ("conv_w",)
REPLICATED = ("g_mix", "pool_scale", "g_ffn", "conv_b", "g_ple", "g_final")
WEIGHT_ORDER = ("g_mix", "w_in", "w_ya", "w_yb", "pool_w", "pool_scale", "w_o", "g_ffn", "w_up", "conv_w", "conv_b",
                "w_down", "g_ple", "w_ple", "w_ple_gate", "g_final")


def _tile(n, pref, mult=LANE):
    if n <= pref:
        return n
    t = (pref // mult) * mult
    while t >= mult:
        if n % t == 0:
            return t
        t -= mult
    return n


def _sigmoid(x):
    return 1.0 / (1.0 + jnp.exp(-x))


def _dot(a, b, dims=NN_DIMS):
    return lax.dot_general(a.astype(BF16), b.astype(BF16), dims, preferred_element_type=F32)


def _rstd(x):
    return lax.rsqrt(jnp.mean(x * x, axis=-1, keepdims=True) + RMS_EPS)


def _rms_bwd(x, g, dh):
    r = _rstd(x)
    u = dh * g
    dx = r * u - x * (r * r * r) * jnp.mean(x * u, axis=-1, keepdims=True)
    return dx, dh * x * r


def _full(a):
    return pl.BlockSpec(a.shape, lambda *_: (0,) * a.ndim)


def _mm(name, a, b, *, grid, a_block, a_map, b_block, b_map, dims, acc_shape, outs, extras=(), epi=None):
    nk = grid[2]
    n_ex = len(extras)
    n_out = len(outs)

    def body(*refs):
        a_ref, b_ref = refs[0], refs[1]
        ex = refs[2:2 + n_ex]
        o = refs[2 + n_ex:2 + n_ex + n_out]
        acc = refs[2 + n_ex + n_out]
        i, j, k = pl.program_id(0), pl.program_id(1), pl.program_id(2)

        @pl.when(k == 0)
        def _():
            acc[...] = jnp.zeros_like(acc)

        acc[...] += _dot(a_ref[...], b_ref[...], dims)

        @pl.when(k == nk - 1)
        def _():
            if epi is None:
                o[0][...] = acc[...].astype(o[0].dtype)
            else:
                epi(acc[...], ex, o, i, j)

    in_specs = [pl.BlockSpec(a_block, a_map), pl.BlockSpec(b_block, b_map)]
    in_specs += [pl.BlockSpec(blk, mp) for (_, blk, mp) in extras]
    out_specs = [pl.BlockSpec(blk, mp) for (_, _, blk, mp) in outs]
    out_shape = [jax.ShapeDtypeStruct(s, d) for (s, d, _, _) in outs]
    return pl.pallas_call(
        body, name=name, grid=grid, in_specs=in_specs, out_specs=out_specs, out_shape=out_shape,
        scratch_shapes=[pltpu.VMEM(acc_shape, F32)],
    )(a, b, *[e[0] for e in extras])


def _mm_nn(name, a, b, *, out_dtype=F32, tm=1024, tn=1024, tk=512, b_col_off=0, n_cols=None, add=None):
    M, K = a.shape
    N = n_cols if n_cols is not None else b.shape[1]
    tm, tn, tk = _tile(M, tm, 8), _tile(N, tn), _tile(K, tk)
    assert b_col_off % tn == 0
    joff = b_col_off // tn
    extras, epi = (), None
    if add is not None:
        extras = ((add, (tm, tn), lambda i, j, k: (i, j)),)

        def epi(acc, ex, o, i, j):
            o[0][...] = (acc + ex[0][...]).astype(o[0].dtype)

    return _mm(name, a, b, grid=(M // tm, N // tn, K // tk),
               a_block=(tm, tk), a_map=lambda i, j, k: (i, k),
               b_block=(tk, tn), b_map=lambda i, j, k: (k, j + joff),
               dims=NN_DIMS, acc_shape=(tm, tn),
               outs=[((M, N), out_dtype, (tm, tn), lambda i, j, k: (i, j))], extras=extras, epi=epi)[0]


def _mm_nt(name, a, b, *, out_dtype=F32, tm=1024, tn=1024, tk=512):
    M, K = a.shape
    N = b.shape[0]
    tm, tn, tk = _tile(M, tm, 8), _tile(N, tn), _tile(K, tk)
    return _mm(name, a, b, grid=(M // tm, N // tn, K // tk),
               a_block=(tm, tk), a_map=lambda i, j, k: (i, k),
               b_block=(tn, tk), b_map=lambda i, j, k: (j, k),
               dims=NT_DIMS, acc_shape=(tm, tn),
               outs=[((M, N), out_dtype, (tm, tn), lambda i, j, k: (i, j))])[0]


def _mm_tn(name, a, b, *, tm=1024, tn=1024, tk=1024):
    K, M = a.shape
    N = b.shape[1]
    tm, tn, tk = _tile(M, tm), _tile(N, tn), _tile(K, tk, 8)
    return _mm(name, a, b, grid=(M // tm, N // tn, K // tk),
               a_block=(tk, tm), a_map=lambda i, j, k: (k, i),
               b_block=(tk, tn), b_map=lambda i, j, k: (k, j),
               dims=TN_DIMS, acc_shape=(tm, tn),
               outs=[((M, N), F32, (tm, tn), lambda i, j, k: (i, j))])[0]


def _mm_nt_rmsbwd(name, a, a_block, a_map, nk, tk, w, x, g, dres):
    T, D = x.shape
    tm = a_block[-2]

    def epi(acc, ex, o, i, j):
        @pl.when(i == 0)
        def _():
            o[1][...] = jnp.zeros_like(o[1])

        dx, dgr = _rms_bwd(ex[0][...], ex[1][...], acc)
        o[0][...] = ex[2][...] + dx
        o[1][...] += jnp.sum(dgr, axis=0, keepdims=True)

    row = lambda i, j, k: (i, 0)
    vec = lambda i, j, k: (0, 0)
    return _mm(name, a, w, grid=(T // tm, 1, nk),
               a_block=a_block, a_map=a_map,
               b_block=(D, tk), b_map=lambda i, j, k: (0, k),
               dims=NT_DIMS, acc_shape=(tm, D),
               outs=[((T, D), F32, (tm, D), row), ((1, D), F32, (1, D), vec)],
               extras=[(x, (tm, D), row), (g.reshape(1, D), (1, D), vec), (dres, (tm, D), row)], epi=epi)


def _in_bwd(name, segs, w_perm, x, g, dres, S):
    T, D = x.shape
    tm = _tile(S, 512, 256)
    nst = S // tm
    tk = GROUP_QKV_W
    steps = [a.shape[1] // tk for a, _ in segs]
    starts = [sum(steps[:s]) for s in range(len(segs))]
    nk = sum(steps)
    ns = len(segs)
    cols = _chunks(D)
    assert all(a.shape[1] % tk == 0 for a, _ in segs) and nk * tk == w_perm.shape[1]

    def body(*refs):
        a_refs = refs[:ns]
        w_ref, x_ref, g_ref, dres_ref, dx_ref, dg_ref, acc = refs[ns:]
        i, k = pl.program_id(0), pl.program_id(1)

        @pl.when(k == 0)
        def _():
            acc[...] = jnp.zeros_like(acc)

        for s in range(ns):
            d = segs[s][1]

            @pl.when((k >= starts[s]) & (k < starts[s] + steps[s]))
            def _():
                prod = _dot(a_refs[s][...].reshape(tm, tk), w_ref[...], NT_DIMS)
                q = tm // d
                for c, cs in enumerate(cols):
                    if d == 1:
                        acc[c] += prod[:, cs]
                    else:
                        for r in range(d):
                            acc[c, pl.ds(r, q, stride=d), :] += prod[r * q:(r + 1) * q, cs]

        @pl.when(k == nk - 1)
        def _():
            @pl.when(i == 0)
            def _():
                dg_ref[...] = jnp.zeros_like(dg_ref)

            dh = jnp.concatenate([acc[c] for c in range(len(cols))], axis=1)
            dx, dgr = _rms_bwd(x_ref[...], g_ref[...], dh)
            dx_ref[...] = dres_ref[...] + dx
            dg_ref[...] += jnp.sum(dgr, axis=0, keepdims=True)

    def seg_spec(s):
        kmap = lambda k: jnp.clip(k - starts[s], 0, steps[s] - 1)
        d = segs[s][1]
        if d == 1:
            return pl.BlockSpec((tm, tk), lambda i, k: (i, kmap(k)))
        return pl.BlockSpec((None, d, tm // d, tk), lambda i, k: (i // nst, 0, i % nst, kmap(k)))

    views = [a if d == 1 else a.reshape(T // S, d, S // d, a.shape[1]) for a, d in segs]
    row = pl.BlockSpec((tm, D), lambda i, k: (i, 0))
    vec = pl.BlockSpec((1, D), lambda i, k: (0, 0))
    return pl.pallas_call(
        body, name=name, grid=(T // tm, nk),
        in_specs=[seg_spec(s) for s in range(ns)] + [pl.BlockSpec((D, tk), lambda i, k: (0, k)), row, vec, row],
        out_specs=[row, vec],
        out_shape=[jax.ShapeDtypeStruct((T, D), F32), jax.ShapeDtypeStruct((1, D), F32)],
        scratch_shapes=[pltpu.VMEM((D // LANE, tm, LANE), F32)],
    )(*views, w_perm, x, g.reshape(1, D), dres)


def _rope_tables(S):
    pos = jnp.arange(S, dtype=F32)
    inv_freq = jnp.exp(jnp.arange(0, ROPE_DIM, 2, dtype=F32) * (-math.log(ROPE_THETA) / ROPE_DIM))
    ang = pos[:, None] * inv_freq[None, :]
    cos, sin = jnp.cos(ang), jnp.sin(ang)
    ones = jnp.ones((S, HEAD_DIM - ROPE_DIM), F32)
    zeros_h = jnp.zeros((S, ROPE_HALF), F32)
    zeros_r = jnp.zeros((S, HEAD_DIM - ROPE_DIM), F32)
    c = jnp.concatenate([cos, cos, ones], axis=1)
    sa = jnp.concatenate([-sin, zeros_h, zeros_r], axis=1)
    sb = jnp.concatenate([zeros_h, sin, zeros_r], axis=1)
    return c, sa, sb


def _rope(t, c, sa, sb):
    return t * c + pltpu.roll(t, HEAD_DIM - ROPE_HALF, 1) * sa + pltpu.roll(t, ROPE_HALF, 1) * sb


def _rms_fwd(name, x, g):
    T, D = x.shape
    tm = _tile(T, 512, 8)

    def body(x_ref, g_ref, h_ref):
        xv = x_ref[...]
        h_ref[...] = (xv * _rstd(xv) * g_ref[...]).astype(BF16)

    return pl.pallas_call(
        body, name=name, grid=(T // tm,),
        in_specs=[pl.BlockSpec((tm, D), lambda i: (i, 0)), pl.BlockSpec((1, D), lambda i: (0, 0))],
        out_specs=pl.BlockSpec((tm, D), lambda i: (i, 0)),
        out_shape=jax.ShapeDtypeStruct((T, D), BF16))(x, g.reshape(1, D))


def _chunks(width):
    return [slice(c * LANE, (c + 1) * LANE) for c in range(width // LANE)]


def _store_residue_major(val, sc, out_ref, d, dtype):
    rows = val.shape[0]
    for c, cs in enumerate(_chunks(val.shape[1])):
        sc[c] = val[:, cs]
    for r in range(d):
        for c, cs in enumerate(_chunks(val.shape[1])):
            out_ref[r, :, cs] = sc[c, pl.ds(r, rows // d, stride=d), :].astype(dtype)


def _load_token_order(blk_ref, sc, d):
    _, q, width = blk_ref.shape
    for r in range(d):
        for c, cs in enumerate(_chunks(width)):
            sc[c, pl.ds(r, q, stride=d), :] = blk_ref[r, :, cs]
    return jnp.concatenate([sc[c] for c in range(width // LANE)], axis=1)


def _residue_major_spec(d, q, width, nst):
    return pl.BlockSpec((None, d, q, width), lambda i, *_: (i // nst, 0, i % nst, 0))


def _rms_mix_fwd(name, x, g, S):
    T, D = x.shape
    Bl = T // S
    tm = _tile(S, 512, 256)
    nst = S // tm
    dils = [d for d in DILATIONS if d > 1]

    def body(x_ref, g_ref, h_ref, *rest):
        rm_refs, sc = rest[:len(dils)], rest[len(dils)]
        xv = x_ref[...]
        hv = xv * _rstd(xv) * g_ref[...]
        h_ref[...] = hv.astype(BF16)
        for d, o_ref in zip(dils, rm_refs):
            _store_residue_major(hv, sc, o_ref, d, BF16)

    row = pl.BlockSpec((tm, D), lambda i: (i, 0))
    return pl.pallas_call(
        body, name=name, grid=(T // tm,),
        in_specs=[row, pl.BlockSpec((1, D), lambda i: (0, 0))],
        out_specs=[row] + [_residue_major_spec(d, tm // d, D, nst) for d in dils],
        out_shape=[jax.ShapeDtypeStruct((T, D), BF16)]
        + [jax.ShapeDtypeStruct((Bl, d, S // d, D), BF16) for d in dils],
        scratch_shapes=[pltpu.VMEM((D // LANE, tm, LANE), F32)])(x, g.reshape(1, D))


def _qkv_proj(name, h, w_perm, rope, S, col_off):
    T, D = h.shape
    tm = _tile(S, 1024, 8)
    tn = GROUP_W
    tk = _tile(D, 1024)
    c_t, sa_t, sb_t = rope
    n_seq_tiles = S // tm
    joff = col_off // tn
    tmap = lambda i, j, k: (i % n_seq_tiles, 0)

    def epi(acc, ex, o, i, j):
        is_rot = j < 2

        @pl.when(is_rot)
        def _():
            c, sa, sb = ex[0][...], ex[1][...], ex[2][...]
            for hh in range(HEADS):
                sl = slice(hh * HEAD_DIM, (hh + 1) * HEAD_DIM)
                o[0][:, sl] = _rope(acc[:, sl], c, sa, sb).astype(BF16)

        @pl.when(jnp.logical_not(is_rot))
        def _():
            o[0][...] = acc.astype(BF16)

    return _mm(name, h, w_perm, grid=(T // tm, 3, D // tk),
               a_block=(tm, tk), a_map=lambda i, j, k: (i, k),
               b_block=(tk, tn), b_map=lambda i, j, k: (k, j + joff),
               dims=NN_DIMS, acc_shape=(tm, tn),
               outs=[((T, GROUP_QKV_W), BF16, (tm, tn), lambda i, j, k: (i, j))],
               extras=[(c_t, (tm, HEAD_DIM), tmap), (sa_t, (tm, HEAD_DIM), tmap), (sb_t, (tm, HEAD_DIM), tmap)],
               epi=epi)[0]


def _attn_mask(n):
    qi = lax.broadcasted_iota(jnp.int32, (BLOCK, 2 * BLOCK), 0)
    ki = lax.broadcasted_iota(jnp.int32, (BLOCK, 2 * BLOCK), 1)
    diff = BLOCK + qi - ki
    return (diff >= 0) & (diff <= BLOCK) & ((n > 0) | (ki >= BLOCK))


def _attn_fwd(name, qkv, g, Bl, S):
    d = DILATIONS[g]
    L = S // d
    nb = L // BLOCK
    qv = qkv.reshape(Bl * d, L, GROUP_QKV_W)
    scale = HEAD_DIM ** -0.5

    def body(q_ref, kc_ref, vc_ref, kp_ref, vp_ref, o_ref, l_ref):
        valid = _attn_mask(pl.program_id(1))
        for hh in range(HEADS):
            sl = slice(hh * HEAD_DIM, (hh + 1) * HEAD_DIM)
            kk = jnp.concatenate([kp_ref[:, sl], kc_ref[:, sl]], axis=0)
            vv = jnp.concatenate([vp_ref[:, sl], vc_ref[:, sl]], axis=0)
            s = jnp.where(valid, _dot(q_ref[:, sl], kk, NT_DIMS) * scale, NEG_INF)
            m = jnp.max(s, axis=-1, keepdims=True)
            p = jnp.exp(s - m)
            l = jnp.sum(p, axis=-1, keepdims=True)
            o_ref[:, sl] = _dot(p, vv) / l
            l_ref[:, sl] = jnp.broadcast_to(m + jnp.log(l), (BLOCK, HEAD_DIM))

    blk = (None, BLOCK, GROUP_W)
    cur = lambda off: (lambda s, n: (s, n, off))
    prev = lambda off: (lambda s, n: (s, jnp.maximum(n - 1, 0), off))
    omap = lambda s, n: (s, n, 0)
    return pl.pallas_call(
        body, name=name, grid=(Bl * d, nb),
        in_specs=[pl.BlockSpec(blk, cur(0)), pl.BlockSpec(blk, cur(1)), pl.BlockSpec(blk, cur(2)),
                  pl.BlockSpec(blk, prev(1)), pl.BlockSpec(blk, prev(2))],
        out_specs=[pl.BlockSpec(blk, omap), pl.BlockSpec(blk, omap)],
        out_shape=[jax.ShapeDtypeStruct((Bl * d, L, GROUP_W), F32)] * 2)(qv, qv, qv, qv, qv)


def _merge_weights(l0, l1, l2):
    mx = jnp.maximum(jnp.maximum(l0, l1), l2)
    e0, e1, e2 = jnp.exp(l0 - mx), jnp.exp(l1 - mx), jnp.exp(l2 - mx)
    inv = 1.0 / (e0 + e1 + e2)
    return e0 * inv, e1 * inv, e2 * inv


def _group_specs(tm, nst, S):
    specs = []
    for d in DILATIONS:
        specs.append(pl.BlockSpec((tm, GROUP_W), lambda i: (i, 0)) if d == 1
                     else _residue_major_spec(d, tm // d, GROUP_W, nst))

    def views(arrs):
        out = []
        for d, a in zip(DILATIONS, arrs):
            out.append(a.reshape(-1, GROUP_W) if d == 1 else a.reshape(-1, d, S // d, GROUP_W))
        return out

    return specs, views


def _group_scratch(tm, per_group):
    n = per_group * sum(1 for d in DILATIONS if d > 1)
    return [pltpu.VMEM((GROUP_W // LANE, tm, LANE), F32) for _ in range(n)]


def _group_values(o_refs, l_refs, scs):
    scs = list(scs)
    ov, lv = [], []
    for d, o_ref, l_ref in zip(DILATIONS, o_refs, l_refs):
        if d == 1:
            ov.append(o_ref[...])
            lv.append(l_ref[...])
        else:
            ov.append(_load_token_order(o_ref, scs.pop(), d))
            lv.append(_load_token_order(l_ref, scs.pop(), d))
    return ov, lv


def _pool_inv_count(tseq, w):
    return 1.0 / jnp.minimum(tseq + 1, w).astype(F32)


def _mix_out_fwd(name, x, zr, o_l, lse_l, w_ya, pool_w, pool_scale, w_yb, w_o, S):
    T, D = x.shape
    gw = D // len(POOL_WINDOWS)
    tm = _tile(S, 256, POOL_HALO)
    nst = S // tm
    hpt = tm // POOL_HALO

    def body(x_ref, u_ref, uh_ref, ga_ref, gb_ref, o0, o1, o2, l0, l1, l2, wya_ref, pw_ref, ps_ref, wyb_ref, wo_ref,
             x1_ref, attn_ref, pooled_ref, mixed_ref, ya_ref, yb_ref, merged_ref, *scs):
        it = pl.program_id(0) % nst
        ov, lv = _group_values((o0, o1, o2), (l0, l1, l2), scs)
        w0, w1, w2 = _merge_weights(*lv)
        attn = w0 * ov[0] + w1 * ov[1] + w2 * ov[2]
        attn_ref[...] = attn.astype(BF16)
        y_a = _dot(attn, wya_ref[...])

        u = u_ref[...]
        halo = uh_ref[...] * jnp.where(it == 0, 0.0, 1.0)
        ext = jnp.concatenate([halo, u], axis=0)
        tseq = it * tm + lax.broadcasted_iota(jnp.int32, (tm, 1), 0)
        pm_parts = []
        for gi, w in enumerate(POOL_WINDOWS):
            cs = slice(gi * gw, (gi + 1) * gw)
            s = ext[:, cs]
            step = 1
            while step < w:
                s = s + pltpu.roll(s, step, 0)
                step *= 2
            pooled_g = (s[POOL_HALO:, :] * _pool_inv_count(tseq, w) - u[:, cs]).astype(BF16)
            pooled_ref[:, cs] = pooled_g
            pm_parts.append(_dot(pooled_g, pw_ref[gi]))
        mixed = (jnp.concatenate(pm_parts, axis=1) * ps_ref[...]).astype(BF16)
        mixed_ref[...] = mixed
        y_b = _dot(mixed, wyb_ref[...])
        merged = (_sigmoid(ga_ref[...]) * y_a + _sigmoid(gb_ref[...]) * y_b).astype(BF16)
        ya_ref[...] = y_a
        yb_ref[...] = y_b
        merged_ref[...] = merged
        x1_ref[...] = x_ref[...] + _dot(merged, wo_ref[...])

    row = lambda c: pl.BlockSpec((tm, D), lambda i: (i, c))
    row512 = pl.BlockSpec((tm, GROUP_W), lambda i: (i, 0))
    ps = pool_scale.reshape(1, D)
    halo_spec = pl.BlockSpec((POOL_HALO, D), lambda i: (jnp.maximum(i * hpt - 1, 0), 0))
    grp_specs, grp_views = _group_specs(tm, nst, S)
    return pl.pallas_call(
        body, name=name, grid=(T // tm,),
        in_specs=[row(0), row(0), halo_spec, row(1), row(2)] + grp_specs * 2
        + [_full(w_ya), _full(pool_w), _full(ps), _full(w_yb), _full(w_o)],
        out_specs=[row(0), row512, row(0), row(0), row(0), row(0), row(0)],
        out_shape=[jax.ShapeDtypeStruct((T, D), F32), jax.ShapeDtypeStruct((T, GROUP_W), BF16),
                   jax.ShapeDtypeStruct((T, D), BF16), jax.ShapeDtypeStruct((T, D), BF16),
                   jax.ShapeDtypeStruct((T, D), F32), jax.ShapeDtypeStruct((T, D), F32),
                   jax.ShapeDtypeStruct((T, D), BF16)],
        scratch_shapes=_group_scratch(tm, 2),
    )(x, zr, zr, zr, zr, *grp_views(o_l), *grp_views(lse_l), w_ya, pool_w, ps, w_yb, w_o)


def _up_proj(name, h2, w_up, F):
    T, D = h2.shape
    tm, tn, tk = _tile(T, 1024, 8), _tile(F, 1408), _tile(D, 512)
    njh = F // tn
    return _mm(name, h2, w_up, grid=(T // tm, 2 * njh, D // tk),
               a_block=(tm, tk), a_map=lambda i, j, k: (i, k),
               b_block=(tk, tn), b_map=lambda i, j, k: (k, j),
               dims=NN_DIMS, acc_shape=(tm, tn),
               outs=[((2, T, F), F32, (None, tm, tn), lambda i, j, k: (j // njh, i, j % njh))])[0]


def _conv_y(ext, w_ref, b_ref):
    return (b_ref[...] + w_ref[2:3, :] * ext + w_ref[1:2, :] * pltpu.roll(ext, 1, 0)
            + w_ref[0:1, :] * pltpu.roll(ext, 2, 0))


def _conv_params(conv_w, conv_b, F):
    cw = conv_w.reshape(3, 2, F).transpose(1, 0, 2)
    return cw, conv_b.reshape(2, 1, F)


def _ffn_act_fwd(name, u, conv_w, conv_b, S):
    _, T, F = u.shape
    tm = _tile(S, 512, CONV_HALO)
    tf = _tile(F, 1408)
    nst = S // tm
    hpt = tm // CONV_HALO
    cw, cb = _conv_params(conv_w, conv_b, F)

    def body(ug_ref, uv_ref, hg_ref, hv_ref, wg_ref, wv_ref, bg_ref, bv_ref, a_ref):
        keep = jnp.where(pl.program_id(0) % nst == 0, 0.0, 1.0)
        yg = _conv_y(jnp.concatenate([hg_ref[...] * keep, ug_ref[...]], axis=0), wg_ref, bg_ref)[CONV_HALO:, :]
        yv = _conv_y(jnp.concatenate([hv_ref[...] * keep, uv_ref[...]], axis=0), wv_ref, bv_ref)[CONV_HALO:, :]
        a_ref[...] = (yg * _sigmoid(yg) * yv).astype(BF16)

    main = lambda h: pl.BlockSpec((None, tm, tf), lambda i, j: (h, i, j))
    halo = lambda h: pl.BlockSpec((None, CONV_HALO, tf), lambda i, j: (h, jnp.maximum(i * hpt - 1, 0), j))
    wsp = lambda h: pl.BlockSpec((None, 3, tf), lambda i, j: (h, 0, j))
    bsp = lambda h: pl.BlockSpec((None, 1, tf), lambda i, j: (h, 0, j))
    return pl.pallas_call(
        body, name=name, grid=(T // tm, F // tf),
        in_specs=[main(0), main(1), halo(0), halo(1), wsp(0), wsp(1), bsp(0), bsp(1)],
        out_specs=pl.BlockSpec((tm, tf), lambda i, j: (i, j)),
        out_shape=jax.ShapeDtypeStruct((T, F), BF16))(u, u, u, u, cw, cw, cb, cb)


def _ple_fwd(name, x2, p, g_ple, w_gate, w_ple):
    T, D = x2.shape
    P = p.shape[1]
    tm = _tile(T, 512, 8)

    def body(x_ref, p_ref, g_ref, wg_ref, wp_ref, x3_ref, e_ref, pg_ref, pbf_ref):
        xv = x_ref[...]
        h3 = xv * _rstd(xv) * g_ref[...]
        pg = _sigmoid(_dot(h3, wg_ref[...]))
        pb = p_ref[...].astype(BF16)
        e = _dot(pb, wp_ref[...])
        x3_ref[...] = xv + e * pg
        e_ref[...] = e
        pg_ref[...] = pg
        pbf_ref[...] = pb

    row = pl.BlockSpec((tm, D), lambda i: (i, 0))
    prow = pl.BlockSpec((tm, P), lambda i: (i, 0))
    g2 = g_ple.reshape(1, D)
    return pl.pallas_call(
        body, name=name, grid=(T // tm,),
        in_specs=[row, prow, _full(g2), _full(w_gate), _full(w_ple)],
        out_specs=[row, row, row, prow],
        out_shape=[jax.ShapeDtypeStruct((T, D), F32)] * 3 + [jax.ShapeDtypeStruct((T, P), BF16)],
    )(x2, p, g2, w_gate, w_ple)


def _loss_bwd(name, xf, target, g_final):
    T, D = xf.shape
    tm = _tile(T, 512, 8)
    nt = T // tm

    def body(x_ref, t_ref, g_ref, dx_ref, loss_ref, dg_ref, lacc):
        i = pl.program_id(0)

        @pl.when(i == 0)
        def _():
            lacc[...] = jnp.zeros_like(lacc)
            dg_ref[...] = jnp.zeros_like(dg_ref)
            loss_ref[...] = jnp.zeros_like(loss_ref)

        xv = x_ref[...]
        g = g_ref[...]
        diff = xv * _rstd(xv) * g - t_ref[...]
        lacc[...] += jnp.sum(diff * diff, axis=0, keepdims=True)
        dx, dgr = _rms_bwd(xv, g, diff * (1.0 / D))
        dx_ref[...] = dx
        dg_ref[...] += jnp.sum(dgr, axis=0, keepdims=True)

        @pl.when(i == nt - 1)
        def _():
            tot = jnp.sum(lacc[...], axis=-1, keepdims=True) * (0.5 / D)
            loss_ref[...] = jnp.broadcast_to(tot, (1, LANE))

    row = pl.BlockSpec((tm, D), lambda i: (i, 0))
    vec = pl.BlockSpec((1, D), lambda i: (0, 0))
    return pl.pallas_call(
        body, name=name, grid=(nt,),
        in_specs=[row, row, vec],
        out_specs=[row, pl.BlockSpec((1, LANE), lambda i: (0, 0)), vec],
        out_shape=[jax.ShapeDtypeStruct((T, D), F32), jax.ShapeDtypeStruct((1, LANE), F32),
                   jax.ShapeDtypeStruct((1, D), F32)],
        scratch_shapes=[pltpu.VMEM((1, D), F32)])(xf, target, g_final.reshape(1, D))


def _ple_bwd(name, dx3, x2, e, pg, g_ple, w_gate):
    T, D = x2.shape
    tm = _tile(T, 512, 8)

    def body(dx3_ref, x_ref, e_ref, pg_ref, g_ref, wg_ref, dx2_ref, de_ref, ds_ref, h3_ref, dg_ref):
        @pl.when(pl.program_id(0) == 0)
        def _():
            dg_ref[...] = jnp.zeros_like(dg_ref)

        dx3v, xv, pgv, g = dx3_ref[...], x_ref[...], pg_ref[...], g_ref[...]
        de_ref[...] = (dx3v * pgv).astype(BF16)
        ds = (dx3v * e_ref[...] * pgv * (1.0 - pgv)).astype(BF16)
        ds_ref[...] = ds
        dh3 = _dot(ds, wg_ref[...], NT_DIMS)
        h3_ref[...] = (xv * _rstd(xv) * g).astype(BF16)
        dx, dgr = _rms_bwd(xv, g, dh3)
        dx2_ref[...] = dx3v + dx
        dg_ref[...] += jnp.sum(dgr, axis=0, keepdims=True)

    row = pl.BlockSpec((tm, D), lambda i: (i, 0))
    vec = pl.BlockSpec((1, D), lambda i: (0, 0))
    return pl.pallas_call(
        body, name=name, grid=(T // tm,),
        in_specs=[row, row, row, row, vec, _full(w_gate)],
        out_specs=[row, row, row, row, vec],
        out_shape=[jax.ShapeDtypeStruct((T, D), F32)] + [jax.ShapeDtypeStruct((T, D), BF16)] * 3
        + [jax.ShapeDtypeStruct((1, D), F32)])(dx3, x2, e, pg, g_ple.reshape(1, D), w_gate)


def _ffn_act_bwd(name, u, d_a, conv_w, conv_b, S):
    _, T, F = u.shape
    H = CONV_HALO
    tm = _tile(S, 512, H)
    tf = _tile(F, 1408)
    nst = S // tm
    hpt = tm // H
    last_halo = T // H - 1
    n_ext = tm + H
    cw, cb = _conv_params(conv_w, conv_b, F)

    def body(ug_ref, uv_ref, pg_ref, pv_ref, ng_ref, nv_ref, da_ref, dan_ref, wg_ref, wv_ref, bg_ref, bv_ref,
             du_ref, dw_ref, db_ref):
        i = pl.program_id(1)
        it = i % nst

        @pl.when(i == 0)
        def _():
            dw_ref[...] = jnp.zeros_like(dw_ref)
            db_ref[...] = jnp.zeros_like(db_ref)

        keep_prev = jnp.where(it == 0, 0.0, 1.0)
        keep_next = jnp.where(it == nst - 1, 0.0, 1.0)
        ext_g = jnp.concatenate([pg_ref[...] * keep_prev, ug_ref[...], ng_ref[...]], axis=0)
        ext_v = jnp.concatenate([pv_ref[...] * keep_prev, uv_ref[...], nv_ref[...]], axis=0)
        yg = _conv_y(ext_g, wg_ref, bg_ref)[H:, :]
        yv = _conv_y(ext_v, wv_ref, bv_ref)[H:, :]
        rows = lax.broadcasted_iota(jnp.int32, (n_ext, 1), 0)
        live = jnp.where(rows < tm, 1.0, keep_next)
        da = jnp.concatenate([da_ref[...], dan_ref[...]], axis=0) * live
        sg = _sigmoid(yg)
        dyv = da * (yg * sg)
        dyg = da * yv * (sg * (1.0 + yg * (1.0 - sg)))
        for half, (dy, ext, w_ref) in enumerate(((dyg, ext_g, wg_ref), (dyv, ext_v, wv_ref))):
            du = (w_ref[2:3, :] * dy + w_ref[1:2, :] * pltpu.roll(dy, n_ext - 1, 0)
                  + w_ref[0:1, :] * pltpu.roll(dy, n_ext - 2, 0))
            du_ref[half] = du[:tm, :].astype(BF16)
            dym = dy[:tm, :]
            db_ref[half] += jnp.sum(dym, axis=0, keepdims=True)
            dw_ref[half, 2:3, :] += jnp.sum(dym * ext[H:H + tm, :], axis=0, keepdims=True)
            dw_ref[half, 1:2, :] += jnp.sum(dym * pltpu.roll(ext, 1, 0)[H:H + tm, :], axis=0, keepdims=True)
            dw_ref[half, 0:1, :] += jnp.sum(dym * pltpu.roll(ext, 2, 0)[H:H + tm, :], axis=0, keepdims=True)

    main = lambda h: pl.BlockSpec((None, tm, tf), lambda j, i: (h, i, j))
    prev = lambda h: pl.BlockSpec((None, H, tf), lambda j, i: (h, jnp.maximum(i * hpt - 1, 0), j))
    nxt = lambda h: pl.BlockSpec((None, H, tf), lambda j, i: (h, jnp.minimum((i + 1) * hpt, last_halo), j))
    wsp = lambda h: pl.BlockSpec((None, 3, tf), lambda j, i: (h, 0, j))
    bsp = lambda h: pl.BlockSpec((None, 1, tf), lambda j, i: (h, 0, j))
    return pl.pallas_call(
        body, name=name, grid=(F // tf, T // tm),
        in_specs=[main(0), main(1), prev(0), prev(1), nxt(0), nxt(1),
                  pl.BlockSpec((tm, tf), lambda j, i: (i, j)),
                  pl.BlockSpec((H, tf), lambda j, i: (jnp.minimum((i + 1) * hpt, last_halo), j)),
                  wsp(0), wsp(1), bsp(0), bsp(1)],
        out_specs=[pl.BlockSpec((2, tm, tf), lambda j, i: (0, i, j)),
                   pl.BlockSpec((2, 3, tf), lambda j, i: (0, 0, j)),
                   pl.BlockSpec((2, 1, tf), lambda j, i: (0, 0, j))],
        out_shape=[jax.ShapeDtypeStruct((2, T, F), BF16), jax.ShapeDtypeStruct((2, 3, F), F32),
                   jax.ShapeDtypeStruct((2, 1, F), F32)],
    )(u, u, u, u, u, u, d_a, d_a, cw, cw, cb, cb)


def _mix_out_bwd(name, dx1, zr, y_a, y_b, o_l, lse_l, pooled, w_o, w_ya, w_yb, pool_w, pool_scale, S):
    T, D = dx1.shape
    gw = D // len(POOL_WINDOWS)
    H = POOL_HALO
    tm = _tile(S, 256, H)
    nst = S // tm
    hpt = tm // H
    last_halo = T // H - 1
    n_ext = tm + H

    def body(dx_ref, dxn_ref, ga_ref, gb_ref, gbn_ref, ya_ref, yb_ref, o0, o1, o2, l0, l1, l2, pooled_ref,
             wo_ref, wya_ref, wyb_ref, pw_ref, ps_ref,
             dz_ref, dya_ref, dyb_ref, dpm_ref, do0, do1, do2, c0, c1, c2, dps_ref, *scs):
        i = pl.program_id(0)
        it = i % nst

        @pl.when(i == 0)
        def _():
            dps_ref[...] = jnp.zeros_like(dps_ref)

        keep_next = jnp.where(it == nst - 1, 0.0, 1.0)
        dm_e = _dot(jnp.concatenate([dx_ref[...], dxn_ref[...]], axis=0), wo_ref[...], NT_DIMS)
        sgb_e = _sigmoid(jnp.concatenate([gb_ref[...], gbn_ref[...]], axis=0))
        dyb_e = dm_e * sgb_e
        dm = dm_e[:tm, :]
        sga = _sigmoid(ga_ref[...])
        sgb = sgb_e[:tm, :]
        d_ga = dm * ya_ref[...] * (sga * (1.0 - sga))
        d_gb = dm * yb_ref[...] * (sgb * (1.0 - sgb))
        dya = (dm * sga).astype(BF16)
        dya_ref[...] = dya
        dyb_ref[...] = dyb_e[:tm, :].astype(BF16)
        dmixed_e = _dot(dyb_e, wyb_ref[...], NT_DIMS)

        rows = lax.broadcasted_iota(jnp.int32, (n_ext, 1), 0)
        tseq = it * tm + rows
        live = jnp.where(rows < tm, 1.0, keep_next)
        ps = ps_ref[...]
        du_parts = []
        for gi, w in enumerate(POOL_WINDOWS):
            cs = slice(gi * gw, (gi + 1) * gw)
            pm_g = _dot(pooled_ref[:, cs], pw_ref[gi])
            dps_ref[:, cs] += jnp.sum(dmixed_e[:tm, cs] * pm_g, axis=0, keepdims=True)
            dpm_e = (dmixed_e[:, cs] * ps[:, cs]).astype(BF16)
            dpm_ref[:, cs] = dpm_e[:tm, :]
            dpooled_e = _dot(dpm_e, pw_ref[gi], NT_DIMS)
            s = dpooled_e * (_pool_inv_count(tseq, w) * live)
            step = 1
            while step < w:
                s = s + pltpu.roll(s, n_ext - step, 0)
                step *= 2
            du_parts.append(s[:tm, :] - dpooled_e[:tm, :])
        dz_ref[...] = jnp.concatenate(du_parts + [d_ga, d_gb], axis=1).astype(BF16)

        d_attn = _dot(dya, wya_ref[...], NT_DIMS)
        ov, lv = _group_values((o0, o1, o2), (l0, l1, l2), scs[:n_in_sc])
        ws = _merge_weights(*lv)
        prod = d_attn * (ws[0] * ov[0] + ws[1] * ov[1] + ws[2] * ov[2])
        rs = jnp.concatenate(
            [jnp.broadcast_to(jnp.sum(prod[:, hh * HEAD_DIM:(hh + 1) * HEAD_DIM], axis=-1, keepdims=True),
                              (tm, HEAD_DIM)) for hh in range(HEADS)], axis=1)
        out_scs = list(scs[n_in_sc:])
        for d, wg, do_ref, c_ref in zip(DILATIONS, ws, (do0, do1, do2), (c0, c1, c2)):
            if d == 1:
                do_ref[...] = (wg * d_attn).astype(BF16)
                c_ref[...] = -wg * rs
            else:
                _store_residue_major(wg * d_attn, out_scs.pop(), do_ref, d, BF16)
                _store_residue_major(-wg * rs, out_scs.pop(), c_ref, d, F32)

    row = lambda c: pl.BlockSpec((tm, D), lambda i: (i, c))
    nxt = lambda c: pl.BlockSpec((H, D), lambda i: (jnp.minimum((i + 1) * hpt, last_halo), c))
    ps2 = pool_scale.reshape(1, D)
    bf = lambda w: jax.ShapeDtypeStruct((T, w), BF16)
    grp_specs, grp_views = _group_specs(tm, nst, S)
    grp_shape = lambda dt: [jax.ShapeDtypeStruct((T, GROUP_W) if d == 1 else (T // S, d, S // d, GROUP_W), dt)
                            for d in DILATIONS]
    n_in_sc = len(_group_scratch(tm, 2))
    return pl.pallas_call(
        body, name=name, grid=(T // tm,),
        in_specs=[row(0), nxt(0), row(1), row(2), nxt(2), row(0), row(0)] + grp_specs * 2 + [row(0)]
        + [_full(w_o), _full(w_ya), _full(w_yb), _full(pool_w), _full(ps2)],
        out_specs=[pl.BlockSpec((tm, 3 * D), lambda i: (i, 0)), row(0), row(0), row(0)] + grp_specs * 2
        + [pl.BlockSpec((1, D), lambda i: (0, 0))],
        out_shape=[bf(3 * D), bf(D), bf(D), bf(D)] + grp_shape(BF16) + grp_shape(F32)
        + [jax.ShapeDtypeStruct((1, D), F32)],
        scratch_shapes=_group_scratch(tm, 4),
    )(dx1, dx1, zr, zr, zr, y_a, y_b, *grp_views(o_l), *grp_views(lse_l), pooled, w_o, w_ya, w_yb, pool_w, ps2)


def _attn_bwd(name, qkv, d_o, lse, cst, rope, g, Bl, S):
    d = DILATIONS[g]
    L = S // d
    nb = L // BLOCK
    qv = qkv.reshape(Bl * d, L, GROUP_QKV_W)
    dov = d_o.reshape(Bl * d, L, GROUP_W)
    lv = lse.reshape(Bl * d, L, GROUP_W)
    cv = cst.reshape(Bl * d, L, GROUP_W)
    tabs = [t.reshape(d, L, HEAD_DIM) for t in rope]
    scale = HEAD_DIM ** -0.5

    def body(q_ref, qn_ref, kp_ref, kc_ref, vp_ref, vc_ref, do_ref, don_ref, l_ref, ln_ref, c_ref, cn_ref,
             cos_ref, sa_ref, sb_ref, out_ref):
        n = pl.program_id(1)
        valid = _attn_mask(n)
        qi = lax.broadcasted_iota(jnp.int32, (BLOCK, BLOCK), 0)
        ki = lax.broadcasted_iota(jnp.int32, (BLOCK, BLOCK), 1)
        valid_n = (ki >= qi) & (n + 1 < nb)
        cos, sa, sb = cos_ref[...], -sa_ref[...], -sb_ref[...]
        for hh in range(HEADS):
            sl = slice(hh * HEAD_DIM, (hh + 1) * HEAD_DIM)
            q, qn, kc, vc, do, don = q_ref[:, sl], qn_ref[:, sl], kc_ref[:, sl], vc_ref[:, sl], do_ref[:, sl], don_ref[:, sl]
            kk = jnp.concatenate([kp_ref[:, sl], kc], axis=0)
            vv = jnp.concatenate([vp_ref[:, sl], vc], axis=0)
            col = slice(hh * HEAD_DIM, hh * HEAD_DIM + 1)
            s = jnp.where(valid, _dot(q, kk, NT_DIMS) * scale, NEG_INF)
            p = jnp.exp(s - l_ref[:, col])
            ds = p * (_dot(do, vv, NT_DIMS) + c_ref[:, col])
            dq = _dot(ds, kk) * scale
            s2 = jnp.where(valid_n, _dot(qn, kc, NT_DIMS) * scale, NEG_INF)
            p2 = jnp.exp(s2 - ln_ref[:, col])
            ds2 = p2 * (_dot(don, vc, NT_DIMS) + cn_ref[:, col])
            dk = (_dot(ds[:, BLOCK:], q, TN_DIMS) + _dot(ds2, qn, TN_DIMS)) * scale
            dv = _dot(p[:, BLOCK:], do, TN_DIMS) + _dot(p2, don, TN_DIMS)
            out_ref[:, sl] = _rope(dq, cos, sa, sb).astype(BF16)
            out_ref[:, GROUP_W + hh * HEAD_DIM:GROUP_W + (hh + 1) * HEAD_DIM] = _rope(dk, cos, sa, sb).astype(BF16)
            out_ref[:, 2 * GROUP_W + hh * HEAD_DIM:2 * GROUP_W + (hh + 1) * HEAD_DIM] = dv.astype(BF16)

    blk = (None, BLOCK, GROUP_W)
    at = lambda f, off: pl.BlockSpec(blk, lambda s, n: (s, f(n), off))
    cur = lambda n: n
    prv = lambda n: jnp.maximum(n - 1, 0)
    nxt = lambda n: jnp.minimum(n + 1, nb - 1)
    tok = lambda f: pl.BlockSpec(blk, lambda s, n: (s, f(n), 0))
    tab = pl.BlockSpec((None, BLOCK, HEAD_DIM), lambda s, n: (s % d, n, 0))
    out = pl.pallas_call(
        body, name=name, grid=(Bl * d, nb),
        in_specs=[at(cur, 0), at(nxt, 0), at(prv, 1), at(cur, 1), at(prv, 2), at(cur, 2),
                  tok(cur), tok(nxt), tok(cur), tok(nxt), tok(cur), tok(nxt), tab, tab, tab],
        out_specs=pl.BlockSpec((None, BLOCK, GROUP_QKV_W), lambda s, n: (s, n, 0)),
        out_shape=jax.ShapeDtypeStruct((Bl * d, L, GROUP_QKV_W), BF16),
    )(qv, qv, qv, qv, qv, qv, dov, dov, lv, lv, cv, cv, *tabs)
    return out.reshape(Bl * S, GROUP_QKV_W)


def _pool_w_grad(name, pooled, d_pm, gw):
    T = pooled.shape[0]
    ng = len(POOL_WINDOWS)
    tk = _tile(T, 1024, 8)
    return _mm(name, pooled, d_pm, grid=(ng, 1, T // tk),
               a_block=(tk, gw), a_map=lambda i, j, k: (k, i),
               b_block=(tk, gw), b_map=lambda i, j, k: (k, i),
               dims=TN_DIMS, acc_shape=(gw, gw),
               outs=[((ng, gw, gw), F32, (None, gw, gw), lambda i, j, k: (i, 0, 0))])[0]


def _up_w_grad(name, h2, du):
    T, D = h2.shape
    F = du.shape[2]
    tm, tn, tk = _tile(D, 1024), _tile(F, 1408), _tile(T, 1024, 8)
    njh = F // tn
    return _mm(name, h2, du, grid=(D // tm, 2 * njh, T // tk),
               a_block=(tk, tm), a_map=lambda i, j, k: (k, i),
               b_block=(None, tk, tn), b_map=lambda i, j, k: (j // njh, k, j % njh),
               dims=TN_DIMS, acc_shape=(tm, tn),
               outs=[((D, 2 * F), F32, (tm, tn), lambda i, j, k: (i, j))])[0]


def _adamw(name, w, m, v, pieces):
    R, C = w.shape
    tr = _tile(R, max(PACK_ROWS, (1 << 18) // C // PACK_ROWS * PACK_ROWS), PACK_ROWS)
    c1 = 1.0 - ADAM_B1 ** ADAM_STEP
    c2 = 1.0 - ADAM_B2 ** ADAM_STEP

    def body(w_ref, m_ref, v_ref, p_ref, g_ref, d_ref, mo_ref, vo_ref):
        g = p_ref[0].astype(F32)
        for dev in range(1, N_DEV):
            g = g + p_ref[dev].astype(F32)
        mn = ADAM_B1 * m_ref[...] + (1.0 - ADAM_B1) * g
        vn = ADAM_B2 * v_ref[...] + (1.0 - ADAM_B2) * (g * g)
        g_ref[...] = g
        mo_ref[...] = mn
        vo_ref[...] = vn
        d_ref[...] = -ADAM_LR * ((mn / c1) / (jnp.sqrt(vn / c2) + ADAM_EPS) + ADAM_WD * w_ref[...])

    row = pl.BlockSpec((tr, C), lambda i: (i, 0))
    return pl.pallas_call(
        body, name=name, grid=(R // tr,),
        in_specs=[row, row, row, pl.BlockSpec((N_DEV, tr, C), lambda i: (0, i, 0))],
        out_specs=[row] * 4,
        out_shape=[jax.ShapeDtypeStruct((R, C), F32)] * 4)(w, m, v, pieces)


def _my_index():
    return 4 * lax.axis_index("x") + 2 * lax.axis_index("y") + lax.axis_index("c")


def _all_gather(name, mine):
    na = len(mine)

    def body(*refs):
        x_refs, out_refs = refs[:na], refs[na:2 * na]
        send_sems, recv_sems, local_sems = refs[2 * na:]
        x, y, c = lax.axis_index("x"), lax.axis_index("y"), lax.axis_index("c")
        me, sibling = (x, y, c), (x, y, 1 - c)
        chips = [(1 - x, y), (x, 1 - y), (1 - x, 1 - y)]

        def slot(a, px, py, pc):
            return out_refs[a].at[4 * px + 2 * py + pc]

        def copy(a, k, block, to, src=None):
            return pltpu.make_async_remote_copy(
                src_ref=slot(a, *block) if src is None else src, dst_ref=slot(a, *block),
                send_sem=send_sems.at[7 * a + k], recv_sem=recv_sems.at[7 * a + k],
                device_id=to, device_id_type=MESH_ID)

        own = [pltpu.make_async_copy(x_refs[a], slot(a, *me), local_sems.at[a]) for a in range(na)]
        for cp in own:
            cp.start()
        first = []
        for a in range(na):
            first.append(copy(a, 0, me, sibling, src=x_refs[a]))
            first += [copy(a, 1 + j, me, (*chip, c), src=x_refs[a]) for j, chip in enumerate(chips)]
        for cp in first:
            cp.start()
        passed = []
        for j, chip in enumerate(chips):
            for a in range(na):
                copy(a, 1 + j, (*chip, c), me).wait_recv()
                fwd = copy(a, 4 + j, (*chip, c), sibling)
                fwd.start()
                passed.append(fwd)
        for a in range(na):
            copy(a, 0, sibling, me).wait_recv()
            for j, chip in enumerate(chips):
                copy(a, 4 + j, (*chip, 1 - c), me).wait_recv()
        for cp in first + passed:
            cp.wait_send()
        for cp in own:
            cp.wait()

    return pl.pallas_call(
        body, name=name,
        in_specs=[pl.BlockSpec(memory_space=pl.ANY)] * na, out_specs=[pl.BlockSpec(memory_space=pl.ANY)] * na,
        out_shape=[jax.ShapeDtypeStruct((N_DEV,) + m.shape, m.dtype) for m in mine],
        scratch_shapes=[pltpu.SemaphoreType.DMA((7 * na,)), pltpu.SemaphoreType.DMA((7 * na,)),
                        pltpu.SemaphoreType.DMA((na,))],
    )(*mine)


def _exchange(name, pieces, bcast):
    n_p, n_b = len(pieces), len(bcast)
    na = n_p + n_b

    def body(*refs):
        src_refs, dst_refs = refs[:na], refs[na:2 * na]
        send_sems, recv_sems, local_sems = refs[2 * na:]
        x, y, c = lax.axis_index("x"), lax.axis_index("y"), lax.axis_index("c")
        me = 4 * x + 2 * y + c

        def src(a, slot):
            return src_refs[a].at[slot] if a < n_p else src_refs[a]

        own = [pltpu.make_async_copy(src(a, me), dst_refs[a].at[me], local_sems.at[a]) for a in range(na)]
        for cp in own:
            cp.start()

        def peer_of(k):
            px = 1 - x if k & 4 else x
            py = 1 - y if k & 2 else y
            pc = 1 - c if k & 1 else c
            return (px, py, pc), 4 * px + 2 * py + pc

        def copy(a, k, src_slot, dst_slot, to):
            return pltpu.make_async_remote_copy(
                src_ref=src(a, src_slot), dst_ref=dst_refs[a].at[dst_slot],
                send_sem=send_sems.at[7 * a + k - 1], recv_sem=recv_sems.at[7 * a + k - 1],
                device_id=to, device_id_type=MESH_ID)

        sent = []
        for k in range(1, N_DEV):
            to, pidx = peer_of(k)
            for a in range(na):
                cp = copy(a, k, pidx, me, to)
                cp.start()
                sent.append(cp)
        for k in range(1, N_DEV):
            to, pidx = peer_of(k)
            for a in range(na):
                copy(a, k, me, pidx, to).wait_recv()
        for cp in sent:
            cp.wait_send()
        for cp in own:
            cp.wait()

    arrays = list(pieces) + list(bcast)
    out_shape = [jax.ShapeDtypeStruct(p.shape, p.dtype) for p in pieces]
    out_shape += [jax.ShapeDtypeStruct((N_DEV,) + b.shape, b.dtype) for b in bcast]
    res = pl.pallas_call(
        body, name=name,
        in_specs=[pl.BlockSpec(memory_space=pl.ANY)] * na, out_specs=[pl.BlockSpec(memory_space=pl.ANY)] * na,
        out_shape=out_shape,
        scratch_shapes=[pltpu.SemaphoreType.DMA((7 * na,)), pltpu.SemaphoreType.DMA((7 * na,)),
                        pltpu.SemaphoreType.DMA((na,))],
    )(*arrays)
    return res[:n_p], res[n_p:]


def _pad_rows(flat, cols, row_mult):
    n = flat.shape[-1]
    unit = cols * row_mult
    padded = -(-n // unit) * unit
    pad = [(0, 0)] * (flat.ndim - 1) + [(0, padded - n)]
    return jnp.pad(flat, pad).reshape(flat.shape[:-1] + (padded // cols, cols))


def _perm_cols(w):
    aw = N_GROUPS * GROUP_W
    parts = [w[..., QKV_W:]]
    parts += [w[..., a * aw + g * GROUP_W:a * aw + (g + 1) * GROUP_W] for g in range(N_GROUPS) for a in range(3)]
    return jnp.concatenate(parts, axis=-1)


def _unperm_cols(wp, rest_w):
    qkv = wp[..., rest_w:]
    parts = [qkv[..., g * GROUP_QKV_W + a * GROUP_W:g * GROUP_QKV_W + (a + 1) * GROUP_W]
             for a in range(3) for g in range(N_GROUPS)]
    return jnp.concatenate(parts + [wp[..., :rest_w]], axis=-1)


def _gather_weights(shards):
    mine = [shards[n] if n in EXACT_F32 else shards[n].astype(BF16) for n, _ in SHARDED]
    got = _all_gather("all_gather_weights", mine)
    full = {}
    for (n, ax), seg in zip(SHARDED, got):
        shp = shards[n].shape
        seg = jnp.moveaxis(seg, 0, ax + 1)
        full[n] = seg.reshape(shp[:ax + 1] + (N_DEV * shp[ax + 1],) + shp[ax + 2:])
    return full


def _scatter_pieces(grads):
    out = []
    for n, ax in SHARDED:
        gr = grads[n]
        shp = gr.shape
        gr = gr.reshape(shp[:ax + 1] + (N_DEV, shp[ax + 1] // N_DEV) + shp[ax + 2:])
        out.append(jnp.moveaxis(gr, ax + 1, 0).astype(BF16))
    return out


def _pack_small(vals):
    flat = jnp.concatenate([vals[n].astype(F32).reshape(-1) for n in REPLICATED])
    return _pad_rows(flat, LANE, 8)


def _layer_fwd(li, x, p_l, W, G, rope, Bl, S, F):
    T, D = x.shape
    rest_w = 3 * D
    sv = {"x0": x}
    hs = _rms_mix_fwd(f"rms_mix_{li}", x, G["g_mix"], S)
    h = hs[0]
    h_g = [h] + [a.reshape(T, D) for a in hs[1:]]
    sv["h_g"] = h_g
    zr = _mm_nn(f"rest_proj_{li}", h, W["w_in"], n_cols=rest_w, tn=1024, tk=_tile(D, 1024))
    sv["zr"] = zr
    qkv_l, o_l, lse_l = [], [], []
    for g in range(N_GROUPS):
        qkv = _qkv_proj(f"qkv_proj_{li}_{g}", h_g[g], W["w_in"], rope[g], S, rest_w + g * GROUP_QKV_W)
        o, lse = _attn_fwd(f"attn_fwd_{li}_{g}", qkv, g, Bl, S)
        qkv_l.append(qkv)
        o_l.append(o)
        lse_l.append(lse)
    sv["qkv"], sv["o"], sv["lse"] = qkv_l, o_l, lse_l
    x1, attn, pooled, mixed, y_a, y_b, merged = _mix_out_fwd(
        f"mix_out_fwd_{li}", x, zr, o_l, lse_l, W["w_ya"], W["pool_w"], G["pool_scale"], W["w_yb"], W["w_o"], S)
    sv.update(x1=x1, attn=attn, pooled=pooled, mixed=mixed, y_a=y_a, y_b=y_b, merged=merged)
    h2 = _rms_fwd(f"rms_ffn_{li}", x1, G["g_ffn"])
    u = _up_proj(f"up_proj_{li}", h2, W["w_up"], F)
    a = _ffn_act_fwd(f"ffn_act_fwd_{li}", u, W["conv_w"], G["conv_b"], S)
    x2 = _mm_nn(f"down_proj_{li}", a, W["w_down"], add=x1, tk=_tile(F, 1408))
    sv.update(h2=h2, u=u, a=a, x2=x2)
    x3, e, pg, p_bf = _ple_fwd(f"ple_fwd_{li}", x2, p_l, G["g_ple"], W["w_ple_gate"], W["w_ple"])
    sv.update(e=e, pg=pg, p_bf=p_bf)
    return x3, sv


def _layer_bwd(li, dx3, sv, W, G, rope, Bl, S, F):
    T, D = dx3.shape
    rest_w = 3 * D
    gr = {}
    dx2, d_e, d_s, h3, dg = _ple_bwd(f"ple_bwd_{li}", dx3, sv["x2"], sv["e"], sv["pg"], G["g_ple"], W["w_ple_gate"])
    gr["g_ple"] = dg[0]
    gr["w_ple"] = _mm_tn(f"w_ple_grad_{li}", sv["p_bf"], d_e)
    gr["w_ple_gate"] = _mm_tn(f"w_ple_gate_grad_{li}", h3, d_s)

    d_a = _mm_nt(f"down_bwd_{li}", dx2, W["w_down"], tn=1408, tk=_tile(D, 1024))
    gr["w_down"] = _mm_tn(f"w_down_grad_{li}", sv["a"], dx2, tm=1408)
    du, d_cw, d_cb = _ffn_act_bwd(f"ffn_act_bwd_{li}", sv["u"], d_a, W["conv_w"], G["conv_b"], S)
    gr["conv_w"] = d_cw.transpose(1, 0, 2).reshape(3, 2 * F)
    gr["conv_b"] = d_cb.reshape(2 * F)
    tk_f = _tile(F, 1408)
    nkh = F // tk_f
    tm_r = _tile(T, 512, 8)
    dx1, dg = _mm_nt_rmsbwd(f"up_bwd_{li}", du, (None, tm_r, tk_f), lambda i, j, k: (k // nkh, i, k % nkh),
                            2 * nkh, tk_f, W["w_up"], sv["x1"], G["g_ffn"], dx2)
    gr["g_ffn"] = dg[0]
    gr["w_up"] = _up_w_grad(f"w_up_grad_{li}", sv["h2"], du)

    (dz_rest, d_ya, d_yb, d_pm, do0, do1, do2, c0, c1, c2, dps) = _mix_out_bwd(
        f"mix_out_bwd_{li}", dx1, sv["zr"], sv["y_a"], sv["y_b"], sv["o"], sv["lse"], sv["pooled"],
        W["w_o"], W["w_ya"], W["w_yb"], W["pool_w"], G["pool_scale"], S)
    gr["pool_scale"] = dps[0]
    gr["w_o"] = _mm_tn(f"w_o_grad_{li}", sv["merged"], dx1)
    gr["w_ya"] = _mm_tn(f"w_ya_grad_{li}", sv["attn"], d_ya)
    gr["w_yb"] = _mm_tn(f"w_yb_grad_{li}", sv["mixed"], d_yb)
    gr["pool_w"] = _pool_w_grad(f"pool_w_grad_{li}", sv["pooled"], d_pm, D // len(POOL_WINDOWS))
    segs = [(dz_rest, 1)]
    for g, (do, cst) in enumerate(((do0, c0), (do1, c1), (do2, c2))):
        dqkv = _attn_bwd(f"attn_bwd_{li}_{g}", sv["qkv"][g], do, sv["lse"][g], cst, rope[g], g, Bl, S)
        segs.append((dqkv, DILATIONS[g]))

    dx0, dg = _in_bwd(f"in_bwd_{li}", segs, W["w_in"], sv["x0"], G["g_mix"], dx1, S)
    gr["g_mix"] = dg[0]
    h_rows = [sv["h_g"][0]] + sv["h_g"]
    w_in_parts = [_mm_tn(f"w_in_grad_{li}_{s}", h_rows[s], seg, tn=1536) for s, (seg, _) in enumerate(segs)]
    gr["w_in"] = _unperm_cols(jnp.concatenate(w_in_parts, axis=1), rest_w)
    return dx0, gr


def kernel(x, p, g_mix, w_in, w_ya, w_yb, pool_w, pool_scale, w_o, g_ffn, w_up, conv_w, conv_b, w_down, g_ple, w_ple, w_ple_gate, g_final, loss_target, m_g_mix, m_w_in, m_w_ya, m_w_yb, m_pool_w, m_pool_scale, m_w_o, m_g_ffn, m_w_up, m_conv_w, m_conv_b, m_w_down, m_g_ple, m_w_ple, m_w_ple_gate, m_g_final, v_g_mix, v_w_in, v_w_ya, v_w_yb, v_pool_w, v_pool_scale, v_w_o, v_g_ffn, v_w_up, v_conv_w, v_conv_b, v_w_down, v_g_ple, v_w_ple, v_w_ple_gate, v_g_final):
    wts = dict(g_mix=g_mix, w_in=w_in, w_ya=w_ya, w_yb=w_yb, pool_w=pool_w, pool_scale=pool_scale, w_o=w_o,
               g_ffn=g_ffn, w_up=w_up, conv_w=conv_w, conv_b=conv_b, w_down=w_down, g_ple=g_ple, w_ple=w_ple,
               w_ple_gate=w_ple_gate, g_final=g_final)
    mom = dict(g_mix=m_g_mix, w_in=m_w_in, w_ya=m_w_ya, w_yb=m_w_yb, pool_w=m_pool_w, pool_scale=m_pool_scale,
               w_o=m_w_o, g_ffn=m_g_ffn, w_up=m_w_up, conv_w=m_conv_w, conv_b=m_conv_b, w_down=m_w_down,
               g_ple=m_g_ple, w_ple=m_w_ple, w_ple_gate=m_w_ple_gate, g_final=m_g_final)
    var = dict(g_mix=v_g_mix, w_in=v_w_in, w_ya=v_w_ya, w_yb=v_w_yb, pool_w=v_pool_w, pool_scale=v_pool_scale,
               w_o=v_w_o, g_ffn=v_g_ffn, w_up=v_w_up, conv_w=v_conv_w, conv_b=v_conv_b, w_down=v_w_down,
               g_ple=v_g_ple, w_ple=v_w_ple, w_ple_gate=v_w_ple_gate, g_final=v_g_final)
    Bl, S, D = x.shape
    depth = g_mix.shape[0]
    F = w_down.shape[1] * N_DEV
    T = Bl * S
    assert S % (BLOCK * DILATIONS[-1]) == 0 and D % GROUP_W == 0 and F % LANE == 0
    rope = [tuple(t if d == 1 else t.reshape(S // d, d, HEAD_DIM).transpose(1, 0, 2).reshape(S, HEAD_DIM)
                  for t in _rope_tables(S)) for d in DILATIONS]

    full = _gather_weights({n: wts[n] for n, _ in SHARDED})
    full["w_in"] = _perm_cols(full["w_in"])

    xs = x.reshape(T, D)
    saved = []
    for li in range(depth):
        W = {n: full[n][li] for n, _ in SHARDED}
        G = {n: wts[n][li] for n in REPLICATED if n != "g_final"}
        xs, sv = _layer_fwd(li, xs, p[li].reshape(T, -1), W, G, rope, Bl, S, F)
        saved.append((sv, W, G))

    dx, loss_row, dg_final = _loss_bwd("loss_bwd", xs, loss_target.reshape(T, D), g_final)
    layer_grads = [None] * depth
    for li in reversed(range(depth)):
        sv, W, G = saved[li]
        dx, layer_grads[li] = _layer_bwd(li, dx, sv, W, G, rope, Bl, S, F)

    grads = {n: jnp.stack([layer_grads[li][n] for li in range(depth)]) for n in WEIGHT_ORDER if n != "g_final"}
    grads["g_final"] = dg_final[0]
    recv, (small_all,) = _exchange("exchange_grads", _scatter_pieces(grads), [_pack_small(grads)])

    out_g, out_d, out_m, out_v = {}, {}, {}, {}
    for (n, _), pieces in zip(SHARDED, recv):
        shp = wts[n].shape
        two_d = (math.prod(shp[:-1]), shp[-1])
        res = _adamw(f"adamw_{n}", wts[n].reshape(two_d), mom[n].reshape(two_d), var[n].reshape(two_d),
                     pieces.reshape((N_DEV,) + two_d))
        out_g[n], out_d[n], out_m[n], out_v[n] = [r.reshape(shp) for r in res]
    res = _adamw("adamw_replicated", _pack_small(wts), _pack_small(mom), _pack_small(var), small_all)
    off = 0
    for n in REPLICATED:
        shp = wts[n].shape
        size = math.prod(shp)
        for dst, r in zip((out_g, out_d, out_m, out_v), res):
            dst[n] = r.reshape(-1)[off:off + size].reshape(shp)
        off += size

    loss = lax.psum(loss_row[0, 0], MESH_AXES)
    outs = [loss, dx.reshape(Bl, S, D)]
    for dct in (out_g, out_d, out_m, out_v):
        outs += [dct[n] for n in WEIGHT_ORDER]
    return tuple(outs)
```

```python
import math

import jax
import jax.numpy as jnp
from jax import lax
from jax.experimental import pallas as pl
from jax.experimental.pallas import tpu as pltpu

F32 = jnp.float32
BF16 = jnp.bfloat16

N_DEV = 8
HEAD_DIM = 128
HEADS = 4
GROUP_W = HEADS * HEAD_DIM
DILATIONS = (1, 4, 16)
N_GROUPS = len(DILATIONS)
QKV_W = 3 * N_GROUPS * GROUP_W
GROUP_QKV_W = 3 * GROUP_W
BLOCK = 128
ROPE_DIM = HEAD_DIM // 4
ROPE_HALF = ROPE_DIM // 2
ROPE_THETA = 500000.0
NEG_INF = -1e30
POOL_WINDOWS = (2, 4, 8, 16)
POOL_HALO = 16
CONV_HALO = 8
RMS_EPS = 1e-6
ADAM_LR = 0.001
ADAM_B1 = 0.9
ADAM_B2 = 0.999
ADAM_EPS = 1e-08
ADAM_WD = 0.01
ADAM_STEP = 10
LANE = 128
PACK_COLS = 1024
PACK_ROWS = 16
MESH_ID = pl.DeviceIdType.MESH
MESH_AXES = ("x", "y", "c")

NT_DIMS = (((1,), (1,)), ((), ()))
TN_DIMS = (((0,), (0,)), ((), ()))
NN_DIMS = (((1,), (0,)), ((), ()))

SHARDED = (("w_in", 1), ("w_ya", 1), ("w_yb", 0), ("pool_w", 1), ("w_o", 0), ("w_up", 1), ("conv_w", 1),
           ("w_down", 0), ("w_ple", 1), ("w_ple_gate", 0))
EXACT_F32 = ("conv_w",)
REPLICATED = ("g_mix", "pool_scale", "g_ffn", "conv_b", "g_ple", "g_final")
WEIGHT_ORDER = ("g_mix", "w_in", "w_ya", "w_yb", "pool_w", "pool_scale", "w_o", "g_ffn", "w_up", "conv_w", "conv_b",
                "w_down", "g_ple", "w_ple", "w_ple_gate", "g_final")


def _tile(n, pref, mult=LANE):
    if n <= pref:
        return n
    t = (pref // mult) * mult
    while t >= mult:
        if n % t == 0:
            return t
        t -= mult
    return n


def _sigmoid(x):
    return 1.0 / (1.0 + jnp.exp(-x))


def _dot(a, b, dims=NN_DIMS):
    return lax.dot_general(a.astype(BF16), b.astype(BF16), dims, preferred_element_type=F32)


def _rstd(x):
    return lax.rsqrt(jnp.mean(x * x, axis=-1, keepdims=True) + RMS_EPS)


def _rms_bwd(x, g, dh):
    r = _rstd(x)
    u = dh * g
    dx = r * u - x * (r * r * r) * jnp.mean(x * u, axis=-1, keepdims=True)
    return dx, dh * x * r


def _full(a):
    return pl.BlockSpec(a.shape, lambda *_: (0,) * a.ndim)


def _mm(name, a, b, *, grid, a_block, a_map, b_block, b_map, dims, acc_shape, outs, extras=(), epi=None):
    nk = grid[2]
    n_ex = len(extras)
    n_out = len(outs)

    def body(*refs):
        a_ref, b_ref = refs[0], refs[1]
        ex = refs[2:2 + n_ex]
        o = refs[2 + n_ex:2 + n_ex + n_out]
        acc = refs[2 + n_ex + n_out]
        i, j, k = pl.program_id(0), pl.program_id(1), pl.program_id(2)

        @pl.when(k == 0)
        def _():
            acc[...] = jnp.zeros_like(acc)

        acc[...] += _dot(a_ref[...], b_ref[...], dims)

        @pl.when(k == nk - 1)
        def _():
            if epi is None:
                o[0][...] = acc[...].astype(o[0].dtype)
            else:
                epi(acc[...], ex, o, i, j)

    in_specs = [pl.BlockSpec(a_block, a_map), pl.BlockSpec(b_block, b_map)]
    in_specs += [pl.BlockSpec(blk, mp) for (_, blk, mp) in extras]
    out_specs = [pl.BlockSpec(blk, mp) for (_, _, blk, mp) in outs]
    out_shape = [jax.ShapeDtypeStruct(s, d) for (s, d, _, _) in outs]
    return pl.pallas_call(
        body, name=name, grid=grid, in_specs=in_specs, out_specs=out_specs, out_shape=out_shape,
        scratch_shapes=[pltpu.VMEM(acc_shape, F32)],
    )(a, b, *[e[0] for e in extras])


def _mm_nn(name, a, b, *, out_dtype=F32, tm=1024, tn=1024, tk=512, b_col_off=0, n_cols=None, add=None):
    M, K = a.shape
    N = n_cols if n_cols is not None else b.shape[1]
    tm, tn, tk = _tile(M, tm, 8), _tile(N, tn), _tile(K, tk)
    assert b_col_off % tn == 0
    joff = b_col_off // tn
    extras, epi = (), None
    if add is not None:
        extras = ((add, (tm, tn), lambda i, j, k: (i, j)),)

        def epi(acc, ex, o, i, j):
            o[0][...] = (acc + ex[0][...]).astype(o[0].dtype)

    return _mm(name, a, b, grid=(M // tm, N // tn, K // tk),
               a_block=(tm, tk), a_map=lambda i, j, k: (i, k),
               b_block=(tk, tn), b_map=lambda i, j, k: (k, j + joff),
               dims=NN_DIMS, acc_shape=(tm, tn),
               outs=[((M, N), out_dtype, (tm, tn), lambda i, j, k: (i, j))], extras=extras, epi=epi)[0]


def _mm_nt(name, a, b, *, out_dtype=F32, tm=1024, tn=1024, tk=512):
    M, K = a.shape
    N = b.shape[0]
    tm, tn, tk = _tile(M, tm, 8), _tile(N, tn), _tile(K, tk)
    return _mm(name, a, b, grid=(M // tm, N // tn, K // tk),
               a_block=(tm, tk), a_map=lambda i, j, k: (i, k),
               b_block=(tn, tk), b_map=lambda i, j, k: (j, k),
               dims=NT_DIMS, acc_shape=(tm, tn),
               outs=[((M, N), out_dtype, (tm, tn), lambda i, j, k: (i, j))])[0]


def _mm_tn(name, a, b, *, tm=1024, tn=1024, tk=1024):
    K, M = a.shape
    N = b.shape[1]
    tm, tn, tk = _tile(M, tm), _tile(N, tn), _tile(K, tk, 8)
    return _mm(name, a, b, grid=(M // tm, N // tn, K // tk),
               a_block=(tk, tm), a_map=lambda i, j, k: (k, i),
               b_block=(tk, tn), b_map=lambda i, j, k: (k, j),
               dims=TN_DIMS, acc_shape=(tm, tn),
               outs=[((M, N), F32, (tm, tn), lambda i, j, k: (i, j))])[0]


def _mm_nt_rmsbwd(name, a, a_block, a_map, nk, tk, w, x, g, dres):
    T, D = x.shape
    tm = a_block[-2]

    def epi(acc, ex, o, i, j):
        @pl.when(i == 0)
        def _():
            o[1][...] = jnp.zeros_like(o[1])

        dx, dgr = _rms_bwd(ex[0][...], ex[1][...], acc)
        o[0][...] = ex[2][...] + dx
        o[1][...] += jnp.sum(dgr, axis=0, keepdims=True)

    row = lambda i, j, k: (i, 0)
    vec = lambda i, j, k: (0, 0)
    return _mm(name, a, w, grid=(T // tm, 1, nk),
               a_block=a_block, a_map=a_map,
               b_block=(D, tk), b_map=lambda i, j, k: (0, k),
               dims=NT_DIMS, acc_shape=(tm, D),
               outs=[((T, D), F32, (tm, D), row), ((1, D), F32, (1, D), vec)],
               extras=[(x, (tm, D), row), (g.reshape(1, D), (1, D), vec), (dres, (tm, D), row)], epi=epi)


def _in_bwd(name, segs, w_perm, x, g, dres, S):
    T, D = x.shape
    tm = _tile(S, 512, 256)
    nst = S // tm
    tk = GROUP_QKV_W
    steps = [a.shape[1] // tk for a, _ in segs]
    starts = [sum(steps[:s]) for s in range(len(segs))]
    nk = sum(steps)
    ns = len(segs)
    cols = _chunks(D)
    assert all(a.shape[1] % tk == 0 for a, _ in segs) and nk * tk == w_perm.shape[1]

    def body(*refs):
        a_refs = refs[:ns]
        w_ref, x_ref, g_ref, dres_ref, dx_ref, dg_ref, acc = refs[ns:]
        i, k = pl.program_id(0), pl.program_id(1)

        @pl.when(k == 0)
        def _():
            acc[...] = jnp.zeros_like(acc)

        for s in range(ns):
            d = segs[s][1]

            @pl.when((k >= starts[s]) & (k < starts[s] + steps[s]))
            def _():
                prod = _dot(a_refs[s][...].reshape(tm, tk), w_ref[...], NT_DIMS)
                q = tm // d
                for c, cs in enumerate(cols):
                    if d == 1:
                        acc[c] += prod[:, cs]
                    else:
                        for r in range(d):
                            acc[c, pl.ds(r, q, stride=d), :] += prod[r * q:(r + 1) * q, cs]

        @pl.when(k == nk - 1)
        def _():
            @pl.when(i == 0)
            def _():
                dg_ref[...] = jnp.zeros_like(dg_ref)

            dh = jnp.concatenate([acc[c] for c in range(len(cols))], axis=1)
            dx, dgr = _rms_bwd(x_ref[...], g_ref[...], dh)
            dx_ref[...] = dres_ref[...] + dx
            dg_ref[...] += jnp.sum(dgr, axis=0, keepdims=True)

    def seg_spec(s):
        kmap = lambda k: jnp.clip(k - starts[s], 0, steps[s] - 1)
        d = segs[s][1]
        if d == 1:
            return pl.BlockSpec((tm, tk), lambda i, k: (i, kmap(k)))
        return pl.BlockSpec((None, d, tm // d, tk), lambda i, k: (i // nst, 0, i % nst, kmap(k)))

    views = [a if d == 1 else a.reshape(T // S, d, S // d, a.shape[1]) for a, d in segs]
    row = pl.BlockSpec((tm, D), lambda i, k: (i, 0))
    vec = pl.BlockSpec((1, D), lambda i, k: (0, 0))
    return pl.pallas_call(
        body, name=name, grid=(T // tm, nk),
        in_specs=[seg_spec(s) for s in range(ns)] + [pl.BlockSpec((D, tk), lambda i, k: (0, k)), row, vec, row],
        out_specs=[row, vec],
        out_shape=[jax.ShapeDtypeStruct((T, D), F32), jax.ShapeDtypeStruct((1, D), F32)],
        scratch_shapes=[pltpu.VMEM((D // LANE, tm, LANE), F32)],
    )(*views, w_perm, x, g.reshape(1, D), dres)


def _rope_tables(S):
    pos = jnp.arange(S, dtype=F32)
    inv_freq = jnp.exp(jnp.arange(0, ROPE_DIM, 2, dtype=F32) * (-math.log(ROPE_THETA) / ROPE_DIM))
    ang = pos[:, None] * inv_freq[None, :]
    cos, sin = jnp.cos(ang), jnp.sin(ang)
    ones = jnp.ones((S, HEAD_DIM - ROPE_DIM), F32)
    zeros_h = jnp.zeros((S, ROPE_HALF), F32)
    zeros_r = jnp.zeros((S, HEAD_DIM - ROPE_DIM), F32)
    c = jnp.concatenate([cos, cos, ones], axis=1)
    sa = jnp.concatenate([-sin, zeros_h, zeros_r], axis=1)
    sb = jnp.concatenate([zeros_h, sin, zeros_r], axis=1)
    return c, sa, sb


def _rope(t, c, sa, sb):
    return t * c + pltpu.roll(t, HEAD_DIM - ROPE_HALF, 1) * sa + pltpu.roll(t, ROPE_HALF, 1) * sb


def _rms_fwd(name, x, g):
    T, D = x.shape
    tm = _tile(T, 512, 8)

    def body(x_ref, g_ref, h_ref):
        xv = x_ref[...]
        h_ref[...] = (xv * _rstd(xv) * g_ref[...]).astype(BF16)

    return pl.pallas_call(
        body, name=name, grid=(T // tm,),
        in_specs=[pl.BlockSpec((tm, D), lambda i: (i, 0)), pl.BlockSpec((1, D), lambda i: (0, 0))],
        out_specs=pl.BlockSpec((tm, D), lambda i: (i, 0)),
        out_shape=jax.ShapeDtypeStruct((T, D), BF16))(x, g.reshape(1, D))


def _chunks(width):
    return [slice(c * LANE, (c + 1) * LANE) for c in range(width // LANE)]


def _store_residue_major(val, sc, out_ref, d, dtype):
    rows = val.shape[0]
    for c, cs in enumerate(_chunks(val.shape[1])):
        sc[c] = val[:, cs]
    for r in range(d):
        for c, cs in enumerate(_chunks(val.shape[1])):
            out_ref[r, :, cs] = sc[c, pl.ds(r, rows // d, stride=d), :].astype(dtype)


def _load_token_order(blk_ref, sc, d):
    _, q, width = blk_ref.shape
    for r in range(d):
        for c, cs in enumerate(_chunks(width)):
            sc[c, pl.ds(r, q, stride=d), :] = blk_ref[r, :, cs]
    return jnp.concatenate([sc[c] for c in range(width // LANE)], axis=1)


def _residue_major_spec(d, q, width, nst):
    return pl.BlockSpec((None, d, q, width), lambda i, *_: (i // nst, 0, i % nst, 0))


def _rms_mix_fwd(name, x, g, S):
    T, D = x.shape
    Bl = T // S
    tm = _tile(S, 512, 256)
    nst = S // tm
    dils = [d for d in DILATIONS if d > 1]

    def body(x_ref, g_ref, h_ref, *rest):
        rm_refs, sc = rest[:len(dils)], rest[len(dils)]
        xv = x_ref[...]
        hv = xv * _rstd(xv) * g_ref[...]
        h_ref[...] = hv.astype(BF16)
        for d, o_ref in zip(dils, rm_refs):
            _store_residue_major(hv, sc, o_ref, d, BF16)

    row = pl.BlockSpec((tm, D), lambda i: (i, 0))
    return pl.pallas_call(
        body, name=name, grid=(T // tm,),
        in_specs=[row, pl.BlockSpec((1, D), lambda i: (0, 0))],
        out_specs=[row] + [_residue_major_spec(d, tm // d, D, nst) for d in dils],
        out_shape=[jax.ShapeDtypeStruct((T, D), BF16)]
        + [jax.ShapeDtypeStruct((Bl, d, S // d, D), BF16) for d in dils],
        scratch_shapes=[pltpu.VMEM((D // LANE, tm, LANE), F32)])(x, g.reshape(1, D))


def _qkv_proj(name, h, w_perm, rope, S, col_off):
    T, D = h.shape
    tm = _tile(S, 1024, 8)
    tn = GROUP_W
    tk = _tile(D, 1024)
    c_t, sa_t, sb_t = rope
    n_seq_tiles = S // tm
    joff = col_off // tn
    tmap = lambda i, j, k: (i % n_seq_tiles, 0)

    def epi(acc, ex, o, i, j):
        is_rot = j < 2

        @pl.when(is_rot)
        def _():
            c, sa, sb = ex[0][...], ex[1][...], ex[2][...]
            for hh in range(HEADS):
                sl = slice(hh * HEAD_DIM, (hh + 1) * HEAD_DIM)
                o[0][:, sl] = _rope(acc[:, sl], c, sa, sb).astype(BF16)

        @pl.when(jnp.logical_not(is_rot))
        def _():
            o[0][...] = acc.astype(BF16)

    return _mm(name, h, w_perm, grid=(T // tm, 3, D // tk),
               a_block=(tm, tk), a_map=lambda i, j, k: (i, k),
               b_block=(tk, tn), b_map=lambda i, j, k: (k, j + joff),
               dims=NN_DIMS, acc_shape=(tm, tn),
               outs=[((T, GROUP_QKV_W), BF16, (tm, tn), lambda i, j, k: (i, j))],
               extras=[(c_t, (tm, HEAD_DIM), tmap), (sa_t, (tm, HEAD_DIM), tmap), (sb_t, (tm, HEAD_DIM), tmap)],
               epi=epi)[0]


def _attn_mask(n):
    qi = lax.broadcasted_iota(jnp.int32, (BLOCK, 2 * BLOCK), 0)
    ki = lax.broadcasted_iota(jnp.int32, (BLOCK, 2 * BLOCK), 1)
    diff = BLOCK + qi - ki
    return (diff >= 0) & (diff <= BLOCK) & ((n > 0) | (ki >= BLOCK))


def _attn_fwd(name, qkv, g, Bl, S):
    d = DILATIONS[g]
    L = S // d
    nb = L // BLOCK
    qv = qkv.reshape(Bl * d, L, GROUP_QKV_W)
    scale = HEAD_DIM ** -0.5

    def body(q_ref, kc_ref, vc_ref, kp_ref, vp_ref, o_ref, l_ref):
        valid = _attn_mask(pl.program_id(1))
        for hh in range(HEADS):
            sl = slice(hh * HEAD_DIM, (hh + 1) * HEAD_DIM)
            kk = jnp.concatenate([kp_ref[:, sl], kc_ref[:, sl]], axis=0)
            vv = jnp.concatenate([vp_ref[:, sl], vc_ref[:, sl]], axis=0)
            s = jnp.where(valid, _dot(q_ref[:, sl], kk, NT_DIMS) * scale, NEG_INF)
            m = jnp.max(s, axis=-1, keepdims=True)
            p = jnp.exp(s - m)
            l = jnp.sum(p, axis=-1, keepdims=True)
            o_ref[:, sl] = _dot(p, vv) / l
            l_ref[:, sl] = jnp.broadcast_to(m + jnp.log(l), (BLOCK, HEAD_DIM))

    blk = (None, BLOCK, GROUP_W)
    cur = lambda off: (lambda s, n: (s, n, off))
    prev = lambda off: (lambda s, n: (s, jnp.maximum(n - 1, 0), off))
    omap = lambda s, n: (s, n, 0)
    return pl.pallas_call(
        body, name=name, grid=(Bl * d, nb),
        in_specs=[pl.BlockSpec(blk, cur(0)), pl.BlockSpec(blk, cur(1)), pl.BlockSpec(blk, cur(2)),
                  pl.BlockSpec(blk, prev(1)), pl.BlockSpec(blk, prev(2))],
        out_specs=[pl.BlockSpec(blk, omap), pl.BlockSpec(blk, omap)],
        out_shape=[jax.ShapeDtypeStruct((Bl * d, L, GROUP_W), F32)] * 2)(qv, qv, qv, qv, qv)


def _merge_weights(l0, l1, l2):
    mx = jnp.maximum(jnp.maximum(l0, l1), l2)
    e0, e1, e2 = jnp.exp(l0 - mx), jnp.exp(l1 - mx), jnp.exp(l2 - mx)
    inv = 1.0 / (e0 + e1 + e2)
    return e0 * inv, e1 * inv, e2 * inv


def _group_specs(tm, nst, S):
    specs = []
    for d in DILATIONS:
        specs.append(pl.BlockSpec((tm, GROUP_W), lambda i: (i, 0)) if d == 1
                     else _residue_major_spec(d, tm // d, GROUP_W, nst))

    def views(arrs):
        out = []
        for d, a in zip(DILATIONS, arrs):
            out.append(a.reshape(-1, GROUP_W) if d == 1 else a.reshape(-1, d, S // d, GROUP_W))
        return out

    return specs, views


def _group_scratch(tm, per_group):
    n = per_group * sum(1 for d in DILATIONS if d > 1)
    return [pltpu.VMEM((GROUP_W // LANE, tm, LANE), F32) for _ in range(n)]


def _group_values(o_refs, l_refs, scs):
    scs = list(scs)
    ov, lv = [], []
    for d, o_ref, l_ref in zip(DILATIONS, o_refs, l_refs):
        if d == 1:
            ov.append(o_ref[...])
            lv.append(l_ref[...])
        else:
            ov.append(_load_token_order(o_ref, scs.pop(), d))
            lv.append(_load_token_order(l_ref, scs.pop(), d))
    return ov, lv


def _pool_inv_count(tseq, w):
    return 1.0 / jnp.minimum(tseq + 1, w).astype(F32)


def _mix_out_fwd(name, x, zr, o_l, lse_l, w_ya, pool_w, pool_scale, w_yb, w_o, S):
    T, D = x.shape
    gw = D // len(POOL_WINDOWS)
    tm = _tile(S, 256, POOL_HALO)
    nst = S // tm
    hpt = tm // POOL_HALO

    def body(x_ref, u_ref, uh_ref, ga_ref, gb_ref, o0, o1, o2, l0, l1, l2, wya_ref, pw_ref, ps_ref, wyb_ref, wo_ref,
             x1_ref, attn_ref, pooled_ref, mixed_ref, ya_ref, yb_ref, merged_ref, *scs):
        it = pl.program_id(0) % nst
        ov, lv = _group_values((o0, o1, o2), (l0, l1, l2), scs)
        w0, w1, w2 = _merge_weights(*lv)
        attn = w0 * ov[0] + w1 * ov[1] + w2 * ov[2]
        attn_ref[...] = attn.astype(BF16)
        y_a = _dot(attn, wya_ref[...])

        u = u_ref[...]
        halo = uh_ref[...] * jnp.where(it == 0, 0.0, 1.0)
        ext = jnp.concatenate([halo, u], axis=0)
        tseq = it * tm + lax.broadcasted_iota(jnp.int32, (tm, 1), 0)
        pm_parts = []
        for gi, w in enumerate(POOL_WINDOWS):
            cs = slice(gi * gw, (gi + 1) * gw)
            s = ext[:, cs]
            step = 1
            while step < w:
                s = s + pltpu.roll(s, step, 0)
                step *= 2
            pooled_g = (s[POOL_HALO:, :] * _pool_inv_count(tseq, w) - u[:, cs]).astype(BF16)
            pooled_ref[:, cs] = pooled_g
            pm_parts.append(_dot(pooled_g, pw_ref[gi]))
        mixed = (jnp.concatenate(pm_parts, axis=1) * ps_ref[...]).astype(BF16)
        mixed_ref[...] = mixed
        y_b = _dot(mixed, wyb_ref[...])
        merged = (_sigmoid(ga_ref[...]) * y_a + _sigmoid(gb_ref[...]) * y_b).astype(BF16)
        ya_ref[...] = y_a
        yb_ref[...] = y_b
        merged_ref[...] = merged
        x1_ref[...] = x_ref[...] + _dot(merged, wo_ref[...])

    row = lambda c: pl.BlockSpec((tm, D), lambda i: (i, c))
    row512 = pl.BlockSpec((tm, GROUP_W), lambda i: (i, 0))
    ps = pool_scale.reshape(1, D)
    halo_spec = pl.BlockSpec((POOL_HALO, D), lambda i: (jnp.maximum(i * hpt - 1, 0), 0))
    grp_specs, grp_views = _group_specs(tm, nst, S)
    return pl.pallas_call(
        body, name=name, grid=(T // tm,),
        in_specs=[row(0), row(0), halo_spec, row(1), row(2)] + grp_specs * 2
        + [_full(w_ya), _full(pool_w), _full(ps), _full(w_yb), _full(w_o)],
        out_specs=[row(0), row512, row(0), row(0), row(0), row(0), row(0)],
        out_shape=[jax.ShapeDtypeStruct((T, D), F32), jax.ShapeDtypeStruct((T, GROUP_W), BF16),
                   jax.ShapeDtypeStruct((T, D), BF16), jax.ShapeDtypeStruct((T, D), BF16),
                   jax.ShapeDtypeStruct((T, D), F32), jax.ShapeDtypeStruct((T, D), F32),
                   jax.ShapeDtypeStruct((T, D), BF16)],
        scratch_shapes=_group_scratch(tm, 2),
    )(x, zr, zr, zr, zr, *grp_views(o_l), *grp_views(lse_l), w_ya, pool_w, ps, w_yb, w_o)


def _up_proj(name, h2, w_up, F):
    T, D = h2.shape
    tm, tn, tk = _tile(T, 1024, 8), _tile(F, 1408), _tile(D, 512)
    njh = F // tn
    return _mm(name, h2, w_up, grid=(T // tm, 2 * njh, D // tk),
               a_block=(tm, tk), a_map=lambda i, j, k: (i, k),
               b_block=(tk, tn), b_map=lambda i, j, k: (k, j),
               dims=NN_DIMS, acc_shape=(tm, tn),
               outs=[((2, T, F), F32, (None, tm, tn), lambda i, j, k: (j // njh, i, j % njh))])[0]


def _conv_y(ext, w_ref, b_ref):
    return (b_ref[...] + w_ref[2:3, :] * ext + w_ref[1:2, :] * pltpu.roll(ext, 1, 0)
            + w_ref[0:1, :] * pltpu.roll(ext, 2, 0))


def _conv_params(conv_w, conv_b, F):
    cw = conv_w.reshape(3, 2, F).transpose(1, 0, 2)
    return cw, conv_b.reshape(2, 1, F)


def _ffn_act_fwd(name, u, conv_w, conv_b, S):
    _, T, F = u.shape
    tm = _tile(S, 512, CONV_HALO)
    tf = _tile(F, 1408)
    nst = S // tm
    hpt = tm // CONV_HALO
    cw, cb = _conv_params(conv_w, conv_b, F)

    def body(ug_ref, uv_ref, hg_ref, hv_ref, wg_ref, wv_ref, bg_ref, bv_ref, a_ref):
        keep = jnp.where(pl.program_id(0) % nst == 0, 0.0, 1.0)
        yg = _conv_y(jnp.concatenate([hg_ref[...] * keep, ug_ref[...]], axis=0), wg_ref, bg_ref)[CONV_HALO:, :]
        yv = _conv_y(jnp.concatenate([hv_ref[...] * keep, uv_ref[...]], axis=0), wv_ref, bv_ref)[CONV_HALO:, :]
        a_ref[...] = (yg * _sigmoid(yg) * yv).astype(BF16)

    main = lambda h: pl.BlockSpec((None, tm, tf), lambda i, j: (h, i, j))
    halo = lambda h: pl.BlockSpec((None, CONV_HALO, tf), lambda i, j: (h, jnp.maximum(i * hpt - 1, 0), j))
    wsp = lambda h: pl.BlockSpec((None, 3, tf), lambda i, j: (h, 0, j))
    bsp = lambda h: pl.BlockSpec((None, 1, tf), lambda i, j: (h, 0, j))
    return pl.pallas_call(
        body, name=name, grid=(T // tm, F // tf),
        in_specs=[main(0), main(1), halo(0), halo(1), wsp(0), wsp(1), bsp(0), bsp(1)],
        out_specs=pl.BlockSpec((tm, tf), lambda i, j: (i, j)),
        out_shape=jax.ShapeDtypeStruct((T, F), BF16))(u, u, u, u, cw, cw, cb, cb)


def _ple_fwd(name, x2, p, g_ple, w_gate, w_ple):
    T, D = x2.shape
    P = p.shape[1]
    tm = _tile(T, 512, 8)

    def body(x_ref, p_ref, g_ref, wg_ref, wp_ref, x3_ref, e_ref, pg_ref, pbf_ref):
        xv = x_ref[...]
        h3 = xv * _rstd(xv) * g_ref[...]
        pg = _sigmoid(_dot(h3, wg_ref[...]))
        pb = p_ref[...].astype(BF16)
        e = _dot(pb, wp_ref[...])
        x3_ref[...] = xv + e * pg
        e_ref[...] = e
        pg_ref[...] = pg
        pbf_ref[...] = pb

    row = pl.BlockSpec((tm, D), lambda i: (i, 0))
    prow = pl.BlockSpec((tm, P), lambda i: (i, 0))
    g2 = g_ple.reshape(1, D)
    return pl.pallas_call(
        body, name=name, grid=(T // tm,),
        in_specs=[row, prow, _full(g2), _full(w_gate), _full(w_ple)],
        out_specs=[row, row, row, prow],
        out_shape=[jax.ShapeDtypeStruct((T, D), F32)] * 3 + [jax.ShapeDtypeStruct((T, P), BF16)],
    )(x2, p, g2, w_gate, w_ple)


def _loss_bwd(name, xf, target, g_final):
    T, D = xf.shape
    tm = _tile(T, 512, 8)
    nt = T // tm

    def body(x_ref, t_ref, g_ref, dx_ref, loss_ref, dg_ref, lacc):
        i = pl.program_id(0)

        @pl.when(i == 0)
        def _():
            lacc[...] = jnp.zeros_like(lacc)
            dg_ref[...] = jnp.zeros_like(dg_ref)
            loss_ref[...] = jnp.zeros_like(loss_ref)

        xv = x_ref[...]
        g = g_ref[...]
        diff = xv * _rstd(xv) * g - t_ref[...]
        lacc[...] += jnp.sum(diff * diff, axis=0, keepdims=True)
        dx, dgr = _rms_bwd(xv, g, diff * (1.0 / D))
        dx_ref[...] = dx
        dg_ref[...] += jnp.sum(dgr, axis=0, keepdims=True)

        @pl.when(i == nt - 1)
        def _():
            tot = jnp.sum(lacc[...], axis=-1, keepdims=True) * (0.5 / D)
            loss_ref[...] = jnp.broadcast_to(tot, (1, LANE))

    row = pl.BlockSpec((tm, D), lambda i: (i, 0))
    vec = pl.BlockSpec((1, D), lambda i: (0, 0))
    return pl.pallas_call(
        body, name=name, grid=(nt,),
        in_specs=[row, row, vec],
        out_specs=[row, pl.BlockSpec((1, LANE), lambda i: (0, 0)), vec],
        out_shape=[jax.ShapeDtypeStruct((T, D), F32), jax.ShapeDtypeStruct((1, LANE), F32),
                   jax.ShapeDtypeStruct((1, D), F32)],
        scratch_shapes=[pltpu.VMEM((1, D), F32)])(xf, target, g_final.reshape(1, D))


def _ple_bwd(name, dx3, x2, e, pg, g_ple, w_gate):
    T, D = x2.shape
    tm = _tile(T, 512, 8)

    def body(dx3_ref, x_ref, e_ref, pg_ref, g_ref, wg_ref, dx2_ref, de_ref, ds_ref, h3_ref, dg_ref):
        @pl.when(pl.program_id(0) == 0)
        def _():
            dg_ref[...] = jnp.zeros_like(dg_ref)

        dx3v, xv, pgv, g = dx3_ref[...], x_ref[...], pg_ref[...], g_ref[...]
        de_ref[...] = (dx3v * pgv).astype(BF16)
        ds = (dx3v * e_ref[...] * pgv * (1.0 - pgv)).astype(BF16)
        ds_ref[...] = ds
        dh3 = _dot(ds, wg_ref[...], NT_DIMS)
        h3_ref[...] = (xv * _rstd(xv) * g).astype(BF16)
        dx, dgr = _rms_bwd(xv, g, dh3)
        dx2_ref[...] = dx3v + dx
        dg_ref[...] += jnp.sum(dgr, axis=0, keepdims=True)

    row = pl.BlockSpec((tm, D), lambda i: (i, 0))
    vec = pl.BlockSpec((1, D), lambda i: (0, 0))
    return pl.pallas_call(
        body, name=name, grid=(T // tm,),
        in_specs=[row, row, row, row, vec, _full(w_gate)],
        out_specs=[row, row, row, row, vec],
        out_shape=[jax.ShapeDtypeStruct((T, D), F32)] + [jax.ShapeDtypeStruct((T, D), BF16)] * 3
        + [jax.ShapeDtypeStruct((1, D), F32)])(dx3, x2, e, pg, g_ple.reshape(1, D), w_gate)


def _ffn_act_bwd(name, u, d_a, conv_w, conv_b, S):
    _, T, F = u.shape
    H = CONV_HALO
    tm = _tile(S, 512, H)
    tf = _tile(F, 1408)
    nst = S // tm
    hpt = tm // H
    last_halo = T // H - 1
    n_ext = tm + H
    cw, cb = _conv_params(conv_w, conv_b, F)

    def body(ug_ref, uv_ref, pg_ref, pv_ref, ng_ref, nv_ref, da_ref, dan_ref, wg_ref, wv_ref, bg_ref, bv_ref,
             du_ref, dw_ref, db_ref):
        i = pl.program_id(1)
        it = i % nst

        @pl.when(i == 0)
        def _():
            dw_ref[...] = jnp.zeros_like(dw_ref)
            db_ref[...] = jnp.zeros_like(db_ref)

        keep_prev = jnp.where(it == 0, 0.0, 1.0)
        keep_next = jnp.where(it == nst - 1, 0.0, 1.0)
        ext_g = jnp.concatenate([pg_ref[...] * keep_prev, ug_ref[...], ng_ref[...]], axis=0)
        ext_v = jnp.concatenate([pv_ref[...] * keep_prev, uv_ref[...], nv_ref[...]], axis=0)
        yg = _conv_y(ext_g, wg_ref, bg_ref)[H:, :]
        yv = _conv_y(ext_v, wv_ref, bv_ref)[H:, :]
        rows = lax.broadcasted_iota(jnp.int32, (n_ext, 1), 0)
        live = jnp.where(rows < tm, 1.0, keep_next)
        da = jnp.concatenate([da_ref[...], dan_ref[...]], axis=0) * live
        sg = _sigmoid(yg)
        dyv = da * (yg * sg)
        dyg = da * yv * (sg * (1.0 + yg * (1.0 - sg)))
        for half, (dy, ext, w_ref) in enumerate(((dyg, ext_g, wg_ref), (dyv, ext_v, wv_ref))):
            du = (w_ref[2:3, :] * dy + w_ref[1:2, :] * pltpu.roll(dy, n_ext - 1, 0)
                  + w_ref[0:1, :] * pltpu.roll(dy, n_ext - 2, 0))
            du_ref[half] = du[:tm, :].astype(BF16)
            dym = dy[:tm, :]
            db_ref[half] += jnp.sum(dym, axis=0, keepdims=True)
            dw_ref[half, 2:3, :] += jnp.sum(dym * ext[H:H + tm, :], axis=0, keepdims=True)
            dw_ref[half, 1:2, :] += jnp.sum(dym * pltpu.roll(ext, 1, 0)[H:H + tm, :], axis=0, keepdims=True)
            dw_ref[half, 0:1, :] += jnp.sum(dym * pltpu.roll(ext, 2, 0)[H:H + tm, :], axis=0, keepdims=True)

    main = lambda h: pl.BlockSpec((None, tm, tf), lambda j, i: (h, i, j))
    prev = lambda h: pl.BlockSpec((None, H, tf), lambda j, i: (h, jnp.maximum(i * hpt - 1, 0), j))
    nxt = lambda h: pl.BlockSpec((None, H, tf), lambda j, i: (h, jnp.minimum((i + 1) * hpt, last_halo), j))
    wsp = lambda h: pl.BlockSpec((None, 3, tf), lambda j, i: (h, 0, j))
    bsp = lambda h: pl.BlockSpec((None, 1, tf), lambda j, i: (h, 0, j))
    return pl.pallas_call(
        body, name=name, grid=(F // tf, T // tm),
        in_specs=[main(0), main(1), prev(0), prev(1), nxt(0), nxt(1),
                  pl.BlockSpec((tm, tf), lambda j, i: (i, j)),
                  pl.BlockSpec((H, tf), lambda j, i: (jnp.minimum((i + 1) * hpt, last_halo), j)),
                  wsp(0), wsp(1), bsp(0), bsp(1)],
        out_specs=[pl.BlockSpec((2, tm, tf), lambda j, i: (0, i, j)),
                   pl.BlockSpec((2, 3, tf), lambda j, i: (0, 0, j)),
                   pl.BlockSpec((2, 1, tf), lambda j, i: (0, 0, j))],
        out_shape=[jax.ShapeDtypeStruct((2, T, F), BF16), jax.ShapeDtypeStruct((2, 3, F), F32),
                   jax.ShapeDtypeStruct((2, 1, F), F32)],
    )(u, u, u, u, u, u, d_a, d_a, cw, cw, cb, cb)


def _mix_out_bwd(name, dx1, zr, y_a, y_b, o_l, lse_l, pooled, w_o, w_ya, w_yb, pool_w, pool_scale, S):
    T, D = dx1.shape
    gw = D // len(POOL_WINDOWS)
    H = POOL_HALO
    tm = _tile(S, 256, H)
    nst = S // tm
    hpt = tm // H
    last_halo = T // H - 1
    n_ext = tm + H

    def body(dx_ref, dxn_ref, ga_ref, gb_ref, gbn_ref, ya_ref, yb_ref, o0, o1, o2, l0, l1, l2, pooled_ref,
             wo_ref, wya_ref, wyb_ref, pw_ref, ps_ref,
             dz_ref, dya_ref, dyb_ref, dpm_ref, do0, do1, do2, c0, c1, c2, dps_ref, *scs):
        i = pl.program_id(0)
        it = i % nst

        @pl.when(i == 0)
        def _():
            dps_ref[...] = jnp.zeros_like(dps_ref)

        keep_next = jnp.where(it == nst - 1, 0.0, 1.0)
        dm_e = _dot(jnp.concatenate([dx_ref[...], dxn_ref[...]], axis=0), wo_ref[...], NT_DIMS)
        sgb_e = _sigmoid(jnp.concatenate([gb_ref[...], gbn_ref[...]], axis=0))
        dyb_e = dm_e * sgb_e
        dm = dm_e[:tm, :]
        sga = _sigmoid(ga_ref[...])
        sgb = sgb_e[:tm, :]
        d_ga = dm * ya_ref[...] * (sga * (1.0 - sga))
        d_gb = dm * yb_ref[...] * (sgb * (1.0 - sgb))
        dya = (dm * sga).astype(BF16)
        dya_ref[...] = dya
        dyb_ref[...] = dyb_e[:tm, :].astype(BF16)
        dmixed_e = _dot(dyb_e, wyb_ref[...], NT_DIMS)

        rows = lax.broadcasted_iota(jnp.int32, (n_ext, 1), 0)
        tseq = it * tm + rows
        live = jnp.where(rows < tm, 1.0, keep_next)
        ps = ps_ref[...]
        du_parts = []
        for gi, w in enumerate(POOL_WINDOWS):
            cs = slice(gi * gw, (gi + 1) * gw)
            pm_g = _dot(pooled_ref[:, cs], pw_ref[gi])
            dps_ref[:, cs] += jnp.sum(dmixed_e[:tm, cs] * pm_g, axis=0, keepdims=True)
            dpm_e = (dmixed_e[:, cs] * ps[:, cs]).astype(BF16)
            dpm_ref[:, cs] = dpm_e[:tm, :]
            dpooled_e = _dot(dpm_e, pw_ref[gi], NT_DIMS)
            s = dpooled_e * (_pool_inv_count(tseq, w) * live)
            step = 1
            while step < w:
                s = s + pltpu.roll(s, n_ext - step, 0)
                step *= 2
            du_parts.append(s[:tm, :] - dpooled_e[:tm, :])
        dz_ref[...] = jnp.concatenate(du_parts + [d_ga, d_gb], axis=1).astype(BF16)

        d_attn = _dot(dya, wya_ref[...], NT_DIMS)
        ov, lv = _group_values((o0, o1, o2), (l0, l1, l2), scs[:n_in_sc])
        ws = _merge_weights(*lv)
        prod = d_attn * (ws[0] * ov[0] + ws[1] * ov[1] + ws[2] * ov[2])
        rs = jnp.concatenate(
            [jnp.broadcast_to(jnp.sum(prod[:, hh * HEAD_DIM:(hh + 1) * HEAD_DIM], axis=-1, keepdims=True),
                              (tm, HEAD_DIM)) for hh in range(HEADS)], axis=1)
        out_scs = list(scs[n_in_sc:])
        for d, wg, do_ref, c_ref in zip(DILATIONS, ws, (do0, do1, do2), (c0, c1, c2)):
            if d == 1:
                do_ref[...] = (wg * d_attn).astype(BF16)
                c_ref[...] = -wg * rs
            else:
                _store_residue_major(wg * d_attn, out_scs.pop(), do_ref, d, BF16)
                _store_residue_major(-wg * rs, out_scs.pop(), c_ref, d, F32)

    row = lambda c: pl.BlockSpec((tm, D), lambda i: (i, c))
    nxt = lambda c: pl.BlockSpec((H, D), lambda i: (jnp.minimum((i + 1) * hpt, last_halo), c))
    ps2 = pool_scale.reshape(1, D)
    bf = lambda w: jax.ShapeDtypeStruct((T, w), BF16)
    grp_specs, grp_views = _group_specs(tm, nst, S)
    grp_shape = lambda dt: [jax.ShapeDtypeStruct((T, GROUP_W) if d == 1 else (T // S, d, S // d, GROUP_W), dt)
                            for d in DILATIONS]
    n_in_sc = len(_group_scratch(tm, 2))
    return pl.pallas_call(
        body, name=name, grid=(T // tm,),
        in_specs=[row(0), nxt(0), row(1), row(2), nxt(2), row(0), row(0)] + grp_specs * 2 + [row(0)]
        + [_full(w_o), _full(w_ya), _full(w_yb), _full(pool_w), _full(ps2)],
        out_specs=[pl.BlockSpec((tm, 3 * D), lambda i: (i, 0)), row(0), row(0), row(0)] + grp_specs * 2
        + [pl.BlockSpec((1, D), lambda i: (0, 0))],
        out_shape=[bf(3 * D), bf(D), bf(D), bf(D)] + grp_shape(BF16) + grp_shape(F32)
        + [jax.ShapeDtypeStruct((1, D), F32)],
        scratch_shapes=_group_scratch(tm, 4),
    )(dx1, dx1, zr, zr, zr, y_a, y_b, *grp_views(o_l), *grp_views(lse_l), pooled, w_o, w_ya, w_yb, pool_w, ps2)


def _attn_bwd(name, qkv, d_o, lse, cst, rope, g, Bl, S):
    d = DILATIONS[g]
    L = S // d
    nb = L // BLOCK
    qv = qkv.reshape(Bl * d, L, GROUP_QKV_W)
    dov = d_o.reshape(Bl * d, L, GROUP_W)
    lv = lse.reshape(Bl * d, L, GROUP_W)
    cv = cst.reshape(Bl * d, L, GROUP_W)
    tabs = [t.reshape(d, L, HEAD_DIM) for t in rope]
    scale = HEAD_DIM ** -0.5

    def body(q_ref, qn_ref, kp_ref, kc_ref, vp_ref, vc_ref, do_ref, don_ref, l_ref, ln_ref, c_ref, cn_ref,
             cos_ref, sa_ref, sb_ref, out_ref):
        n = pl.program_id(1)
        valid = _attn_mask(n)
        qi = lax.broadcasted_iota(jnp.int32, (BLOCK, BLOCK), 0)
        ki = lax.broadcasted_iota(jnp.int32, (BLOCK, BLOCK), 1)
        valid_n = (ki >= qi) & (n + 1 < nb)
        cos, sa, sb = cos_ref[...], -sa_ref[...], -sb_ref[...]
        for hh in range(HEADS):
            sl = slice(hh * HEAD_DIM, (hh + 1) * HEAD_DIM)
            q, qn, kc, vc, do, don = q_ref[:, sl], qn_ref[:, sl], kc_ref[:, sl], vc_ref[:, sl], do_ref[:, sl], don_ref[:, sl]
            kk = jnp.concatenate([kp_ref[:, sl], kc], axis=0)
            vv = jnp.concatenate([vp_ref[:, sl], vc], axis=0)
            col = slice(hh * HEAD_DIM, hh * HEAD_DIM + 1)
            s = jnp.where(valid, _dot(q, kk, NT_DIMS) * scale, NEG_INF)
            p = jnp.exp(s - l_ref[:, col])
            ds = p * (_dot(do, vv, NT_DIMS) + c_ref[:, col])
            dq = _dot(ds, kk) * scale
            s2 = jnp.where(valid_n, _dot(qn, kc, NT_DIMS) * scale, NEG_INF)
            p2 = jnp.exp(s2 - ln_ref[:, col])
            ds2 = p2 * (_dot(don, vc, NT_DIMS) + cn_ref[:, col])
            dk = (_dot(ds[:, BLOCK:], q, TN_DIMS) + _dot(ds2, qn, TN_DIMS)) * scale
            dv = _dot(p[:, BLOCK:], do, TN_DIMS) + _dot(p2, don, TN_DIMS)
            out_ref[:, sl] = _rope(dq, cos, sa, sb).astype(BF16)
            out_ref[:, GROUP_W + hh * HEAD_DIM:GROUP_W + (hh + 1) * HEAD_DIM] = _rope(dk, cos, sa, sb).astype(BF16)
            out_ref[:, 2 * GROUP_W + hh * HEAD_DIM:2 * GROUP_W + (hh + 1) * HEAD_DIM] = dv.astype(BF16)

    blk = (None, BLOCK, GROUP_W)
    at = lambda f, off: pl.BlockSpec(blk, lambda s, n: (s, f(n), off))
    cur = lambda n: n
    prv = lambda n: jnp.maximum(n - 1, 0)
    nxt = lambda n: jnp.minimum(n + 1, nb - 1)
    tok = lambda f: pl.BlockSpec(blk, lambda s, n: (s, f(n), 0))
    tab = pl.BlockSpec((None, BLOCK, HEAD_DIM), lambda s, n: (s % d, n, 0))
    out = pl.pallas_call(
        body, name=name, grid=(Bl * d, nb),
        in_specs=[at(cur, 0), at(nxt, 0), at(prv, 1), at(cur, 1), at(prv, 2), at(cur, 2),
                  tok(cur), tok(nxt), tok(cur), tok(nxt), tok(cur), tok(nxt), tab, tab, tab],
        out_specs=pl.BlockSpec((None, BLOCK, GROUP_QKV_W), lambda s, n: (s, n, 0)),
        out_shape=jax.ShapeDtypeStruct((Bl * d, L, GROUP_QKV_W), BF16),
    )(qv, qv, qv, qv, qv, qv, dov, dov, lv, lv, cv, cv, *tabs)
    return out.reshape(Bl * S, GROUP_QKV_W)


def _pool_w_grad(name, pooled, d_pm, gw):
    T = pooled.shape[0]
    ng = len(POOL_WINDOWS)
    tk = _tile(T, 1024, 8)
    return _mm(name, pooled, d_pm, grid=(ng, 1, T // tk),
               a_block=(tk, gw), a_map=lambda i, j, k: (k, i),
               b_block=(tk, gw), b_map=lambda i, j, k: (k, i),
               dims=TN_DIMS, acc_shape=(gw, gw),
               outs=[((ng, gw, gw), F32, (None, gw, gw), lambda i, j, k: (i, 0, 0))])[0]


def _up_w_grad(name, h2, du):
    T, D = h2.shape
    F = du.shape[2]
    tm, tn, tk = _tile(D, 1024), _tile(F, 1408), _tile(T, 1024, 8)
    njh = F // tn
    return _mm(name, h2, du, grid=(D // tm, 2 * njh, T // tk),
               a_block=(tk, tm), a_map=lambda i, j, k: (k, i),
               b_block=(None, tk, tn), b_map=lambda i, j, k: (j // njh, k, j % njh),
               dims=TN_DIMS, acc_shape=(tm, tn),
               outs=[((D, 2 * F), F32, (tm, tn), lambda i, j, k: (i, j))])[0]


def _adamw(name, w, m, v, pieces):
    R, C = w.shape
    nl = len(pieces)
    rl = R // nl
    if nl > 1 and rl % 8:
        per = [_adamw(f"{name}_{l}", w[l * rl:(l + 1) * rl], m[l * rl:(l + 1) * rl], v[l * rl:(l + 1) * rl],
                      [pieces[l]]) for l in range(nl)]
        return [jnp.concatenate([p[o] for p in per], axis=0) for o in range(4)]
    tr = _tile(rl, max(PACK_ROWS, (1 << 18) // C // PACK_ROWS * PACK_ROWS), PACK_ROWS)
    nbl = rl // tr
    c1 = 1.0 - ADAM_B1 ** ADAM_STEP
    c2 = 1.0 - ADAM_B2 ** ADAM_STEP

    def body(w_ref, m_ref, v_ref, *rest):
        p_refs = rest[:nl]
        g_ref, d_ref, mo_ref, vo_ref = rest[nl:]
        i = pl.program_id(0)
        for l in range(nl):
            @pl.when((i >= l * nbl) & (i < (l + 1) * nbl))
            def _():
                g = p_refs[l][0].astype(F32)
                for dev in range(1, N_DEV):
                    g = g + p_refs[l][dev].astype(F32)
                mn = ADAM_B1 * m_ref[...] + (1.0 - ADAM_B1) * g
                vn = ADAM_B2 * v_ref[...] + (1.0 - ADAM_B2) * (g * g)
                g_ref[...] = g
                mo_ref[...] = mn
                vo_ref[...] = vn
                d_ref[...] = -ADAM_LR * ((mn / c1) / (jnp.sqrt(vn / c2) + ADAM_EPS) + ADAM_WD * w_ref[...])

    row = pl.BlockSpec((tr, C), lambda i: (i, 0))
    piece = lambda l: pl.BlockSpec((N_DEV, tr, C), lambda i: (0, jnp.clip(i - l * nbl, 0, nbl - 1), 0))
    return pl.pallas_call(
        body, name=name, grid=(R // tr,),
        in_specs=[row, row, row] + [piece(l) for l in range(nl)],
        out_specs=[row] * 4,
        out_shape=[jax.ShapeDtypeStruct((R, C), F32)] * 4)(w, m, v, *pieces)


def _my_index():
    return 4 * lax.axis_index("x") + 2 * lax.axis_index("y") + lax.axis_index("c")


def _all_gather(name, mine):
    na = len(mine)

    def body(*refs):
        x_refs, out_refs, token = refs[:na], refs[na:2 * na], refs[2 * na]
        send_sems, recv_sems, local_sems = refs[2 * na + 1:]
        token[...] = jnp.zeros_like(token)
        x, y, c = lax.axis_index("x"), lax.axis_index("y"), lax.axis_index("c")
        me, sibling = (x, y, c), (x, y, 1 - c)
        chips = [(1 - x, y), (x, 1 - y), (1 - x, 1 - y)]

        def slot(a, px, py, pc):
            return out_refs[a].at[4 * px + 2 * py + pc]

        def copy(a, k, block, to, src=None):
            return pltpu.make_async_remote_copy(
                src_ref=slot(a, *block) if src is None else src, dst_ref=slot(a, *block),
                send_sem=send_sems.at[7 * a + k], recv_sem=recv_sems.at[7 * a + k],
                device_id=to, device_id_type=MESH_ID)

        own = [pltpu.make_async_copy(x_refs[a], slot(a, *me), local_sems.at[a]) for a in range(na)]
        for cp in own:
            cp.start()
        first = []
        for a in range(na):
            first.append(copy(a, 0, me, sibling, src=x_refs[a]))
            first += [copy(a, 1 + j, me, (*chip, c), src=x_refs[a]) for j, chip in enumerate(chips)]
        for cp in first:
            cp.start()
        passed = []
        for j, chip in enumerate(chips):
            for a in range(na):
                copy(a, 1 + j, (*chip, c), me).wait_recv()
                fwd = copy(a, 4 + j, (*chip, c), sibling)
                fwd.start()
                passed.append(fwd)
        for a in range(na):
            copy(a, 0, sibling, me).wait_recv()
            for j, chip in enumerate(chips):
                copy(a, 4 + j, (*chip, 1 - c), me).wait_recv()
        for cp in first + passed:
            cp.wait_send()
        for cp in own:
            cp.wait()

    res = pl.pallas_call(
        body, name=name,
        in_specs=[pl.BlockSpec(memory_space=pl.ANY)] * na,
        out_specs=[pl.BlockSpec(memory_space=pl.ANY)] * na + [pl.BlockSpec(memory_space=pltpu.VMEM)],
        out_shape=[jax.ShapeDtypeStruct((N_DEV,) + m.shape, m.dtype) for m in mine]
        + [jax.ShapeDtypeStruct((8, LANE), F32)],
        scratch_shapes=[pltpu.SemaphoreType.DMA((7 * na,)), pltpu.SemaphoreType.DMA((7 * na,)),
                        pltpu.SemaphoreType.DMA((na,))],
    )(*mine)
    return res[:na], res[na]


_HBM_SPEC = pl.BlockSpec(memory_space=pltpu.HBM)
_SEM_SPEC = pl.BlockSpec(memory_space=pltpu.SEMAPHORE)
_SIDE_EFFECT = pltpu.SideEffectType.DATAFLOW_SIDE_EFFECTING


def _peer_of(k):
    x, y, c = lax.axis_index("x"), lax.axis_index("y"), lax.axis_index("c")
    px = 1 - x if k & 4 else x
    py = 1 - y if k & 2 else y
    pc = 1 - c if k & 1 else c
    return (px, py, pc), 4 * px + 2 * py + pc


def _send_start(name, srcs, pieces):
    na = len(srcs)
    land_shapes = [s.shape if pieces else (N_DEV,) + s.shape for s in srcs]
    lands = [lax.empty(shp, s.dtype) for shp, s in zip(land_shapes, srcs)]

    def body(*refs):
        src_refs, land_refs = refs[:na], refs[na:2 * na]
        send_sems, recv_sems, token = refs[2 * na], refs[2 * na + 1], refs[4 * na + 2]
        me = 4 * lax.axis_index("x") + 2 * lax.axis_index("y") + lax.axis_index("c")
        for k in range(1, N_DEV):
            to, pidx = _peer_of(k)
            for a in range(na):
                pltpu.make_async_remote_copy(
                    src_ref=src_refs[a].at[pidx] if pieces else src_refs[a], dst_ref=land_refs[a].at[me],
                    send_sem=send_sems.at[7 * a + k - 1], recv_sem=recv_sems.at[7 * a + k - 1],
                    device_id=to, device_id_type=MESH_ID).start()
        token[...] = jnp.zeros_like(token)

    hbm = lambda arrs: [pltpu.HBM(a.shape, a.dtype) for a in arrs]
    outs = pl.pallas_call(
        body, name=name,
        out_shape=(pltpu.SemaphoreType.DMA((7 * na,)), pltpu.SemaphoreType.DMA((7 * na,)), *hbm(srcs), *hbm(lands),
                   jax.ShapeDtypeStruct((8, LANE), F32)),
        in_specs=[_HBM_SPEC] * (2 * na),
        out_specs=(_SEM_SPEC, _SEM_SPEC, *([_HBM_SPEC] * (2 * na)), pl.BlockSpec(memory_space=pltpu.VMEM)),
        input_output_aliases={i: 2 + i for i in range(2 * na)},
        compiler_params=pltpu.CompilerParams(has_side_effects=_SIDE_EFFECT),
    )(*[pltpu.with_memory_space_constraint(s, pltpu.HBM) for s in srcs],
      *[pltpu.with_memory_space_constraint(l, pltpu.HBM) for l in lands])
    return outs[0], outs[1], outs[2:2 + na], outs[2 + na:2 + 2 * na], outs[-1]


def _send_wait(name, send_sems, recv_sems, srcs, lands, pieces, after):
    na = len(srcs)

    def body(*refs):
        src_refs, land_refs = refs[:na], refs[na:2 * na]
        send_sems, recv_sems = refs[2 * na], refs[2 * na + 1]
        for k in range(1, N_DEV):
            to, pidx = _peer_of(k)
            for a in range(na):
                cp = pltpu.make_async_remote_copy(
                    src_ref=src_refs[a].at[pidx] if pieces else src_refs[a], dst_ref=land_refs[a].at[pidx],
                    send_sem=send_sems.at[7 * a + k - 1], recv_sem=recv_sems.at[7 * a + k - 1],
                    device_id=to, device_id_type=MESH_ID)
                cp.wait_send()
                cp.wait_recv()

    hbm = lambda arrs: [pltpu.HBM(a.shape, a.dtype) for a in arrs]
    outs = pl.pallas_call(
        body, name=name,
        out_shape=tuple(hbm(srcs) + hbm(lands)),
        in_specs=[_HBM_SPEC] * (2 * na) + [_SEM_SPEC, _SEM_SPEC, pl.BlockSpec(memory_space=pl.ANY)],
        out_specs=tuple([_HBM_SPEC] * (2 * na)),
        input_output_aliases={i: i for i in range(2 * na)},
        compiler_params=pltpu.CompilerParams(has_side_effects=_SIDE_EFFECT),
    )(*srcs, *lands, send_sems, recv_sems, after)
    return outs[:na], outs[na:]


def _own_slot(land, own):
    me = 4 * lax.axis_index("x") + 2 * lax.axis_index("y") + lax.axis_index("c")
    mine = lax.broadcasted_iota(jnp.int32, land.shape, 0) == me
    return jnp.where(mine, jnp.broadcast_to(own, land.shape), land)


def _exchange(name, pieces, bcast):
    n_p, n_b = len(pieces), len(bcast)
    na = n_p + n_b

    def body(*refs):
        src_refs, dst_refs = refs[:na], refs[na:2 * na]
        send_sems, recv_sems, local_sems = refs[2 * na:]
        x, y, c = lax.axis_index("x"), lax.axis_index("y"), lax.axis_index("c")
        me = 4 * x + 2 * y + c

        def src(a, slot):
            return src_refs[a].at[slot] if a < n_p else src_refs[a]

        own = [pltpu.make_async_copy(src(a, me), dst_refs[a].at[me], local_sems.at[a]) for a in range(na)]
        for cp in own:
            cp.start()

        def peer_of(k):
            px = 1 - x if k & 4 else x
            py = 1 - y if k & 2 else y
            pc = 1 - c if k & 1 else c
            return (px, py, pc), 4 * px + 2 * py + pc

        def copy(a, k, src_slot, dst_slot, to):
            return pltpu.make_async_remote_copy(
                src_ref=src(a, src_slot), dst_ref=dst_refs[a].at[dst_slot],
                send_sem=send_sems.at[7 * a + k - 1], recv_sem=recv_sems.at[7 * a + k - 1],
                device_id=to, device_id_type=MESH_ID)

        sent = []
        for k in range(1, N_DEV):
            to, pidx = peer_of(k)
            for a in range(na):
                cp = copy(a, k, pidx, me, to)
                cp.start()
                sent.append(cp)
        for k in range(1, N_DEV):
            to, pidx = peer_of(k)
            for a in range(na):
                copy(a, k, me, pidx, to).wait_recv()
        for cp in sent:
            cp.wait_send()
        for cp in own:
            cp.wait()

    arrays = list(pieces) + list(bcast)
    out_shape = [jax.ShapeDtypeStruct(p.shape, p.dtype) for p in pieces]
    out_shape += [jax.ShapeDtypeStruct((N_DEV,) + b.shape, b.dtype) for b in bcast]
    res = pl.pallas_call(
        body, name=name,
        in_specs=[pl.BlockSpec(memory_space=pl.ANY)] * na, out_specs=[pl.BlockSpec(memory_space=pl.ANY)] * na,
        out_shape=out_shape,
        scratch_shapes=[pltpu.SemaphoreType.DMA((7 * na,)), pltpu.SemaphoreType.DMA((7 * na,)),
                        pltpu.SemaphoreType.DMA((na,))],
    )(*arrays)
    return res[:n_p], res[n_p:]


def _pad_rows(flat, cols, row_mult):
    n = flat.shape[-1]
    unit = cols * row_mult
    padded = -(-n // unit) * unit
    pad = [(0, 0)] * (flat.ndim - 1) + [(0, padded - n)]
    return jnp.pad(flat, pad).reshape(flat.shape[:-1] + (padded // cols, cols))


def _perm_cols(w):
    aw = N_GROUPS * GROUP_W
    parts = [w[..., QKV_W:]]
    parts += [w[..., a * aw + g * GROUP_W:a * aw + (g + 1) * GROUP_W] for g in range(N_GROUPS) for a in range(3)]
    return jnp.concatenate(parts, axis=-1)


def _unperm_cols(wp, rest_w):
    qkv = wp[..., rest_w:]
    parts = [qkv[..., g * GROUP_QKV_W + a * GROUP_W:g * GROUP_QKV_W + (a + 1) * GROUP_W]
             for a in range(3) for g in range(N_GROUPS)]
    return jnp.concatenate(parts + [wp[..., :rest_w]], axis=-1)


def _layer_shards(wts, li, zero):
    out = []
    for n, _ in SHARDED:
        w = wts[n][li] if zero is None else wts[n][li] + zero
        out.append(w if n in EXACT_F32 else w.astype(BF16))
    return out


def _assemble_weights(segs):
    W = {}
    for (n, ax), seg in zip(SHARDED, segs):
        shp = seg.shape[1:]
        seg = jnp.moveaxis(seg, 0, ax)
        W[n] = seg.reshape(shp[:ax] + (N_DEV * shp[ax],) + shp[ax + 1:])
    W["w_in"] = _perm_cols(W["w_in"])
    return W


def _grad_pieces(gr):
    out = []
    for n, ax in SHARDED:
        shp = gr[n].shape
        g = gr[n].reshape(shp[:ax] + (N_DEV, shp[ax] // N_DEV) + shp[ax + 1:])
        out.append(jnp.moveaxis(g, ax, 0).astype(BF16))
    return out


def _pack_small(vals):
    flat = jnp.concatenate([vals[n].astype(F32).reshape(-1) for n in REPLICATED])
    return _pad_rows(flat, LANE, 8)


def _layer_fwd(li, x, p_l, W, G, rope, Bl, S, F):
    T, D = x.shape
    rest_w = 3 * D
    sv = {"x0": x}
    hs = _rms_mix_fwd(f"rms_mix_{li}", x, G["g_mix"], S)
    h = hs[0]
    h_g = [h] + [a.reshape(T, D) for a in hs[1:]]
    sv["h_g"] = h_g
    zr = _mm_nn(f"rest_proj_{li}", h, W["w_in"], n_cols=rest_w, tn=1024, tk=_tile(D, 1024))
    sv["zr"] = zr
    qkv_l, o_l, lse_l = [], [], []
    for g in range(N_GROUPS):
        qkv = _qkv_proj(f"qkv_proj_{li}_{g}", h_g[g], W["w_in"], rope[g], S, rest_w + g * GROUP_QKV_W)
        o, lse = _attn_fwd(f"attn_fwd_{li}_{g}", qkv, g, Bl, S)
        qkv_l.append(qkv)
        o_l.append(o)
        lse_l.append(lse)
    sv["qkv"], sv["o"], sv["lse"] = qkv_l, o_l, lse_l
    x1, attn, pooled, mixed, y_a, y_b, merged = _mix_out_fwd(
        f"mix_out_fwd_{li}", x, zr, o_l, lse_l, W["w_ya"], W["pool_w"], G["pool_scale"], W["w_yb"], W["w_o"], S)
    sv.update(x1=x1, attn=attn, pooled=pooled, mixed=mixed, y_a=y_a, y_b=y_b, merged=merged)
    h2 = _rms_fwd(f"rms_ffn_{li}", x1, G["g_ffn"])
    u = _up_proj(f"up_proj_{li}", h2, W["w_up"], F)
    a = _ffn_act_fwd(f"ffn_act_fwd_{li}", u, W["conv_w"], G["conv_b"], S)
    x2 = _mm_nn(f"down_proj_{li}", a, W["w_down"], add=x1, tk=_tile(F, 1408))
    sv.update(h2=h2, u=u, a=a, x2=x2)
    x3, e, pg, p_bf = _ple_fwd(f"ple_fwd_{li}", x2, p_l, G["g_ple"], W["w_ple_gate"], W["w_ple"])
    sv.update(e=e, pg=pg, p_bf=p_bf)
    return x3, sv


def _layer_bwd(li, dx3, sv, W, G, rope, Bl, S, F):
    T, D = dx3.shape
    rest_w = 3 * D
    gr = {}
    dx2, d_e, d_s, h3, dg = _ple_bwd(f"ple_bwd_{li}", dx3, sv["x2"], sv["e"], sv["pg"], G["g_ple"], W["w_ple_gate"])
    gr["g_ple"] = dg[0]
    gr["w_ple"] = _mm_tn(f"w_ple_grad_{li}", sv["p_bf"], d_e)
    gr["w_ple_gate"] = _mm_tn(f"w_ple_gate_grad_{li}", h3, d_s)

    d_a = _mm_nt(f"down_bwd_{li}", dx2, W["w_down"], tn=1408, tk=_tile(D, 1024))
    gr["w_down"] = _mm_tn(f"w_down_grad_{li}", sv["a"], dx2, tm=1408)
    du, d_cw, d_cb = _ffn_act_bwd(f"ffn_act_bwd_{li}", sv["u"], d_a, W["conv_w"], G["conv_b"], S)
    gr["conv_w"] = d_cw.transpose(1, 0, 2).reshape(3, 2 * F)
    gr["conv_b"] = d_cb.reshape(2 * F)
    tk_f = _tile(F, 1408)
    nkh = F // tk_f
    tm_r = _tile(T, 512, 8)
    dx1, dg = _mm_nt_rmsbwd(f"up_bwd_{li}", du, (None, tm_r, tk_f), lambda i, j, k: (k // nkh, i, k % nkh),
                            2 * nkh, tk_f, W["w_up"], sv["x1"], G["g_ffn"], dx2)
    gr["g_ffn"] = dg[0]
    gr["w_up"] = _up_w_grad(f"w_up_grad_{li}", sv["h2"], du)

    (dz_rest, d_ya, d_yb, d_pm, do0, do1, do2, c0, c1, c2, dps) = _mix_out_bwd(
        f"mix_out_bwd_{li}", dx1, sv["zr"], sv["y_a"], sv["y_b"], sv["o"], sv["lse"], sv["pooled"],
        W["w_o"], W["w_ya"], W["w_yb"], W["pool_w"], G["pool_scale"], S)
    gr["pool_scale"] = dps[0]
    gr["w_o"] = _mm_tn(f"w_o_grad_{li}", sv["merged"], dx1)
    gr["w_ya"] = _mm_tn(f"w_ya_grad_{li}", sv["attn"], d_ya)
    gr["w_yb"] = _mm_tn(f"w_yb_grad_{li}", sv["mixed"], d_yb)
    gr["pool_w"] = _pool_w_grad(f"pool_w_grad_{li}", sv["pooled"], d_pm, D // len(POOL_WINDOWS))
    segs = [(dz_rest, 1)]
    for g, (do, cst) in enumerate(((do0, c0), (do1, c1), (do2, c2))):
        dqkv = _attn_bwd(f"attn_bwd_{li}_{g}", sv["qkv"][g], do, sv["lse"][g], cst, rope[g], g, Bl, S)
        segs.append((dqkv, DILATIONS[g]))

    dx0, dg = _in_bwd(f"in_bwd_{li}", segs, W["w_in"], sv["x0"], G["g_mix"], dx1, S)
    gr["g_mix"] = dg[0]
    h_rows = [sv["h_g"][0]] + sv["h_g"]
    w_in_parts = [_mm_tn(f"w_in_grad_{li}_{s}", h_rows[s], seg, tn=1536) for s, (seg, _) in enumerate(segs)]
    gr["w_in"] = _unperm_cols(jnp.concatenate(w_in_parts, axis=1), rest_w)
    return dx0, gr


def kernel(x, p, g_mix, w_in, w_ya, w_yb, pool_w, pool_scale, w_o, g_ffn, w_up, conv_w, conv_b, w_down, g_ple, w_ple, w_ple_gate, g_final, loss_target, m_g_mix, m_w_in, m_w_ya, m_w_yb, m_pool_w, m_pool_scale, m_w_o, m_g_ffn, m_w_up, m_conv_w, m_conv_b, m_w_down, m_g_ple, m_w_ple, m_w_ple_gate, m_g_final, v_g_mix, v_w_in, v_w_ya, v_w_yb, v_pool_w, v_pool_scale, v_w_o, v_g_ffn, v_w_up, v_conv_w, v_conv_b, v_w_down, v_g_ple, v_w_ple, v_w_ple_gate, v_g_final):
    wts = dict(g_mix=g_mix, w_in=w_in, w_ya=w_ya, w_yb=w_yb, pool_w=pool_w, pool_scale=pool_scale, w_o=w_o,
               g_ffn=g_ffn, w_up=w_up, conv_w=conv_w, conv_b=conv_b, w_down=w_down, g_ple=g_ple, w_ple=w_ple,
               w_ple_gate=w_ple_gate, g_final=g_final)
    mom = dict(g_mix=m_g_mix, w_in=m_w_in, w_ya=m_w_ya, w_yb=m_w_yb, pool_w=m_pool_w, pool_scale=m_pool_scale,
               w_o=m_w_o, g_ffn=m_g_ffn, w_up=m_w_up, conv_w=m_conv_w, conv_b=m_conv_b, w_down=m_w_down,
               g_ple=m_g_ple, w_ple=m_w_ple, w_ple_gate=m_w_ple_gate, g_final=m_g_final)
    var = dict(g_mix=v_g_mix, w_in=v_w_in, w_ya=v_w_ya, w_yb=v_w_yb, pool_w=v_pool_w, pool_scale=v_pool_scale,
               w_o=v_w_o, g_ffn=v_g_ffn, w_up=v_w_up, conv_w=v_conv_w, conv_b=v_conv_b, w_down=v_w_down,
               g_ple=v_g_ple, w_ple=v_w_ple, w_ple_gate=v_w_ple_gate, g_final=v_g_final)
    Bl, S, D = x.shape
    depth = g_mix.shape[0]
    F = w_down.shape[1] * N_DEV
    T = Bl * S
    assert S % (BLOCK * DILATIONS[-1]) == 0 and D % GROUP_W == 0 and F % LANE == 0
    rope = [tuple(t if d == 1 else t.reshape(S // d, d, HEAD_DIM).transpose(1, 0, 2).reshape(S, HEAD_DIM)
                  for t in _rope_tables(S)) for d in DILATIONS]

    got0, tok = _all_gather("gather_w0", _layer_shards(wts, 0, None))
    gathers = {}
    for li in range(1, depth):
        *gathers[li], tok = _send_start(f"gather_w{li}_start", _layer_shards(wts, li, tok[0, 0]), False)

    xs = x.reshape(T, D)
    saved = []
    for li in range(depth):
        G = {n: wts[n][li] for n in REPLICATED if n != "g_final"}
        if li == 0:
            W = _assemble_weights(got0)
            G["g_mix"] = G["g_mix"] + tok[0, 0]
        else:
            shards, lands = _send_wait(f"gather_w{li}_wait", *gathers[li], False, xs)
            W = _assemble_weights([_own_slot(l, s) for l, s in zip(lands, shards)])
        xs, sv = _layer_fwd(li, xs, p[li].reshape(T, -1), W, G, rope, Bl, S, F)
        saved.append((sv, W, G))

    dx, loss_row, dg_final = _loss_bwd("loss_bwd", xs, loss_target.reshape(T, D), g_final)
    layer_grads = [None] * depth
    sends = {}
    tok = None
    for li in reversed(range(depth)):
        sv, W, G = saved[li]
        if tok is not None:
            G = dict(G, g_ple=G["g_ple"] + tok[0, 0])
        dx, layer_grads[li] = _layer_bwd(li, dx, sv, W, G, rope, Bl, S, F)
        if li > 0:
            *sends[li], tok = _send_start(f"exchange_g{li}_start", _grad_pieces(layer_grads[li]), True)

    recv_l = [None] * depth
    for li in range(1, depth):
        pcs, lands = _send_wait(f"exchange_g{li}_wait", *sends[li], True, dx)
        recv_l[li] = [_own_slot(l, s) for l, s in zip(lands, pcs)]
    grads = {n: jnp.stack([layer_grads[li][n] for li in range(depth)]) for n in REPLICATED if n != "g_final"}
    grads["g_final"] = dg_final[0]
    recv_l[0], (small_all,) = _exchange("exchange_g0", _grad_pieces(layer_grads[0]), [_pack_small(grads)])

    out_g, out_d, out_m, out_v = {}, {}, {}, {}
    for idx, (n, _) in enumerate(SHARDED):
        shp = wts[n].shape
        two_d = (math.prod(shp[:-1]), shp[-1])
        pieces = [recv_l[li][idx].reshape(N_DEV, two_d[0] // depth, two_d[1]) for li in range(depth)]
        res = _adamw(f"adamw_{n}", wts[n].reshape(two_d), mom[n].reshape(two_d), var[n].reshape(two_d), pieces)
        out_g[n], out_d[n], out_m[n], out_v[n] = [r.reshape(shp) for r in res]
    res = _adamw("adamw_replicated", _pack_small(wts), _pack_small(mom), _pack_small(var), [small_all])
    off = 0
    for n in REPLICATED:
        shp = wts[n].shape
        size = math.prod(shp)
        for dst, r in zip((out_g, out_d, out_m, out_v), res):
            dst[n] = r.reshape(-1)[off:off + size].reshape(shp)
        off += size

    loss = lax.psum(loss_row[0, 0], MESH_AXES)
    outs = [loss, dx.reshape(Bl, S, D)]
    for dct in (out_g, out_d, out_m, out_v):
        outs += [dct[n] for n in WEIGHT_ORDER]
    return tuple(outs)
```

```python
import math

import jax
import jax.numpy as jnp
from jax import lax
from jax.experimental import pallas as pl
from jax.experimental.pallas import tpu as pltpu

F32 = jnp.float32
BF16 = jnp.bfloat16

N_DEV = 8
HEAD_DIM = 128
HEADS = 4
GROUP_W = HEADS * HEAD_DIM
DILATIONS = (1, 4, 16)
N_GROUPS = len(DILATIONS)
QKV_W = 3 * N_GROUPS * GROUP_W
GROUP_QKV_W = 3 * GROUP_W
BLOCK = 128
ROPE_DIM = HEAD_DIM // 4
ROPE_HALF = ROPE_DIM // 2
ROPE_THETA = 500000.0
NEG_INF = -1e30
POOL_WINDOWS = (2, 4, 8, 16)
POOL_HALO = 16
CONV_HALO = 8
RMS_EPS = 1e-6
ADAM_LR = 0.001
ADAM_B1 = 0.9
ADAM_B2 = 0.999
ADAM_EPS = 1e-08
ADAM_WD = 0.01
ADAM_STEP = 10
LANE = 128
PACK_COLS = 1024
PACK_ROWS = 16
MESH_ID = pl.DeviceIdType.MESH
MESH_AXES = ("x", "y", "c")

NT_DIMS = (((1,), (1,)), ((), ()))
TN_DIMS = (((0,), (0,)), ((), ()))
NN_DIMS = (((1,), (0,)), ((), ()))

SHARDED = (("w_in", 1), ("w_ya", 1), ("w_yb", 0), ("pool_w", 1), ("w_o", 0), ("w_up", 1), ("conv_w", 1),
           ("w_down", 0), ("w_ple", 1), ("w_ple_gate", 0))
EXACT_F32 = ("conv_w",)
REPLICATED = ("g_mix", "pool_scale", "g_ffn", "conv_b", "g_ple", "g_final")
WEIGHT_ORDER = ("g_mix", "w_in", "w_ya", "w_yb", "pool_w", "pool_scale", "w_o", "g_ffn", "w_up", "conv_w", "conv_b",
                "w_down", "g_ple", "w_ple", "w_ple_gate", "g_final")


def _tile(n, pref, mult=LANE):
    if n <= pref:
        return n
    t = (pref // mult) * mult
    while t >= mult:
        if n % t == 0:
            return t
        t -= mult
    return n


def _sigmoid(x):
    return 1.0 / (1.0 + jnp.exp(-x))


def _dot(a, b, dims=NN_DIMS):
    return lax.dot_general(a.astype(BF16), b.astype(BF16), dims, preferred_element_type=F32)


def _rstd(x):
    return lax.rsqrt(jnp.mean(x * x, axis=-1, keepdims=True) + RMS_EPS)


def _rms_bwd(x, g, dh):
    r = _rstd(x)
    u = dh * g
    dx = r * u - x * (r * r * r) * jnp.mean(x * u, axis=-1, keepdims=True)
    return dx, dh * x * r


def _full(a):
    return pl.BlockSpec(a.shape, lambda *_: (0,) * a.ndim)


def _mm(name, a, b, *, grid, a_block, a_map, b_block, b_map, dims, acc_shape, outs, extras=(), epi=None):
    nk = grid[2]
    n_ex = len(extras)
    n_out = len(outs)

    def body(*refs):
        a_ref, b_ref = refs[0], refs[1]
        ex = refs[2:2 + n_ex]
        o = refs[2 + n_ex:2 + n_ex + n_out]
        acc = refs[2 + n_ex + n_out]
        i, j, k = pl.program_id(0), pl.program_id(1), pl.program_id(2)

        @pl.when(k == 0)
        def _():
            acc[...] = jnp.zeros_like(acc)

        acc[...] += _dot(a_ref[...], b_ref[...], dims)

        @pl.when(k == nk - 1)
        def _():
            if epi is None:
                o[0][...] = acc[...].astype(o[0].dtype)
            else:
                epi(acc[...], ex, o, i, j)

    in_specs = [pl.BlockSpec(a_block, a_map), pl.BlockSpec(b_block, b_map)]
    in_specs += [pl.BlockSpec(blk, mp) for (_, blk, mp) in extras]
    out_specs = [pl.BlockSpec(blk, mp) for (_, _, blk, mp) in outs]
    out_shape = [jax.ShapeDtypeStruct(s, d) for (s, d, _, _) in outs]
    return pl.pallas_call(
        body, name=name, grid=grid, in_specs=in_specs, out_specs=out_specs, out_shape=out_shape,
        scratch_shapes=[pltpu.VMEM(acc_shape, F32)],
    )(a, b, *[e[0] for e in extras])


def _mm_nn(name, a, b, *, out_dtype=F32, tm=1024, tn=1024, tk=512, b_col_off=0, n_cols=None, add=None):
    M, K = a.shape
    N = n_cols if n_cols is not None else b.shape[1]
    tm, tn, tk = _tile(M, tm, 8), _tile(N, tn), _tile(K, tk)
    assert b_col_off % tn == 0
    joff = b_col_off // tn
    extras, epi = (), None
    if add is not None:
        extras = ((add, (tm, tn), lambda i, j, k: (i, j)),)

        def epi(acc, ex, o, i, j):
            o[0][...] = (acc + ex[0][...]).astype(o[0].dtype)

    return _mm(name, a, b, grid=(M // tm, N // tn, K // tk),
               a_block=(tm, tk), a_map=lambda i, j, k: (i, k),
               b_block=(tk, tn), b_map=lambda i, j, k: (k, j + joff),
               dims=NN_DIMS, acc_shape=(tm, tn),
               outs=[((M, N), out_dtype, (tm, tn), lambda i, j, k: (i, j))], extras=extras, epi=epi)[0]


def _mm_nt(name, a, b, *, out_dtype=F32, tm=1024, tn=1024, tk=512):
    M, K = a.shape
    N = b.shape[0]
    tm, tn, tk = _tile(M, tm, 8), _tile(N, tn), _tile(K, tk)
    return _mm(name, a, b, grid=(M // tm, N // tn, K // tk),
               a_block=(tm, tk), a_map=lambda i, j, k: (i, k),
               b_block=(tn, tk), b_map=lambda i, j, k: (j, k),
               dims=NT_DIMS, acc_shape=(tm, tn),
               outs=[((M, N), out_dtype, (tm, tn), lambda i, j, k: (i, j))])[0]


def _mm_tn(name, a, b, *, tm=1024, tn=1024, tk=1024):
    K, M = a.shape
    N = b.shape[1]
    tm, tn, tk = _tile(M, tm), _tile(N, tn), _tile(K, tk, 8)
    return _mm(name, a, b, grid=(M // tm, N // tn, K // tk),
               a_block=(tk, tm), a_map=lambda i, j, k: (k, i),
               b_block=(tk, tn), b_map=lambda i, j, k: (k, j),
               dims=TN_DIMS, acc_shape=(tm, tn),
               outs=[((M, N), F32, (tm, tn), lambda i, j, k: (i, j))])[0]


def _mm_nt_rmsbwd(name, a, a_block, a_map, nk, tk, w, x, g, dres):
    T, D = x.shape
    tm = a_block[-2]

    def epi(acc, ex, o, i, j):
        @pl.when(i == 0)
        def _():
            o[1][...] = jnp.zeros_like(o[1])

        dx, dgr = _rms_bwd(ex[0][...], ex[1][...], acc)
        o[0][...] = ex[2][...] + dx
        o[1][...] += jnp.sum(dgr, axis=0, keepdims=True)

    row = lambda i, j, k: (i, 0)
    vec = lambda i, j, k: (0, 0)
    return _mm(name, a, w, grid=(T // tm, 1, nk),
               a_block=a_block, a_map=a_map,
               b_block=(D, tk), b_map=lambda i, j, k: (0, k),
               dims=NT_DIMS, acc_shape=(tm, D),
               outs=[((T, D), F32, (tm, D), row), ((1, D), F32, (1, D), vec)],
               extras=[(x, (tm, D), row), (g.reshape(1, D), (1, D), vec), (dres, (tm, D), row)], epi=epi)


def _in_bwd(name, segs, w_perm, x, g, dres, S):
    T, D = x.shape
    tm = _tile(S, 512, 256)
    nst = S // tm
    tk = GROUP_QKV_W
    steps = [a.shape[1] // tk for a, _ in segs]
    starts = [sum(steps[:s]) for s in range(len(segs))]
    nk = sum(steps)
    ns = len(segs)
    cols = _chunks(D)
    assert all(a.shape[1] % tk == 0 for a, _ in segs) and nk * tk == w_perm.shape[1]

    def body(*refs):
        a_refs = refs[:ns]
        w_ref, x_ref, g_ref, dres_ref, dx_ref, dg_ref, acc = refs[ns:]
        i, k = pl.program_id(0), pl.program_id(1)

        @pl.when(k == 0)
        def _():
            acc[...] = jnp.zeros_like(acc)

        for s in range(ns):
            d = segs[s][1]

            @pl.when((k >= starts[s]) & (k < starts[s] + steps[s]))
            def _():
                prod = _dot(a_refs[s][...].reshape(tm, tk), w_ref[...], NT_DIMS)
                q = tm // d
                for c, cs in enumerate(cols):
                    if d == 1:
                        acc[c] += prod[:, cs]
                    else:
                        for r in range(d):
                            acc[c, pl.ds(r, q, stride=d), :] += prod[r * q:(r + 1) * q, cs]

        @pl.when(k == nk - 1)
        def _():
            @pl.when(i == 0)
            def _():
                dg_ref[...] = jnp.zeros_like(dg_ref)

            dh = jnp.concatenate([acc[c] for c in range(len(cols))], axis=1)
            dx, dgr = _rms_bwd(x_ref[...], g_ref[...], dh)
            dx_ref[...] = dres_ref[...] + dx
            dg_ref[...] += jnp.sum(dgr, axis=0, keepdims=True)

    def seg_spec(s):
        kmap = lambda k: jnp.clip(k - starts[s], 0, steps[s] - 1)
        d = segs[s][1]
        if d == 1:
            return pl.BlockSpec((tm, tk), lambda i, k: (i, kmap(k)))
        return pl.BlockSpec((None, d, tm // d, tk), lambda i, k: (i // nst, 0, i % nst, kmap(k)))

    views = [a if d == 1 else a.reshape(T // S, d, S // d, a.shape[1]) for a, d in segs]
    row = pl.BlockSpec((tm, D), lambda i, k: (i, 0))
    vec = pl.BlockSpec((1, D), lambda i, k: (0, 0))
    return pl.pallas_call(
        body, name=name, grid=(T // tm, nk),
        in_specs=[seg_spec(s) for s in range(ns)] + [pl.BlockSpec((D, tk), lambda i, k: (0, k)), row, vec, row],
        out_specs=[row, vec],
        out_shape=[jax.ShapeDtypeStruct((T, D), F32), jax.ShapeDtypeStruct((1, D), F32)],
        scratch_shapes=[pltpu.VMEM((D // LANE, tm, LANE), F32)],
    )(*views, w_perm, x, g.reshape(1, D), dres)


def _rope_tables(S):
    pos = jnp.arange(S, dtype=F32)
    inv_freq = jnp.exp(jnp.arange(0, ROPE_DIM, 2, dtype=F32) * (-math.log(ROPE_THETA) / ROPE_DIM))
    ang = pos[:, None] * inv_freq[None, :]
    cos, sin = jnp.cos(ang), jnp.sin(ang)
    ones = jnp.ones((S, HEAD_DIM - ROPE_DIM), F32)
    zeros_h = jnp.zeros((S, ROPE_HALF), F32)
    zeros_r = jnp.zeros((S, HEAD_DIM - ROPE_DIM), F32)
    c = jnp.concatenate([cos, cos, ones], axis=1)
    sa = jnp.concatenate([-sin, zeros_h, zeros_r], axis=1)
    sb = jnp.concatenate([zeros_h, sin, zeros_r], axis=1)
    return c, sa, sb


def _rope(t, c, sa, sb):
    return t * c + pltpu.roll(t, HEAD_DIM - ROPE_HALF, 1) * sa + pltpu.roll(t, ROPE_HALF, 1) * sb


def _rms_fwd(name, x, g):
    T, D = x.shape
    tm = _tile(T, 512, 8)

    def body(x_ref, g_ref, h_ref):
        xv = x_ref[...]
        h_ref[...] = (xv * _rstd(xv) * g_ref[...]).astype(BF16)

    return pl.pallas_call(
        body, name=name, grid=(T // tm,),
        in_specs=[pl.BlockSpec((tm, D), lambda i: (i, 0)), pl.BlockSpec((1, D), lambda i: (0, 0))],
        out_specs=pl.BlockSpec((tm, D), lambda i: (i, 0)),
        out_shape=jax.ShapeDtypeStruct((T, D), BF16))(x, g.reshape(1, D))


def _chunks(width):
    return [slice(c * LANE, (c + 1) * LANE) for c in range(width // LANE)]


def _store_residue_major(val, sc, out_ref, d, dtype):
    rows = val.shape[0]
    for c, cs in enumerate(_chunks(val.shape[1])):
        sc[c] = val[:, cs]
    for r in range(d):
        for c, cs in enumerate(_chunks(val.shape[1])):
            out_ref[r, :, cs] = sc[c, pl.ds(r, rows // d, stride=d), :].astype(dtype)


def _load_token_order(blk_ref, sc, d):
    _, q, width = blk_ref.shape
    for r in range(d):
        for c, cs in enumerate(_chunks(width)):
            sc[c, pl.ds(r, q, stride=d), :] = blk_ref[r, :, cs]
    return jnp.concatenate([sc[c] for c in range(width // LANE)], axis=1)


def _residue_major_spec(d, q, width, nst):
    return pl.BlockSpec((None, d, q, width), lambda i, *_: (i // nst, 0, i % nst, 0))


def _rms_mix_fwd(name, x, g, S):
    T, D = x.shape
    Bl = T // S
    tm = _tile(S, 512, 256)
    nst = S // tm
    dils = [d for d in DILATIONS if d > 1]

    def body(x_ref, g_ref, h_ref, *rest):
        rm_refs, sc = rest[:len(dils)], rest[len(dils)]
        xv = x_ref[...]
        hv = xv * _rstd(xv) * g_ref[...]
        h_ref[...] = hv.astype(BF16)
        for d, o_ref in zip(dils, rm_refs):
            _store_residue_major(hv, sc, o_ref, d, BF16)

    row = pl.BlockSpec((tm, D), lambda i: (i, 0))
    return pl.pallas_call(
        body, name=name, grid=(T // tm,),
        in_specs=[row, pl.BlockSpec((1, D), lambda i: (0, 0))],
        out_specs=[row] + [_residue_major_spec(d, tm // d, D, nst) for d in dils],
        out_shape=[jax.ShapeDtypeStruct((T, D), BF16)]
        + [jax.ShapeDtypeStruct((Bl, d, S // d, D), BF16) for d in dils],
        scratch_shapes=[pltpu.VMEM((D // LANE, tm, LANE), F32)])(x, g.reshape(1, D))


def _qkv_proj(name, h, w_perm, rope, S, col_off):
    T, D = h.shape
    tm = _tile(S, 1024, 8)
    tn = GROUP_W
    tk = _tile(D, 1024)
    c_t, sa_t, sb_t = rope
    n_seq_tiles = S // tm
    joff = col_off // tn
    tmap = lambda i, j, k: (i % n_seq_tiles, 0)

    def epi(acc, ex, o, i, j):
        is_rot = j < 2

        @pl.when(is_rot)
        def _():
            c, sa, sb = ex[0][...], ex[1][...], ex[2][...]
            for hh in range(HEADS):
                sl = slice(hh * HEAD_DIM, (hh + 1) * HEAD_DIM)
                o[0][:, sl] = _rope(acc[:, sl], c, sa, sb).astype(BF16)

        @pl.when(jnp.logical_not(is_rot))
        def _():
            o[0][...] = acc.astype(BF16)

    return _mm(name, h, w_perm, grid=(T // tm, 3, D // tk),
               a_block=(tm, tk), a_map=lambda i, j, k: (i, k),
               b_block=(tk, tn), b_map=lambda i, j, k: (k, j + joff),
               dims=NN_DIMS, acc_shape=(tm, tn),
               outs=[((T, GROUP_QKV_W), BF16, (tm, tn), lambda i, j, k: (i, j))],
               extras=[(c_t, (tm, HEAD_DIM), tmap), (sa_t, (tm, HEAD_DIM), tmap), (sb_t, (tm, HEAD_DIM), tmap)],
               epi=epi)[0]


def _attn_mask(n):
    qi = lax.broadcasted_iota(jnp.int32, (BLOCK, 2 * BLOCK), 0)
    ki = lax.broadcasted_iota(jnp.int32, (BLOCK, 2 * BLOCK), 1)
    diff = BLOCK + qi - ki
    return (diff >= 0) & (diff <= BLOCK) & ((n > 0) | (ki >= BLOCK))


def _attn_fwd(name, qkv, g, Bl, S):
    d = DILATIONS[g]
    L = S // d
    nb = L // BLOCK
    qv = qkv.reshape(Bl * d, L, GROUP_QKV_W)
    scale = HEAD_DIM ** -0.5

    def body(q_ref, kc_ref, vc_ref, kp_ref, vp_ref, o_ref, l_ref):
        valid = _attn_mask(pl.program_id(1))
        for hh in range(HEADS):
            sl = slice(hh * HEAD_DIM, (hh + 1) * HEAD_DIM)
            kk = jnp.concatenate([kp_ref[:, sl], kc_ref[:, sl]], axis=0)
            vv = jnp.concatenate([vp_ref[:, sl], vc_ref[:, sl]], axis=0)
            s = jnp.where(valid, _dot(q_ref[:, sl], kk, NT_DIMS) * scale, NEG_INF)
            m = jnp.max(s, axis=-1, keepdims=True)
            p = jnp.exp(s - m)
            l = jnp.sum(p, axis=-1, keepdims=True)
            o_ref[:, sl] = _dot(p, vv) / l
            l_ref[:, sl] = jnp.broadcast_to(m + jnp.log(l), (BLOCK, HEAD_DIM))

    blk = (None, BLOCK, GROUP_W)
    cur = lambda off: (lambda s, n: (s, n, off))
    prev = lambda off: (lambda s, n: (s, jnp.maximum(n - 1, 0), off))
    omap = lambda s, n: (s, n, 0)
    return pl.pallas_call(
        body, name=name, grid=(Bl * d, nb),
        in_specs=[pl.BlockSpec(blk, cur(0)), pl.BlockSpec(blk, cur(1)), pl.BlockSpec(blk, cur(2)),
                  pl.BlockSpec(blk, prev(1)), pl.BlockSpec(blk, prev(2))],
        out_specs=[pl.BlockSpec(blk, omap), pl.BlockSpec(blk, omap)],
        out_shape=[jax.ShapeDtypeStruct((Bl * d, L, GROUP_W), F32)] * 2)(qv, qv, qv, qv, qv)


def _merge_weights(l0, l1, l2):
    mx = jnp.maximum(jnp.maximum(l0, l1), l2)
    e0, e1, e2 = jnp.exp(l0 - mx), jnp.exp(l1 - mx), jnp.exp(l2 - mx)
    inv = 1.0 / (e0 + e1 + e2)
    return e0 * inv, e1 * inv, e2 * inv


def _group_specs(tm, nst, S):
    specs = []
    for d in DILATIONS:
        specs.append(pl.BlockSpec((tm, GROUP_W), lambda i: (i, 0)) if d == 1
                     else _residue_major_spec(d, tm // d, GROUP_W, nst))

    def views(arrs):
        out = []
        for d, a in zip(DILATIONS, arrs):
            out.append(a.reshape(-1, GROUP_W) if d == 1 else a.reshape(-1, d, S // d, GROUP_W))
        return out

    return specs, views


def _group_scratch(tm, per_group):
    n = per_group * sum(1 for d in DILATIONS if d > 1)
    return [pltpu.VMEM((GROUP_W // LANE, tm, LANE), F32) for _ in range(n)]


def _group_values(o_refs, l_refs, scs):
    scs = list(scs)
    ov, lv = [], []
    for d, o_ref, l_ref in zip(DILATIONS, o_refs, l_refs):
        if d == 1:
            ov.append(o_ref[...])
            lv.append(l_ref[...])
        else:
            ov.append(_load_token_order(o_ref, scs.pop(), d))
            lv.append(_load_token_order(l_ref, scs.pop(), d))
    return ov, lv


def _pool_inv_count(tseq, w):
    return 1.0 / jnp.minimum(tseq + 1, w).astype(F32)


def _mix_out_fwd(name, x, zr, o_l, lse_l, w_ya, pool_w, pool_scale, w_yb, w_o, S):
    T, D = x.shape
    gw = D // len(POOL_WINDOWS)
    tm = _tile(S, 256, POOL_HALO)
    nst = S // tm
    hpt = tm // POOL_HALO

    def body(x_ref, u_ref, uh_ref, ga_ref, gb_ref, o0, o1, o2, l0, l1, l2, wya_ref, pw_ref, ps_ref, wyb_ref, wo_ref,
             x1_ref, attn_ref, pooled_ref, mixed_ref, ya_ref, yb_ref, merged_ref, *scs):
        it = pl.program_id(0) % nst
        ov, lv = _group_values((o0, o1, o2), (l0, l1, l2), scs)
        w0, w1, w2 = _merge_weights(*lv)
        attn = w0 * ov[0] + w1 * ov[1] + w2 * ov[2]
        attn_ref[...] = attn.astype(BF16)
        y_a = _dot(attn, wya_ref[...])

        u = u_ref[...]
        halo = uh_ref[...] * jnp.where(it == 0, 0.0, 1.0)
        ext = jnp.concatenate([halo, u], axis=0)
        tseq = it * tm + lax.broadcasted_iota(jnp.int32, (tm, 1), 0)
        pm_parts = []
        for gi, w in enumerate(POOL_WINDOWS):
            cs = slice(gi * gw, (gi + 1) * gw)
            s = ext[:, cs]
            step = 1
            while step < w:
                s = s + pltpu.roll(s, step, 0)
                step *= 2
            pooled_g = (s[POOL_HALO:, :] * _pool_inv_count(tseq, w) - u[:, cs]).astype(BF16)
            pooled_ref[:, cs] = pooled_g
            pm_parts.append(_dot(pooled_g, pw_ref[gi]))
        mixed = (jnp.concatenate(pm_parts, axis=1) * ps_ref[...]).astype(BF16)
        mixed_ref[...] = mixed
        y_b = _dot(mixed, wyb_ref[...])
        merged = (_sigmoid(ga_ref[...]) * y_a + _sigmoid(gb_ref[...]) * y_b).astype(BF16)
        ya_ref[...] = y_a
        yb_ref[...] = y_b
        merged_ref[...] = merged
        x1_ref[...] = x_ref[...] + _dot(merged, wo_ref[...])

    row = lambda c: pl.BlockSpec((tm, D), lambda i: (i, c))
    row512 = pl.BlockSpec((tm, GROUP_W), lambda i: (i, 0))
    ps = pool_scale.reshape(1, D)
    halo_spec = pl.BlockSpec((POOL_HALO, D), lambda i: (jnp.maximum(i * hpt - 1, 0), 0))
    grp_specs, grp_views = _group_specs(tm, nst, S)
    return pl.pallas_call(
        body, name=name, grid=(T // tm,),
        in_specs=[row(0), row(0), halo_spec, row(1), row(2)] + grp_specs * 2
        + [_full(w_ya), _full(pool_w), _full(ps), _full(w_yb), _full(w_o)],
        out_specs=[row(0), row512, row(0), row(0), row(0), row(0), row(0)],
        out_shape=[jax.ShapeDtypeStruct((T, D), F32), jax.ShapeDtypeStruct((T, GROUP_W), BF16),
                   jax.ShapeDtypeStruct((T, D), BF16), jax.ShapeDtypeStruct((T, D), BF16),
                   jax.ShapeDtypeStruct((T, D), F32), jax.ShapeDtypeStruct((T, D), F32),
                   jax.ShapeDtypeStruct((T, D), BF16)],
        scratch_shapes=_group_scratch(tm, 2),
    )(x, zr, zr, zr, zr, *grp_views(o_l), *grp_views(lse_l), w_ya, pool_w, ps, w_yb, w_o)


def _up_proj(name, h2, w_up, F):
    T, D = h2.shape
    tm, tn, tk = _tile(T, 1024, 8), _tile(F, 1408), _tile(D, 512)
    njh = F // tn
    return _mm(name, h2, w_up, grid=(T // tm, 2 * njh, D // tk),
               a_block=(tm, tk), a_map=lambda i, j, k: (i, k),
               b_block=(tk, tn), b_map=lambda i, j, k: (k, j),
               dims=NN_DIMS, acc_shape=(tm, tn),
               outs=[((2, T, F), F32, (None, tm, tn), lambda i, j, k: (j // njh, i, j % njh))])[0]


def _conv_y(ext, w_ref, b_ref):
    return (b_ref[...] + w_ref[2:3, :] * ext + w_ref[1:2, :] * pltpu.roll(ext, 1, 0)
            + w_ref[0:1, :] * pltpu.roll(ext, 2, 0))


def _conv_params(conv_w, conv_b, F):
    cw = conv_w.reshape(3, 2, F).transpose(1, 0, 2)
    return cw, conv_b.reshape(2, 1, F)


def _ffn_act_fwd(name, u, conv_w, conv_b, S):
    _, T, F = u.shape
    tm = _tile(S, 512, CONV_HALO)
    tf = _tile(F, 1408)
    nst = S // tm
    hpt = tm // CONV_HALO
    cw, cb = _conv_params(conv_w, conv_b, F)

    def body(ug_ref, uv_ref, hg_ref, hv_ref, wg_ref, wv_ref, bg_ref, bv_ref, a_ref):
        keep = jnp.where(pl.program_id(0) % nst == 0, 0.0, 1.0)
        yg = _conv_y(jnp.concatenate([hg_ref[...] * keep, ug_ref[...]], axis=0), wg_ref, bg_ref)[CONV_HALO:, :]
        yv = _conv_y(jnp.concatenate([hv_ref[...] * keep, uv_ref[...]], axis=0), wv_ref, bv_ref)[CONV_HALO:, :]
        a_ref[...] = (yg * _sigmoid(yg) * yv).astype(BF16)

    main = lambda h: pl.BlockSpec((None, tm, tf), lambda i, j: (h, i, j))
    halo = lambda h: pl.BlockSpec((None, CONV_HALO, tf), lambda i, j: (h, jnp.maximum(i * hpt - 1, 0), j))
    wsp = lambda h: pl.BlockSpec((None, 3, tf), lambda i, j: (h, 0, j))
    bsp = lambda h: pl.BlockSpec((None, 1, tf), lambda i, j: (h, 0, j))
    return pl.pallas_call(
        body, name=name, grid=(T // tm, F // tf),
        in_specs=[main(0), main(1), halo(0), halo(1), wsp(0), wsp(1), bsp(0), bsp(1)],
        out_specs=pl.BlockSpec((tm, tf), lambda i, j: (i, j)),
        out_shape=jax.ShapeDtypeStruct((T, F), BF16))(u, u, u, u, cw, cw, cb, cb)


def _ple_fwd(name, x2, p, g_ple, w_gate, w_ple):
    T, D = x2.shape
    P = p.shape[1]
    tm = _tile(T, 512, 8)

    def body(x_ref, p_ref, g_ref, wg_ref, wp_ref, x3_ref, e_ref, pg_ref, pbf_ref):
        xv = x_ref[...]
        h3 = xv * _rstd(xv) * g_ref[...]
        pg = _sigmoid(_dot(h3, wg_ref[...]))
        pb = p_ref[...].astype(BF16)
        e = _dot(pb, wp_ref[...])
        x3_ref[...] = xv + e * pg
        e_ref[...] = e
        pg_ref[...] = pg
        pbf_ref[...] = pb

    row = pl.BlockSpec((tm, D), lambda i: (i, 0))
    prow = pl.BlockSpec((tm, P), lambda i: (i, 0))
    g2 = g_ple.reshape(1, D)
    return pl.pallas_call(
        body, name=name, grid=(T // tm,),
        in_specs=[row, prow, _full(g2), _full(w_gate), _full(w_ple)],
        out_specs=[row, row, row, prow],
        out_shape=[jax.ShapeDtypeStruct((T, D), F32)] * 3 + [jax.ShapeDtypeStruct((T, P), BF16)],
    )(x2, p, g2, w_gate, w_ple)


def _loss_bwd(name, xf, target, g_final):
    T, D = xf.shape
    tm = _tile(T, 512, 8)
    nt = T // tm

    def body(x_ref, t_ref, g_ref, dx_ref, loss_ref, dg_ref, lacc):
        i = pl.program_id(0)

        @pl.when(i == 0)
        def _():
            lacc[...] = jnp.zeros_like(lacc)
            dg_ref[...] = jnp.zeros_like(dg_ref)
            loss_ref[...] = jnp.zeros_like(loss_ref)

        xv = x_ref[...]
        g = g_ref[...]
        diff = xv * _rstd(xv) * g - t_ref[...]
        lacc[...] += jnp.sum(diff * diff, axis=0, keepdims=True)
        dx, dgr = _rms_bwd(xv, g, diff * (1.0 / D))
        dx_ref[...] = dx
        dg_ref[...] += jnp.sum(dgr, axis=0, keepdims=True)

        @pl.when(i == nt - 1)
        def _():
            tot = jnp.sum(lacc[...], axis=-1, keepdims=True) * (0.5 / D)
            loss_ref[...] = jnp.broadcast_to(tot, (1, LANE))

    row = pl.BlockSpec((tm, D), lambda i: (i, 0))
    vec = pl.BlockSpec((1, D), lambda i: (0, 0))
    return pl.pallas_call(
        body, name=name, grid=(nt,),
        in_specs=[row, row, vec],
        out_specs=[row, pl.BlockSpec((1, LANE), lambda i: (0, 0)), vec],
        out_shape=[jax.ShapeDtypeStruct((T, D), F32), jax.ShapeDtypeStruct((1, LANE), F32),
                   jax.ShapeDtypeStruct((1, D), F32)],
        scratch_shapes=[pltpu.VMEM((1, D), F32)])(xf, target, g_final.reshape(1, D))


def _ple_bwd(name, dx3, x2, e, pg, g_ple, w_gate):
    T, D = x2.shape
    tm = _tile(T, 512, 8)

    def body(dx3_ref, x_ref, e_ref, pg_ref, g_ref, wg_ref, dx2_ref, de_ref, ds_ref, h3_ref, dg_ref):
        @pl.when(pl.program_id(0) == 0)
        def _():
            dg_ref[...] = jnp.zeros_like(dg_ref)

        dx3v, xv, pgv, g = dx3_ref[...], x_ref[...], pg_ref[...], g_ref[...]
        de_ref[...] = (dx3v * pgv).astype(BF16)
        ds = (dx3v * e_ref[...] * pgv * (1.0 - pgv)).astype(BF16)
        ds_ref[...] = ds
        dh3 = _dot(ds, wg_ref[...], NT_DIMS)
        h3_ref[...] = (xv * _rstd(xv) * g).astype(BF16)
        dx, dgr = _rms_bwd(xv, g, dh3)
        dx2_ref[...] = dx3v + dx
        dg_ref[...] += jnp.sum(dgr, axis=0, keepdims=True)

    row = pl.BlockSpec((tm, D), lambda i: (i, 0))
    vec = pl.BlockSpec((1, D), lambda i: (0, 0))
    return pl.pallas_call(
        body, name=name, grid=(T // tm,),
        in_specs=[row, row, row, row, vec, _full(w_gate)],
        out_specs=[row, row, row, row, vec],
        out_shape=[jax.ShapeDtypeStruct((T, D), F32)] + [jax.ShapeDtypeStruct((T, D), BF16)] * 3
        + [jax.ShapeDtypeStruct((1, D), F32)])(dx3, x2, e, pg, g_ple.reshape(1, D), w_gate)


def _ffn_act_bwd(name, u, d_a, conv_w, conv_b, S):
    _, T, F = u.shape
    H = CONV_HALO
    tm = _tile(S, 512, H)
    tf = _tile(F, 1408)
    nst = S // tm
    hpt = tm // H
    last_halo = T // H - 1
    n_ext = tm + H
    cw, cb = _conv_params(conv_w, conv_b, F)

    def body(ug_ref, uv_ref, pg_ref, pv_ref, ng_ref, nv_ref, da_ref, dan_ref, wg_ref, wv_ref, bg_ref, bv_ref,
             du_ref, dw_ref, db_ref):
        i = pl.program_id(1)
        it = i % nst

        @pl.when(i == 0)
        def _():
            dw_ref[...] = jnp.zeros_like(dw_ref)
            db_ref[...] = jnp.zeros_like(db_ref)

        keep_prev = jnp.where(it == 0, 0.0, 1.0)
        keep_next = jnp.where(it == nst - 1, 0.0, 1.0)
        ext_g = jnp.concatenate([pg_ref[...] * keep_prev, ug_ref[...], ng_ref[...]], axis=0)
        ext_v = jnp.concatenate([pv_ref[...] * keep_prev, uv_ref[...], nv_ref[...]], axis=0)
        yg = _conv_y(ext_g, wg_ref, bg_ref)[H:, :]
        yv = _conv_y(ext_v, wv_ref, bv_ref)[H:, :]
        rows = lax.broadcasted_iota(jnp.int32, (n_ext, 1), 0)
        live = jnp.where(rows < tm, 1.0, keep_next)
        da = jnp.concatenate([da_ref[...], dan_ref[...]], axis=0) * live
        sg = _sigmoid(yg)
        dyv = da * (yg * sg)
        dyg = da * yv * (sg * (1.0 + yg * (1.0 - sg)))
        for half, (dy, ext, w_ref) in enumerate(((dyg, ext_g, wg_ref), (dyv, ext_v, wv_ref))):
            du = (w_ref[2:3, :] * dy + w_ref[1:2, :] * pltpu.roll(dy, n_ext - 1, 0)
                  + w_ref[0:1, :] * pltpu.roll(dy, n_ext - 2, 0))
            du_ref[half] = du[:tm, :].astype(BF16)
            dym = dy[:tm, :]
            db_ref[half] += jnp.sum(dym, axis=0, keepdims=True)
            dw_ref[half, 2:3, :] += jnp.sum(dym * ext[H:H + tm, :], axis=0, keepdims=True)
            dw_ref[half, 1:2, :] += jnp.sum(dym * pltpu.roll(ext, 1, 0)[H:H + tm, :], axis=0, keepdims=True)
            dw_ref[half, 0:1, :] += jnp.sum(dym * pltpu.roll(ext, 2, 0)[H:H + tm, :], axis=0, keepdims=True)

    main = lambda h: pl.BlockSpec((None, tm, tf), lambda j, i: (h, i, j))
    prev = lambda h: pl.BlockSpec((None, H, tf), lambda j, i: (h, jnp.maximum(i * hpt - 1, 0), j))
    nxt = lambda h: pl.BlockSpec((None, H, tf), lambda j, i: (h, jnp.minimum((i + 1) * hpt, last_halo), j))
    wsp = lambda h: pl.BlockSpec((None, 3, tf), lambda j, i: (h, 0, j))
    bsp = lambda h: pl.BlockSpec((None, 1, tf), lambda j, i: (h, 0, j))
    return pl.pallas_call(
        body, name=name, grid=(F // tf, T // tm),
        in_specs=[main(0), main(1), prev(0), prev(1), nxt(0), nxt(1),
                  pl.BlockSpec((tm, tf), lambda j, i: (i, j)),
                  pl.BlockSpec((H, tf), lambda j, i: (jnp.minimum((i + 1) * hpt, last_halo), j)),
                  wsp(0), wsp(1), bsp(0), bsp(1)],
        out_specs=[pl.BlockSpec((2, tm, tf), lambda j, i: (0, i, j)),
                   pl.BlockSpec((2, 3, tf), lambda j, i: (0, 0, j)),
                   pl.BlockSpec((2, 1, tf), lambda j, i: (0, 0, j))],
        out_shape=[jax.ShapeDtypeStruct((2, T, F), BF16), jax.ShapeDtypeStruct((2, 3, F), F32),
                   jax.ShapeDtypeStruct((2, 1, F), F32)],
    )(u, u, u, u, u, u, d_a, d_a, cw, cw, cb, cb)


def _mix_out_bwd(name, dx1, zr, y_a, y_b, o_l, lse_l, pooled, w_o, w_ya, w_yb, pool_w, pool_scale, S):
    T, D = dx1.shape
    gw = D // len(POOL_WINDOWS)
    H = POOL_HALO
    tm = _tile(S, 256, H)
    nst = S // tm
    hpt = tm // H
    last_halo = T // H - 1
    n_ext = tm + H

    def body(dx_ref, dxn_ref, ga_ref, gb_ref, gbn_ref, ya_ref, yb_ref, o0, o1, o2, l0, l1, l2, pooled_ref,
             wo_ref, wya_ref, wyb_ref, pw_ref, ps_ref,
             dz_ref, dya_ref, dyb_ref, dpm_ref, do0, do1, do2, c0, c1, c2, dps_ref, *scs):
        i = pl.program_id(0)
        it = i % nst

        @pl.when(i == 0)
        def _():
            dps_ref[...] = jnp.zeros_like(dps_ref)

        keep_next = jnp.where(it == nst - 1, 0.0, 1.0)
        dm_e = _dot(jnp.concatenate([dx_ref[...], dxn_ref[...]], axis=0), wo_ref[...], NT_DIMS)
        sgb_e = _sigmoid(jnp.concatenate([gb_ref[...], gbn_ref[...]], axis=0))
        dyb_e = dm_e * sgb_e
        dm = dm_e[:tm, :]
        sga = _sigmoid(ga_ref[...])
        sgb = sgb_e[:tm, :]
        d_ga = dm * ya_ref[...] * (sga * (1.0 - sga))
        d_gb = dm * yb_ref[...] * (sgb * (1.0 - sgb))
        dya = (dm * sga).astype(BF16)
        dya_ref[...] = dya
        dyb_ref[...] = dyb_e[:tm, :].astype(BF16)
        dmixed_e = _dot(dyb_e, wyb_ref[...], NT_DIMS)

        rows = lax.broadcasted_iota(jnp.int32, (n_ext, 1), 0)
        tseq = it * tm + rows
        live = jnp.where(rows < tm, 1.0, keep_next)
        ps = ps_ref[...]
        du_parts = []
        for gi, w in enumerate(POOL_WINDOWS):
            cs = slice(gi * gw, (gi + 1) * gw)
            pm_g = _dot(pooled_ref[:, cs], pw_ref[gi])
            dps_ref[:, cs] += jnp.sum(dmixed_e[:tm, cs] * pm_g, axis=0, keepdims=True)
            dpm_e = (dmixed_e[:, cs] * ps[:, cs]).astype(BF16)
            dpm_ref[:, cs] = dpm_e[:tm, :]
            dpooled_e = _dot(dpm_e, pw_ref[gi], NT_DIMS)
            s = dpooled_e * (_pool_inv_count(tseq, w) * live)
            step = 1
            while step < w:
                s = s + pltpu.roll(s, n_ext - step, 0)
                step *= 2
            du_parts.append(s[:tm, :] - dpooled_e[:tm, :])
        dz_ref[...] = jnp.concatenate(du_parts + [d_ga, d_gb], axis=1).astype(BF16)

        d_attn = _dot(dya, wya_ref[...], NT_DIMS)
        ov, lv = _group_values((o0, o1, o2), (l0, l1, l2), scs[:n_in_sc])
        ws = _merge_weights(*lv)
        prod = d_attn * (ws[0] * ov[0] + ws[1] * ov[1] + ws[2] * ov[2])
        rs = jnp.concatenate(
            [jnp.broadcast_to(jnp.sum(prod[:, hh * HEAD_DIM:(hh + 1) * HEAD_DIM], axis=-1, keepdims=True),
                              (tm, HEAD_DIM)) for hh in range(HEADS)], axis=1)
        out_scs = list(scs[n_in_sc:])
        for d, wg, do_ref, c_ref in zip(DILATIONS, ws, (do0, do1, do2), (c0, c1, c2)):
            if d == 1:
                do_ref[...] = (wg * d_attn).astype(BF16)
                c_ref[...] = -wg * rs
            else:
                _store_residue_major(wg * d_attn, out_scs.pop(), do_ref, d, BF16)
                _store_residue_major(-wg * rs, out_scs.pop(), c_ref, d, F32)

    row = lambda c: pl.BlockSpec((tm, D), lambda i: (i, c))
    nxt = lambda c: pl.BlockSpec((H, D), lambda i: (jnp.minimum((i + 1) * hpt, last_halo), c))
    ps2 = pool_scale.reshape(1, D)
    bf = lambda w: jax.ShapeDtypeStruct((T, w), BF16)
    grp_specs, grp_views = _group_specs(tm, nst, S)
    grp_shape = lambda dt: [jax.ShapeDtypeStruct((T, GROUP_W) if d == 1 else (T // S, d, S // d, GROUP_W), dt)
                            for d in DILATIONS]
    n_in_sc = len(_group_scratch(tm, 2))
    return pl.pallas_call(
        body, name=name, grid=(T // tm,),
        in_specs=[row(0), nxt(0), row(1), row(2), nxt(2), row(0), row(0)] + grp_specs * 2 + [row(0)]
        + [_full(w_o), _full(w_ya), _full(w_yb), _full(pool_w), _full(ps2)],
        out_specs=[pl.BlockSpec((tm, 3 * D), lambda i: (i, 0)), row(0), row(0), row(0)] + grp_specs * 2
        + [pl.BlockSpec((1, D), lambda i: (0, 0))],
        out_shape=[bf(3 * D), bf(D), bf(D), bf(D)] + grp_shape(BF16) + grp_shape(F32)
        + [jax.ShapeDtypeStruct((1, D), F32)],
        scratch_shapes=_group_scratch(tm, 4),
    )(dx1, dx1, zr, zr, zr, y_a, y_b, *grp_views(o_l), *grp_views(lse_l), pooled, w_o, w_ya, w_yb, pool_w, ps2)


def _attn_bwd(name, qkv, d_o, lse, cst, rope, g, Bl, S):
    d = DILATIONS[g]
    L = S // d
    nb = L // BLOCK
    qv = qkv.reshape(Bl * d, L, GROUP_QKV_W)
    dov = d_o.reshape(Bl * d, L, GROUP_W)
    lv = lse.reshape(Bl * d, L, GROUP_W)
    cv = cst.reshape(Bl * d, L, GROUP_W)
    tabs = [t.reshape(d, L, HEAD_DIM) for t in rope]
    scale = HEAD_DIM ** -0.5

    def body(q_ref, qn_ref, kp_ref, kc_ref, vp_ref, vc_ref, do_ref, don_ref, l_ref, ln_ref, c_ref, cn_ref,
             cos_ref, sa_ref, sb_ref, out_ref):
        n = pl.program_id(1)
        valid = _attn_mask(n)
        qi = lax.broadcasted_iota(jnp.int32, (BLOCK, BLOCK), 0)
        ki = lax.broadcasted_iota(jnp.int32, (BLOCK, BLOCK), 1)
        valid_n = (ki >= qi) & (n + 1 < nb)
        cos, sa, sb = cos_ref[...], -sa_ref[...], -sb_ref[...]
        for hh in range(HEADS):
            sl = slice(hh * HEAD_DIM, (hh + 1) * HEAD_DIM)
            q, qn, kc, vc, do, don = q_ref[:, sl], qn_ref[:, sl], kc_ref[:, sl], vc_ref[:, sl], do_ref[:, sl], don_ref[:, sl]
            kk = jnp.concatenate([kp_ref[:, sl], kc], axis=0)
            vv = jnp.concatenate([vp_ref[:, sl], vc], axis=0)
            col = slice(hh * HEAD_DIM, hh * HEAD_DIM + 1)
            s = jnp.where(valid, _dot(q, kk, NT_DIMS) * scale, NEG_INF)
            p = jnp.exp(s - l_ref[:, col])
            ds = p * (_dot(do, vv, NT_DIMS) + c_ref[:, col])
            dq = _dot(ds, kk) * scale
            s2 = jnp.where(valid_n, _dot(qn, kc, NT_DIMS) * scale, NEG_INF)
            p2 = jnp.exp(s2 - ln_ref[:, col])
            ds2 = p2 * (_dot(don, vc, NT_DIMS) + cn_ref[:, col])
            dk = (_dot(ds[:, BLOCK:], q, TN_DIMS) + _dot(ds2, qn, TN_DIMS)) * scale
            dv = _dot(p[:, BLOCK:], do, TN_DIMS) + _dot(p2, don, TN_DIMS)
            out_ref[:, sl] = _rope(dq, cos, sa, sb).astype(BF16)
            out_ref[:, GROUP_W + hh * HEAD_DIM:GROUP_W + (hh + 1) * HEAD_DIM] = _rope(dk, cos, sa, sb).astype(BF16)
            out_ref[:, 2 * GROUP_W + hh * HEAD_DIM:2 * GROUP_W + (hh + 1) * HEAD_DIM] = dv.astype(BF16)

    blk = (None, BLOCK, GROUP_W)
    at = lambda f, off: pl.BlockSpec(blk, lambda s, n: (s, f(n), off))
    cur = lambda n: n
    prv = lambda n: jnp.maximum(n - 1, 0)
    nxt = lambda n: jnp.minimum(n + 1, nb - 1)
    tok = lambda f: pl.BlockSpec(blk, lambda s, n: (s, f(n), 0))
    tab = pl.BlockSpec((None, BLOCK, HEAD_DIM), lambda s, n: (s % d, n, 0))
    out = pl.pallas_call(
        body, name=name, grid=(Bl * d, nb),
        in_specs=[at(cur, 0), at(nxt, 0), at(prv, 1), at(cur, 1), at(prv, 2), at(cur, 2),
                  tok(cur), tok(nxt), tok(cur), tok(nxt), tok(cur), tok(nxt), tab, tab, tab],
        out_specs=pl.BlockSpec((None, BLOCK, GROUP_QKV_W), lambda s, n: (s, n, 0)),
        out_shape=jax.ShapeDtypeStruct((Bl * d, L, GROUP_QKV_W), BF16),
    )(qv, qv, qv, qv, qv, qv, dov, dov, lv, lv, cv, cv, *tabs)
    return out.reshape(Bl * S, GROUP_QKV_W)


def _pool_w_grad(name, pooled, d_pm, gw):
    T = pooled.shape[0]
    ng = len(POOL_WINDOWS)
    tk = _tile(T, 1024, 8)
    return _mm(name, pooled, d_pm, grid=(ng, 1, T // tk),
               a_block=(tk, gw), a_map=lambda i, j, k: (k, i),
               b_block=(tk, gw), b_map=lambda i, j, k: (k, i),
               dims=TN_DIMS, acc_shape=(gw, gw),
               outs=[((ng, gw, gw), F32, (None, gw, gw), lambda i, j, k: (i, 0, 0))])[0]


def _up_w_grad(name, h2, du):
    T, D = h2.shape
    F = du.shape[2]
    tm, tn, tk = _tile(D, 1024), _tile(F, 1408), _tile(T, 1024, 8)
    njh = F // tn
    return _mm(name, h2, du, grid=(D // tm, 2 * njh, T // tk),
               a_block=(tk, tm), a_map=lambda i, j, k: (k, i),
               b_block=(None, tk, tn), b_map=lambda i, j, k: (j // njh, k, j % njh),
               dims=TN_DIMS, acc_shape=(tm, tn),
               outs=[((D, 2 * F), F32, (tm, tn), lambda i, j, k: (i, j))])[0]


def _adamw(name, w, m, v, pieces):
    R, C = w.shape
    nl = len(pieces)
    rl = R // nl
    if nl > 1 and rl % 8:
        per = [_adamw(f"{name}_{l}", w[l * rl:(l + 1) * rl], m[l * rl:(l + 1) * rl], v[l * rl:(l + 1) * rl],
                      [pieces[l]]) for l in range(nl)]
        return [jnp.concatenate([p[o] for p in per], axis=0) for o in range(4)]
    tr = _tile(rl, max(PACK_ROWS, (1 << 18) // C // PACK_ROWS * PACK_ROWS), PACK_ROWS)
    nbl = rl // tr
    c1 = 1.0 - ADAM_B1 ** ADAM_STEP
    c2 = 1.0 - ADAM_B2 ** ADAM_STEP

    def body(w_ref, m_ref, v_ref, *rest):
        p_refs = rest[:nl]
        g_ref, d_ref, mo_ref, vo_ref = rest[nl:]
        i = pl.program_id(0)
        for l in range(nl):
            @pl.when((i >= l * nbl) & (i < (l + 1) * nbl))
            def _():
                g = p_refs[l][0].astype(F32)
                for dev in range(1, N_DEV):
                    g = g + p_refs[l][dev].astype(F32)
                mn = ADAM_B1 * m_ref[...] + (1.0 - ADAM_B1) * g
                vn = ADAM_B2 * v_ref[...] + (1.0 - ADAM_B2) * (g * g)
                g_ref[...] = g
                mo_ref[...] = mn
                vo_ref[...] = vn
                d_ref[...] = -ADAM_LR * ((mn / c1) / (jnp.sqrt(vn / c2) + ADAM_EPS) + ADAM_WD * w_ref[...])

    row = pl.BlockSpec((tr, C), lambda i: (i, 0))
    piece = lambda l: pl.BlockSpec((N_DEV, tr, C), lambda i: (0, jnp.clip(i - l * nbl, 0, nbl - 1), 0))
    return pl.pallas_call(
        body, name=name, grid=(R // tr,),
        in_specs=[row, row, row] + [piece(l) for l in range(nl)],
        out_specs=[row] * 4,
        out_shape=[jax.ShapeDtypeStruct((R, C), F32)] * 4)(w, m, v, *pieces)


def _my_index():
    return 4 * lax.axis_index("x") + 2 * lax.axis_index("y") + lax.axis_index("c")


def _all_gather(name, mine):
    na = len(mine)

    def body(*refs):
        x_refs, out_refs, token = refs[:na], refs[na:2 * na], refs[2 * na]
        send_sems, recv_sems, local_sems = refs[2 * na + 1:]
        token[...] = jnp.zeros_like(token)
        x, y, c = lax.axis_index("x"), lax.axis_index("y"), lax.axis_index("c")
        me, sibling = (x, y, c), (x, y, 1 - c)
        chips = [(1 - x, y), (x, 1 - y), (1 - x, 1 - y)]

        def slot(a, px, py, pc):
            return out_refs[a].at[4 * px + 2 * py + pc]

        def copy(a, k, block, to, src=None):
            return pltpu.make_async_remote_copy(
                src_ref=slot(a, *block) if src is None else src, dst_ref=slot(a, *block),
                send_sem=send_sems.at[7 * a + k], recv_sem=recv_sems.at[7 * a + k],
                device_id=to, device_id_type=MESH_ID)

        own = [pltpu.make_async_copy(x_refs[a], slot(a, *me), local_sems.at[a]) for a in range(na)]
        for cp in own:
            cp.start()
        first = []
        for a in range(na):
            first.append(copy(a, 0, me, sibling, src=x_refs[a]))
            first += [copy(a, 1 + j, me, (*chip, c), src=x_refs[a]) for j, chip in enumerate(chips)]
        for cp in first:
            cp.start()
        passed = []
        for j, chip in enumerate(chips):
            for a in range(na):
                copy(a, 1 + j, (*chip, c), me).wait_recv()
                fwd = copy(a, 4 + j, (*chip, c), sibling)
                fwd.start()
                passed.append(fwd)
        for a in range(na):
            copy(a, 0, sibling, me).wait_recv()
            for j, chip in enumerate(chips):
                copy(a, 4 + j, (*chip, 1 - c), me).wait_recv()
        for cp in first + passed:
            cp.wait_send()
        for cp in own:
            cp.wait()

    res = pl.pallas_call(
        body, name=name,
        in_specs=[pl.BlockSpec(memory_space=pl.ANY)] * na,
        out_specs=[pl.BlockSpec(memory_space=pl.ANY)] * na + [pl.BlockSpec(memory_space=pltpu.VMEM)],
        out_shape=[jax.ShapeDtypeStruct((N_DEV,) + m.shape, m.dtype) for m in mine]
        + [jax.ShapeDtypeStruct((8, LANE), F32)],
        scratch_shapes=[pltpu.SemaphoreType.DMA((7 * na,)), pltpu.SemaphoreType.DMA((7 * na,)),
                        pltpu.SemaphoreType.DMA((na,))],
    )(*mine)
    return res[:na], res[na]


_HBM_SPEC = pl.BlockSpec(memory_space=pltpu.HBM)
_SEM_SPEC = pl.BlockSpec(memory_space=pltpu.SEMAPHORE)
_SIDE_EFFECT = pltpu.SideEffectType.DATAFLOW_SIDE_EFFECTING


def _peer_of(k):
    x, y, c = lax.axis_index("x"), lax.axis_index("y"), lax.axis_index("c")
    px = 1 - x if k & 4 else x
    py = 1 - y if k & 2 else y
    pc = 1 - c if k & 1 else c
    return (px, py, pc), 4 * px + 2 * py + pc


def _send_start(name, srcs, pieces):
    na = len(srcs)
    land_shapes = [s.shape if pieces else (N_DEV,) + s.shape for s in srcs]
    lands = [lax.empty(shp, s.dtype) for shp, s in zip(land_shapes, srcs)]

    def body(*refs):
        src_refs, land_refs = refs[:na], refs[na:2 * na]
        send_sems, recv_sems, token = refs[2 * na], refs[2 * na + 1], refs[4 * na + 2]
        me = 4 * lax.axis_index("x") + 2 * lax.axis_index("y") + lax.axis_index("c")
        for k in range(1, N_DEV):
            to, pidx = _peer_of(k)
            for a in range(na):
                pltpu.make_async_remote_copy(
                    src_ref=src_refs[a].at[pidx] if pieces else src_refs[a], dst_ref=land_refs[a].at[me],
                    send_sem=send_sems.at[7 * a + k - 1], recv_sem=recv_sems.at[7 * a + k - 1],
                    device_id=to, device_id_type=MESH_ID).start()
        token[...] = jnp.zeros_like(token)

    hbm = lambda arrs: [pltpu.HBM(a.shape, a.dtype) for a in arrs]
    outs = pl.pallas_call(
        body, name=name,
        out_shape=(pltpu.SemaphoreType.DMA((7 * na,)), pltpu.SemaphoreType.DMA((7 * na,)), *hbm(srcs), *hbm(lands),
                   jax.ShapeDtypeStruct((8, LANE), F32)),
        in_specs=[_HBM_SPEC] * (2 * na),
        out_specs=(_SEM_SPEC, _SEM_SPEC, *([_HBM_SPEC] * (2 * na)), pl.BlockSpec(memory_space=pltpu.VMEM)),
        input_output_aliases={i: 2 + i for i in range(2 * na)},
        compiler_params=pltpu.CompilerParams(has_side_effects=_SIDE_EFFECT),
    )(*[pltpu.with_memory_space_constraint(s, pltpu.HBM) for s in srcs],
      *[pltpu.with_memory_space_constraint(l, pltpu.HBM) for l in lands])
    return outs[0], outs[1], outs[2:2 + na], outs[2 + na:2 + 2 * na], outs[-1]


def _send_wait(name, send_sems, recv_sems, srcs, lands, pieces, after):
    na = len(srcs)

    def body(*refs):
        src_refs, land_refs = refs[:na], refs[na:2 * na]
        send_sems, recv_sems = refs[2 * na], refs[2 * na + 1]
        for k in range(1, N_DEV):
            to, pidx = _peer_of(k)
            for a in range(na):
                cp = pltpu.make_async_remote_copy(
                    src_ref=src_refs[a].at[pidx] if pieces else src_refs[a], dst_ref=land_refs[a].at[pidx],
                    send_sem=send_sems.at[7 * a + k - 1], recv_sem=recv_sems.at[7 * a + k - 1],
                    device_id=to, device_id_type=MESH_ID)
                cp.wait_send()
                cp.wait_recv()

    hbm = lambda arrs: [pltpu.HBM(a.shape, a.dtype) for a in arrs]
    outs = pl.pallas_call(
        body, name=name,
        out_shape=tuple(hbm(srcs) + hbm(lands)),
        in_specs=[_HBM_SPEC] * (2 * na) + [_SEM_SPEC, _SEM_SPEC, pl.BlockSpec(memory_space=pl.ANY)],
        out_specs=tuple([_HBM_SPEC] * (2 * na)),
        input_output_aliases={i: i for i in range(2 * na)},
        compiler_params=pltpu.CompilerParams(has_side_effects=_SIDE_EFFECT),
    )(*srcs, *lands, send_sems, recv_sems, after)
    return outs[:na], outs[na:]


def _own_slot(land, own):
    me = 4 * lax.axis_index("x") + 2 * lax.axis_index("y") + lax.axis_index("c")
    mine = lax.broadcasted_iota(jnp.int32, land.shape, 0) == me
    return jnp.where(mine, jnp.broadcast_to(own, land.shape), land)


def _exchange(name, pieces, bcast):
    n_p, n_b = len(pieces), len(bcast)
    na = n_p + n_b

    def body(*refs):
        src_refs, dst_refs = refs[:na], refs[na:2 * na]
        send_sems, recv_sems, local_sems = refs[2 * na:]
        x, y, c = lax.axis_index("x"), lax.axis_index("y"), lax.axis_index("c")
        me = 4 * x + 2 * y + c

        def src(a, slot):
            return src_refs[a].at[slot] if a < n_p else src_refs[a]

        own = [pltpu.make_async_copy(src(a, me), dst_refs[a].at[me], local_sems.at[a]) for a in range(na)]
        for cp in own:
            cp.start()

        def peer_of(k):
            px = 1 - x if k & 4 else x
            py = 1 - y if k & 2 else y
            pc = 1 - c if k & 1 else c
            return (px, py, pc), 4 * px + 2 * py + pc

        def copy(a, k, src_slot, dst_slot, to):
            return pltpu.make_async_remote_copy(
                src_ref=src(a, src_slot), dst_ref=dst_refs[a].at[dst_slot],
                send_sem=send_sems.at[7 * a + k - 1], recv_sem=recv_sems.at[7 * a + k - 1],
                device_id=to, device_id_type=MESH_ID)

        sent = []
        for k in range(1, N_DEV):
            to, pidx = peer_of(k)
            for a in range(na):
                cp = copy(a, k, pidx, me, to)
                cp.start()
                sent.append(cp)
        for k in range(1, N_DEV):
            to, pidx = peer_of(k)
            for a in range(na):
                copy(a, k, me, pidx, to).wait_recv()
        for cp in sent:
            cp.wait_send()
        for cp in own:
            cp.wait()

    arrays = list(pieces) + list(bcast)
    out_shape = [jax.ShapeDtypeStruct(p.shape, p.dtype) for p in pieces]
    out_shape += [jax.ShapeDtypeStruct((N_DEV,) + b.shape, b.dtype) for b in bcast]
    res = pl.pallas_call(
        body, name=name,
        in_specs=[pl.BlockSpec(memory_space=pl.ANY)] * na, out_specs=[pl.BlockSpec(memory_space=pl.ANY)] * na,
        out_shape=out_shape,
        scratch_shapes=[pltpu.SemaphoreType.DMA((7 * na,)), pltpu.SemaphoreType.DMA((7 * na,)),
                        pltpu.SemaphoreType.DMA((na,))],
    )(*arrays)
    return res[:n_p], res[n_p:]


def _pad_rows(flat, cols, row_mult):
    n = flat.shape[-1]
    unit = cols * row_mult
    padded = -(-n // unit) * unit
    pad = [(0, 0)] * (flat.ndim - 1) + [(0, padded - n)]
    return jnp.pad(flat, pad).reshape(flat.shape[:-1] + (padded // cols, cols))


def _perm_cols(w):
    aw = N_GROUPS * GROUP_W
    parts = [w[..., QKV_W:]]
    parts += [w[..., a * aw + g * GROUP_W:a * aw + (g + 1) * GROUP_W] for g in range(N_GROUPS) for a in range(3)]
    return jnp.concatenate(parts, axis=-1)


def _unperm_cols(wp, rest_w):
    qkv = wp[..., rest_w:]
    parts = [qkv[..., g * GROUP_QKV_W + a * GROUP_W:g * GROUP_QKV_W + (a + 1) * GROUP_W]
             for a in range(3) for g in range(N_GROUPS)]
    return jnp.concatenate(parts + [wp[..., :rest_w]], axis=-1)


SHARD_AXIS = dict(SHARDED)


def _layer_shards(wts, li, names, zero):
    out = []
    for n in names:
        w = wts[n][li] if zero is None else wts[n][li] + zero
        out.append(w if n in EXACT_F32 else w.astype(BF16))
    return out


def _assemble_weights(names, segs):
    W = {}
    for n, seg in zip(names, segs):
        ax = SHARD_AXIS[n]
        shp = seg.shape[1:]
        seg = jnp.moveaxis(seg, 0, ax)
        W[n] = seg.reshape(shp[:ax] + (N_DEV * shp[ax],) + shp[ax + 1:])
    if "w_in" in W:
        W["w_in"] = _perm_cols(W["w_in"])
    return W


def _grad_pieces(gr):
    out = []
    for n in gr:
        ax = SHARD_AXIS[n]
        shp = gr[n].shape
        g = gr[n].reshape(shp[:ax] + (N_DEV, shp[ax] // N_DEV) + shp[ax + 1:])
        out.append(jnp.moveaxis(g, ax, 0).astype(BF16))
    return out


def _pack_small(vals):
    flat = jnp.concatenate([vals[n].astype(F32).reshape(-1) for n in REPLICATED])
    return _pad_rows(flat, LANE, 8)


def _layer_fwd(li, x, p_l, w_in, other_weights, G, rope, Bl, S, F):
    T, D = x.shape
    rest_w = 3 * D
    sv = {"x0": x}
    W = {"w_in": w_in}
    hs = _rms_mix_fwd(f"rms_mix_{li}", x, G["g_mix"], S)
    h = hs[0]
    h_g = [h] + [a.reshape(T, D) for a in hs[1:]]
    sv["h_g"] = h_g
    zr = _mm_nn(f"rest_proj_{li}", h, W["w_in"], n_cols=rest_w, tn=1024, tk=_tile(D, 1024))
    sv["zr"] = zr
    qkv_l, o_l, lse_l = [], [], []
    for g in range(N_GROUPS):
        qkv = _qkv_proj(f"qkv_proj_{li}_{g}", h_g[g], W["w_in"], rope[g], S, rest_w + g * GROUP_QKV_W)
        o, lse = _attn_fwd(f"attn_fwd_{li}_{g}", qkv, g, Bl, S)
        qkv_l.append(qkv)
        o_l.append(o)
        lse_l.append(lse)
    sv["qkv"], sv["o"], sv["lse"] = qkv_l, o_l, lse_l
    W.update(other_weights(o_l[-1]))
    x1, attn, pooled, mixed, y_a, y_b, merged = _mix_out_fwd(
        f"mix_out_fwd_{li}", x, zr, o_l, lse_l, W["w_ya"], W["pool_w"], G["pool_scale"], W["w_yb"], W["w_o"], S)
    sv.update(x1=x1, attn=attn, pooled=pooled, mixed=mixed, y_a=y_a, y_b=y_b, merged=merged)
    h2 = _rms_fwd(f"rms_ffn_{li}", x1, G["g_ffn"])
    u = _up_proj(f"up_proj_{li}", h2, W["w_up"], F)
    a = _ffn_act_fwd(f"ffn_act_fwd_{li}", u, W["conv_w"], G["conv_b"], S)
    x2 = _mm_nn(f"down_proj_{li}", a, W["w_down"], add=x1, tk=_tile(F, 1408))
    sv.update(h2=h2, u=u, a=a, x2=x2)
    x3, e, pg, p_bf = _ple_fwd(f"ple_fwd_{li}", x2, p_l, G["g_ple"], W["w_ple_gate"], W["w_ple"])
    sv.update(e=e, pg=pg, p_bf=p_bf)
    return x3, sv, W


EARLY_GRADS = ("w_ple", "w_ple_gate", "w_down", "conv_w", "w_up")
LATE_GRADS = ("w_in", "w_ya", "w_yb", "pool_w", "w_o")


def _layer_bwd(li, dx3, sv, W, G, rope, Bl, S, F, send):
    T, D = dx3.shape
    rest_w = 3 * D
    gr = {}
    dx2, d_e, d_s, h3, dg = _ple_bwd(f"ple_bwd_{li}", dx3, sv["x2"], sv["e"], sv["pg"], G["g_ple"], W["w_ple_gate"])
    gr["g_ple"] = dg[0]
    gr["w_ple"] = _mm_tn(f"w_ple_grad_{li}", sv["p_bf"], d_e)
    gr["w_ple_gate"] = _mm_tn(f"w_ple_gate_grad_{li}", h3, d_s)

    d_a = _mm_nt(f"down_bwd_{li}", dx2, W["w_down"], tn=1408, tk=_tile(D, 1024))
    gr["w_down"] = _mm_tn(f"w_down_grad_{li}", sv["a"], dx2, tm=1408)
    du, d_cw, d_cb = _ffn_act_bwd(f"ffn_act_bwd_{li}", sv["u"], d_a, W["conv_w"], G["conv_b"], S)
    gr["conv_w"] = d_cw.transpose(1, 0, 2).reshape(3, 2 * F)
    gr["conv_b"] = d_cb.reshape(2 * F)
    tk_f = _tile(F, 1408)
    nkh = F // tk_f
    tm_r = _tile(T, 512, 8)
    dx1, dg = _mm_nt_rmsbwd(f"up_bwd_{li}", du, (None, tm_r, tk_f), lambda i, j, k: (k // nkh, i, k % nkh),
                            2 * nkh, tk_f, W["w_up"], sv["x1"], G["g_ffn"], dx2)
    gr["g_ffn"] = dg[0]
    gr["w_up"] = _up_w_grad(f"w_up_grad_{li}", sv["h2"], du)
    zero = send("a", {n: gr[n] for n in EARLY_GRADS})

    (dz_rest, d_ya, d_yb, d_pm, do0, do1, do2, c0, c1, c2, dps) = _mix_out_bwd(
        f"mix_out_bwd_{li}", dx1, sv["zr"], sv["y_a"], sv["y_b"], sv["o"], sv["lse"], sv["pooled"],
        W["w_o"], W["w_ya"], W["w_yb"], W["pool_w"], G["pool_scale"] + zero, S)
    gr["pool_scale"] = dps[0]
    gr["w_o"] = _mm_tn(f"w_o_grad_{li}", sv["merged"], dx1)
    gr["w_ya"] = _mm_tn(f"w_ya_grad_{li}", sv["attn"], d_ya)
    gr["w_yb"] = _mm_tn(f"w_yb_grad_{li}", sv["mixed"], d_yb)
    gr["pool_w"] = _pool_w_grad(f"pool_w_grad_{li}", sv["pooled"], d_pm, D // len(POOL_WINDOWS))
    segs = [(dz_rest, 1)]
    for g, (do, cst) in enumerate(((do0, c0), (do1, c1), (do2, c2))):
        dqkv = _attn_bwd(f"attn_bwd_{li}_{g}", sv["qkv"][g], do, sv["lse"][g], cst, rope[g], g, Bl, S)
        segs.append((dqkv, DILATIONS[g]))

    h_rows = [sv["h_g"][0]] + sv["h_g"]
    w_in_parts = [_mm_tn(f"w_in_grad_{li}_{s}", h_rows[s], seg, tn=1536) for s, (seg, _) in enumerate(segs)]
    gr["w_in"] = _unperm_cols(jnp.concatenate(w_in_parts, axis=1), rest_w)
    zero = send("b", {n: gr[n] for n in LATE_GRADS})
    dx0, dg = _in_bwd(f"in_bwd_{li}", segs, W["w_in"], sv["x0"], G["g_mix"] + zero, dx1, S)
    gr["g_mix"] = dg[0]
    return dx0, gr


def kernel(x, p, g_mix, w_in, w_ya, w_yb, pool_w, pool_scale, w_o, g_ffn, w_up, conv_w, conv_b, w_down, g_ple, w_ple, w_ple_gate, g_final, loss_target, m_g_mix, m_w_in, m_w_ya, m_w_yb, m_pool_w, m_pool_scale, m_w_o, m_g_ffn, m_w_up, m_conv_w, m_conv_b, m_w_down, m_g_ple, m_w_ple, m_w_ple_gate, m_g_final, v_g_mix, v_w_in, v_w_ya, v_w_yb, v_pool_w, v_pool_scale, v_w_o, v_g_ffn, v_w_up, v_conv_w, v_conv_b, v_w_down, v_g_ple, v_w_ple, v_w_ple_gate, v_g_final):
    wts = dict(g_mix=g_mix, w_in=w_in, w_ya=w_ya, w_yb=w_yb, pool_w=pool_w, pool_scale=pool_scale, w_o=w_o,
               g_ffn=g_ffn, w_up=w_up, conv_w=conv_w, conv_b=conv_b, w_down=w_down, g_ple=g_ple, w_ple=w_ple,
               w_ple_gate=w_ple_gate, g_final=g_final)
    mom = dict(g_mix=m_g_mix, w_in=m_w_in, w_ya=m_w_ya, w_yb=m_w_yb, pool_w=m_pool_w, pool_scale=m_pool_scale,
               w_o=m_w_o, g_ffn=m_g_ffn, w_up=m_w_up, conv_w=m_conv_w, conv_b=m_conv_b, w_down=m_w_down,
               g_ple=m_g_ple, w_ple=m_w_ple, w_ple_gate=m_w_ple_gate, g_final=m_g_final)
    var = dict(g_mix=v_g_mix, w_in=v_w_in, w_ya=v_w_ya, w_yb=v_w_yb, pool_w=v_pool_w, pool_scale=v_pool_scale,
               w_o=v_w_o, g_ffn=v_g_ffn, w_up=v_w_up, conv_w=v_conv_w, conv_b=v_conv_b, w_down=v_w_down,
               g_ple=v_g_ple, w_ple=v_w_ple, w_ple_gate=v_w_ple_gate, g_final=v_g_final)
    Bl, S, D = x.shape
    depth = g_mix.shape[0]
    F = w_down.shape[1] * N_DEV
    T = Bl * S
    assert S % (BLOCK * DILATIONS[-1]) == 0 and D % GROUP_W == 0 and F % LANE == 0
    rope = [tuple(t if d == 1 else t.reshape(S // d, d, HEAD_DIM).transpose(1, 0, 2).reshape(S, HEAD_DIM)
                  for t in _rope_tables(S)) for d in DILATIONS]

    first, others = ("w_in",), tuple(n for n, _ in SHARDED if n != "w_in")
    got_in, tok = _all_gather("gather_w0_in", _layer_shards(wts, 0, first, None))
    gathers = {}
    for li in range(depth):
        names = others if li == 0 else first + others
        *gathers[li], tok = _send_start(f"gather_w{li}_start", _layer_shards(wts, li, names, tok[0, 0]), False)

    def gathered(li, names, after):
        shards, lands = _send_wait(f"gather_w{li}_wait", *gathers[li], False, after)
        return _assemble_weights(names, [_own_slot(l, s) for l, s in zip(lands, shards)])

    xs = x.reshape(T, D)
    saved = []
    for li in range(depth):
        G = {n: wts[n][li] for n in REPLICATED if n != "g_final"}
        if li == 0:
            G["g_mix"] = G["g_mix"] + tok[0, 0]
            w_in_full = _assemble_weights(first, got_in)["w_in"]
            rest_of = lambda after: gathered(0, others, after)
        else:
            W_all = gathered(li, first + others, xs)
            w_in_full = W_all["w_in"]
            rest_of = lambda after, W_all=W_all: W_all
        xs, sv, W = _layer_fwd(li, xs, p[li].reshape(T, -1), w_in_full, rest_of, G, rope, Bl, S, F)
        saved.append((sv, W, G))

    dx, loss_row, dg_final = _loss_bwd("loss_bwd", xs, loss_target.reshape(T, D), g_final)
    layer_grads = [None] * depth
    sends = []
    zero = [None]
    for li in reversed(range(depth)):
        sv, W, G = saved[li]
        if zero[0] is not None:
            G = dict(G, g_ple=G["g_ple"] + zero[0])

        def send(tag, group, li=li):
            *handles, tok = _send_start(f"exchange_g{li}{tag}_start", _grad_pieces(group), True)
            sends.append((li, tag, tuple(group), handles))
            zero[0] = tok[0, 0]
            return zero[0]

        dx, layer_grads[li] = _layer_bwd(li, dx, sv, W, G, rope, Bl, S, F, send)

    recv = {}
    for li, tag, names, handles in sends:
        pcs, lands = _send_wait(f"exchange_g{li}{tag}_wait", *handles, True, dx)
        for n, l, s in zip(names, lands, pcs):
            recv[(li, n)] = _own_slot(l, s)
    grads = {n: jnp.stack([layer_grads[li][n] for li in range(depth)]) for n in REPLICATED if n != "g_final"}
    grads["g_final"] = dg_final[0]
    _, (small_all,) = _exchange("exchange_small", [], [_pack_small(grads)])

    out_g, out_d, out_m, out_v = {}, {}, {}, {}
    for n, _ in SHARDED:
        shp = wts[n].shape
        two_d = (math.prod(shp[:-1]), shp[-1])
        pieces = [recv[(li, n)].reshape(N_DEV, two_d[0] // depth, two_d[1]) for li in range(depth)]
        res = _adamw(f"adamw_{n}", wts[n].reshape(two_d), mom[n].reshape(two_d), var[n].reshape(two_d), pieces)
        out_g[n], out_d[n], out_m[n], out_v[n] = [r.reshape(shp) for r in res]
    res = _adamw("adamw_replicated", _pack_small(wts), _pack_small(mom), _pack_small(var), [small_all])
    off = 0
    for n in REPLICATED:
        shp = wts[n].shape
        size = math.prod(shp)
        for dst, r in zip((out_g, out_d, out_m, out_v), res):
            dst[n] = r.reshape(-1)[off:off + size].reshape(shp)
        off += size

    loss = lax.psum(loss_row[0, 0], MESH_AXES)
    outs = [loss, dx.reshape(Bl, S, D)]
    for dct in (out_g, out_d, out_m, out_v):
        outs += [dct[n] for n in WEIGHT_ORDER]
    return tuple(outs)
```

```python
import math

import jax
import jax.numpy as jnp
from jax import lax
from jax.experimental import pallas as pl
from jax.experimental.pallas import tpu as pltpu

F32 = jnp.float32
BF16 = jnp.bfloat16

N_DEV = 8
HEAD_DIM = 128
HEADS = 4
GROUP_W = HEADS * HEAD_DIM
DILATIONS = (1, 4, 16)
N_GROUPS = len(DILATIONS)
QKV_W = 3 * N_GROUPS * GROUP_W
GROUP_QKV_W = 3 * GROUP_W
BLOCK = 128
ROPE_DIM = HEAD_DIM // 4
ROPE_HALF = ROPE_DIM // 2
ROPE_THETA = 500000.0
NEG_INF = -1e30
POOL_WINDOWS = (2, 4, 8, 16)
POOL_HALO = 16
CONV_HALO = 16
RMS_EPS = 1e-6
ADAM_LR = 0.001
ADAM_B1 = 0.9
ADAM_B2 = 0.999
ADAM_EPS = 1e-08
ADAM_WD = 0.01
ADAM_STEP = 10
LANE = 128
PACK_COLS = 1024
PACK_ROWS = 16
MESH_ID = pl.DeviceIdType.MESH
MESH_AXES = ("x", "y", "c")

NT_DIMS = (((1,), (1,)), ((), ()))
TN_DIMS = (((0,), (0,)), ((), ()))
NN_DIMS = (((1,), (0,)), ((), ()))

SHARDED = (("w_in", 1), ("w_ya", 1), ("w_yb", 0), ("pool_w", 1), ("w_o", 0), ("w_up", 1), ("conv_w", 1),
           ("w_down", 0), ("w_ple", 1), ("w_ple_gate", 0))
EXACT_F32 = ("conv_w",)
REPLICATED = ("g_mix", "pool_scale", "g_ffn", "conv_b", "g_ple", "g_final")
WEIGHT_ORDER = ("g_mix", "w_in", "w_ya", "w_yb", "pool_w", "pool_scale", "w_o", "g_ffn", "w_up", "conv_w", "conv_b",
                "w_down", "g_ple", "w_ple", "w_ple_gate", "g_final")


def _tile(n, pref, mult=LANE):
    if n <= pref:
        return n
    t = (pref // mult) * mult
    while t >= mult:
        if n % t == 0:
            return t
        t -= mult
    return n


def _sigmoid(x):
    return 1.0 / (1.0 + jnp.exp(-x))


def _dot(a, b, dims=NN_DIMS):
    return lax.dot_general(a.astype(BF16), b.astype(BF16), dims, preferred_element_type=F32)


def _rstd(x):
    return lax.rsqrt(jnp.mean(x * x, axis=-1, keepdims=True) + RMS_EPS)


def _rms_bwd(x, g, dh):
    r = _rstd(x)
    u = dh * g
    dx = r * u - x * (r * r * r) * jnp.mean(x * u, axis=-1, keepdims=True)
    return dx, dh * x * r


def _full(a):
    return pl.BlockSpec(a.shape, lambda *_: (0,) * a.ndim)


def _mm(name, a, b, *, grid, a_block, a_map, b_block, b_map, dims, acc_shape, outs, extras=(), epi=None):
    nk = grid[2]
    n_ex = len(extras)
    n_out = len(outs)

    def body(*refs):
        a_ref, b_ref = refs[0], refs[1]
        ex = refs[2:2 + n_ex]
        o = refs[2 + n_ex:2 + n_ex + n_out]
        acc = refs[2 + n_ex + n_out]
        i, j, k = pl.program_id(0), pl.program_id(1), pl.program_id(2)

        @pl.when(k == 0)
        def _():
            acc[...] = jnp.zeros_like(acc)

        acc[...] += _dot(a_ref[...], b_ref[...], dims)

        @pl.when(k == nk - 1)
        def _():
            if epi is None:
                o[0][...] = acc[...].astype(o[0].dtype)
            else:
                epi(acc[...], ex, o, i, j)

    in_specs = [pl.BlockSpec(a_block, a_map), pl.BlockSpec(b_block, b_map)]
    in_specs += [pl.BlockSpec(blk, mp) for (_, blk, mp) in extras]
    out_specs = [pl.BlockSpec(blk, mp) for (_, _, blk, mp) in outs]
    out_shape = [jax.ShapeDtypeStruct(s, d) for (s, d, _, _) in outs]
    return pl.pallas_call(
        body, name=name, grid=grid, in_specs=in_specs, out_specs=out_specs, out_shape=out_shape,
        scratch_shapes=[pltpu.VMEM(acc_shape, F32)],
    )(a, b, *[e[0] for e in extras])


def _mm_nn(name, a, b, *, out_dtype=F32, tm=1024, tn=1024, tk=512, b_col_off=0, n_cols=None, add=None):
    M, K = a.shape
    N = n_cols if n_cols is not None else b.shape[1]
    tm, tn, tk = _tile(M, tm, 8), _tile(N, tn), _tile(K, tk)
    assert b_col_off % tn == 0
    joff = b_col_off // tn
    extras, epi = (), None
    if add is not None:
        extras = ((add, (tm, tn), lambda i, j, k: (i, j)),)

        def epi(acc, ex, o, i, j):
            o[0][...] = (acc + ex[0][...]).astype(o[0].dtype)

    return _mm(name, a, b, grid=(M // tm, N // tn, K // tk),
               a_block=(tm, tk), a_map=lambda i, j, k: (i, k),
               b_block=(tk, tn), b_map=lambda i, j, k: (k, j + joff),
               dims=NN_DIMS, acc_shape=(tm, tn),
               outs=[((M, N), out_dtype, (tm, tn), lambda i, j, k: (i, j))], extras=extras, epi=epi)[0]


def _mm_nt(name, a, b, *, out_dtype=F32, tm=1024, tn=1024, tk=512):
    M, K = a.shape
    N = b.shape[0]
    tm, tn, tk = _tile(M, tm, 8), _tile(N, tn), _tile(K, tk)
    return _mm(name, a, b, grid=(M // tm, N // tn, K // tk),
               a_block=(tm, tk), a_map=lambda i, j, k: (i, k),
               b_block=(tn, tk), b_map=lambda i, j, k: (j, k),
               dims=NT_DIMS, acc_shape=(tm, tn),
               outs=[((M, N), out_dtype, (tm, tn), lambda i, j, k: (i, j))])[0]


def _mm_tn(name, a, b, *, tm=1024, tn=1024, tk=1024):
    K, M = a.shape
    N = b.shape[1]
    tm, tn, tk = _tile(M, tm), _tile(N, tn), _tile(K, tk, 8)
    return _mm(name, a, b, grid=(M // tm, N // tn, K // tk),
               a_block=(tk, tm), a_map=lambda i, j, k: (k, i),
               b_block=(tk, tn), b_map=lambda i, j, k: (k, j),
               dims=TN_DIMS, acc_shape=(tm, tn),
               outs=[((M, N), F32, (tm, tn), lambda i, j, k: (i, j))])[0]


def _mm_nt_rmsbwd(name, a, a_block, a_map, nk, tk, w, x, g, dres):
    T, D = x.shape
    tm = a_block[-2]

    def epi(acc, ex, o, i, j):
        @pl.when(i == 0)
        def _():
            o[1][...] = jnp.zeros_like(o[1])

        dx, dgr = _rms_bwd(ex[0][...], ex[1][...], acc)
        o[0][...] = ex[2][...] + dx
        o[1][...] += jnp.sum(dgr, axis=0, keepdims=True)

    row = lambda i, j, k: (i, 0)
    vec = lambda i, j, k: (0, 0)
    return _mm(name, a, w, grid=(T // tm, 1, nk),
               a_block=a_block, a_map=a_map,
               b_block=(D, tk), b_map=lambda i, j, k: (0, k),
               dims=NT_DIMS, acc_shape=(tm, D),
               outs=[((T, D), F32, (tm, D), row), ((1, D), F32, (1, D), vec)],
               extras=[(x, (tm, D), row), (g.reshape(1, D), (1, D), vec), (dres, (tm, D), row)], epi=epi)


def _in_bwd(name, segs, w_perm, x, g, dres, S):
    T, D = x.shape
    tm = _tile(S, 512, 256)
    nst = S // tm
    tk = GROUP_QKV_W
    steps = [a.shape[1] // tk for a, _ in segs]
    starts = [sum(steps[:s]) for s in range(len(segs))]
    nk = sum(steps)
    ns = len(segs)
    cols = _chunks(D)
    assert all(a.shape[1] % tk == 0 for a, _ in segs) and nk * tk == w_perm.shape[1]

    def body(*refs):
        a_refs = refs[:ns]
        w_ref, x_ref, g_ref, dres_ref, dx_ref, dg_ref, acc = refs[ns:]
        i, k = pl.program_id(0), pl.program_id(1)

        @pl.when(k == 0)
        def _():
            acc[...] = jnp.zeros_like(acc)

        for s in range(ns):
            d = segs[s][1]

            @pl.when((k >= starts[s]) & (k < starts[s] + steps[s]))
            def _():
                prod = _dot(a_refs[s][...].reshape(tm, tk), w_ref[...], NT_DIMS)
                q = tm // d
                for c, cs in enumerate(cols):
                    if d == 1:
                        acc[c] += prod[:, cs]
                    else:
                        for r in range(d):
                            acc[c, pl.ds(r, q, stride=d), :] += prod[r * q:(r + 1) * q, cs]

        @pl.when(k == nk - 1)
        def _():
            @pl.when(i == 0)
            def _():
                dg_ref[...] = jnp.zeros_like(dg_ref)

            dh = jnp.concatenate([acc[c] for c in range(len(cols))], axis=1)
            dx, dgr = _rms_bwd(x_ref[...], g_ref[...], dh)
            dx_ref[...] = dres_ref[...] + dx
            dg_ref[...] += jnp.sum(dgr, axis=0, keepdims=True)

    def seg_spec(s):
        kmap = lambda k: jnp.clip(k - starts[s], 0, steps[s] - 1)
        d = segs[s][1]
        if d == 1:
            return pl.BlockSpec((tm, tk), lambda i, k: (i, kmap(k)))
        return pl.BlockSpec((None, d, tm // d, tk), lambda i, k: (i // nst, 0, i % nst, kmap(k)))

    views = [a if d == 1 else a.reshape(T // S, d, S // d, a.shape[1]) for a, d in segs]
    row = pl.BlockSpec((tm, D), lambda i, k: (i, 0))
    vec = pl.BlockSpec((1, D), lambda i, k: (0, 0))
    return pl.pallas_call(
        body, name=name, grid=(T // tm, nk),
        in_specs=[seg_spec(s) for s in range(ns)] + [pl.BlockSpec((D, tk), lambda i, k: (0, k)), row, vec, row],
        out_specs=[row, vec],
        out_shape=[jax.ShapeDtypeStruct((T, D), F32), jax.ShapeDtypeStruct((1, D), F32)],
        scratch_shapes=[pltpu.VMEM((D // LANE, tm, LANE), F32)],
    )(*views, w_perm, x, g.reshape(1, D), dres)


def _rope_tables(S):
    pos = jnp.arange(S, dtype=F32)
    inv_freq = jnp.exp(jnp.arange(0, ROPE_DIM, 2, dtype=F32) * (-math.log(ROPE_THETA) / ROPE_DIM))
    ang = pos[:, None] * inv_freq[None, :]
    cos, sin = jnp.cos(ang), jnp.sin(ang)
    ones = jnp.ones((S, HEAD_DIM - ROPE_DIM), F32)
    zeros_h = jnp.zeros((S, ROPE_HALF), F32)
    zeros_r = jnp.zeros((S, HEAD_DIM - ROPE_DIM), F32)
    c = jnp.concatenate([cos, cos, ones], axis=1)
    sa = jnp.concatenate([-sin, zeros_h, zeros_r], axis=1)
    sb = jnp.concatenate([zeros_h, sin, zeros_r], axis=1)
    return c, sa, sb


def _rope(t, c, sa, sb):
    return t * c + pltpu.roll(t, HEAD_DIM - ROPE_HALF, 1) * sa + pltpu.roll(t, ROPE_HALF, 1) * sb


def _rms_fwd(name, x, g):
    T, D = x.shape
    tm = _tile(T, 512, 8)

    def body(x_ref, g_ref, h_ref):
        xv = x_ref[...]
        h_ref[...] = (xv * _rstd(xv) * g_ref[...]).astype(BF16)

    return pl.pallas_call(
        body, name=name, grid=(T // tm,),
        in_specs=[pl.BlockSpec((tm, D), lambda i: (i, 0)), pl.BlockSpec((1, D), lambda i: (0, 0))],
        out_specs=pl.BlockSpec((tm, D), lambda i: (i, 0)),
        out_shape=jax.ShapeDtypeStruct((T, D), BF16))(x, g.reshape(1, D))


def _chunks(width):
    return [slice(c * LANE, (c + 1) * LANE) for c in range(width // LANE)]


def _store_residue_major(val, sc, out_ref, d, dtype):
    rows = val.shape[0]
    for c, cs in enumerate(_chunks(val.shape[1])):
        sc[c] = val[:, cs]
    for r in range(d):
        for c, cs in enumerate(_chunks(val.shape[1])):
            out_ref[r, :, cs] = sc[c, pl.ds(r, rows // d, stride=d), :].astype(dtype)


def _load_token_order(blk_ref, sc, d):
    _, q, width = blk_ref.shape
    for r in range(d):
        for c, cs in enumerate(_chunks(width)):
            sc[c, pl.ds(r, q, stride=d), :] = blk_ref[r, :, cs]
    return jnp.concatenate([sc[c] for c in range(width // LANE)], axis=1)


def _residue_major_spec(d, q, width, nst):
    return pl.BlockSpec((None, d, q, width), lambda i, *_: (i // nst, 0, i % nst, 0))


def _rms_mix_fwd(name, x, g, S):
    T, D = x.shape
    Bl = T // S
    tm = _tile(S, 512, 256)
    nst = S // tm
    dils = [d for d in DILATIONS if d > 1]

    def body(x_ref, g_ref, h_ref, *rest):
        rm_refs, sc = rest[:len(dils)], rest[len(dils)]
        xv = x_ref[...]
        hv = xv * _rstd(xv) * g_ref[...]
        h_ref[...] = hv.astype(BF16)
        for d, o_ref in zip(dils, rm_refs):
            _store_residue_major(hv, sc, o_ref, d, BF16)

    row = pl.BlockSpec((tm, D), lambda i: (i, 0))
    return pl.pallas_call(
        body, name=name, grid=(T // tm,),
        in_specs=[row, pl.BlockSpec((1, D), lambda i: (0, 0))],
        out_specs=[row] + [_residue_major_spec(d, tm // d, D, nst) for d in dils],
        out_shape=[jax.ShapeDtypeStruct((T, D), BF16)]
        + [jax.ShapeDtypeStruct((Bl, d, S // d, D), BF16) for d in dils],
        scratch_shapes=[pltpu.VMEM((D // LANE, tm, LANE), F32)])(x, g.reshape(1, D))


def _qkv_proj(name, h, w_perm, rope, S, col_off):
    T, D = h.shape
    tm = _tile(S, 1024, 8)
    tn = GROUP_W
    tk = _tile(D, 1024)
    c_t, sa_t, sb_t = rope
    n_seq_tiles = S // tm
    joff = col_off // tn
    tmap = lambda i, j, k: (i % n_seq_tiles, 0)

    def epi(acc, ex, o, i, j):
        is_rot = j < 2

        @pl.when(is_rot)
        def _():
            c, sa, sb = ex[0][...], ex[1][...], ex[2][...]
            for hh in range(HEADS):
                sl = slice(hh * HEAD_DIM, (hh + 1) * HEAD_DIM)
                o[0][:, sl] = _rope(acc[:, sl], c, sa, sb).astype(BF16)

        @pl.when(jnp.logical_not(is_rot))
        def _():
            o[0][...] = acc.astype(BF16)

    return _mm(name, h, w_perm, grid=(T // tm, 3, D // tk),
               a_block=(tm, tk), a_map=lambda i, j, k: (i, k),
               b_block=(tk, tn), b_map=lambda i, j, k: (k, j + joff),
               dims=NN_DIMS, acc_shape=(tm, tn),
               outs=[((T, GROUP_QKV_W), BF16, (tm, tn), lambda i, j, k: (i, j))],
               extras=[(c_t, (tm, HEAD_DIM), tmap), (sa_t, (tm, HEAD_DIM), tmap), (sb_t, (tm, HEAD_DIM), tmap)],
               epi=epi)[0]


ATTN_BLOCKS_PER_STEP = 4


def _attn_mask(has_prev):
    qi = lax.broadcasted_iota(jnp.int32, (BLOCK, 2 * BLOCK), 0)
    ki = lax.broadcasted_iota(jnp.int32, (BLOCK, 2 * BLOCK), 1)
    diff = BLOCK + qi - ki
    band = (diff >= 0) & (diff <= BLOCK)
    return band if has_prev is True else band & (has_prev | (ki >= BLOCK))


def _attn_chunking(nb):
    cb = min(ATTN_BLOCKS_PER_STEP, nb)
    return cb, ATTN_BLOCKS_PER_STEP // cb


def _attn_fwd(name, qkv, g, Bl, S):
    d = DILATIONS[g]
    L = S // d
    nb = L // BLOCK
    cb, cs = _attn_chunking(nb)
    qv = qkv.reshape(Bl * d, L, GROUP_QKV_W)
    scale = HEAD_DIM ** -0.5

    def body(q_ref, kc_ref, vc_ref, kp_ref, vp_ref, o_ref, l_ref):
        n = pl.program_id(1)
        for si in range(cs):
            for bi in range(cb):
                rows = slice(bi * BLOCK, (bi + 1) * BLOCK)
                before = slice((bi - 1) * BLOCK, bi * BLOCK)
                valid = _attn_mask(n > 0 if bi == 0 else True)
                for hh in range(HEADS):
                    sl = slice(hh * HEAD_DIM, (hh + 1) * HEAD_DIM)
                    kp = kp_ref[si, :, sl] if bi == 0 else kc_ref[si, before, sl]
                    vp = vp_ref[si, :, sl] if bi == 0 else vc_ref[si, before, sl]
                    kk = jnp.concatenate([kp, kc_ref[si, rows, sl]], axis=0)
                    vv = jnp.concatenate([vp, vc_ref[si, rows, sl]], axis=0)
                    s = jnp.where(valid, _dot(q_ref[si, rows, sl], kk, NT_DIMS) * scale, NEG_INF)
                    m = jnp.max(s, axis=-1, keepdims=True)
                    p = jnp.exp(s - m)
                    l = jnp.sum(p, axis=-1, keepdims=True)
                    o_ref[si, rows, sl] = _dot(p, vv) / l
                    l_ref[si, rows, sl] = jnp.broadcast_to(m + jnp.log(l), (BLOCK, HEAD_DIM))

    main = (cs, cb * BLOCK, GROUP_W)
    edge = (cs, BLOCK, GROUP_W)
    cur = lambda off: pl.BlockSpec(main, lambda s, n: (s, n, off))
    prev = lambda off: pl.BlockSpec(edge, lambda s, n: (s, jnp.maximum(n * cb - 1, 0), off))
    out = pl.BlockSpec(main, lambda s, n: (s, n, 0))
    return pl.pallas_call(
        body, name=name, grid=(Bl * d // cs, nb // cb),
        in_specs=[cur(0), cur(1), cur(2), prev(1), prev(2)],
        out_specs=[out, out],
        out_shape=[jax.ShapeDtypeStruct((Bl * d, L, GROUP_W), F32)] * 2)(qv, qv, qv, qv, qv)


def _merge_weights(l0, l1, l2):
    mx = jnp.maximum(jnp.maximum(l0, l1), l2)
    e0, e1, e2 = jnp.exp(l0 - mx), jnp.exp(l1 - mx), jnp.exp(l2 - mx)
    inv = 1.0 / (e0 + e1 + e2)
    return e0 * inv, e1 * inv, e2 * inv


def _group_specs(tm, nst, S):
    specs = []
    for d in DILATIONS:
        specs.append(pl.BlockSpec((tm, GROUP_W), lambda i: (i, 0)) if d == 1
                     else _residue_major_spec(d, tm // d, GROUP_W, nst))

    def views(arrs):
        out = []
        for d, a in zip(DILATIONS, arrs):
            out.append(a.reshape(-1, GROUP_W) if d == 1 else a.reshape(-1, d, S // d, GROUP_W))
        return out

    return specs, views


def _group_scratch(tm, per_group):
    n = per_group * sum(1 for d in DILATIONS if d > 1)
    return [pltpu.VMEM((GROUP_W // LANE, tm, LANE), F32) for _ in range(n)]


def _group_values(o_refs, l_refs, scs):
    scs = list(scs)
    ov, lv = [], []
    for d, o_ref, l_ref in zip(DILATIONS, o_refs, l_refs):
        if d == 1:
            ov.append(o_ref[...])
            lv.append(l_ref[...])
        else:
            ov.append(_load_token_order(o_ref, scs.pop(), d))
            lv.append(_load_token_order(l_ref, scs.pop(), d))
    return ov, lv


def _pool_inv_count(tseq, w):
    return 1.0 / jnp.minimum(tseq + 1, w).astype(F32)


def _mix_out_fwd(name, x, zr, o_l, lse_l, w_ya, pool_w, pool_scale, w_yb, w_o, S):
    T, D = x.shape
    gw = D // len(POOL_WINDOWS)
    tm = _tile(S, 256, POOL_HALO)
    nst = S // tm
    hpt = tm // POOL_HALO

    def body(x_ref, u_ref, uh_ref, ga_ref, gb_ref, o0, o1, o2, l0, l1, l2, wya_ref, pw_ref, ps_ref, wyb_ref, wo_ref,
             x1_ref, attn_ref, pooled_ref, mixed_ref, ya_ref, yb_ref, merged_ref, *scs):
        it = pl.program_id(0) % nst
        ov, lv = _group_values((o0, o1, o2), (l0, l1, l2), scs)
        w0, w1, w2 = _merge_weights(*lv)
        attn = w0 * ov[0] + w1 * ov[1] + w2 * ov[2]
        attn_ref[...] = attn.astype(BF16)
        y_a = _dot(attn, wya_ref[...])

        u = u_ref[...]
        halo = uh_ref[...] * jnp.where(it == 0, 0.0, 1.0)
        ext = jnp.concatenate([halo, u], axis=0)
        tseq = it * tm + lax.broadcasted_iota(jnp.int32, (tm, 1), 0)
        pm_parts = []
        for gi, w in enumerate(POOL_WINDOWS):
            cs = slice(gi * gw, (gi + 1) * gw)
            s = ext[:, cs]
            step = 1
            while step < w:
                s = s + pltpu.roll(s, step, 0)
                step *= 2
            pooled_g = (s[POOL_HALO:, :] * _pool_inv_count(tseq, w) - u[:, cs]).astype(BF16)
            pooled_ref[:, cs] = pooled_g
            pm_parts.append(_dot(pooled_g, pw_ref[gi]))
        mixed = (jnp.concatenate(pm_parts, axis=1) * ps_ref[...]).astype(BF16)
        mixed_ref[...] = mixed
        y_b = _dot(mixed, wyb_ref[...])
        merged = (_sigmoid(ga_ref[...]) * y_a + _sigmoid(gb_ref[...]) * y_b).astype(BF16)
        ya_ref[...] = y_a
        yb_ref[...] = y_b
        merged_ref[...] = merged
        x1_ref[...] = x_ref[...] + _dot(merged, wo_ref[...])

    row = lambda c: pl.BlockSpec((tm, D), lambda i: (i, c))
    row512 = pl.BlockSpec((tm, GROUP_W), lambda i: (i, 0))
    ps = pool_scale.reshape(1, D)
    halo_spec = pl.BlockSpec((POOL_HALO, D), lambda i: (jnp.maximum(i * hpt - 1, 0), 0))
    grp_specs, grp_views = _group_specs(tm, nst, S)
    return pl.pallas_call(
        body, name=name, grid=(T // tm,),
        in_specs=[row(0), row(0), halo_spec, row(1), row(2)] + grp_specs * 2
        + [_full(w_ya), _full(pool_w), _full(ps), _full(w_yb), _full(w_o)],
        out_specs=[row(0), row512, row(0), row(0), row(0), row(0), row(0)],
        out_shape=[jax.ShapeDtypeStruct((T, D), F32), jax.ShapeDtypeStruct((T, GROUP_W), BF16),
                   jax.ShapeDtypeStruct((T, D), BF16), jax.ShapeDtypeStruct((T, D), BF16),
                   jax.ShapeDtypeStruct((T, D), F32), jax.ShapeDtypeStruct((T, D), F32),
                   jax.ShapeDtypeStruct((T, D), BF16)],
        scratch_shapes=_group_scratch(tm, 2),
    )(x, zr, zr, zr, zr, *grp_views(o_l), *grp_views(lse_l), w_ya, pool_w, ps, w_yb, w_o)


def _up_proj(name, h2, w_up, F):
    T, D = h2.shape
    tm, tn, tk = _tile(T, 1024, 8), _tile(F, 1408), _tile(D, 1024)
    njh = F // tn
    return _mm(name, h2, w_up, grid=(T // tm, 2 * njh, D // tk),
               a_block=(tm, tk), a_map=lambda i, j, k: (i, k),
               b_block=(tk, tn), b_map=lambda i, j, k: (k, j),
               dims=NN_DIMS, acc_shape=(tm, tn),
               outs=[((2, T, F), BF16, (None, tm, tn), lambda i, j, k: (j // njh, i, j % njh))])[0]


def _conv_y(ext, w_ref, b_ref):
    return (b_ref[...] + w_ref[2:3, :] * ext + w_ref[1:2, :] * pltpu.roll(ext, 1, 0)
            + w_ref[0:1, :] * pltpu.roll(ext, 2, 0))


def _conv_params(conv_w, conv_b, F):
    cw = conv_w.reshape(3, 2, F).transpose(1, 0, 2)
    return cw, conv_b.reshape(2, 1, F)


def _ffn_act_fwd(name, u, conv_w, conv_b, S):
    _, T, F = u.shape
    tm = _tile(S, 512, CONV_HALO)
    tf = _tile(F, 1408)
    nst = S // tm
    hpt = tm // CONV_HALO
    cw, cb = _conv_params(conv_w, conv_b, F)

    def body(ug_ref, uv_ref, hg_ref, hv_ref, wg_ref, wv_ref, bg_ref, bv_ref, a_ref):
        keep = jnp.where(pl.program_id(0) % nst == 0, 0.0, 1.0)
        ext = lambda h_ref, u_ref: jnp.concatenate([h_ref[...].astype(F32) * keep, u_ref[...].astype(F32)], axis=0)
        yg = _conv_y(ext(hg_ref, ug_ref), wg_ref, bg_ref)[CONV_HALO:, :]
        yv = _conv_y(ext(hv_ref, uv_ref), wv_ref, bv_ref)[CONV_HALO:, :]
        a_ref[...] = (yg * _sigmoid(yg) * yv).astype(BF16)

    main = lambda h: pl.BlockSpec((None, tm, tf), lambda i, j: (h, i, j))
    halo = lambda h: pl.BlockSpec((None, CONV_HALO, tf), lambda i, j: (h, jnp.maximum(i * hpt - 1, 0), j))
    wsp = lambda h: pl.BlockSpec((None, 3, tf), lambda i, j: (h, 0, j))
    bsp = lambda h: pl.BlockSpec((None, 1, tf), lambda i, j: (h, 0, j))
    return pl.pallas_call(
        body, name=name, grid=(T // tm, F // tf),
        in_specs=[main(0), main(1), halo(0), halo(1), wsp(0), wsp(1), bsp(0), bsp(1)],
        out_specs=pl.BlockSpec((tm, tf), lambda i, j: (i, j)),
        out_shape=jax.ShapeDtypeStruct((T, F), BF16))(u, u, u, u, cw, cw, cb, cb)


def _ple_fwd(name, x2, p, g_ple, w_gate, w_ple):
    T, D = x2.shape
    P = p.shape[1]
    tm = _tile(T, 512, 8)

    def body(x_ref, p_ref, g_ref, wg_ref, wp_ref, x3_ref, e_ref, pg_ref, pbf_ref):
        xv = x_ref[...]
        h3 = xv * _rstd(xv) * g_ref[...]
        pg = _sigmoid(_dot(h3, wg_ref[...]))
        pb = p_ref[...].astype(BF16)
        e = _dot(pb, wp_ref[...])
        x3_ref[...] = xv + e * pg
        e_ref[...] = e
        pg_ref[...] = pg
        pbf_ref[...] = pb

    row = pl.BlockSpec((tm, D), lambda i: (i, 0))
    prow = pl.BlockSpec((tm, P), lambda i: (i, 0))
    g2 = g_ple.reshape(1, D)
    return pl.pallas_call(
        body, name=name, grid=(T // tm,),
        in_specs=[row, prow, _full(g2), _full(w_gate), _full(w_ple)],
        out_specs=[row, row, row, prow],
        out_shape=[jax.ShapeDtypeStruct((T, D), F32)] * 3 + [jax.ShapeDtypeStruct((T, P), BF16)],
    )(x2, p, g2, w_gate, w_ple)


def _loss_bwd(name, xf, target, g_final):
    T, D = xf.shape
    tm = _tile(T, 512, 8)
    nt = T // tm

    def body(x_ref, t_ref, g_ref, dx_ref, loss_ref, dg_ref, lacc):
        i = pl.program_id(0)

        @pl.when(i == 0)
        def _():
            lacc[...] = jnp.zeros_like(lacc)
            dg_ref[...] = jnp.zeros_like(dg_ref)
            loss_ref[...] = jnp.zeros_like(loss_ref)

        xv = x_ref[...]
        g = g_ref[...]
        diff = xv * _rstd(xv) * g - t_ref[...]
        lacc[...] += jnp.sum(diff * diff, axis=0, keepdims=True)
        dx, dgr = _rms_bwd(xv, g, diff * (1.0 / D))
        dx_ref[...] = dx
        dg_ref[...] += jnp.sum(dgr, axis=0, keepdims=True)

        @pl.when(i == nt - 1)
        def _():
            tot = jnp.sum(lacc[...], axis=-1, keepdims=True) * (0.5 / D)
            loss_ref[...] = jnp.broadcast_to(tot, (1, LANE))

    row = pl.BlockSpec((tm, D), lambda i: (i, 0))
    vec = pl.BlockSpec((1, D), lambda i: (0, 0))
    return pl.pallas_call(
        body, name=name, grid=(nt,),
        in_specs=[row, row, vec],
        out_specs=[row, pl.BlockSpec((1, LANE), lambda i: (0, 0)), vec],
        out_shape=[jax.ShapeDtypeStruct((T, D), F32), jax.ShapeDtypeStruct((1, LANE), F32),
                   jax.ShapeDtypeStruct((1, D), F32)],
        scratch_shapes=[pltpu.VMEM((1, D), F32)])(xf, target, g_final.reshape(1, D))


def _ple_bwd(name, dx3, x2, e, pg, g_ple, w_gate):
    T, D = x2.shape
    tm = _tile(T, 512, 8)

    def body(dx3_ref, x_ref, e_ref, pg_ref, g_ref, wg_ref, dx2_ref, de_ref, ds_ref, h3_ref, dg_ref):
        @pl.when(pl.program_id(0) == 0)
        def _():
            dg_ref[...] = jnp.zeros_like(dg_ref)

        dx3v, xv, pgv, g = dx3_ref[...], x_ref[...], pg_ref[...], g_ref[...]
        de_ref[...] = (dx3v * pgv).astype(BF16)
        ds = (dx3v * e_ref[...] * pgv * (1.0 - pgv)).astype(BF16)
        ds_ref[...] = ds
        dh3 = _dot(ds, wg_ref[...], NT_DIMS)
        h3_ref[...] = (xv * _rstd(xv) * g).astype(BF16)
        dx, dgr = _rms_bwd(xv, g, dh3)
        dx2_ref[...] = dx3v + dx
        dg_ref[...] += jnp.sum(dgr, axis=0, keepdims=True)

    row = pl.BlockSpec((tm, D), lambda i: (i, 0))
    vec = pl.BlockSpec((1, D), lambda i: (0, 0))
    return pl.pallas_call(
        body, name=name, grid=(T // tm,),
        in_specs=[row, row, row, row, vec, _full(w_gate)],
        out_specs=[row, row, row, row, vec],
        out_shape=[jax.ShapeDtypeStruct((T, D), F32)] + [jax.ShapeDtypeStruct((T, D), BF16)] * 3
        + [jax.ShapeDtypeStruct((1, D), F32)])(dx3, x2, e, pg, g_ple.reshape(1, D), w_gate)


def _ffn_act_bwd(name, u, d_a, conv_w, conv_b, S):
    _, T, F = u.shape
    H = CONV_HALO
    tm = _tile(S, 512, H)
    tf = _tile(F, 1408)
    nst = S // tm
    hpt = tm // H
    last_halo = T // H - 1
    n_ext = tm + H
    cw, cb = _conv_params(conv_w, conv_b, F)

    def body(ug_ref, uv_ref, pg_ref, pv_ref, ng_ref, nv_ref, da_ref, dan_ref, wg_ref, wv_ref, bg_ref, bv_ref,
             du_ref, dw_ref, db_ref):
        i = pl.program_id(1)
        it = i % nst

        @pl.when(i == 0)
        def _():
            dw_ref[...] = jnp.zeros_like(dw_ref)
            db_ref[...] = jnp.zeros_like(db_ref)

        keep_prev = jnp.where(it == 0, 0.0, 1.0)
        keep_next = jnp.where(it == nst - 1, 0.0, 1.0)
        def shifted(p_ref, u_ref, n_ref):
            ext = jnp.concatenate([p_ref[...].astype(F32) * keep_prev, u_ref[...].astype(F32),
                                   n_ref[...].astype(F32)], axis=0)
            return ext[H:, :], pltpu.roll(ext, 1, 0)[H:, :], pltpu.roll(ext, 2, 0)[H:, :]

        us_g, us_v = shifted(pg_ref, ug_ref, ng_ref), shifted(pv_ref, uv_ref, nv_ref)
        conv = lambda us, w_ref, b_ref: b_ref[...] + w_ref[2:3, :] * us[0] + w_ref[1:2, :] * us[1] + w_ref[0:1, :] * us[2]
        yg, yv = conv(us_g, wg_ref, bg_ref), conv(us_v, wv_ref, bv_ref)
        da = jnp.concatenate([da_ref[...], dan_ref[...] * keep_next], axis=0)
        sg = _sigmoid(yg)
        silu = yg * sg
        dyv = da * silu
        dyg = (da * yv) * (sg + silu * (1.0 - sg))
        for half, (dy, us, w_ref) in enumerate(((dyg, us_g, wg_ref), (dyv, us_v, wv_ref))):
            du = (w_ref[2:3, :] * dy + w_ref[1:2, :] * pltpu.roll(dy, n_ext - 1, 0)
                  + w_ref[0:1, :] * pltpu.roll(dy, n_ext - 2, 0))
            du_ref[half] = du[:tm, :].astype(BF16)
            dym = dy[:tm, :]
            db_ref[half] += jnp.sum(dym, axis=0, keepdims=True)
            for tap in range(3):
                dw_ref[half, tap:tap + 1, :] += jnp.sum(dym * us[2 - tap][:tm, :], axis=0, keepdims=True)

    main = lambda h: pl.BlockSpec((None, tm, tf), lambda j, i: (h, i, j))
    prev = lambda h: pl.BlockSpec((None, H, tf), lambda j, i: (h, jnp.maximum(i * hpt - 1, 0), j))
    nxt = lambda h: pl.BlockSpec((None, H, tf), lambda j, i: (h, jnp.minimum((i + 1) * hpt, last_halo), j))
    wsp = lambda h: pl.BlockSpec((None, 3, tf), lambda j, i: (h, 0, j))
    bsp = lambda h: pl.BlockSpec((None, 1, tf), lambda j, i: (h, 0, j))
    return pl.pallas_call(
        body, name=name, grid=(F // tf, T // tm),
        in_specs=[main(0), main(1), prev(0), prev(1), nxt(0), nxt(1),
                  pl.BlockSpec((tm, tf), lambda j, i: (i, j)),
                  pl.BlockSpec((H, tf), lambda j, i: (jnp.minimum((i + 1) * hpt, last_halo), j)),
                  wsp(0), wsp(1), bsp(0), bsp(1)],
        out_specs=[pl.BlockSpec((2, tm, tf), lambda j, i: (0, i, j)),
                   pl.BlockSpec((2, 3, tf), lambda j, i: (0, 0, j)),
                   pl.BlockSpec((2, 1, tf), lambda j, i: (0, 0, j))],
        out_shape=[jax.ShapeDtypeStruct((2, T, F), BF16), jax.ShapeDtypeStruct((2, 3, F), F32),
                   jax.ShapeDtypeStruct((2, 1, F), F32)],
    )(u, u, u, u, u, u, d_a, d_a, cw, cw, cb, cb)


def _mix_out_bwd(name, dx1, zr, y_a, y_b, o_l, lse_l, pooled, w_o, w_ya, w_yb, pool_w, pool_scale, S):
    T, D = dx1.shape
    gw = D // len(POOL_WINDOWS)
    H = POOL_HALO
    tm = _tile(S, 256, H)
    nst = S // tm
    hpt = tm // H
    last_halo = T // H - 1
    n_ext = tm + H

    def body(dx_ref, dxn_ref, ga_ref, gb_ref, gbn_ref, ya_ref, yb_ref, o0, o1, o2, l0, l1, l2, pooled_ref,
             wo_ref, wya_ref, wyb_ref, pw_ref, ps_ref,
             dz_ref, dya_ref, dyb_ref, dpm_ref, do0, do1, do2, c0, c1, c2, dps_ref, *scs):
        i = pl.program_id(0)
        it = i % nst

        @pl.when(i == 0)
        def _():
            dps_ref[...] = jnp.zeros_like(dps_ref)

        keep_next = jnp.where(it == nst - 1, 0.0, 1.0)
        dm_e = _dot(jnp.concatenate([dx_ref[...], dxn_ref[...]], axis=0), wo_ref[...], NT_DIMS)
        sgb_e = _sigmoid(jnp.concatenate([gb_ref[...], gbn_ref[...]], axis=0))
        dyb_e = dm_e * sgb_e
        dm = dm_e[:tm, :]
        sga = _sigmoid(ga_ref[...])
        sgb = sgb_e[:tm, :]
        d_ga = dm * ya_ref[...] * (sga * (1.0 - sga))
        d_gb = dm * yb_ref[...] * (sgb * (1.0 - sgb))
        dya = (dm * sga).astype(BF16)
        dya_ref[...] = dya
        dyb_ref[...] = dyb_e[:tm, :].astype(BF16)
        dmixed_e = _dot(dyb_e, wyb_ref[...], NT_DIMS)

        rows = lax.broadcasted_iota(jnp.int32, (n_ext, 1), 0)
        tseq = it * tm + rows
        live = jnp.where(rows < tm, 1.0, keep_next)
        ps = ps_ref[...]
        du_parts = []
        for gi, w in enumerate(POOL_WINDOWS):
            cs = slice(gi * gw, (gi + 1) * gw)
            pm_g = _dot(pooled_ref[:, cs], pw_ref[gi])
            dps_ref[:, cs] += jnp.sum(dmixed_e[:tm, cs] * pm_g, axis=0, keepdims=True)
            dpm_e = (dmixed_e[:, cs] * ps[:, cs]).astype(BF16)
            dpm_ref[:, cs] = dpm_e[:tm, :]
            dpooled_e = _dot(dpm_e, pw_ref[gi], NT_DIMS)
            s = dpooled_e * (_pool_inv_count(tseq, w) * live)
            step = 1
            while step < w:
                s = s + pltpu.roll(s, n_ext - step, 0)
                step *= 2
            du_parts.append(s[:tm, :] - dpooled_e[:tm, :])
        dz_ref[...] = jnp.concatenate(du_parts + [d_ga, d_gb], axis=1).astype(BF16)

        d_attn = _dot(dya, wya_ref[...], NT_DIMS)
        ov, lv = _group_values((o0, o1, o2), (l0, l1, l2), scs[:n_in_sc])
        ws = _merge_weights(*lv)
        prod = d_attn * (ws[0] * ov[0] + ws[1] * ov[1] + ws[2] * ov[2])
        rs = jnp.concatenate(
            [jnp.broadcast_to(jnp.sum(prod[:, hh * HEAD_DIM:(hh + 1) * HEAD_DIM], axis=-1, keepdims=True),
                              (tm, HEAD_DIM)) for hh in range(HEADS)], axis=1)
        out_scs = list(scs[n_in_sc:])
        for d, wg, do_ref, c_ref in zip(DILATIONS, ws, (do0, do1, do2), (c0, c1, c2)):
            if d == 1:
                do_ref[...] = (wg * d_attn).astype(BF16)
                c_ref[...] = -wg * rs
            else:
                _store_residue_major(wg * d_attn, out_scs.pop(), do_ref, d, BF16)
                _store_residue_major(-wg * rs, out_scs.pop(), c_ref, d, F32)

    row = lambda c: pl.BlockSpec((tm, D), lambda i: (i, c))
    nxt = lambda c: pl.BlockSpec((H, D), lambda i: (jnp.minimum((i + 1) * hpt, last_halo), c))
    ps2 = pool_scale.reshape(1, D)
    bf = lambda w: jax.ShapeDtypeStruct((T, w), BF16)
    grp_specs, grp_views = _group_specs(tm, nst, S)
    grp_shape = lambda dt: [jax.ShapeDtypeStruct((T, GROUP_W) if d == 1 else (T // S, d, S // d, GROUP_W), dt)
                            for d in DILATIONS]
    n_in_sc = len(_group_scratch(tm, 2))
    return pl.pallas_call(
        body, name=name, grid=(T // tm,),
        in_specs=[row(0), nxt(0), row(1), row(2), nxt(2), row(0), row(0)] + grp_specs * 2 + [row(0)]
        + [_full(w_o), _full(w_ya), _full(w_yb), _full(pool_w), _full(ps2)],
        out_specs=[pl.BlockSpec((tm, 3 * D), lambda i: (i, 0)), row(0), row(0), row(0)] + grp_specs * 2
        + [pl.BlockSpec((1, D), lambda i: (0, 0))],
        out_shape=[bf(3 * D), bf(D), bf(D), bf(D)] + grp_shape(BF16) + grp_shape(F32)
        + [jax.ShapeDtypeStruct((1, D), F32)],
        scratch_shapes=_group_scratch(tm, 4),
    )(dx1, dx1, zr, zr, zr, y_a, y_b, *grp_views(o_l), *grp_views(lse_l), pooled, w_o, w_ya, w_yb, pool_w, ps2)


def _attn_bwd(name, qkv, d_o, lse, cst, rope, g, Bl, S):
    d = DILATIONS[g]
    L = S // d
    nb = L // BLOCK
    qv = qkv.reshape(Bl * d, L, GROUP_QKV_W)
    dov = d_o.reshape(Bl * d, L, GROUP_W)
    lv = lse.reshape(Bl * d, L, GROUP_W)
    cv = cst.reshape(Bl * d, L, GROUP_W)
    tabs = [t.reshape(d, L, HEAD_DIM) for t in rope]
    scale = HEAD_DIM ** -0.5

    cb, cs = _attn_chunking(nb)

    def body(q_ref, qn_ref, kp_ref, kc_ref, vp_ref, vc_ref, do_ref, don_ref, l_ref, ln_ref, c_ref, cn_ref,
             cos_ref, sa_ref, sb_ref, out_ref):
        n = pl.program_id(1)
        qi = lax.broadcasted_iota(jnp.int32, (BLOCK, BLOCK), 0)
        ki = lax.broadcasted_iota(jnp.int32, (BLOCK, BLOCK), 1)
        upper = ki >= qi
        for si in range(cs):
            for bi in range(cb):
                rows = slice(bi * BLOCK, (bi + 1) * BLOCK)
                before = slice((bi - 1) * BLOCK, bi * BLOCK)
                after = slice((bi + 1) * BLOCK, (bi + 2) * BLOCK)
                last = bi == cb - 1
                valid = _attn_mask(n > 0 if bi == 0 else True)
                valid_n = upper & ((n + 1) * cb < nb) if last else upper
                cos, sa, sb = cos_ref[si, rows, :], -sa_ref[si, rows, :], -sb_ref[si, rows, :]
                for hh in range(HEADS):
                    sl = slice(hh * HEAD_DIM, (hh + 1) * HEAD_DIM)
                    col = slice(hh * HEAD_DIM, hh * HEAD_DIM + 1)
                    q, kc, vc, do = q_ref[si, rows, sl], kc_ref[si, rows, sl], vc_ref[si, rows, sl], do_ref[si, rows, sl]
                    if last:
                        qn, don, ln, cn = qn_ref[si, :, sl], don_ref[si, :, sl], ln_ref[si, :, col], cn_ref[si, :, col]
                    else:
                        qn, don, ln, cn = q_ref[si, after, sl], do_ref[si, after, sl], l_ref[si, after, col], c_ref[si, after, col]
                    kp = kp_ref[si, :, sl] if bi == 0 else kc_ref[si, before, sl]
                    vp = vp_ref[si, :, sl] if bi == 0 else vc_ref[si, before, sl]
                    kk = jnp.concatenate([kp, kc], axis=0)
                    vv = jnp.concatenate([vp, vc], axis=0)
                    s = jnp.where(valid, _dot(q, kk, NT_DIMS) * scale, NEG_INF)
                    p = jnp.exp(s - l_ref[si, rows, col])
                    ds = p * (_dot(do, vv, NT_DIMS) + c_ref[si, rows, col])
                    dq = _dot(ds, kk) * scale
                    s2 = jnp.where(valid_n, _dot(qn, kc, NT_DIMS) * scale, NEG_INF)
                    p2 = jnp.exp(s2 - ln)
                    ds2 = p2 * (_dot(don, vc, NT_DIMS) + cn)
                    dk = (_dot(ds[:, BLOCK:], q, TN_DIMS) + _dot(ds2, qn, TN_DIMS)) * scale
                    dv = _dot(p[:, BLOCK:], do, TN_DIMS) + _dot(p2, don, TN_DIMS)
                    out_ref[si, rows, sl] = _rope(dq, cos, sa, sb).astype(BF16)
                    out_ref[si, rows, GROUP_W + hh * HEAD_DIM:GROUP_W + (hh + 1) * HEAD_DIM] = (
                        _rope(dk, cos, sa, sb).astype(BF16))
                    out_ref[si, rows, 2 * GROUP_W + hh * HEAD_DIM:2 * GROUP_W + (hh + 1) * HEAD_DIM] = dv.astype(BF16)

    main = (cs, cb * BLOCK, GROUP_W)
    edge = (cs, BLOCK, GROUP_W)
    cur = lambda off: pl.BlockSpec(main, lambda s, n: (s, n, off))
    prv = lambda off: pl.BlockSpec(edge, lambda s, n: (s, jnp.maximum(n * cb - 1, 0), off))
    nxt = lambda off: pl.BlockSpec(edge, lambda s, n: (s, jnp.minimum((n + 1) * cb, nb - 1), off))
    tab = pl.BlockSpec((cs, cb * BLOCK, HEAD_DIM), lambda s, n: (s % (d // cs), n, 0))
    out = pl.pallas_call(
        body, name=name, grid=(Bl * d // cs, nb // cb),
        in_specs=[cur(0), nxt(0), prv(1), cur(1), prv(2), cur(2),
                  cur(0), nxt(0), cur(0), nxt(0), cur(0), nxt(0), tab, tab, tab],
        out_specs=pl.BlockSpec((cs, cb * BLOCK, GROUP_QKV_W), lambda s, n: (s, n, 0)),
        out_shape=jax.ShapeDtypeStruct((Bl * d, L, GROUP_QKV_W), BF16),
    )(qv, qv, qv, qv, qv, qv, dov, dov, lv, lv, cv, cv, *tabs)
    return out.reshape(Bl * S, GROUP_QKV_W)


def _pool_w_grad(name, pooled, d_pm, gw):
    T = pooled.shape[0]
    ng = len(POOL_WINDOWS)
    tk = _tile(T, 1024, 8)
    return _mm(name, pooled, d_pm, grid=(ng, 1, T // tk),
               a_block=(tk, gw), a_map=lambda i, j, k: (k, i),
               b_block=(tk, gw), b_map=lambda i, j, k: (k, i),
               dims=TN_DIMS, acc_shape=(gw, gw),
               outs=[((ng, gw, gw), F32, (None, gw, gw), lambda i, j, k: (i, 0, 0))])[0]


def _up_w_grad(name, h2, du):
    T, D = h2.shape
    F = du.shape[2]
    tm, tn, tk = _tile(D, 1024), _tile(F, 1408), _tile(T, 1024, 8)
    njh = F // tn
    return _mm(name, h2, du, grid=(D // tm, 2 * njh, T // tk),
               a_block=(tk, tm), a_map=lambda i, j, k: (k, i),
               b_block=(None, tk, tn), b_map=lambda i, j, k: (j // njh, k, j % njh),
               dims=TN_DIMS, acc_shape=(tm, tn),
               outs=[((D, 2 * F), F32, (tm, tn), lambda i, j, k: (i, j))])[0]


def _adamw(name, w, m, v, pieces):
    R, C = w.shape
    nl = len(pieces)
    rl = R // nl
    if nl > 1 and rl % 8:
        per = [_adamw(f"{name}_{l}", w[l * rl:(l + 1) * rl], m[l * rl:(l + 1) * rl], v[l * rl:(l + 1) * rl],
                      [pieces[l]]) for l in range(nl)]
        return [jnp.concatenate([p[o] for p in per], axis=0) for o in range(4)]
    tr = _tile(rl, max(PACK_ROWS, (1 << 18) // C // PACK_ROWS * PACK_ROWS), PACK_ROWS)
    nbl = rl // tr
    c1 = 1.0 - ADAM_B1 ** ADAM_STEP
    c2 = 1.0 - ADAM_B2 ** ADAM_STEP

    def body(w_ref, m_ref, v_ref, *rest):
        p_refs = rest[:nl]
        g_ref, d_ref, mo_ref, vo_ref = rest[nl:]
        i = pl.program_id(0)
        for l in range(nl):
            @pl.when((i >= l * nbl) & (i < (l + 1) * nbl))
            def _():
                g = p_refs[l][0].astype(F32)
                for dev in range(1, N_DEV):
                    g = g + p_refs[l][dev].astype(F32)
                mn = ADAM_B1 * m_ref[...] + (1.0 - ADAM_B1) * g
                vn = ADAM_B2 * v_ref[...] + (1.0 - ADAM_B2) * (g * g)
                g_ref[...] = g
                mo_ref[...] = mn
                vo_ref[...] = vn
                d_ref[...] = -ADAM_LR * ((mn / c1) / (jnp.sqrt(vn / c2) + ADAM_EPS) + ADAM_WD * w_ref[...])

    row = pl.BlockSpec((tr, C), lambda i: (i, 0))
    piece = lambda l: pl.BlockSpec((N_DEV, tr, C), lambda i: (0, jnp.clip(i - l * nbl, 0, nbl - 1), 0))
    return pl.pallas_call(
        body, name=name, grid=(R // tr,),
        in_specs=[row, row, row] + [piece(l) for l in range(nl)],
        out_specs=[row] * 4,
        out_shape=[jax.ShapeDtypeStruct((R, C), F32)] * 4)(w, m, v, *pieces)


def _my_index():
    return 4 * lax.axis_index("x") + 2 * lax.axis_index("y") + lax.axis_index("c")


def _all_gather(name, mine):
    na = len(mine)

    def body(*refs):
        x_refs, out_refs, token = refs[:na], refs[na:2 * na], refs[2 * na]
        send_sems, recv_sems, local_sems = refs[2 * na + 1:]
        token[...] = jnp.zeros_like(token)
        x, y, c = lax.axis_index("x"), lax.axis_index("y"), lax.axis_index("c")
        me, sibling = (x, y, c), (x, y, 1 - c)
        chips = [(1 - x, y), (x, 1 - y), (1 - x, 1 - y)]

        def slot(a, px, py, pc):
            return out_refs[a].at[4 * px + 2 * py + pc]

        def copy(a, k, block, to, src=None):
            return pltpu.make_async_remote_copy(
                src_ref=slot(a, *block) if src is None else src, dst_ref=slot(a, *block),
                send_sem=send_sems.at[7 * a + k], recv_sem=recv_sems.at[7 * a + k],
                device_id=to, device_id_type=MESH_ID)

        own = [pltpu.make_async_copy(x_refs[a], slot(a, *me), local_sems.at[a]) for a in range(na)]
        for cp in own:
            cp.start()
        first = []
        for a in range(na):
            first.append(copy(a, 0, me, sibling, src=x_refs[a]))
            first += [copy(a, 1 + j, me, (*chip, c), src=x_refs[a]) for j, chip in enumerate(chips)]
        for cp in first:
            cp.start()
        passed = []
        for j, chip in enumerate(chips):
            for a in range(na):
                copy(a, 1 + j, (*chip, c), me).wait_recv()
                fwd = copy(a, 4 + j, (*chip, c), sibling)
                fwd.start()
                passed.append(fwd)
        for a in range(na):
            copy(a, 0, sibling, me).wait_recv()
            for j, chip in enumerate(chips):
                copy(a, 4 + j, (*chip, 1 - c), me).wait_recv()
        for cp in first + passed:
            cp.wait_send()
        for cp in own:
            cp.wait()

    res = pl.pallas_call(
        body, name=name,
        in_specs=[pl.BlockSpec(memory_space=pl.ANY)] * na,
        out_specs=[pl.BlockSpec(memory_space=pl.ANY)] * na + [pl.BlockSpec(memory_space=pltpu.VMEM)],
        out_shape=[jax.ShapeDtypeStruct((N_DEV,) + m.shape, m.dtype) for m in mine]
        + [jax.ShapeDtypeStruct((8, LANE), F32)],
        scratch_shapes=[pltpu.SemaphoreType.DMA((7 * na,)), pltpu.SemaphoreType.DMA((7 * na,)),
                        pltpu.SemaphoreType.DMA((na,))],
    )(*mine)
    return res[:na], res[na]


_HBM_SPEC = pl.BlockSpec(memory_space=pltpu.HBM)
_SEM_SPEC = pl.BlockSpec(memory_space=pltpu.SEMAPHORE)
_SIDE_EFFECT = pltpu.SideEffectType.DATAFLOW_SIDE_EFFECTING


def _peer_of(k):
    x, y, c = lax.axis_index("x"), lax.axis_index("y"), lax.axis_index("c")
    px = 1 - x if k & 4 else x
    py = 1 - y if k & 2 else y
    pc = 1 - c if k & 1 else c
    return (px, py, pc), 4 * px + 2 * py + pc


def _send_start(name, srcs, pieces):
    na = len(srcs)
    land_shapes = [s.shape if pieces else (N_DEV,) + s.shape for s in srcs]
    lands = [lax.empty(shp, s.dtype) for shp, s in zip(land_shapes, srcs)]

    def body(*refs):
        src_refs, land_refs = refs[:na], refs[na:2 * na]
        send_sems, recv_sems, token = refs[2 * na], refs[2 * na + 1], refs[4 * na + 2]
        me = 4 * lax.axis_index("x") + 2 * lax.axis_index("y") + lax.axis_index("c")
        for k in range(1, N_DEV):
            to, pidx = _peer_of(k)
            for a in range(na):
                pltpu.make_async_remote_copy(
                    src_ref=src_refs[a].at[pidx] if pieces else src_refs[a], dst_ref=land_refs[a].at[me],
                    send_sem=send_sems.at[7 * a + k - 1], recv_sem=recv_sems.at[7 * a + k - 1],
                    device_id=to, device_id_type=MESH_ID).start()
        token[...] = jnp.zeros_like(token)

    hbm = lambda arrs: [pltpu.HBM(a.shape, a.dtype) for a in arrs]
    outs = pl.pallas_call(
        body, name=name,
        out_shape=(pltpu.SemaphoreType.DMA((7 * na,)), pltpu.SemaphoreType.DMA((7 * na,)), *hbm(srcs), *hbm(lands),
                   jax.ShapeDtypeStruct((8, LANE), F32)),
        in_specs=[_HBM_SPEC] * (2 * na),
        out_specs=(_SEM_SPEC, _SEM_SPEC, *([_HBM_SPEC] * (2 * na)), pl.BlockSpec(memory_space=pltpu.VMEM)),
        input_output_aliases={i: 2 + i for i in range(2 * na)},
        compiler_params=pltpu.CompilerParams(has_side_effects=_SIDE_EFFECT),
    )(*[pltpu.with_memory_space_constraint(s, pltpu.HBM) for s in srcs],
      *[pltpu.with_memory_space_constraint(l, pltpu.HBM) for l in lands])
    return outs[0], outs[1], outs[2:2 + na], outs[2 + na:2 + 2 * na], outs[-1]


def _send_wait(name, send_sems, recv_sems, srcs, lands, pieces, after):
    na = len(srcs)

    def body(*refs):
        src_refs, land_refs = refs[:na], refs[na:2 * na]
        send_sems, recv_sems = refs[2 * na], refs[2 * na + 1]
        for k in range(1, N_DEV):
            to, pidx = _peer_of(k)
            for a in range(na):
                cp = pltpu.make_async_remote_copy(
                    src_ref=src_refs[a].at[pidx] if pieces else src_refs[a], dst_ref=land_refs[a].at[pidx],
                    send_sem=send_sems.at[7 * a + k - 1], recv_sem=recv_sems.at[7 * a + k - 1],
                    device_id=to, device_id_type=MESH_ID)
                cp.wait_send()
                cp.wait_recv()

    hbm = lambda arrs: [pltpu.HBM(a.shape, a.dtype) for a in arrs]
    outs = pl.pallas_call(
        body, name=name,
        out_shape=tuple(hbm(srcs) + hbm(lands)),
        in_specs=[_HBM_SPEC] * (2 * na) + [_SEM_SPEC, _SEM_SPEC, pl.BlockSpec(memory_space=pl.ANY)],
        out_specs=tuple([_HBM_SPEC] * (2 * na)),
        input_output_aliases={i: i for i in range(2 * na)},
        compiler_params=pltpu.CompilerParams(has_side_effects=_SIDE_EFFECT),
    )(*srcs, *lands, send_sems, recv_sems, after)
    return outs[:na], outs[na:]


def _own_slot(land, own):
    me = 4 * lax.axis_index("x") + 2 * lax.axis_index("y") + lax.axis_index("c")
    mine = lax.broadcasted_iota(jnp.int32, land.shape, 0) == me
    return jnp.where(mine, jnp.broadcast_to(own, land.shape), land)


def _exchange(name, pieces, bcast):
    n_p, n_b = len(pieces), len(bcast)
    na = n_p + n_b

    def body(*refs):
        src_refs, dst_refs = refs[:na], refs[na:2 * na]
        send_sems, recv_sems, local_sems = refs[2 * na:]
        x, y, c = lax.axis_index("x"), lax.axis_index("y"), lax.axis_index("c")
        me = 4 * x + 2 * y + c

        def src(a, slot):
            return src_refs[a].at[slot] if a < n_p else src_refs[a]

        own = [pltpu.make_async_copy(src(a, me), dst_refs[a].at[me], local_sems.at[a]) for a in range(na)]
        for cp in own:
            cp.start()

        def peer_of(k):
            px = 1 - x if k & 4 else x
            py = 1 - y if k & 2 else y
            pc = 1 - c if k & 1 else c
            return (px, py, pc), 4 * px + 2 * py + pc

        def copy(a, k, src_slot, dst_slot, to):
            return pltpu.make_async_remote_copy(
                src_ref=src(a, src_slot), dst_ref=dst_refs[a].at[dst_slot],
                send_sem=send_sems.at[7 * a + k - 1], recv_sem=recv_sems.at[7 * a + k - 1],
                device_id=to, device_id_type=MESH_ID)

        sent = []
        for k in range(1, N_DEV):
            to, pidx = peer_of(k)
            for a in range(na):
                cp = copy(a, k, pidx, me, to)
                cp.start()
                sent.append(cp)
        for k in range(1, N_DEV):
            to, pidx = peer_of(k)
            for a in range(na):
                copy(a, k, me, pidx, to).wait_recv()
        for cp in sent:
            cp.wait_send()
        for cp in own:
            cp.wait()

    arrays = list(pieces) + list(bcast)
    out_shape = [jax.ShapeDtypeStruct(p.shape, p.dtype) for p in pieces]
    out_shape += [jax.ShapeDtypeStruct((N_DEV,) + b.shape, b.dtype) for b in bcast]
    res = pl.pallas_call(
        body, name=name,
        in_specs=[pl.BlockSpec(memory_space=pl.ANY)] * na, out_specs=[pl.BlockSpec(memory_space=pl.ANY)] * na,
        out_shape=out_shape,
        scratch_shapes=[pltpu.SemaphoreType.DMA((7 * na,)), pltpu.SemaphoreType.DMA((7 * na,)),
                        pltpu.SemaphoreType.DMA((na,))],
    )(*arrays)
    return res[:n_p], res[n_p:]


def _pad_rows(flat, cols, row_mult):
    n = flat.shape[-1]
    unit = cols * row_mult
    padded = -(-n // unit) * unit
    pad = [(0, 0)] * (flat.ndim - 1) + [(0, padded - n)]
    return jnp.pad(flat, pad).reshape(flat.shape[:-1] + (padded // cols, cols))


def _perm_cols(w):
    aw = N_GROUPS * GROUP_W
    parts = [w[..., QKV_W:]]
    parts += [w[..., a * aw + g * GROUP_W:a * aw + (g + 1) * GROUP_W] for g in range(N_GROUPS) for a in range(3)]
    return jnp.concatenate(parts, axis=-1)


def _unperm_cols(wp, rest_w):
    qkv = wp[..., rest_w:]
    parts = [qkv[..., g * GROUP_QKV_W + a * GROUP_W:g * GROUP_QKV_W + (a + 1) * GROUP_W]
             for a in range(3) for g in range(N_GROUPS)]
    return jnp.concatenate(parts + [wp[..., :rest_w]], axis=-1)


SHARD_AXIS = dict(SHARDED)


def _layer_shards(wts, li, names, zero):
    out = []
    for n in names:
        w = wts[n][li] if zero is None else wts[n][li] + zero
        out.append(w if n in EXACT_F32 else w.astype(BF16))
    return out


def _assemble_weights(names, segs):
    W = {}
    for n, seg in zip(names, segs):
        ax = SHARD_AXIS[n]
        shp = seg.shape[1:]
        seg = jnp.moveaxis(seg, 0, ax)
        W[n] = seg.reshape(shp[:ax] + (N_DEV * shp[ax],) + shp[ax + 1:])
    if "w_in" in W:
        W["w_in"] = _perm_cols(W["w_in"])
    return W


def _grad_pieces(gr):
    out = []
    for n in gr:
        ax = SHARD_AXIS[n]
        shp = gr[n].shape
        g = gr[n].reshape(shp[:ax] + (N_DEV, shp[ax] // N_DEV) + shp[ax + 1:])
        out.append(jnp.moveaxis(g, ax, 0).astype(BF16))
    return out


def _pack_small(vals):
    flat = jnp.concatenate([vals[n].astype(F32).reshape(-1) for n in REPLICATED])
    return _pad_rows(flat, LANE, 8)


def _layer_fwd(li, x, p_l, w_in, other_weights, G, rope, Bl, S, F):
    T, D = x.shape
    rest_w = 3 * D
    sv = {"x0": x}
    W = {"w_in": w_in}
    hs = _rms_mix_fwd(f"rms_mix_{li}", x, G["g_mix"], S)
    h = hs[0]
    h_g = [h] + [a.reshape(T, D) for a in hs[1:]]
    sv["h_g"] = h_g
    zr = _mm_nn(f"rest_proj_{li}", h, W["w_in"], n_cols=rest_w, tn=1024, tk=_tile(D, 1024))
    sv["zr"] = zr
    qkv_l, o_l, lse_l = [], [], []
    for g in range(N_GROUPS):
        qkv = _qkv_proj(f"qkv_proj_{li}_{g}", h_g[g], W["w_in"], rope[g], S, rest_w + g * GROUP_QKV_W)
        o, lse = _attn_fwd(f"attn_fwd_{li}_{g}", qkv, g, Bl, S)
        qkv_l.append(qkv)
        o_l.append(o)
        lse_l.append(lse)
    sv["qkv"], sv["o"], sv["lse"] = qkv_l, o_l, lse_l
    W.update(other_weights(o_l[-1]))
    x1, attn, pooled, mixed, y_a, y_b, merged = _mix_out_fwd(
        f"mix_out_fwd_{li}", x, zr, o_l, lse_l, W["w_ya"], W["pool_w"], G["pool_scale"], W["w_yb"], W["w_o"], S)
    sv.update(x1=x1, attn=attn, pooled=pooled, mixed=mixed, y_a=y_a, y_b=y_b, merged=merged)
    h2 = _rms_fwd(f"rms_ffn_{li}", x1, G["g_ffn"])
    u = _up_proj(f"up_proj_{li}", h2, W["w_up"], F)
    a = _ffn_act_fwd(f"ffn_act_fwd_{li}", u, W["conv_w"], G["conv_b"], S)
    x2 = _mm_nn(f"down_proj_{li}", a, W["w_down"], add=x1, tk=_tile(F, 1408))
    sv.update(h2=h2, u=u, a=a, x2=x2)
    x3, e, pg, p_bf = _ple_fwd(f"ple_fwd_{li}", x2, p_l, G["g_ple"], W["w_ple_gate"], W["w_ple"])
    sv.update(e=e, pg=pg, p_bf=p_bf)
    return x3, sv, W


EARLY_GRADS = ("w_ple", "w_ple_gate", "w_down", "conv_w", "w_up")
LATE_GRADS = ("w_in", "w_ya", "w_yb", "pool_w", "w_o")


def _layer_bwd(li, dx3, sv, W, G, rope, Bl, S, F, send):
    T, D = dx3.shape
    rest_w = 3 * D
    gr = {}
    dx2, d_e, d_s, h3, dg = _ple_bwd(f"ple_bwd_{li}", dx3, sv["x2"], sv["e"], sv["pg"], G["g_ple"], W["w_ple_gate"])
    gr["g_ple"] = dg[0]
    gr["w_ple"] = _mm_tn(f"w_ple_grad_{li}", sv["p_bf"], d_e)
    gr["w_ple_gate"] = _mm_tn(f"w_ple_gate_grad_{li}", h3, d_s)

    d_a = _mm_nt(f"down_bwd_{li}", dx2, W["w_down"], tn=1408, tk=_tile(D, 1024))
    gr["w_down"] = _mm_tn(f"w_down_grad_{li}", sv["a"], dx2, tm=1408)
    du, d_cw, d_cb = _ffn_act_bwd(f"ffn_act_bwd_{li}", sv["u"], d_a, W["conv_w"], G["conv_b"], S)
    gr["conv_w"] = d_cw.transpose(1, 0, 2).reshape(3, 2 * F)
    gr["conv_b"] = d_cb.reshape(2 * F)
    tk_f = _tile(F, 1408)
    nkh = F // tk_f
    tm_r = _tile(T, 1024, 8)
    dx1, dg = _mm_nt_rmsbwd(f"up_bwd_{li}", du, (None, tm_r, tk_f), lambda i, j, k: (k // nkh, i, k % nkh),
                            2 * nkh, tk_f, W["w_up"], sv["x1"], G["g_ffn"], dx2)
    gr["g_ffn"] = dg[0]
    gr["w_up"] = _up_w_grad(f"w_up_grad_{li}", sv["h2"], du)
    zero = send("a", {n: gr[n] for n in EARLY_GRADS})

    (dz_rest, d_ya, d_yb, d_pm, do0, do1, do2, c0, c1, c2, dps) = _mix_out_bwd(
        f"mix_out_bwd_{li}", dx1, sv["zr"], sv["y_a"], sv["y_b"], sv["o"], sv["lse"], sv["pooled"],
        W["w_o"], W["w_ya"], W["w_yb"], W["pool_w"], G["pool_scale"] + zero, S)
    gr["pool_scale"] = dps[0]
    gr["w_o"] = _mm_tn(f"w_o_grad_{li}", sv["merged"], dx1)
    gr["w_ya"] = _mm_tn(f"w_ya_grad_{li}", sv["attn"], d_ya)
    gr["w_yb"] = _mm_tn(f"w_yb_grad_{li}", sv["mixed"], d_yb)
    gr["pool_w"] = _pool_w_grad(f"pool_w_grad_{li}", sv["pooled"], d_pm, D // len(POOL_WINDOWS))
    segs = [(dz_rest, 1)]
    for g, (do, cst) in enumerate(((do0, c0), (do1, c1), (do2, c2))):
        dqkv = _attn_bwd(f"attn_bwd_{li}_{g}", sv["qkv"][g], do, sv["lse"][g], cst, rope[g], g, Bl, S)
        segs.append((dqkv, DILATIONS[g]))

    h_rows = [sv["h_g"][0]] + sv["h_g"]
    w_in_parts = [_mm_tn(f"w_in_grad_{li}_{s}", h_rows[s], seg, tn=1536) for s, (seg, _) in enumerate(segs)]
    gr["w_in"] = _unperm_cols(jnp.concatenate(w_in_parts, axis=1), rest_w)
    zero = send("b", {n: gr[n] for n in LATE_GRADS})
    dx0, dg = _in_bwd(f"in_bwd_{li}", segs, W["w_in"], sv["x0"], G["g_mix"] + zero, dx1, S)
    gr["g_mix"] = dg[0]
    return dx0, gr


def kernel(x, p, g_mix, w_in, w_ya, w_yb, pool_w, pool_scale, w_o, g_ffn, w_up, conv_w, conv_b, w_down, g_ple, w_ple, w_ple_gate, g_final, loss_target, m_g_mix, m_w_in, m_w_ya, m_w_yb, m_pool_w, m_pool_scale, m_w_o, m_g_ffn, m_w_up, m_conv_w, m_conv_b, m_w_down, m_g_ple, m_w_ple, m_w_ple_gate, m_g_final, v_g_mix, v_w_in, v_w_ya, v_w_yb, v_pool_w, v_pool_scale, v_w_o, v_g_ffn, v_w_up, v_conv_w, v_conv_b, v_w_down, v_g_ple, v_w_ple, v_w_ple_gate, v_g_final):
    wts = dict(g_mix=g_mix, w_in=w_in, w_ya=w_ya, w_yb=w_yb, pool_w=pool_w, pool_scale=pool_scale, w_o=w_o,
               g_ffn=g_ffn, w_up=w_up, conv_w=conv_w, conv_b=conv_b, w_down=w_down, g_ple=g_ple, w_ple=w_ple,
               w_ple_gate=w_ple_gate, g_final=g_final)
    mom = dict(g_mix=m_g_mix, w_in=m_w_in, w_ya=m_w_ya, w_yb=m_w_yb, pool_w=m_pool_w, pool_scale=m_pool_scale,
               w_o=m_w_o, g_ffn=m_g_ffn, w_up=m_w_up, conv_w=m_conv_w, conv_b=m_conv_b, w_down=m_w_down,
               g_ple=m_g_ple, w_ple=m_w_ple, w_ple_gate=m_w_ple_gate, g_final=m_g_final)
    var = dict(g_mix=v_g_mix, w_in=v_w_in, w_ya=v_w_ya, w_yb=v_w_yb, pool_w=v_pool_w, pool_scale=v_pool_scale,
               w_o=v_w_o, g_ffn=v_g_ffn, w_up=v_w_up, conv_w=v_conv_w, conv_b=v_conv_b, w_down=v_w_down,
               g_ple=v_g_ple, w_ple=v_w_ple, w_ple_gate=v_w_ple_gate, g_final=v_g_final)
    Bl, S, D = x.shape
    depth = g_mix.shape[0]
    F = w_down.shape[1] * N_DEV
    T = Bl * S
    assert S % (BLOCK * DILATIONS[-1]) == 0 and D % GROUP_W == 0 and F % LANE == 0
    rope = [tuple(t if d == 1 else t.reshape(S // d, d, HEAD_DIM).transpose(1, 0, 2).reshape(S, HEAD_DIM)
                  for t in _rope_tables(S)) for d in DILATIONS]

    first, others = ("w_in",), tuple(n for n, _ in SHARDED if n != "w_in")
    got_in, tok = _all_gather("gather_w0_in", _layer_shards(wts, 0, first, None))
    gathers = {}
    for li in range(depth):
        names = others if li == 0 else first + others
        *gathers[li], tok = _send_start(f"gather_w{li}_start", _layer_shards(wts, li, names, tok[0, 0]), False)

    def gathered(li, names, after):
        shards, lands = _send_wait(f"gather_w{li}_wait", *gathers[li], False, after)
        return _assemble_weights(names, [_own_slot(l, s) for l, s in zip(lands, shards)])

    xs = x.reshape(T, D)
    saved = []
    for li in range(depth):
        G = {n: wts[n][li] for n in REPLICATED if n != "g_final"}
        if li == 0:
            G["g_mix"] = G["g_mix"] + tok[0, 0]
            w_in_full = _assemble_weights(first, got_in)["w_in"]
            rest_of = lambda after: gathered(0, others, after)
        else:
            W_all = gathered(li, first + others, xs)
            w_in_full = W_all["w_in"]
            rest_of = lambda after, W_all=W_all: W_all
        xs, sv, W = _layer_fwd(li, xs, p[li].reshape(T, -1), w_in_full, rest_of, G, rope, Bl, S, F)
        saved.append((sv, W, G))

    dx, loss_row, dg_final = _loss_bwd("loss_bwd", xs, loss_target.reshape(T, D), g_final)
    layer_grads = [None] * depth
    sends = []
    zero = [None]
    for li in reversed(range(depth)):
        sv, W, G = saved[li]
        if zero[0] is not None:
            G = dict(G, g_ple=G["g_ple"] + zero[0])

        def send(tag, group, li=li):
            *handles, tok = _send_start(f"exchange_g{li}{tag}_start", _grad_pieces(group), True)
            sends.append((li, tag, tuple(group), handles))
            zero[0] = tok[0, 0]
            return zero[0]

        dx, layer_grads[li] = _layer_bwd(li, dx, sv, W, G, rope, Bl, S, F, send)

    recv = {}
    for li, tag, names, handles in sends:
        pcs, lands = _send_wait(f"exchange_g{li}{tag}_wait", *handles, True, dx)
        for n, l, s in zip(names, lands, pcs):
            recv[(li, n)] = _own_slot(l, s)
    grads = {n: jnp.stack([layer_grads[li][n] for li in range(depth)]) for n in REPLICATED if n != "g_final"}
    grads["g_final"] = dg_final[0]
    _, (small_all,) = _exchange("exchange_small", [], [_pack_small(grads)])

    out_g, out_d, out_m, out_v = {}, {}, {}, {}
    for n, _ in SHARDED:
        shp = wts[n].shape
        two_d = (math.prod(shp[:-1]), shp[-1])
        pieces = [recv[(li, n)].reshape(N_DEV, two_d[0] // depth, two_d[1]) for li in range(depth)]
        res = _adamw(f"adamw_{n}", wts[n].reshape(two_d), mom[n].reshape(two_d), var[n].reshape(two_d), pieces)
        out_g[n], out_d[n], out_m[n], out_v[n] = [r.reshape(shp) for r in res]
    res = _adamw("adamw_replicated", _pack_small(wts), _pack_small(mom), _pack_small(var), [small_all])
    off = 0
    for n in REPLICATED:
        shp = wts[n].shape
        size = math.prod(shp)
        for dst, r in zip((out_g, out_d, out_m, out_v), res):
            dst[n] = r.reshape(-1)[off:off + size].reshape(shp)
        off += size

    loss = lax.psum(loss_row[0, 0], MESH_AXES)
    outs = [loss, dx.reshape(Bl, S, D)]
    for dct in (out_g, out_d, out_m, out_v):
        outs += [dct[n] for n in WEIGHT_ORDER]
    return tuple(outs)
```

```python
import math

import jax
import jax.numpy as jnp
from jax import lax
from jax.experimental import pallas as pl
from jax.experimental.pallas import tpu as pltpu

F32 = jnp.float32
BF16 = jnp.bfloat16

N_DEV = 8
HEAD_DIM = 128
HEADS = 4
GROUP_W = HEADS * HEAD_DIM
DILATIONS = (1, 4, 16)
N_GROUPS = len(DILATIONS)
QKV_W = 3 * N_GROUPS * GROUP_W
GROUP_QKV_W = 3 * GROUP_W
BLOCK = 128
ROPE_DIM = HEAD_DIM // 4
ROPE_HALF = ROPE_DIM // 2
ROPE_THETA = 500000.0
NEG_INF = -1e30
POOL_WINDOWS = (2, 4, 8, 16)
POOL_HALO = 16
CONV_HALO = 16
RMS_EPS = 1e-6
ADAM_LR = 0.001
ADAM_B1 = 0.9
ADAM_B2 = 0.999
ADAM_EPS = 1e-08
ADAM_WD = 0.01
ADAM_STEP = 10
LANE = 128
PACK_COLS = 1024
PACK_ROWS = 16
MESH_ID = pl.DeviceIdType.MESH
MESH_AXES = ("x", "y", "c")

NT_DIMS = (((1,), (1,)), ((), ()))
TN_DIMS = (((0,), (0,)), ((), ()))
NN_DIMS = (((1,), (0,)), ((), ()))

SHARDED = (("w_in", 1), ("w_ya", 1), ("w_yb", 0), ("pool_w", 1), ("w_o", 0), ("w_up", 1), ("conv_w", 1),
           ("w_down", 0), ("w_ple", 1), ("w_ple_gate", 0))
EXACT_F32 = ("conv_w",)
REPLICATED = ("g_mix", "pool_scale", "g_ffn", "conv_b", "g_ple", "g_final")
WEIGHT_ORDER = ("g_mix", "w_in", "w_ya", "w_yb", "pool_w", "pool_scale", "w_o", "g_ffn", "w_up", "conv_w", "conv_b",
                "w_down", "g_ple", "w_ple", "w_ple_gate", "g_final")


def _tile(n, pref, mult=LANE):
    if n <= pref:
        return n
    t = (pref // mult) * mult
    while t >= mult:
        if n % t == 0:
            return t
        t -= mult
    return n


def _sigmoid(x):
    return 1.0 / (1.0 + jnp.exp(-x))


def _dot(a, b, dims=NN_DIMS):
    return lax.dot_general(a.astype(BF16), b.astype(BF16), dims, preferred_element_type=F32)


def _rstd(x):
    return lax.rsqrt(jnp.mean(x * x, axis=-1, keepdims=True) + RMS_EPS)


def _rms_bwd(x, g, dh):
    r = _rstd(x)
    u = dh * g
    dx = r * u - x * (r * r * r) * jnp.mean(x * u, axis=-1, keepdims=True)
    return dx, dh * x * r


def _full(a):
    return pl.BlockSpec(a.shape, lambda *_: (0,) * a.ndim)


def _mm(name, a, b, *, grid, a_block, a_map, b_block, b_map, dims, acc_shape, outs, extras=(), epi=None):
    nk = grid[2]
    n_ex = len(extras)
    n_out = len(outs)

    def body(*refs):
        a_ref, b_ref = refs[0], refs[1]
        ex = refs[2:2 + n_ex]
        o = refs[2 + n_ex:2 + n_ex + n_out]
        acc = refs[2 + n_ex + n_out]
        i, j, k = pl.program_id(0), pl.program_id(1), pl.program_id(2)

        @pl.when(k == 0)
        def _():
            acc[...] = jnp.zeros_like(acc)

        acc[...] += _dot(a_ref[...], b_ref[...], dims)

        @pl.when(k == nk - 1)
        def _():
            if epi is None:
                o[0][...] = acc[...].astype(o[0].dtype)
            else:
                epi(acc[...], ex, o, i, j)

    in_specs = [pl.BlockSpec(a_block, a_map), pl.BlockSpec(b_block, b_map)]
    in_specs += [pl.BlockSpec(blk, mp) for (_, blk, mp) in extras]
    out_specs = [pl.BlockSpec(blk, mp) for (_, _, blk, mp) in outs]
    out_shape = [jax.ShapeDtypeStruct(s, d) for (s, d, _, _) in outs]
    return pl.pallas_call(
        body, name=name, grid=grid, in_specs=in_specs, out_specs=out_specs, out_shape=out_shape,
        scratch_shapes=[pltpu.VMEM(acc_shape, F32)],
    )(a, b, *[e[0] for e in extras])


def _mm_nn(name, a, b, *, out_dtype=F32, tm=1024, tn=1024, tk=512, b_col_off=0, n_cols=None, add=None):
    M, K = a.shape
    N = n_cols if n_cols is not None else b.shape[1]
    tm, tn, tk = _tile(M, tm, 8), _tile(N, tn), _tile(K, tk)
    assert b_col_off % tn == 0
    joff = b_col_off // tn
    extras, epi = (), None
    if add is not None:
        extras = ((add, (tm, tn), lambda i, j, k: (i, j)),)

        def epi(acc, ex, o, i, j):
            o[0][...] = (acc + ex[0][...]).astype(o[0].dtype)

    return _mm(name, a, b, grid=(M // tm, N // tn, K // tk),
               a_block=(tm, tk), a_map=lambda i, j, k: (i, k),
               b_block=(tk, tn), b_map=lambda i, j, k: (k, j + joff),
               dims=NN_DIMS, acc_shape=(tm, tn),
               outs=[((M, N), out_dtype, (tm, tn), lambda i, j, k: (i, j))], extras=extras, epi=epi)[0]


def _mm_nt(name, a, b, *, out_dtype=F32, tm=1024, tn=1024, tk=512):
    M, K = a.shape
    N = b.shape[0]
    tm, tn, tk = _tile(M, tm, 8), _tile(N, tn), _tile(K, tk)
    return _mm(name, a, b, grid=(M // tm, N // tn, K // tk),
               a_block=(tm, tk), a_map=lambda i, j, k: (i, k),
               b_block=(tn, tk), b_map=lambda i, j, k: (j, k),
               dims=NT_DIMS, acc_shape=(tm, tn),
               outs=[((M, N), out_dtype, (tm, tn), lambda i, j, k: (i, j))])[0]


def _mm_tn(name, a, b, *, tm=1024, tn=1024, tk=1024):
    K, M = a.shape
    N = b.shape[1]
    tm, tn, tk = _tile(M, tm), _tile(N, tn), _tile(K, tk, 8)
    return _mm(name, a, b, grid=(M // tm, N // tn, K // tk),
               a_block=(tk, tm), a_map=lambda i, j, k: (k, i),
               b_block=(tk, tn), b_map=lambda i, j, k: (k, j),
               dims=TN_DIMS, acc_shape=(tm, tn),
               outs=[((M, N), F32, (tm, tn), lambda i, j, k: (i, j))])[0]


def _mm_nt_rmsbwd(name, a, a_block, a_map, nk, tk, w, x, g, dres):
    T, D = x.shape
    tm = a_block[-2]

    def epi(acc, ex, o, i, j):
        @pl.when(i == 0)
        def _():
            o[1][...] = jnp.zeros_like(o[1])

        dx, dgr = _rms_bwd(ex[0][...], ex[1][...], acc)
        o[0][...] = ex[2][...] + dx
        o[1][...] += jnp.sum(dgr, axis=0, keepdims=True)

    row = lambda i, j, k: (i, 0)
    vec = lambda i, j, k: (0, 0)
    return _mm(name, a, w, grid=(T // tm, 1, nk),
               a_block=a_block, a_map=a_map,
               b_block=(D, tk), b_map=lambda i, j, k: (0, k),
               dims=NT_DIMS, acc_shape=(tm, D),
               outs=[((T, D), F32, (tm, D), row), ((1, D), F32, (1, D), vec)],
               extras=[(x, (tm, D), row), (g.reshape(1, D), (1, D), vec), (dres, (tm, D), row)], epi=epi)


def _in_bwd(name, segs, w_perm, x, g, dres, S):
    T, D = x.shape
    tm = _tile(S, 512, 256)
    nst = S // tm
    tk = GROUP_QKV_W
    steps = [a.shape[1] // tk for a, _ in segs]
    starts = [sum(steps[:s]) for s in range(len(segs))]
    nk = sum(steps)
    ns = len(segs)
    cols = _chunks(D)
    assert all(a.shape[1] % tk == 0 for a, _ in segs) and nk * tk == w_perm.shape[1]

    def body(*refs):
        a_refs = refs[:ns]
        w_ref, x_ref, g_ref, dres_ref, dx_ref, dg_ref, acc = refs[ns:]
        i, k = pl.program_id(0), pl.program_id(1)

        @pl.when(k == 0)
        def _():
            acc[...] = jnp.zeros_like(acc)

        for s in range(ns):
            d = segs[s][1]

            @pl.when((k >= starts[s]) & (k < starts[s] + steps[s]))
            def _():
                prod = _dot(a_refs[s][...].reshape(tm, tk), w_ref[...], NT_DIMS)
                q = tm // d
                for c, cs in enumerate(cols):
                    if d == 1:
                        acc[c] += prod[:, cs]
                    else:
                        for r in range(d):
                            acc[c, pl.ds(r, q, stride=d), :] += prod[r * q:(r + 1) * q, cs]

        @pl.when(k == nk - 1)
        def _():
            @pl.when(i == 0)
            def _():
                dg_ref[...] = jnp.zeros_like(dg_ref)

            dh = jnp.concatenate([acc[c] for c in range(len(cols))], axis=1)
            dx, dgr = _rms_bwd(x_ref[...], g_ref[...], dh)
            dx_ref[...] = dres_ref[...] + dx
            dg_ref[...] += jnp.sum(dgr, axis=0, keepdims=True)

    def seg_spec(s):
        kmap = lambda k: jnp.clip(k - starts[s], 0, steps[s] - 1)
        d = segs[s][1]
        if d == 1:
            return pl.BlockSpec((tm, tk), lambda i, k: (i, kmap(k)))
        return pl.BlockSpec((None, d, tm // d, tk), lambda i, k: (i // nst, 0, i % nst, kmap(k)))

    views = [a if d == 1 else a.reshape(T // S, d, S // d, a.shape[1]) for a, d in segs]
    row = pl.BlockSpec((tm, D), lambda i, k: (i, 0))
    vec = pl.BlockSpec((1, D), lambda i, k: (0, 0))
    return pl.pallas_call(
        body, name=name, grid=(T // tm, nk),
        in_specs=[seg_spec(s) for s in range(ns)] + [pl.BlockSpec((D, tk), lambda i, k: (0, k)), row, vec, row],
        out_specs=[row, vec],
        out_shape=[jax.ShapeDtypeStruct((T, D), F32), jax.ShapeDtypeStruct((1, D), F32)],
        scratch_shapes=[pltpu.VMEM((D // LANE, tm, LANE), F32)],
    )(*views, w_perm, x, g.reshape(1, D), dres)


def _rope_tables(S):
    pos = jnp.arange(S, dtype=F32)
    inv_freq = jnp.exp(jnp.arange(0, ROPE_DIM, 2, dtype=F32) * (-math.log(ROPE_THETA) / ROPE_DIM))
    ang = pos[:, None] * inv_freq[None, :]
    cos, sin = jnp.cos(ang), jnp.sin(ang)
    ones = jnp.ones((S, HEAD_DIM - ROPE_DIM), F32)
    zeros_h = jnp.zeros((S, ROPE_HALF), F32)
    zeros_r = jnp.zeros((S, HEAD_DIM - ROPE_DIM), F32)
    c = jnp.concatenate([cos, cos, ones], axis=1)
    sa = jnp.concatenate([-sin, zeros_h, zeros_r], axis=1)
    sb = jnp.concatenate([zeros_h, sin, zeros_r], axis=1)
    return c, sa, sb


def _rope(t, c, sa, sb):
    return t * c + pltpu.roll(t, HEAD_DIM - ROPE_HALF, 1) * sa + pltpu.roll(t, ROPE_HALF, 1) * sb


def _rms_fwd(name, x, g):
    T, D = x.shape
    tm = _tile(T, 512, 8)

    def body(x_ref, g_ref, h_ref):
        xv = x_ref[...]
        h_ref[...] = (xv * _rstd(xv) * g_ref[...]).astype(BF16)

    return pl.pallas_call(
        body, name=name, grid=(T // tm,),
        in_specs=[pl.BlockSpec((tm, D), lambda i: (i, 0)), pl.BlockSpec((1, D), lambda i: (0, 0))],
        out_specs=pl.BlockSpec((tm, D), lambda i: (i, 0)),
        out_shape=jax.ShapeDtypeStruct((T, D), BF16))(x, g.reshape(1, D))


def _chunks(width):
    return [slice(c * LANE, (c + 1) * LANE) for c in range(width // LANE)]


def _store_residue_major(val, sc, out_ref, d, dtype):
    rows = val.shape[0]
    for c, cs in enumerate(_chunks(val.shape[1])):
        sc[c] = val[:, cs]
    for r in range(d):
        for c, cs in enumerate(_chunks(val.shape[1])):
            out_ref[r, :, cs] = sc[c, pl.ds(r, rows // d, stride=d), :].astype(dtype)


def _load_token_order(blk_ref, sc, d):
    _, q, width = blk_ref.shape
    for r in range(d):
        for c, cs in enumerate(_chunks(width)):
            sc[c, pl.ds(r, q, stride=d), :] = blk_ref[r, :, cs]
    return jnp.concatenate([sc[c] for c in range(width // LANE)], axis=1)


def _residue_major_spec(d, q, width, nst):
    return pl.BlockSpec((None, d, q, width), lambda i, *_: (i // nst, 0, i % nst, 0))


def _rms_mix_fwd(name, x, g, S):
    T, D = x.shape
    Bl = T // S
    tm = _tile(S, 512, 256)
    nst = S // tm
    dils = [d for d in DILATIONS if d > 1]

    def body(x_ref, g_ref, h_ref, *rest):
        rm_refs, sc = rest[:len(dils)], rest[len(dils)]
        xv = x_ref[...]
        hv = xv * _rstd(xv) * g_ref[...]
        h_ref[...] = hv.astype(BF16)
        for d, o_ref in zip(dils, rm_refs):
            _store_residue_major(hv, sc, o_ref, d, BF16)

    row = pl.BlockSpec((tm, D), lambda i: (i, 0))
    return pl.pallas_call(
        body, name=name, grid=(T // tm,),
        in_specs=[row, pl.BlockSpec((1, D), lambda i: (0, 0))],
        out_specs=[row] + [_residue_major_spec(d, tm // d, D, nst) for d in dils],
        out_shape=[jax.ShapeDtypeStruct((T, D), BF16)]
        + [jax.ShapeDtypeStruct((Bl, d, S // d, D), BF16) for d in dils],
        scratch_shapes=[pltpu.VMEM((D // LANE, tm, LANE), F32)])(x, g.reshape(1, D))


def _qkv_proj(name, h, w_perm, rope, S, col_off):
    T, D = h.shape
    tm = _tile(S, 1024, 8)
    tn = GROUP_W
    tk = _tile(D, 1024)
    c_t, sa_t, sb_t = rope
    n_seq_tiles = S // tm
    joff = col_off // tn
    tmap = lambda i, j, k: (i % n_seq_tiles, 0)

    def epi(acc, ex, o, i, j):
        is_rot = j < 2

        @pl.when(is_rot)
        def _():
            c, sa, sb = ex[0][...], ex[1][...], ex[2][...]
            for hh in range(HEADS):
                sl = slice(hh * HEAD_DIM, (hh + 1) * HEAD_DIM)
                o[0][:, sl] = _rope(acc[:, sl], c, sa, sb).astype(BF16)

        @pl.when(jnp.logical_not(is_rot))
        def _():
            o[0][...] = acc.astype(BF16)

    return _mm(name, h, w_perm, grid=(T // tm, 3, D // tk),
               a_block=(tm, tk), a_map=lambda i, j, k: (i, k),
               b_block=(tk, tn), b_map=lambda i, j, k: (k, j + joff),
               dims=NN_DIMS, acc_shape=(tm, tn),
               outs=[((T, GROUP_QKV_W), BF16, (tm, tn), lambda i, j, k: (i, j))],
               extras=[(c_t, (tm, HEAD_DIM), tmap), (sa_t, (tm, HEAD_DIM), tmap), (sb_t, (tm, HEAD_DIM), tmap)],
               epi=epi)[0]


ATTN_BLOCKS_PER_STEP = 4


def _attn_mask(rows, lead, has_prev):
    qi = lax.broadcasted_iota(jnp.int32, (rows, lead + rows), 0)
    ki = lax.broadcasted_iota(jnp.int32, (rows, lead + rows), 1)
    diff = lead + qi - ki
    band = (diff >= 0) & (diff <= BLOCK)
    return band & (has_prev | (ki >= lead)) if lead else band


def _attn_chunking(nb):
    cb = min(ATTN_BLOCKS_PER_STEP, nb)
    return cb, ATTN_BLOCKS_PER_STEP // cb


def _attn_fwd(name, qkv, g, Bl, S):
    d = DILATIONS[g]
    L = S // d
    nb = L // BLOCK
    cb, cs = _attn_chunking(nb)
    qv = qkv.reshape(Bl * d, L, GROUP_QKV_W)
    scale = HEAD_DIM ** -0.5

    qc = cb * BLOCK
    lead = BLOCK if nb > 1 else 0

    def body(q_ref, kc_ref, vc_ref, kp_ref, vp_ref, o_ref, l_ref):
        valid = _attn_mask(qc, lead, pl.program_id(1) > 0)
        for si in range(cs):
            for hh in range(HEADS):
                sl = slice(hh * HEAD_DIM, (hh + 1) * HEAD_DIM)
                kk, vv = kc_ref[si, :, sl], vc_ref[si, :, sl]
                if lead:
                    kk = jnp.concatenate([kp_ref[si, :, sl], kk], axis=0)
                    vv = jnp.concatenate([vp_ref[si, :, sl], vv], axis=0)
                s = jnp.where(valid, _dot(q_ref[si, :, sl], kk, NT_DIMS) * scale, NEG_INF)
                m = jnp.max(s, axis=-1, keepdims=True)
                p = jnp.exp(s - m)
                l = jnp.sum(p, axis=-1, keepdims=True)
                o_ref[si, :, sl] = _dot(p, vv) / l
                l_ref[si, :, sl] = jnp.broadcast_to(m + jnp.log(l), (qc, HEAD_DIM))

    main = (cs, cb * BLOCK, GROUP_W)
    edge = (cs, BLOCK, GROUP_W)
    cur = lambda off: pl.BlockSpec(main, lambda s, n: (s, n, off))
    prev = lambda off: pl.BlockSpec(edge, lambda s, n: (s, jnp.maximum(n * cb - 1, 0), off))
    out = pl.BlockSpec(main, lambda s, n: (s, n, 0))
    return pl.pallas_call(
        body, name=name, grid=(Bl * d // cs, nb // cb),
        in_specs=[cur(0), cur(1), cur(2), prev(1), prev(2)],
        out_specs=[out, out],
        out_shape=[jax.ShapeDtypeStruct((Bl * d, L, GROUP_W), F32)] * 2)(qv, qv, qv, qv, qv)


def _merge_weights(l0, l1, l2):
    mx = jnp.maximum(jnp.maximum(l0, l1), l2)
    e0, e1, e2 = jnp.exp(l0 - mx), jnp.exp(l1 - mx), jnp.exp(l2 - mx)
    inv = 1.0 / (e0 + e1 + e2)
    return e0 * inv, e1 * inv, e2 * inv


def _group_specs(tm, nst, S):
    specs = []
    for d in DILATIONS:
        specs.append(pl.BlockSpec((tm, GROUP_W), lambda i: (i, 0)) if d == 1
                     else _residue_major_spec(d, tm // d, GROUP_W, nst))

    def views(arrs):
        out = []
        for d, a in zip(DILATIONS, arrs):
            out.append(a.reshape(-1, GROUP_W) if d == 1 else a.reshape(-1, d, S // d, GROUP_W))
        return out

    return specs, views


def _group_scratch(tm, per_group):
    n = per_group * sum(1 for d in DILATIONS if d > 1)
    return [pltpu.VMEM((GROUP_W // LANE, tm, LANE), F32) for _ in range(n)]


def _group_values(o_refs, l_refs, scs):
    scs = list(scs)
    ov, lv = [], []
    for d, o_ref, l_ref in zip(DILATIONS, o_refs, l_refs):
        if d == 1:
            ov.append(o_ref[...])
            lv.append(l_ref[...])
        else:
            ov.append(_load_token_order(o_ref, scs.pop(), d))
            lv.append(_load_token_order(l_ref, scs.pop(), d))
    return ov, lv


def _pool_inv_count(tseq, w):
    return 1.0 / jnp.minimum(tseq + 1, w).astype(F32)


def _mix_out_fwd(name, x, zr, o_l, lse_l, w_ya, pool_w, pool_scale, w_yb, w_o, S):
    T, D = x.shape
    gw = D // len(POOL_WINDOWS)
    tm = _tile(S, 256, POOL_HALO)
    nst = S // tm
    hpt = tm // POOL_HALO

    def body(x_ref, u_ref, uh_ref, ga_ref, gb_ref, o0, o1, o2, l0, l1, l2, wya_ref, pw_ref, ps_ref, wyb_ref, wo_ref,
             x1_ref, attn_ref, pooled_ref, mixed_ref, ya_ref, yb_ref, merged_ref, *scs):
        it = pl.program_id(0) % nst
        ov, lv = _group_values((o0, o1, o2), (l0, l1, l2), scs)
        w0, w1, w2 = _merge_weights(*lv)
        attn = w0 * ov[0] + w1 * ov[1] + w2 * ov[2]
        attn_ref[...] = attn.astype(BF16)
        y_a = _dot(attn, wya_ref[...])

        u = u_ref[...]
        halo = uh_ref[...] * jnp.where(it == 0, 0.0, 1.0)
        ext = jnp.concatenate([halo, u], axis=0)
        tseq = it * tm + lax.broadcasted_iota(jnp.int32, (tm, 1), 0)
        pm_parts = []
        for gi, w in enumerate(POOL_WINDOWS):
            cs = slice(gi * gw, (gi + 1) * gw)
            s = ext[:, cs]
            step = 1
            while step < w:
                s = s + pltpu.roll(s, step, 0)
                step *= 2
            pooled_g = (s[POOL_HALO:, :] * _pool_inv_count(tseq, w) - u[:, cs]).astype(BF16)
            pooled_ref[:, cs] = pooled_g
            pm_parts.append(_dot(pooled_g, pw_ref[gi]))
        mixed = (jnp.concatenate(pm_parts, axis=1) * ps_ref[...]).astype(BF16)
        mixed_ref[...] = mixed
        y_b = _dot(mixed, wyb_ref[...])
        merged = (_sigmoid(ga_ref[...]) * y_a + _sigmoid(gb_ref[...]) * y_b).astype(BF16)
        ya_ref[...] = y_a
        yb_ref[...] = y_b
        merged_ref[...] = merged
        x1_ref[...] = x_ref[...] + _dot(merged, wo_ref[...])

    row = lambda c: pl.BlockSpec((tm, D), lambda i: (i, c))
    row512 = pl.BlockSpec((tm, GROUP_W), lambda i: (i, 0))
    ps = pool_scale.reshape(1, D)
    halo_spec = pl.BlockSpec((POOL_HALO, D), lambda i: (jnp.maximum(i * hpt - 1, 0), 0))
    grp_specs, grp_views = _group_specs(tm, nst, S)
    return pl.pallas_call(
        body, name=name, grid=(T // tm,),
        in_specs=[row(0), row(0), halo_spec, row(1), row(2)] + grp_specs * 2
        + [_full(w_ya), _full(pool_w), _full(ps), _full(w_yb), _full(w_o)],
        out_specs=[row(0), row512, row(0), row(0), row(0), row(0), row(0)],
        out_shape=[jax.ShapeDtypeStruct((T, D), F32), jax.ShapeDtypeStruct((T, GROUP_W), BF16),
                   jax.ShapeDtypeStruct((T, D), BF16), jax.ShapeDtypeStruct((T, D), BF16),
                   jax.ShapeDtypeStruct((T, D), F32), jax.ShapeDtypeStruct((T, D), F32),
                   jax.ShapeDtypeStruct((T, D), BF16)],
        scratch_shapes=_group_scratch(tm, 2),
    )(x, zr, zr, zr, zr, *grp_views(o_l), *grp_views(lse_l), w_ya, pool_w, ps, w_yb, w_o)


def _up_proj(name, h2, w_up, F):
    T, D = h2.shape
    tm, tn, tk = _tile(T, 1024, 8), _tile(F, 1408), _tile(D, 1024)
    njh = F // tn
    return _mm(name, h2, w_up, grid=(T // tm, 2 * njh, D // tk),
               a_block=(tm, tk), a_map=lambda i, j, k: (i, k),
               b_block=(tk, tn), b_map=lambda i, j, k: (k, j),
               dims=NN_DIMS, acc_shape=(tm, tn),
               outs=[((2, T, F), BF16, (None, tm, tn), lambda i, j, k: (j // njh, i, j % njh))])[0]


def _conv_y(ext, w_ref, b_ref):
    return (b_ref[...] + w_ref[2:3, :] * ext + w_ref[1:2, :] * pltpu.roll(ext, 1, 0)
            + w_ref[0:1, :] * pltpu.roll(ext, 2, 0))


def _conv_params(conv_w, conv_b, F):
    cw = conv_w.reshape(3, 2, F).transpose(1, 0, 2)
    return cw, conv_b.reshape(2, 1, F)


def _ffn_act_fwd(name, u, conv_w, conv_b, S):
    _, T, F = u.shape
    tm = _tile(S, 512, CONV_HALO)
    tf = _tile(F, 1408)
    nst = S // tm
    hpt = tm // CONV_HALO
    cw, cb = _conv_params(conv_w, conv_b, F)

    def body(ug_ref, uv_ref, hg_ref, hv_ref, wg_ref, wv_ref, bg_ref, bv_ref, a_ref):
        keep = jnp.where(pl.program_id(0) % nst == 0, 0.0, 1.0)
        ext = lambda h_ref, u_ref: jnp.concatenate([h_ref[...].astype(F32) * keep, u_ref[...].astype(F32)], axis=0)
        yg = _conv_y(ext(hg_ref, ug_ref), wg_ref, bg_ref)[CONV_HALO:, :]
        yv = _conv_y(ext(hv_ref, uv_ref), wv_ref, bv_ref)[CONV_HALO:, :]
        a_ref[...] = (yg * _sigmoid(yg) * yv).astype(BF16)

    main = lambda h: pl.BlockSpec((None, tm, tf), lambda i, j: (h, i, j))
    halo = lambda h: pl.BlockSpec((None, CONV_HALO, tf), lambda i, j: (h, jnp.maximum(i * hpt - 1, 0), j))
    wsp = lambda h: pl.BlockSpec((None, 3, tf), lambda i, j: (h, 0, j))
    bsp = lambda h: pl.BlockSpec((None, 1, tf), lambda i, j: (h, 0, j))
    return pl.pallas_call(
        body, name=name, grid=(T // tm, F // tf),
        in_specs=[main(0), main(1), halo(0), halo(1), wsp(0), wsp(1), bsp(0), bsp(1)],
        out_specs=pl.BlockSpec((tm, tf), lambda i, j: (i, j)),
        out_shape=jax.ShapeDtypeStruct((T, F), BF16))(u, u, u, u, cw, cw, cb, cb)


def _ple_fwd(name, x2, p, g_ple, w_gate, w_ple):
    T, D = x2.shape
    P = p.shape[1]
    tm = _tile(T, 512, 8)

    def body(x_ref, p_ref, g_ref, wg_ref, wp_ref, x3_ref, e_ref, pg_ref, pbf_ref):
        xv = x_ref[...]
        h3 = xv * _rstd(xv) * g_ref[...]
        pg = _sigmoid(_dot(h3, wg_ref[...]))
        pb = p_ref[...].astype(BF16)
        e = _dot(pb, wp_ref[...])
        x3_ref[...] = xv + e * pg
        e_ref[...] = e
        pg_ref[...] = pg
        pbf_ref[...] = pb

    row = pl.BlockSpec((tm, D), lambda i: (i, 0))
    prow = pl.BlockSpec((tm, P), lambda i: (i, 0))
    g2 = g_ple.reshape(1, D)
    return pl.pallas_call(
        body, name=name, grid=(T // tm,),
        in_specs=[row, prow, _full(g2), _full(w_gate), _full(w_ple)],
        out_specs=[row, row, row, prow],
        out_shape=[jax.ShapeDtypeStruct((T, D), F32)] * 3 + [jax.ShapeDtypeStruct((T, P), BF16)],
    )(x2, p, g2, w_gate, w_ple)


def _loss_bwd(name, xf, target, g_final):
    T, D = xf.shape
    tm = _tile(T, 512, 8)
    nt = T // tm

    def body(x_ref, t_ref, g_ref, dx_ref, loss_ref, dg_ref, lacc):
        i = pl.program_id(0)

        @pl.when(i == 0)
        def _():
            lacc[...] = jnp.zeros_like(lacc)
            dg_ref[...] = jnp.zeros_like(dg_ref)
            loss_ref[...] = jnp.zeros_like(loss_ref)

        xv = x_ref[...]
        g = g_ref[...]
        diff = xv * _rstd(xv) * g - t_ref[...]
        lacc[...] += jnp.sum(diff * diff, axis=0, keepdims=True)
        dx, dgr = _rms_bwd(xv, g, diff * (1.0 / D))
        dx_ref[...] = dx
        dg_ref[...] += jnp.sum(dgr, axis=0, keepdims=True)

        @pl.when(i == nt - 1)
        def _():
            tot = jnp.sum(lacc[...], axis=-1, keepdims=True) * (0.5 / D)
            loss_ref[...] = jnp.broadcast_to(tot, (1, LANE))

    row = pl.BlockSpec((tm, D), lambda i: (i, 0))
    vec = pl.BlockSpec((1, D), lambda i: (0, 0))
    return pl.pallas_call(
        body, name=name, grid=(nt,),
        in_specs=[row, row, vec],
        out_specs=[row, pl.BlockSpec((1, LANE), lambda i: (0, 0)), vec],
        out_shape=[jax.ShapeDtypeStruct((T, D), F32), jax.ShapeDtypeStruct((1, LANE), F32),
                   jax.ShapeDtypeStruct((1, D), F32)],
        scratch_shapes=[pltpu.VMEM((1, D), F32)])(xf, target, g_final.reshape(1, D))


def _ple_bwd(name, dx3, x2, e, pg, g_ple, w_gate):
    T, D = x2.shape
    tm = _tile(T, 512, 8)

    def body(dx3_ref, x_ref, e_ref, pg_ref, g_ref, wg_ref, dx2_ref, de_ref, ds_ref, h3_ref, dg_ref):
        @pl.when(pl.program_id(0) == 0)
        def _():
            dg_ref[...] = jnp.zeros_like(dg_ref)

        dx3v, xv, pgv, g = dx3_ref[...], x_ref[...], pg_ref[...], g_ref[...]
        de_ref[...] = (dx3v * pgv).astype(BF16)
        ds = (dx3v * e_ref[...] * pgv * (1.0 - pgv)).astype(BF16)
        ds_ref[...] = ds
        dh3 = _dot(ds, wg_ref[...], NT_DIMS)
        h3_ref[...] = (xv * _rstd(xv) * g).astype(BF16)
        dx, dgr = _rms_bwd(xv, g, dh3)
        dx2_ref[...] = dx3v + dx
        dg_ref[...] += jnp.sum(dgr, axis=0, keepdims=True)

    row = pl.BlockSpec((tm, D), lambda i: (i, 0))
    vec = pl.BlockSpec((1, D), lambda i: (0, 0))
    return pl.pallas_call(
        body, name=name, grid=(T // tm,),
        in_specs=[row, row, row, row, vec, _full(w_gate)],
        out_specs=[row, row, row, row, vec],
        out_shape=[jax.ShapeDtypeStruct((T, D), F32)] + [jax.ShapeDtypeStruct((T, D), BF16)] * 3
        + [jax.ShapeDtypeStruct((1, D), F32)])(dx3, x2, e, pg, g_ple.reshape(1, D), w_gate)


def _ffn_act_bwd(name, u, d_a, conv_w, conv_b, S):
    _, T, F = u.shape
    H = CONV_HALO
    tm = _tile(S, 512, H)
    tf = _tile(F, 1408)
    nst = S // tm
    hpt = tm // H
    last_halo = T // H - 1
    n_ext = tm + H
    cw, cb = _conv_params(conv_w, conv_b, F)

    def body(ug_ref, uv_ref, pg_ref, pv_ref, ng_ref, nv_ref, da_ref, dan_ref, wg_ref, wv_ref, bg_ref, bv_ref,
             du_ref, dw_ref, db_ref):
        i = pl.program_id(1)
        it = i % nst

        @pl.when(i == 0)
        def _():
            dw_ref[...] = jnp.zeros_like(dw_ref)
            db_ref[...] = jnp.zeros_like(db_ref)

        keep_prev = jnp.where(it == 0, 0.0, 1.0)
        keep_next = jnp.where(it == nst - 1, 0.0, 1.0)
        def shifted(p_ref, u_ref, n_ref):
            ext = jnp.concatenate([p_ref[...].astype(F32) * keep_prev, u_ref[...].astype(F32),
                                   n_ref[...].astype(F32)], axis=0)
            return ext[H:, :], pltpu.roll(ext, 1, 0)[H:, :], pltpu.roll(ext, 2, 0)[H:, :]

        us_g, us_v = shifted(pg_ref, ug_ref, ng_ref), shifted(pv_ref, uv_ref, nv_ref)
        conv = lambda us, w_ref, b_ref: b_ref[...] + w_ref[2:3, :] * us[0] + w_ref[1:2, :] * us[1] + w_ref[0:1, :] * us[2]
        yg, yv = conv(us_g, wg_ref, bg_ref), conv(us_v, wv_ref, bv_ref)
        da = jnp.concatenate([da_ref[...].astype(F32), dan_ref[...].astype(F32) * keep_next], axis=0)
        sg = _sigmoid(yg)
        silu = yg * sg
        dyv = da * silu
        dyg = (da * yv) * (sg + silu * (1.0 - sg))
        for half, (dy, us, w_ref) in enumerate(((dyg, us_g, wg_ref), (dyv, us_v, wv_ref))):
            du = (w_ref[2:3, :] * dy + w_ref[1:2, :] * pltpu.roll(dy, n_ext - 1, 0)
                  + w_ref[0:1, :] * pltpu.roll(dy, n_ext - 2, 0))
            du_ref[half] = du[:tm, :].astype(BF16)
            dym = dy[:tm, :]
            db_ref[half] += jnp.sum(dym, axis=0, keepdims=True)
            for tap in range(3):
                dw_ref[half, tap:tap + 1, :] += jnp.sum(dym * us[2 - tap][:tm, :], axis=0, keepdims=True)

    main = lambda h: pl.BlockSpec((None, tm, tf), lambda j, i: (h, i, j))
    prev = lambda h: pl.BlockSpec((None, H, tf), lambda j, i: (h, jnp.maximum(i * hpt - 1, 0), j))
    nxt = lambda h: pl.BlockSpec((None, H, tf), lambda j, i: (h, jnp.minimum((i + 1) * hpt, last_halo), j))
    wsp = lambda h: pl.BlockSpec((None, 3, tf), lambda j, i: (h, 0, j))
    bsp = lambda h: pl.BlockSpec((None, 1, tf), lambda j, i: (h, 0, j))
    return pl.pallas_call(
        body, name=name, grid=(F // tf, T // tm),
        in_specs=[main(0), main(1), prev(0), prev(1), nxt(0), nxt(1),
                  pl.BlockSpec((tm, tf), lambda j, i: (i, j)),
                  pl.BlockSpec((H, tf), lambda j, i: (jnp.minimum((i + 1) * hpt, last_halo), j)),
                  wsp(0), wsp(1), bsp(0), bsp(1)],
        out_specs=[pl.BlockSpec((2, tm, tf), lambda j, i: (0, i, j)),
                   pl.BlockSpec((2, 3, tf), lambda j, i: (0, 0, j)),
                   pl.BlockSpec((2, 1, tf), lambda j, i: (0, 0, j))],
        out_shape=[jax.ShapeDtypeStruct((2, T, F), BF16), jax.ShapeDtypeStruct((2, 3, F), F32),
                   jax.ShapeDtypeStruct((2, 1, F), F32)],
    )(u, u, u, u, u, u, d_a, d_a, cw, cw, cb, cb)


def _mix_out_bwd(name, dx1, zr, y_a, y_b, o_l, lse_l, pooled, w_o, w_ya, w_yb, pool_w, pool_scale, S):
    T, D = dx1.shape
    gw = D // len(POOL_WINDOWS)
    H = POOL_HALO
    tm = _tile(S, 256, H)
    nst = S // tm
    hpt = tm // H
    last_halo = T // H - 1
    n_ext = tm + H

    def body(dx_ref, dxn_ref, ga_ref, gb_ref, gbn_ref, ya_ref, yb_ref, o0, o1, o2, l0, l1, l2, pooled_ref,
             wo_ref, wya_ref, wyb_ref, pw_ref, ps_ref,
             dz_ref, dya_ref, dyb_ref, dpm_ref, do0, do1, do2, c0, c1, c2, dps_ref, *scs):
        i = pl.program_id(0)
        it = i % nst

        @pl.when(i == 0)
        def _():
            dps_ref[...] = jnp.zeros_like(dps_ref)

        keep_next = jnp.where(it == nst - 1, 0.0, 1.0)
        dm_e = _dot(jnp.concatenate([dx_ref[...], dxn_ref[...]], axis=0), wo_ref[...], NT_DIMS)
        sgb_e = _sigmoid(jnp.concatenate([gb_ref[...], gbn_ref[...]], axis=0))
        dyb_e = dm_e * sgb_e
        dm = dm_e[:tm, :]
        sga = _sigmoid(ga_ref[...])
        sgb = sgb_e[:tm, :]
        d_ga = dm * ya_ref[...] * (sga * (1.0 - sga))
        d_gb = dm * yb_ref[...] * (sgb * (1.0 - sgb))
        dya = (dm * sga).astype(BF16)
        dya_ref[...] = dya
        dyb_ref[...] = dyb_e[:tm, :].astype(BF16)
        dmixed_e = _dot(dyb_e, wyb_ref[...], NT_DIMS)

        rows = lax.broadcasted_iota(jnp.int32, (n_ext, 1), 0)
        tseq = it * tm + rows
        live = jnp.where(rows < tm, 1.0, keep_next)
        ps = ps_ref[...]
        du_parts = []
        for gi, w in enumerate(POOL_WINDOWS):
            cs = slice(gi * gw, (gi + 1) * gw)
            pm_g = _dot(pooled_ref[:, cs], pw_ref[gi])
            dps_ref[:, cs] += jnp.sum(dmixed_e[:tm, cs] * pm_g, axis=0, keepdims=True)
            dpm_e = (dmixed_e[:, cs] * ps[:, cs]).astype(BF16)
            dpm_ref[:, cs] = dpm_e[:tm, :]
            dpooled_e = _dot(dpm_e, pw_ref[gi], NT_DIMS)
            s = dpooled_e * (_pool_inv_count(tseq, w) * live)
            step = 1
            while step < w:
                s = s + pltpu.roll(s, n_ext - step, 0)
                step *= 2
            du_parts.append(s[:tm, :] - dpooled_e[:tm, :])
        dz_ref[...] = jnp.concatenate(du_parts + [d_ga, d_gb], axis=1).astype(BF16)

        d_attn = _dot(dya, wya_ref[...], NT_DIMS)
        ov, lv = _group_values((o0, o1, o2), (l0, l1, l2), scs[:n_in_sc])
        ws = _merge_weights(*lv)
        prod = d_attn * (ws[0] * ov[0] + ws[1] * ov[1] + ws[2] * ov[2])
        rs = jnp.concatenate(
            [jnp.broadcast_to(jnp.sum(prod[:, hh * HEAD_DIM:(hh + 1) * HEAD_DIM], axis=-1, keepdims=True),
                              (tm, HEAD_DIM)) for hh in range(HEADS)], axis=1)
        out_scs = list(scs[n_in_sc:])
        for d, wg, do_ref, c_ref in zip(DILATIONS, ws, (do0, do1, do2), (c0, c1, c2)):
            if d == 1:
                do_ref[...] = (wg * d_attn).astype(BF16)
                c_ref[...] = -wg * rs
            else:
                _store_residue_major(wg * d_attn, out_scs.pop(), do_ref, d, BF16)
                _store_residue_major(-wg * rs, out_scs.pop(), c_ref, d, F32)

    row = lambda c: pl.BlockSpec((tm, D), lambda i: (i, c))
    nxt = lambda c: pl.BlockSpec((H, D), lambda i: (jnp.minimum((i + 1) * hpt, last_halo), c))
    ps2 = pool_scale.reshape(1, D)
    bf = lambda w: jax.ShapeDtypeStruct((T, w), BF16)
    grp_specs, grp_views = _group_specs(tm, nst, S)
    grp_shape = lambda dt: [jax.ShapeDtypeStruct((T, GROUP_W) if d == 1 else (T // S, d, S // d, GROUP_W), dt)
                            for d in DILATIONS]
    n_in_sc = len(_group_scratch(tm, 2))
    return pl.pallas_call(
        body, name=name, grid=(T // tm,),
        in_specs=[row(0), nxt(0), row(1), row(2), nxt(2), row(0), row(0)] + grp_specs * 2 + [row(0)]
        + [_full(w_o), _full(w_ya), _full(w_yb), _full(pool_w), _full(ps2)],
        out_specs=[pl.BlockSpec((tm, 3 * D), lambda i: (i, 0)), row(0), row(0), row(0)] + grp_specs * 2
        + [pl.BlockSpec((1, D), lambda i: (0, 0))],
        out_shape=[bf(3 * D), bf(D), bf(D), bf(D)] + grp_shape(BF16) + grp_shape(F32)
        + [jax.ShapeDtypeStruct((1, D), F32)],
        scratch_shapes=_group_scratch(tm, 4),
    )(dx1, dx1, zr, zr, zr, y_a, y_b, *grp_views(o_l), *grp_views(lse_l), pooled, w_o, w_ya, w_yb, pool_w, ps2)


def _attn_bwd(name, qkv, d_o, lse, cst, rope, g, Bl, S):
    d = DILATIONS[g]
    L = S // d
    nb = L // BLOCK
    qv = qkv.reshape(Bl * d, L, GROUP_QKV_W)
    dov = d_o.reshape(Bl * d, L, GROUP_W)
    lv = lse.reshape(Bl * d, L, GROUP_W)
    cv = cst.reshape(Bl * d, L, GROUP_W)
    tabs = [t.reshape(d, L, HEAD_DIM) for t in rope]
    scale = HEAD_DIM ** -0.5

    cb, cs = _attn_chunking(nb)
    qc = cb * BLOCK
    lead = BLOCK if nb > 1 else 0

    def body(q_ref, qn_ref, kp_ref, kc_ref, vp_ref, vc_ref, do_ref, don_ref, l_ref, ln_ref, c_ref, cn_ref,
             cos_ref, sa_ref, sb_ref, out_ref):
        n = pl.program_id(1)
        valid = _attn_mask(qc, lead, n > 0)
        qi = lax.broadcasted_iota(jnp.int32, (BLOCK, BLOCK), 0)
        ki = lax.broadcasted_iota(jnp.int32, (BLOCK, BLOCK), 1)
        valid_n = (ki >= qi) & ((n + 1) * cb < nb)
        tail = slice(qc - BLOCK, qc)
        for si in range(cs):
            cos, sa, sb = cos_ref[si], -sa_ref[si], -sb_ref[si]
            for hh in range(HEADS):
                sl = slice(hh * HEAD_DIM, (hh + 1) * HEAD_DIM)
                col = slice(hh * HEAD_DIM, hh * HEAD_DIM + 1)
                q, kc, vc, do = q_ref[si, :, sl], kc_ref[si, :, sl], vc_ref[si, :, sl], do_ref[si, :, sl]
                kk, vv = kc, vc
                if lead:
                    kk = jnp.concatenate([kp_ref[si, :, sl], kc], axis=0)
                    vv = jnp.concatenate([vp_ref[si, :, sl], vc], axis=0)
                s = jnp.where(valid, _dot(q, kk, NT_DIMS) * scale, NEG_INF)
                p = jnp.exp(s - l_ref[si, :, col])
                ds = p * (_dot(do, vv, NT_DIMS) + c_ref[si, :, col])
                dq = _dot(ds, kk) * scale
                dk = _dot(ds[:, lead:], q, TN_DIMS)
                dv = _dot(p[:, lead:], do, TN_DIMS)
                if nb > cb:
                    qn, don = qn_ref[si, :, sl], don_ref[si, :, sl]
                    s2 = jnp.where(valid_n, _dot(qn, kc[tail], NT_DIMS) * scale, NEG_INF)
                    p2 = jnp.exp(s2 - ln_ref[si, :, col])
                    ds2 = p2 * (_dot(don, vc[tail], NT_DIMS) + cn_ref[si, :, col])
                    dk = jnp.concatenate([dk[:qc - BLOCK], dk[tail] + _dot(ds2, qn, TN_DIMS)], axis=0)
                    dv = jnp.concatenate([dv[:qc - BLOCK], dv[tail] + _dot(p2, don, TN_DIMS)], axis=0)
                out_ref[si, :, sl] = _rope(dq, cos, sa, sb).astype(BF16)
                out_ref[si, :, GROUP_W + hh * HEAD_DIM:GROUP_W + (hh + 1) * HEAD_DIM] = (
                    _rope(dk * scale, cos, sa, sb).astype(BF16))
                out_ref[si, :, 2 * GROUP_W + hh * HEAD_DIM:2 * GROUP_W + (hh + 1) * HEAD_DIM] = dv.astype(BF16)

    main = (cs, cb * BLOCK, GROUP_W)
    edge = (cs, BLOCK, GROUP_W)
    cur = lambda off: pl.BlockSpec(main, lambda s, n: (s, n, off))
    prv = lambda off: pl.BlockSpec(edge, lambda s, n: (s, jnp.maximum(n * cb - 1, 0), off))
    nxt = lambda off: pl.BlockSpec(edge, lambda s, n: (s, jnp.minimum((n + 1) * cb, nb - 1), off))
    tab = pl.BlockSpec((cs, cb * BLOCK, HEAD_DIM), lambda s, n: (s % (d // cs), n, 0))
    out = pl.pallas_call(
        body, name=name, grid=(Bl * d // cs, nb // cb),
        in_specs=[cur(0), nxt(0), prv(1), cur(1), prv(2), cur(2),
                  cur(0), nxt(0), cur(0), nxt(0), cur(0), nxt(0), tab, tab, tab],
        out_specs=pl.BlockSpec((cs, cb * BLOCK, GROUP_QKV_W), lambda s, n: (s, n, 0)),
        out_shape=jax.ShapeDtypeStruct((Bl * d, L, GROUP_QKV_W), BF16),
    )(qv, qv, qv, qv, qv, qv, dov, dov, lv, lv, cv, cv, *tabs)
    return out.reshape(Bl * S, GROUP_QKV_W)


def _pool_w_grad(name, pooled, d_pm, gw):
    T = pooled.shape[0]
    ng = len(POOL_WINDOWS)
    tk = _tile(T, 1024, 8)
    return _mm(name, pooled, d_pm, grid=(ng, 1, T // tk),
               a_block=(tk, gw), a_map=lambda i, j, k: (k, i),
               b_block=(tk, gw), b_map=lambda i, j, k: (k, i),
               dims=TN_DIMS, acc_shape=(gw, gw),
               outs=[((ng, gw, gw), F32, (None, gw, gw), lambda i, j, k: (i, 0, 0))])[0]


def _up_w_grad(name, h2, du):
    T, D = h2.shape
    F = du.shape[2]
    tm, tn, tk = _tile(D, 1024), _tile(F, 1408), _tile(T, 1024, 8)
    njh = F // tn
    return _mm(name, h2, du, grid=(D // tm, 2 * njh, T // tk),
               a_block=(tk, tm), a_map=lambda i, j, k: (k, i),
               b_block=(None, tk, tn), b_map=lambda i, j, k: (j // njh, k, j % njh),
               dims=TN_DIMS, acc_shape=(tm, tn),
               outs=[((D, 2 * F), F32, (tm, tn), lambda i, j, k: (i, j))])[0]


def _adamw(name, w, m, v, pieces):
    R, C = w.shape
    nl = len(pieces)
    rl = R // nl
    if nl > 1 and rl % 8:
        per = [_adamw(f"{name}_{l}", w[l * rl:(l + 1) * rl], m[l * rl:(l + 1) * rl], v[l * rl:(l + 1) * rl],
                      [pieces[l]]) for l in range(nl)]
        return [jnp.concatenate([p[o] for p in per], axis=0) for o in range(4)]
    tr = _tile(rl, max(PACK_ROWS, (1 << 18) // C // PACK_ROWS * PACK_ROWS), PACK_ROWS)
    nbl = rl // tr
    c1 = 1.0 - ADAM_B1 ** ADAM_STEP
    c2 = 1.0 - ADAM_B2 ** ADAM_STEP

    def body(w_ref, m_ref, v_ref, *rest):
        p_refs = rest[:nl]
        g_ref, d_ref, mo_ref, vo_ref = rest[nl:]
        i = pl.program_id(0)
        for l in range(nl):
            @pl.when((i >= l * nbl) & (i < (l + 1) * nbl))
            def _():
                g = p_refs[l][0].astype(F32)
                for dev in range(1, N_DEV):
                    g = g + p_refs[l][dev].astype(F32)
                mn = ADAM_B1 * m_ref[...] + (1.0 - ADAM_B1) * g
                vn = ADAM_B2 * v_ref[...] + (1.0 - ADAM_B2) * (g * g)
                g_ref[...] = g
                mo_ref[...] = mn
                vo_ref[...] = vn
                d_ref[...] = -ADAM_LR * ((mn / c1) / (jnp.sqrt(vn / c2) + ADAM_EPS) + ADAM_WD * w_ref[...])

    row = pl.BlockSpec((tr, C), lambda i: (i, 0))
    piece = lambda l: pl.BlockSpec((N_DEV, tr, C), lambda i: (0, jnp.clip(i - l * nbl, 0, nbl - 1), 0))
    return pl.pallas_call(
        body, name=name, grid=(R // tr,),
        in_specs=[row, row, row] + [piece(l) for l in range(nl)],
        out_specs=[row] * 4,
        out_shape=[jax.ShapeDtypeStruct((R, C), F32)] * 4)(w, m, v, *pieces)


def _my_index():
    return 4 * lax.axis_index("x") + 2 * lax.axis_index("y") + lax.axis_index("c")


def _all_gather(name, mine):
    na = len(mine)

    def body(*refs):
        x_refs, out_refs, token = refs[:na], refs[na:2 * na], refs[2 * na]
        send_sems, recv_sems, local_sems = refs[2 * na + 1:]
        token[...] = jnp.zeros_like(token)
        x, y, c = lax.axis_index("x"), lax.axis_index("y"), lax.axis_index("c")
        me, sibling = (x, y, c), (x, y, 1 - c)
        chips = [(1 - x, y), (x, 1 - y), (1 - x, 1 - y)]

        def slot(a, px, py, pc):
            return out_refs[a].at[4 * px + 2 * py + pc]

        def copy(a, k, block, to, src=None):
            return pltpu.make_async_remote_copy(
                src_ref=slot(a, *block) if src is None else src, dst_ref=slot(a, *block),
                send_sem=send_sems.at[7 * a + k], recv_sem=recv_sems.at[7 * a + k],
                device_id=to, device_id_type=MESH_ID)

        own = [pltpu.make_async_copy(x_refs[a], slot(a, *me), local_sems.at[a]) for a in range(na)]
        for cp in own:
            cp.start()
        first = []
        for a in range(na):
            first.append(copy(a, 0, me, sibling, src=x_refs[a]))
            first += [copy(a, 1 + j, me, (*chip, c), src=x_refs[a]) for j, chip in enumerate(chips)]
        for cp in first:
            cp.start()
        passed = []
        for j, chip in enumerate(chips):
            for a in range(na):
                copy(a, 1 + j, (*chip, c), me).wait_recv()
                fwd = copy(a, 4 + j, (*chip, c), sibling)
                fwd.start()
                passed.append(fwd)
        for a in range(na):
            copy(a, 0, sibling, me).wait_recv()
            for j, chip in enumerate(chips):
                copy(a, 4 + j, (*chip, 1 - c), me).wait_recv()
        for cp in first + passed:
            cp.wait_send()
        for cp in own:
            cp.wait()

    res = pl.pallas_call(
        body, name=name,
        in_specs=[pl.BlockSpec(memory_space=pl.ANY)] * na,
        out_specs=[pl.BlockSpec(memory_space=pl.ANY)] * na + [pl.BlockSpec(memory_space=pltpu.VMEM)],
        out_shape=[jax.ShapeDtypeStruct((N_DEV,) + m.shape, m.dtype) for m in mine]
        + [jax.ShapeDtypeStruct((8, LANE), F32)],
        scratch_shapes=[pltpu.SemaphoreType.DMA((7 * na,)), pltpu.SemaphoreType.DMA((7 * na,)),
                        pltpu.SemaphoreType.DMA((na,))],
    )(*mine)
    return res[:na], res[na]


_HBM_SPEC = pl.BlockSpec(memory_space=pltpu.HBM)
_SEM_SPEC = pl.BlockSpec(memory_space=pltpu.SEMAPHORE)
_SIDE_EFFECT = pltpu.SideEffectType.DATAFLOW_SIDE_EFFECTING


def _peer_of(k):
    x, y, c = lax.axis_index("x"), lax.axis_index("y"), lax.axis_index("c")
    px = 1 - x if k & 4 else x
    py = 1 - y if k & 2 else y
    pc = 1 - c if k & 1 else c
    return (px, py, pc), 4 * px + 2 * py + pc


def _send_start(name, srcs, pieces):
    na = len(srcs)
    land_shapes = [s.shape if pieces else (N_DEV,) + s.shape for s in srcs]
    lands = [lax.empty(shp, s.dtype) for shp, s in zip(land_shapes, srcs)]

    def body(*refs):
        src_refs, land_refs = refs[:na], refs[na:2 * na]
        send_sems, recv_sems, token = refs[2 * na], refs[2 * na + 1], refs[4 * na + 2]
        me = 4 * lax.axis_index("x") + 2 * lax.axis_index("y") + lax.axis_index("c")
        for k in range(1, N_DEV):
            to, pidx = _peer_of(k)
            for a in range(na):
                pltpu.make_async_remote_copy(
                    src_ref=src_refs[a].at[pidx] if pieces else src_refs[a], dst_ref=land_refs[a].at[me],
                    send_sem=send_sems.at[7 * a + k - 1], recv_sem=recv_sems.at[7 * a + k - 1],
                    device_id=to, device_id_type=MESH_ID).start()
        token[...] = jnp.zeros_like(token)

    hbm = lambda arrs: [pltpu.HBM(a.shape, a.dtype) for a in arrs]
    outs = pl.pallas_call(
        body, name=name,
        out_shape=(pltpu.SemaphoreType.DMA((7 * na,)), pltpu.SemaphoreType.DMA((7 * na,)), *hbm(srcs), *hbm(lands),
                   jax.ShapeDtypeStruct((8, LANE), F32)),
        in_specs=[_HBM_SPEC] * (2 * na),
        out_specs=(_SEM_SPEC, _SEM_SPEC, *([_HBM_SPEC] * (2 * na)), pl.BlockSpec(memory_space=pltpu.VMEM)),
        input_output_aliases={i: 2 + i for i in range(2 * na)},
        compiler_params=pltpu.CompilerParams(has_side_effects=_SIDE_EFFECT),
    )(*[pltpu.with_memory_space_constraint(s, pltpu.HBM) for s in srcs],
      *[pltpu.with_memory_space_constraint(l, pltpu.HBM) for l in lands])
    return outs[0], outs[1], outs[2:2 + na], outs[2 + na:2 + 2 * na], outs[-1]


def _send_wait(name, send_sems, recv_sems, srcs, lands, pieces, after):
    na = len(srcs)

    def body(*refs):
        src_refs, land_refs = refs[:na], refs[na:2 * na]
        send_sems, recv_sems = refs[2 * na], refs[2 * na + 1]
        for k in range(1, N_DEV):
            to, pidx = _peer_of(k)
            for a in range(na):
                cp = pltpu.make_async_remote_copy(
                    src_ref=src_refs[a].at[pidx] if pieces else src_refs[a], dst_ref=land_refs[a].at[pidx],
                    send_sem=send_sems.at[7 * a + k - 1], recv_sem=recv_sems.at[7 * a + k - 1],
                    device_id=to, device_id_type=MESH_ID)
                cp.wait_send()
                cp.wait_recv()

    hbm = lambda arrs: [pltpu.HBM(a.shape, a.dtype) for a in arrs]
    outs = pl.pallas_call(
        body, name=name,
        out_shape=tuple(hbm(srcs) + hbm(lands)),
        in_specs=[_HBM_SPEC] * (2 * na) + [_SEM_SPEC, _SEM_SPEC, pl.BlockSpec(memory_space=pl.ANY)],
        out_specs=tuple([_HBM_SPEC] * (2 * na)),
        input_output_aliases={i: i for i in range(2 * na)},
        compiler_params=pltpu.CompilerParams(has_side_effects=_SIDE_EFFECT),
    )(*srcs, *lands, send_sems, recv_sems, after)
    return outs[:na], outs[na:]


def _own_slot(land, own):
    me = 4 * lax.axis_index("x") + 2 * lax.axis_index("y") + lax.axis_index("c")
    mine = lax.broadcasted_iota(jnp.int32, land.shape, 0) == me
    return jnp.where(mine, jnp.broadcast_to(own, land.shape), land)


def _exchange(name, pieces, bcast):
    n_p, n_b = len(pieces), len(bcast)
    na = n_p + n_b

    def body(*refs):
        src_refs, dst_refs = refs[:na], refs[na:2 * na]
        send_sems, recv_sems, local_sems = refs[2 * na:]
        x, y, c = lax.axis_index("x"), lax.axis_index("y"), lax.axis_index("c")
        me = 4 * x + 2 * y + c

        def src(a, slot):
            return src_refs[a].at[slot] if a < n_p else src_refs[a]

        own = [pltpu.make_async_copy(src(a, me), dst_refs[a].at[me], local_sems.at[a]) for a in range(na)]
        for cp in own:
            cp.start()

        def peer_of(k):
            px = 1 - x if k & 4 else x
            py = 1 - y if k & 2 else y
            pc = 1 - c if k & 1 else c
            return (px, py, pc), 4 * px + 2 * py + pc

        def copy(a, k, src_slot, dst_slot, to):
            return pltpu.make_async_remote_copy(
                src_ref=src(a, src_slot), dst_ref=dst_refs[a].at[dst_slot],
                send_sem=send_sems.at[7 * a + k - 1], recv_sem=recv_sems.at[7 * a + k - 1],
                device_id=to, device_id_type=MESH_ID)

        sent = []
        for k in range(1, N_DEV):
            to, pidx = peer_of(k)
            for a in range(na):
                cp = copy(a, k, pidx, me, to)
                cp.start()
                sent.append(cp)
        for k in range(1, N_DEV):
            to, pidx = peer_of(k)
            for a in range(na):
                copy(a, k, me, pidx, to).wait_recv()
        for cp in sent:
            cp.wait_send()
        for cp in own:
            cp.wait()

    arrays = list(pieces) + list(bcast)
    out_shape = [jax.ShapeDtypeStruct(p.shape, p.dtype) for p in pieces]
    out_shape += [jax.ShapeDtypeStruct((N_DEV,) + b.shape, b.dtype) for b in bcast]
    res = pl.pallas_call(
        body, name=name,
        in_specs=[pl.BlockSpec(memory_space=pl.ANY)] * na, out_specs=[pl.BlockSpec(memory_space=pl.ANY)] * na,
        out_shape=out_shape,
        scratch_shapes=[pltpu.SemaphoreType.DMA((7 * na,)), pltpu.SemaphoreType.DMA((7 * na,)),
                        pltpu.SemaphoreType.DMA((na,))],
    )(*arrays)
    return res[:n_p], res[n_p:]


def _pad_rows(flat, cols, row_mult):
    n = flat.shape[-1]
    unit = cols * row_mult
    padded = -(-n // unit) * unit
    pad = [(0, 0)] * (flat.ndim - 1) + [(0, padded - n)]
    return jnp.pad(flat, pad).reshape(flat.shape[:-1] + (padded // cols, cols))


def _perm_cols(w):
    aw = N_GROUPS * GROUP_W
    parts = [w[..., QKV_W:]]
    parts += [w[..., a * aw + g * GROUP_W:a * aw + (g + 1) * GROUP_W] for g in range(N_GROUPS) for a in range(3)]
    return jnp.concatenate(parts, axis=-1)


def _unperm_cols(wp, rest_w):
    qkv = wp[..., rest_w:]
    parts = [qkv[..., g * GROUP_QKV_W + a * GROUP_W:g * GROUP_QKV_W + (a + 1) * GROUP_W]
             for a in range(3) for g in range(N_GROUPS)]
    return jnp.concatenate(parts + [wp[..., :rest_w]], axis=-1)


SHARD_AXIS = dict(SHARDED)


def _layer_shards(wts, li, names, zero):
    out = []
    for n in names:
        w = wts[n][li] if zero is None else wts[n][li] + zero
        out.append(w if n in EXACT_F32 else w.astype(BF16))
    return out


def _assemble_weights(names, segs):
    W = {}
    for n, seg in zip(names, segs):
        ax = SHARD_AXIS[n]
        shp = seg.shape[1:]
        seg = jnp.moveaxis(seg, 0, ax)
        W[n] = seg.reshape(shp[:ax] + (N_DEV * shp[ax],) + shp[ax + 1:])
    if "w_in" in W:
        W["w_in"] = _perm_cols(W["w_in"])
    return W


def _grad_pieces(gr):
    out = []
    for n in gr:
        ax = SHARD_AXIS[n]
        shp = gr[n].shape
        g = gr[n].reshape(shp[:ax] + (N_DEV, shp[ax] // N_DEV) + shp[ax + 1:])
        out.append(jnp.moveaxis(g, ax, 0).astype(BF16))
    return out


def _pack_small(vals):
    flat = jnp.concatenate([vals[n].astype(F32).reshape(-1) for n in REPLICATED])
    return _pad_rows(flat, LANE, 8)


def _layer_fwd(li, x, p_l, w_in, other_weights, G, rope, Bl, S, F):
    T, D = x.shape
    rest_w = 3 * D
    sv = {"x0": x}
    W = {"w_in": w_in}
    hs = _rms_mix_fwd(f"rms_mix_{li}", x, G["g_mix"], S)
    h = hs[0]
    h_g = [h] + [a.reshape(T, D) for a in hs[1:]]
    sv["h_g"] = h_g
    zr = _mm_nn(f"rest_proj_{li}", h, W["w_in"], n_cols=rest_w, tn=1024, tk=_tile(D, 1024))
    sv["zr"] = zr
    qkv_l, o_l, lse_l = [], [], []
    for g in range(N_GROUPS):
        qkv = _qkv_proj(f"qkv_proj_{li}_{g}", h_g[g], W["w_in"], rope[g], S, rest_w + g * GROUP_QKV_W)
        o, lse = _attn_fwd(f"attn_fwd_{li}_{g}", qkv, g, Bl, S)
        qkv_l.append(qkv)
        o_l.append(o)
        lse_l.append(lse)
    sv["qkv"], sv["o"], sv["lse"] = qkv_l, o_l, lse_l
    W.update(other_weights(o_l[-1]))
    x1, attn, pooled, mixed, y_a, y_b, merged = _mix_out_fwd(
        f"mix_out_fwd_{li}", x, zr, o_l, lse_l, W["w_ya"], W["pool_w"], G["pool_scale"], W["w_yb"], W["w_o"], S)
    sv.update(x1=x1, attn=attn, pooled=pooled, mixed=mixed, y_a=y_a, y_b=y_b, merged=merged)
    h2 = _rms_fwd(f"rms_ffn_{li}", x1, G["g_ffn"])
    u = _up_proj(f"up_proj_{li}", h2, W["w_up"], F)
    a = _ffn_act_fwd(f"ffn_act_fwd_{li}", u, W["conv_w"], G["conv_b"], S)
    x2 = _mm_nn(f"down_proj_{li}", a, W["w_down"], add=x1, tk=_tile(F, 1408))
    sv.update(h2=h2, u=u, a=a, x2=x2)
    x3, e, pg, p_bf = _ple_fwd(f"ple_fwd_{li}", x2, p_l, G["g_ple"], W["w_ple_gate"], W["w_ple"])
    sv.update(e=e, pg=pg, p_bf=p_bf)
    return x3, sv, W


EARLY_GRADS = ("w_ple", "w_ple_gate", "w_down", "conv_w", "w_up")
LATE_GRADS = ("w_in", "w_ya", "w_yb", "pool_w", "w_o")


def _layer_bwd(li, dx3, sv, W, G, rope, Bl, S, F, send):
    T, D = dx3.shape
    rest_w = 3 * D
    gr = {}
    dx2, d_e, d_s, h3, dg = _ple_bwd(f"ple_bwd_{li}", dx3, sv["x2"], sv["e"], sv["pg"], G["g_ple"], W["w_ple_gate"])
    gr["g_ple"] = dg[0]
    gr["w_ple"] = _mm_tn(f"w_ple_grad_{li}", sv["p_bf"], d_e)
    gr["w_ple_gate"] = _mm_tn(f"w_ple_gate_grad_{li}", h3, d_s)

    d_a = _mm_nt(f"down_bwd_{li}", dx2, W["w_down"], out_dtype=BF16, tn=1408, tk=_tile(D, 1024))
    gr["w_down"] = _mm_tn(f"w_down_grad_{li}", sv["a"], dx2, tm=1408)
    du, d_cw, d_cb = _ffn_act_bwd(f"ffn_act_bwd_{li}", sv["u"], d_a, W["conv_w"], G["conv_b"], S)
    gr["conv_w"] = d_cw.transpose(1, 0, 2).reshape(3, 2 * F)
    gr["conv_b"] = d_cb.reshape(2 * F)
    tk_f = _tile(F, 1408)
    nkh = F // tk_f
    tm_r = _tile(T, 1024, 8)
    dx1, dg = _mm_nt_rmsbwd(f"up_bwd_{li}", du, (None, tm_r, tk_f), lambda i, j, k: (k // nkh, i, k % nkh),
                            2 * nkh, tk_f, W["w_up"], sv["x1"], G["g_ffn"], dx2)
    gr["g_ffn"] = dg[0]
    gr["w_up"] = _up_w_grad(f"w_up_grad_{li}", sv["h2"], du)
    zero = send("a", {n: gr[n] for n in EARLY_GRADS})

    (dz_rest, d_ya, d_yb, d_pm, do0, do1, do2, c0, c1, c2, dps) = _mix_out_bwd(
        f"mix_out_bwd_{li}", dx1, sv["zr"], sv["y_a"], sv["y_b"], sv["o"], sv["lse"], sv["pooled"],
        W["w_o"], W["w_ya"], W["w_yb"], W["pool_w"], G["pool_scale"] + zero, S)
    gr["pool_scale"] = dps[0]
    gr["w_o"] = _mm_tn(f"w_o_grad_{li}", sv["merged"], dx1)
    gr["w_ya"] = _mm_tn(f"w_ya_grad_{li}", sv["attn"], d_ya)
    gr["w_yb"] = _mm_tn(f"w_yb_grad_{li}", sv["mixed"], d_yb)
    gr["pool_w"] = _pool_w_grad(f"pool_w_grad_{li}", sv["pooled"], d_pm, D // len(POOL_WINDOWS))
    segs = [(dz_rest, 1)]
    for g, (do, cst) in enumerate(((do0, c0), (do1, c1), (do2, c2))):
        dqkv = _attn_bwd(f"attn_bwd_{li}_{g}", sv["qkv"][g], do, sv["lse"][g], cst, rope[g], g, Bl, S)
        segs.append((dqkv, DILATIONS[g]))

    h_rows = [sv["h_g"][0]] + sv["h_g"]
    w_in_parts = [_mm_tn(f"w_in_grad_{li}_{s}", h_rows[s], seg, tn=1536) for s, (seg, _) in enumerate(segs)]
    gr["w_in"] = _unperm_cols(jnp.concatenate(w_in_parts, axis=1), rest_w)
    zero = send("b", {n: gr[n] for n in LATE_GRADS})
    dx0, dg = _in_bwd(f"in_bwd_{li}", segs, W["w_in"], sv["x0"], G["g_mix"] + zero, dx1, S)
    gr["g_mix"] = dg[0]
    return dx0, gr


def kernel(x, p, g_mix, w_in, w_ya, w_yb, pool_w, pool_scale, w_o, g_ffn, w_up, conv_w, conv_b, w_down, g_ple, w_ple, w_ple_gate, g_final, loss_target, m_g_mix, m_w_in, m_w_ya, m_w_yb, m_pool_w, m_pool_scale, m_w_o, m_g_ffn, m_w_up, m_conv_w, m_conv_b, m_w_down, m_g_ple, m_w_ple, m_w_ple_gate, m_g_final, v_g_mix, v_w_in, v_w_ya, v_w_yb, v_pool_w, v_pool_scale, v_w_o, v_g_ffn, v_w_up, v_conv_w, v_conv_b, v_w_down, v_g_ple, v_w_ple, v_w_ple_gate, v_g_final):
    wts = dict(g_mix=g_mix, w_in=w_in, w_ya=w_ya, w_yb=w_yb, pool_w=pool_w, pool_scale=pool_scale, w_o=w_o,
               g_ffn=g_ffn, w_up=w_up, conv_w=conv_w, conv_b=conv_b, w_down=w_down, g_ple=g_ple, w_ple=w_ple,
               w_ple_gate=w_ple_gate, g_final=g_final)
    mom = dict(g_mix=m_g_mix, w_in=m_w_in, w_ya=m_w_ya, w_yb=m_w_yb, pool_w=m_pool_w, pool_scale=m_pool_scale,
               w_o=m_w_o, g_ffn=m_g_ffn, w_up=m_w_up, conv_w=m_conv_w, conv_b=m_conv_b, w_down=m_w_down,
               g_ple=m_g_ple, w_ple=m_w_ple, w_ple_gate=m_w_ple_gate, g_final=m_g_final)
    var = dict(g_mix=v_g_mix, w_in=v_w_in, w_ya=v_w_ya, w_yb=v_w_yb, pool_w=v_pool_w, pool_scale=v_pool_scale,
               w_o=v_w_o, g_ffn=v_g_ffn, w_up=v_w_up, conv_w=v_conv_w, conv_b=v_conv_b, w_down=v_w_down,
               g_ple=v_g_ple, w_ple=v_w_ple, w_ple_gate=v_w_ple_gate, g_final=v_g_final)
    Bl, S, D = x.shape
    depth = g_mix.shape[0]
    F = w_down.shape[1] * N_DEV
    T = Bl * S
    assert S % (BLOCK * DILATIONS[-1]) == 0 and D % GROUP_W == 0 and F % LANE == 0
    rope = [tuple(t if d == 1 else t.reshape(S // d, d, HEAD_DIM).transpose(1, 0, 2).reshape(S, HEAD_DIM)
                  for t in _rope_tables(S)) for d in DILATIONS]

    first, others = ("w_in",), tuple(n for n, _ in SHARDED if n != "w_in")
    got_in, tok = _all_gather("gather_w0_in", _layer_shards(wts, 0, first, None))
    gathers = {}
    for li in range(depth):
        names = others if li == 0 else first + others
        *gathers[li], tok = _send_start(f"gather_w{li}_start", _layer_shards(wts, li, names, tok[0, 0]), False)

    def gathered(li, names, after):
        shards, lands = _send_wait(f"gather_w{li}_wait", *gathers[li], False, after)
        return _assemble_weights(names, [_own_slot(l, s) for l, s in zip(lands, shards)])

    xs = x.reshape(T, D)
    saved = []
    for li in range(depth):
        G = {n: wts[n][li] for n in REPLICATED if n != "g_final"}
        if li == 0:
            G["g_mix"] = G["g_mix"] + tok[0, 0]
            w_in_full = _assemble_weights(first, got_in)["w_in"]
            rest_of = lambda after: gathered(0, others, after)
        else:
            W_all = gathered(li, first + others, xs)
            w_in_full = W_all["w_in"]
            rest_of = lambda after, W_all=W_all: W_all
        xs, sv, W = _layer_fwd(li, xs, p[li].reshape(T, -1), w_in_full, rest_of, G, rope, Bl, S, F)
        saved.append((sv, W, G))

    dx, loss_row, dg_final = _loss_bwd("loss_bwd", xs, loss_target.reshape(T, D), g_final)
    layer_grads = [None] * depth
    sends = []
    zero = [None]
    for li in reversed(range(depth)):
        sv, W, G = saved[li]
        if zero[0] is not None:
            G = dict(G, g_ple=G["g_ple"] + zero[0])

        def send(tag, group, li=li):
            *handles, tok = _send_start(f"exchange_g{li}{tag}_start", _grad_pieces(group), True)
            sends.append((li, tag, tuple(group), handles))
            zero[0] = tok[0, 0]
            return zero[0]

        dx, layer_grads[li] = _layer_bwd(li, dx, sv, W, G, rope, Bl, S, F, send)

    recv = {}
    for li, tag, names, handles in sends:
        pcs, lands = _send_wait(f"exchange_g{li}{tag}_wait", *handles, True, dx)
        for n, l, s in zip(names, lands, pcs):
            recv[(li, n)] = _own_slot(l, s)
    grads = {n: jnp.stack([layer_grads[li][n] for li in range(depth)]) for n in REPLICATED if n != "g_final"}
    grads["g_final"] = dg_final[0]
    _, (small_all,) = _exchange("exchange_small", [], [_pack_small(grads)])

    out_g, out_d, out_m, out_v = {}, {}, {}, {}
    for n, _ in SHARDED:
        shp = wts[n].shape
        two_d = (math.prod(shp[:-1]), shp[-1])
        pieces = [recv[(li, n)].reshape(N_DEV, two_d[0] // depth, two_d[1]) for li in range(depth)]
        res = _adamw(f"adamw_{n}", wts[n].reshape(two_d), mom[n].reshape(two_d), var[n].reshape(two_d), pieces)
        out_g[n], out_d[n], out_m[n], out_v[n] = [r.reshape(shp) for r in res]
    res = _adamw("adamw_replicated", _pack_small(wts), _pack_small(mom), _pack_small(var), [small_all])
    off = 0
    for n in REPLICATED:
        shp = wts[n].shape
        size = math.prod(shp)
        for dst, r in zip((out_g, out_d, out_m, out_v), res):
            dst[n] = r.reshape(-1)[off:off + size].reshape(shp)
        off += size

    loss = lax.psum(loss_row[0, 0], MESH_AXES)
    outs = [loss, dx.reshape(Bl, S, D)]
    for dct in (out_g, out_d, out_m, out_v):
        outs += [dct[n] for n in WEIGHT_ORDER]
    return tuple(outs)
```

```python
import math

import jax
import jax.numpy as jnp
from jax import lax
from jax.experimental import pallas as pl
from jax.experimental.pallas import tpu as pltpu

F32 = jnp.float32
BF16 = jnp.bfloat16

N_DEV = 8
HEAD_DIM = 128
HEADS = 4
GROUP_W = HEADS * HEAD_DIM
DILATIONS = (1, 4, 16)
N_GROUPS = len(DILATIONS)
QKV_W = 3 * N_GROUPS * GROUP_W
GROUP_QKV_W = 3 * GROUP_W
BLOCK = 128
ROPE_DIM = HEAD_DIM // 4
ROPE_HALF = ROPE_DIM // 2
ROPE_THETA = 500000.0
NEG_INF = -1e30
POOL_WINDOWS = (2, 4, 8, 16)
POOL_HALO = 16
CONV_HALO = 16
RMS_EPS = 1e-6
ADAM_LR = 0.001
ADAM_B1 = 0.9
ADAM_B2 = 0.999
ADAM_EPS = 1e-08
ADAM_WD = 0.01
ADAM_STEP = 10
LANE = 128
PACK_COLS = 1024
PACK_ROWS = 16
MESH_ID = pl.DeviceIdType.MESH
MESH_AXES = ("x", "y", "c")

NT_DIMS = (((1,), (1,)), ((), ()))
TN_DIMS = (((0,), (0,)), ((), ()))
NN_DIMS = (((1,), (0,)), ((), ()))

SHARDED = (("w_in", 1), ("w_ya", 1), ("w_yb", 0), ("pool_w", 1), ("w_o", 0), ("w_up", 1), ("conv_w", 1),
           ("w_down", 0), ("w_ple", 1), ("w_ple_gate", 0))
EXACT_F32 = ("conv_w",)
REPLICATED = ("g_mix", "pool_scale", "g_ffn", "conv_b", "g_ple", "g_final")
WEIGHT_ORDER = ("g_mix", "w_in", "w_ya", "w_yb", "pool_w", "pool_scale", "w_o", "g_ffn", "w_up", "conv_w", "conv_b",
                "w_down", "g_ple", "w_ple", "w_ple_gate", "g_final")


def _tile(n, pref, mult=LANE):
    if n <= pref:
        return n
    t = (pref // mult) * mult
    while t >= mult:
        if n % t == 0:
            return t
        t -= mult
    return n


def _sigmoid(x):
    return 1.0 / (1.0 + jnp.exp(-x))


def _dot(a, b, dims=NN_DIMS):
    return lax.dot_general(a.astype(BF16), b.astype(BF16), dims, preferred_element_type=F32)


def _rstd(x):
    return lax.rsqrt(jnp.mean(x * x, axis=-1, keepdims=True) + RMS_EPS)


def _rms_bwd(x, g, dh):
    r = _rstd(x)
    u = dh * g
    dx = r * u - x * (r * r * r) * jnp.mean(x * u, axis=-1, keepdims=True)
    return dx, dh * x * r


def _full(a):
    return pl.BlockSpec(a.shape, lambda *_: (0,) * a.ndim)


def _mm(name, a, b, *, grid, a_block, a_map, b_block, b_map, dims, acc_shape, outs, extras=(), epi=None):
    nk = grid[2]
    n_ex = len(extras)
    n_out = len(outs)

    def body(*refs):
        a_ref, b_ref = refs[0], refs[1]
        ex = refs[2:2 + n_ex]
        o = refs[2 + n_ex:2 + n_ex + n_out]
        acc = refs[2 + n_ex + n_out]
        i, j, k = pl.program_id(0), pl.program_id(1), pl.program_id(2)

        @pl.when(k == 0)
        def _():
            acc[...] = jnp.zeros_like(acc)

        acc[...] += _dot(a_ref[...], b_ref[...], dims)

        @pl.when(k == nk - 1)
        def _():
            if epi is None:
                o[0][...] = acc[...].astype(o[0].dtype)
            else:
                epi(acc[...], ex, o, i, j)

    in_specs = [pl.BlockSpec(a_block, a_map), pl.BlockSpec(b_block, b_map)]
    in_specs += [pl.BlockSpec(blk, mp) for (_, blk, mp) in extras]
    out_specs = [pl.BlockSpec(blk, mp) for (_, _, blk, mp) in outs]
    out_shape = [jax.ShapeDtypeStruct(s, d) for (s, d, _, _) in outs]
    return pl.pallas_call(
        body, name=name, grid=grid, in_specs=in_specs, out_specs=out_specs, out_shape=out_shape,
        scratch_shapes=[pltpu.VMEM(acc_shape, F32)],
    )(a, b, *[e[0] for e in extras])


def _mm_nn(name, a, b, *, out_dtype=F32, tm=1024, tn=1024, tk=512, b_col_off=0, n_cols=None, add=None):
    M, K = a.shape
    N = n_cols if n_cols is not None else b.shape[1]
    tm, tn, tk = _tile(M, tm, 8), _tile(N, tn), _tile(K, tk)
    assert b_col_off % tn == 0
    joff = b_col_off // tn
    extras, epi = (), None
    if add is not None:
        extras = ((add, (tm, tn), lambda i, j, k: (i, j)),)

        def epi(acc, ex, o, i, j):
            o[0][...] = (acc + ex[0][...]).astype(o[0].dtype)

    return _mm(name, a, b, grid=(M // tm, N // tn, K // tk),
               a_block=(tm, tk), a_map=lambda i, j, k: (i, k),
               b_block=(tk, tn), b_map=lambda i, j, k: (k, j + joff),
               dims=NN_DIMS, acc_shape=(tm, tn),
               outs=[((M, N), out_dtype, (tm, tn), lambda i, j, k: (i, j))], extras=extras, epi=epi)[0]


def _mm_nt(name, a, b, *, out_dtype=F32, tm=1024, tn=1024, tk=512):
    M, K = a.shape
    N = b.shape[0]
    tm, tn, tk = _tile(M, tm, 8), _tile(N, tn), _tile(K, tk)
    return _mm(name, a, b, grid=(M // tm, N // tn, K // tk),
               a_block=(tm, tk), a_map=lambda i, j, k: (i, k),
               b_block=(tn, tk), b_map=lambda i, j, k: (j, k),
               dims=NT_DIMS, acc_shape=(tm, tn),
               outs=[((M, N), out_dtype, (tm, tn), lambda i, j, k: (i, j))])[0]


def _mm_tn(name, a, b, *, tm=1024, tn=1024, tk=1024):
    K, M = a.shape
    N = b.shape[1]
    tm, tn, tk = _tile(M, tm), _tile(N, tn), _tile(K, tk, 8)
    return _mm(name, a, b, grid=(M // tm, N // tn, K // tk),
               a_block=(tk, tm), a_map=lambda i, j, k: (k, i),
               b_block=(tk, tn), b_map=lambda i, j, k: (k, j),
               dims=TN_DIMS, acc_shape=(tm, tn),
               outs=[((M, N), F32, (tm, tn), lambda i, j, k: (i, j))])[0]


def _mm_nt_rmsbwd(name, a, a_block, a_map, nk, tk, w, x, g, dres):
    T, D = x.shape
    tm = a_block[-2]

    def epi(acc, ex, o, i, j):
        @pl.when(i == 0)
        def _():
            o[1][...] = jnp.zeros_like(o[1])

        dx, dgr = _rms_bwd(ex[0][...], ex[1][...], acc)
        o[0][...] = ex[2][...] + dx
        o[1][...] += jnp.sum(dgr, axis=0, keepdims=True)

    row = lambda i, j, k: (i, 0)
    vec = lambda i, j, k: (0, 0)
    return _mm(name, a, w, grid=(T // tm, 1, nk),
               a_block=a_block, a_map=a_map,
               b_block=(D, tk), b_map=lambda i, j, k: (0, k),
               dims=NT_DIMS, acc_shape=(tm, D),
               outs=[((T, D), F32, (tm, D), row), ((1, D), F32, (1, D), vec)],
               extras=[(x, (tm, D), row), (g.reshape(1, D), (1, D), vec), (dres, (tm, D), row)], epi=epi)


def _in_bwd(name, segs, w_perm, x, g, dres, S):
    T, D = x.shape
    tm = _tile(S, 512, 256)
    nst = S // tm
    tk = GROUP_QKV_W
    steps = [a.shape[1] // tk for a, _ in segs]
    starts = [sum(steps[:s]) for s in range(len(segs))]
    nk = sum(steps)
    ns = len(segs)
    cols = _chunks(D)
    assert all(a.shape[1] % tk == 0 for a, _ in segs) and nk * tk == w_perm.shape[1]

    def body(*refs):
        a_refs = refs[:ns]
        w_ref, x_ref, g_ref, dres_ref, dx_ref, dg_ref, acc, acc_tok = refs[ns:]
        i, k = pl.program_id(0), pl.program_id(1)

        @pl.when(k == 0)
        def _():
            acc[...] = jnp.zeros_like(acc)
            acc_tok[...] = jnp.zeros_like(acc_tok)

        for s in range(ns):
            d = segs[s][1]

            @pl.when((k >= starts[s]) & (k < starts[s] + steps[s]))
            def _():
                if d == 1:
                    acc_tok[...] += _dot(a_refs[s][...], w_ref[...], NT_DIMS)
                else:
                    prod = _dot(a_refs[s][...].reshape(tm, tk), w_ref[...], NT_DIMS)
                    q = tm // d
                    for c, cs in enumerate(cols):
                        for r in range(d):
                            acc[c, pl.ds(r, q, stride=d), :] += prod[r * q:(r + 1) * q, cs]

        @pl.when(k == nk - 1)
        def _():
            @pl.when(i == 0)
            def _():
                dg_ref[...] = jnp.zeros_like(dg_ref)

            dh = acc_tok[...] + jnp.concatenate([acc[c] for c in range(len(cols))], axis=1)
            dx, dgr = _rms_bwd(x_ref[...], g_ref[...], dh)
            dx_ref[...] = dres_ref[...] + dx
            dg_ref[...] += jnp.sum(dgr, axis=0, keepdims=True)

    def seg_spec(s):
        kmap = lambda k: jnp.clip(k - starts[s], 0, steps[s] - 1)
        d = segs[s][1]
        if d == 1:
            return pl.BlockSpec((tm, tk), lambda i, k: (i, kmap(k)))
        return pl.BlockSpec((None, d, tm // d, tk), lambda i, k: (i // nst, 0, i % nst, kmap(k)))

    views = [a if d == 1 else a.reshape(T // S, d, S // d, a.shape[1]) for a, d in segs]
    row = pl.BlockSpec((tm, D), lambda i, k: (i, 0))
    vec = pl.BlockSpec((1, D), lambda i, k: (0, 0))
    return pl.pallas_call(
        body, name=name, grid=(T // tm, nk),
        in_specs=[seg_spec(s) for s in range(ns)] + [pl.BlockSpec((D, tk), lambda i, k: (0, k)), row, vec, row],
        out_specs=[row, vec],
        out_shape=[jax.ShapeDtypeStruct((T, D), F32), jax.ShapeDtypeStruct((1, D), F32)],
        scratch_shapes=[pltpu.VMEM((D // LANE, tm, LANE), F32), pltpu.VMEM((tm, D), F32)],
    )(*views, w_perm, x, g.reshape(1, D), dres)


def _rope_tables(S):
    pos = jnp.arange(S, dtype=F32)
    inv_freq = jnp.exp(jnp.arange(0, ROPE_DIM, 2, dtype=F32) * (-math.log(ROPE_THETA) / ROPE_DIM))
    ang = pos[:, None] * inv_freq[None, :]
    cos, sin = jnp.cos(ang), jnp.sin(ang)
    ones = jnp.ones((S, HEAD_DIM - ROPE_DIM), F32)
    zeros_h = jnp.zeros((S, ROPE_HALF), F32)
    zeros_r = jnp.zeros((S, HEAD_DIM - ROPE_DIM), F32)
    c = jnp.concatenate([cos, cos, ones], axis=1)
    sa = jnp.concatenate([-sin, zeros_h, zeros_r], axis=1)
    sb = jnp.concatenate([zeros_h, sin, zeros_r], axis=1)
    return c, sa, sb


def _rope(t, c, sa, sb):
    return t * c + pltpu.roll(t, HEAD_DIM - ROPE_HALF, 1) * sa + pltpu.roll(t, ROPE_HALF, 1) * sb


def _rms_fwd(name, x, g):
    T, D = x.shape
    tm = _tile(T, 512, 8)

    def body(x_ref, g_ref, h_ref):
        xv = x_ref[...]
        h_ref[...] = (xv * _rstd(xv) * g_ref[...]).astype(BF16)

    return pl.pallas_call(
        body, name=name, grid=(T // tm,),
        in_specs=[pl.BlockSpec((tm, D), lambda i: (i, 0)), pl.BlockSpec((1, D), lambda i: (0, 0))],
        out_specs=pl.BlockSpec((tm, D), lambda i: (i, 0)),
        out_shape=jax.ShapeDtypeStruct((T, D), BF16))(x, g.reshape(1, D))


def _chunks(width):
    return [slice(c * LANE, (c + 1) * LANE) for c in range(width // LANE)]


def _store_residue_major(val, sc, out_ref, d, dtype):
    rows = val.shape[0]
    for c, cs in enumerate(_chunks(val.shape[1])):
        sc[c] = val[:, cs]
    for r in range(d):
        for c, cs in enumerate(_chunks(val.shape[1])):
            out_ref[r, :, cs] = sc[c, pl.ds(r, rows // d, stride=d), :].astype(dtype)


def _load_token_order(blk_ref, sc, d):
    _, q, width = blk_ref.shape
    for r in range(d):
        for c, cs in enumerate(_chunks(width)):
            sc[c, pl.ds(r, q, stride=d), :] = blk_ref[r, :, cs]
    return jnp.concatenate([sc[c] for c in range(width // LANE)], axis=1)


def _residue_major_spec(d, q, width, nst):
    return pl.BlockSpec((None, d, q, width), lambda i, *_: (i // nst, 0, i % nst, 0))


def _rms_mix_fwd(name, x, g, S):
    T, D = x.shape
    Bl = T // S
    tm = _tile(S, 512, 256)
    nst = S // tm
    dils = [d for d in DILATIONS if d > 1]

    def body(x_ref, g_ref, h_ref, *rest):
        rm_refs, sc = rest[:len(dils)], rest[len(dils)]
        xv = x_ref[...]
        hv = xv * _rstd(xv) * g_ref[...]
        h_ref[...] = hv.astype(BF16)
        for d, o_ref in zip(dils, rm_refs):
            _store_residue_major(hv, sc, o_ref, d, BF16)

    row = pl.BlockSpec((tm, D), lambda i: (i, 0))
    return pl.pallas_call(
        body, name=name, grid=(T // tm,),
        in_specs=[row, pl.BlockSpec((1, D), lambda i: (0, 0))],
        out_specs=[row] + [_residue_major_spec(d, tm // d, D, nst) for d in dils],
        out_shape=[jax.ShapeDtypeStruct((T, D), BF16)]
        + [jax.ShapeDtypeStruct((Bl, d, S // d, D), BF16) for d in dils],
        scratch_shapes=[pltpu.VMEM((D // LANE, tm, LANE), F32)])(x, g.reshape(1, D))


def _qkv_proj(name, h, w_perm, rope, S, col_off):
    T, D = h.shape
    tm = _tile(S, 1024, 8)
    tn = GROUP_W
    tk = _tile(D, 1024)
    c_t, sa_t, sb_t = rope
    n_seq_tiles = S // tm
    joff = col_off // tn
    tmap = lambda i, j, k: (i % n_seq_tiles, 0)

    def epi(acc, ex, o, i, j):
        is_rot = j < 2

        @pl.when(is_rot)
        def _():
            c, sa, sb = ex[0][...], ex[1][...], ex[2][...]
            for hh in range(HEADS):
                sl = slice(hh * HEAD_DIM, (hh + 1) * HEAD_DIM)
                o[0][:, sl] = _rope(acc[:, sl], c, sa, sb).astype(BF16)

        @pl.when(jnp.logical_not(is_rot))
        def _():
            o[0][...] = acc.astype(BF16)

    return _mm(name, h, w_perm, grid=(T // tm, 3, D // tk),
               a_block=(tm, tk), a_map=lambda i, j, k: (i, k),
               b_block=(tk, tn), b_map=lambda i, j, k: (k, j + joff),
               dims=NN_DIMS, acc_shape=(tm, tn),
               outs=[((T, GROUP_QKV_W), BF16, (tm, tn), lambda i, j, k: (i, j))],
               extras=[(c_t, (tm, HEAD_DIM), tmap), (sa_t, (tm, HEAD_DIM), tmap), (sb_t, (tm, HEAD_DIM), tmap)],
               epi=epi)[0]


ATTN_BLOCKS_PER_STEP = 4


def _attn_mask(rows, lead, has_prev):
    qi = lax.broadcasted_iota(jnp.int32, (rows, lead + rows), 0)
    ki = lax.broadcasted_iota(jnp.int32, (rows, lead + rows), 1)
    diff = lead + qi - ki
    band = (diff >= 0) & (diff <= BLOCK)
    return band & (has_prev | (ki >= lead)) if lead else band


def _attn_chunking(nb):
    cb = min(ATTN_BLOCKS_PER_STEP, nb)
    return cb, ATTN_BLOCKS_PER_STEP // cb


def _attn_fwd(name, qkv, g, Bl, S):
    d = DILATIONS[g]
    L = S // d
    nb = L // BLOCK
    cb, cs = _attn_chunking(nb)
    qv = qkv.reshape(Bl * d, L, GROUP_QKV_W)
    scale = HEAD_DIM ** -0.5

    def body(q_ref, kc_ref, vc_ref, kp_ref, vp_ref, o_ref, l_ref):
        n = pl.program_id(1)
        for si in range(cs):
            for bi in range(cb):
                rows = slice(bi * BLOCK, (bi + 1) * BLOCK)
                before = slice((bi - 1) * BLOCK, bi * BLOCK)
                valid = _attn_mask(BLOCK, BLOCK, n > 0 if bi == 0 else True)
                for hh in range(HEADS):
                    sl = slice(hh * HEAD_DIM, (hh + 1) * HEAD_DIM)
                    kp = kp_ref[si, :, sl] if bi == 0 else kc_ref[si, before, sl]
                    vp = vp_ref[si, :, sl] if bi == 0 else vc_ref[si, before, sl]
                    kk = jnp.concatenate([kp, kc_ref[si, rows, sl]], axis=0)
                    vv = jnp.concatenate([vp, vc_ref[si, rows, sl]], axis=0)
                    s = jnp.where(valid, _dot(q_ref[si, rows, sl], kk, NT_DIMS) * scale, NEG_INF)
                    m = jnp.max(s, axis=-1, keepdims=True)
                    p = jnp.exp(s - m)
                    l = jnp.sum(p, axis=-1, keepdims=True)
                    o_ref[si, rows, sl] = _dot(p, vv) / l
                    l_ref[si, rows, sl] = jnp.broadcast_to(m + jnp.log(l), (BLOCK, HEAD_DIM))

    main = (cs, cb * BLOCK, GROUP_W)
    edge = (cs, BLOCK, GROUP_W)
    cur = lambda off: pl.BlockSpec(main, lambda s, n: (s, n, off))
    prev = lambda off: pl.BlockSpec(edge, lambda s, n: (s, jnp.maximum(n * cb - 1, 0), off))
    out = pl.BlockSpec(main, lambda s, n: (s, n, 0))
    return pl.pallas_call(
        body, name=name, grid=(Bl * d // cs, nb // cb),
        in_specs=[cur(0), cur(1), cur(2), prev(1), prev(2)],
        out_specs=[out, out],
        out_shape=[jax.ShapeDtypeStruct((Bl * d, L, GROUP_W), F32)] * 2)(qv, qv, qv, qv, qv)


def _merge_weights(l0, l1, l2):
    mx = jnp.maximum(jnp.maximum(l0, l1), l2)
    e0, e1, e2 = jnp.exp(l0 - mx), jnp.exp(l1 - mx), jnp.exp(l2 - mx)
    inv = 1.0 / (e0 + e1 + e2)
    return e0 * inv, e1 * inv, e2 * inv


def _group_specs(tm, nst, S):
    specs = []
    for d in DILATIONS:
        specs.append(pl.BlockSpec((tm, GROUP_W), lambda i: (i, 0)) if d == 1
                     else _residue_major_spec(d, tm // d, GROUP_W, nst))

    def views(arrs):
        out = []
        for d, a in zip(DILATIONS, arrs):
            out.append(a.reshape(-1, GROUP_W) if d == 1 else a.reshape(-1, d, S // d, GROUP_W))
        return out

    return specs, views


def _group_scratch(tm, per_group):
    n = per_group * sum(1 for d in DILATIONS if d > 1)
    return [pltpu.VMEM((GROUP_W // LANE, tm, LANE), F32) for _ in range(n)]


def _group_values(o_refs, l_refs, scs):
    scs = list(scs)
    ov, lv = [], []
    for d, o_ref, l_ref in zip(DILATIONS, o_refs, l_refs):
        if d == 1:
            ov.append(o_ref[...])
            lv.append(l_ref[...])
        else:
            ov.append(_load_token_order(o_ref, scs.pop(), d))
            lv.append(_load_token_order(l_ref, scs.pop(), d))
    return ov, lv


def _pool_inv_count(tseq, w):
    return 1.0 / jnp.minimum(tseq + 1, w).astype(F32)


def _mix_out_fwd(name, x, zr, o_l, lse_l, w_ya, pool_w, pool_scale, w_yb, w_o, S):
    T, D = x.shape
    gw = D // len(POOL_WINDOWS)
    tm = _tile(S, 256, POOL_HALO)
    nst = S // tm
    hpt = tm // POOL_HALO

    def body(x_ref, u_ref, uh_ref, ga_ref, gb_ref, o0, o1, o2, l0, l1, l2, wya_ref, pw_ref, ps_ref, wyb_ref, wo_ref,
             x1_ref, attn_ref, pooled_ref, mixed_ref, ya_ref, yb_ref, merged_ref, *scs):
        it = pl.program_id(0) % nst
        ov, lv = _group_values((o0, o1, o2), (l0, l1, l2), scs)
        w0, w1, w2 = _merge_weights(*lv)
        attn = w0 * ov[0] + w1 * ov[1] + w2 * ov[2]
        attn_ref[...] = attn.astype(BF16)
        y_a = _dot(attn, wya_ref[...])

        u = u_ref[...]
        halo = uh_ref[...] * jnp.where(it == 0, 0.0, 1.0)
        ext = jnp.concatenate([halo, u], axis=0)
        tseq = it * tm + lax.broadcasted_iota(jnp.int32, (tm, 1), 0)
        pm_parts = []
        for gi, w in enumerate(POOL_WINDOWS):
            cs = slice(gi * gw, (gi + 1) * gw)
            s = ext[:, cs]
            step = 1
            while step < w:
                s = s + pltpu.roll(s, step, 0)
                step *= 2
            pooled_g = (s[POOL_HALO:, :] * _pool_inv_count(tseq, w) - u[:, cs]).astype(BF16)
            pooled_ref[:, cs] = pooled_g
            pm_parts.append(_dot(pooled_g, pw_ref[gi]))
        mixed = (jnp.concatenate(pm_parts, axis=1) * ps_ref[...]).astype(BF16)
        mixed_ref[...] = mixed
        y_b = _dot(mixed, wyb_ref[...])
        merged = (_sigmoid(ga_ref[...]) * y_a + _sigmoid(gb_ref[...]) * y_b).astype(BF16)
        ya_ref[...] = y_a.astype(BF16)
        yb_ref[...] = y_b.astype(BF16)
        merged_ref[...] = merged
        x1_ref[...] = x_ref[...] + _dot(merged, wo_ref[...])

    row = lambda c: pl.BlockSpec((tm, D), lambda i: (i, c))
    row512 = pl.BlockSpec((tm, GROUP_W), lambda i: (i, 0))
    ps = pool_scale.reshape(1, D)
    halo_spec = pl.BlockSpec((POOL_HALO, D), lambda i: (jnp.maximum(i * hpt - 1, 0), 0))
    grp_specs, grp_views = _group_specs(tm, nst, S)
    return pl.pallas_call(
        body, name=name, grid=(T // tm,),
        in_specs=[row(0), row(0), halo_spec, row(1), row(2)] + grp_specs * 2
        + [_full(w_ya), _full(pool_w), _full(ps), _full(w_yb), _full(w_o)],
        out_specs=[row(0), row512, row(0), row(0), row(0), row(0), row(0)],
        out_shape=[jax.ShapeDtypeStruct((T, D), F32), jax.ShapeDtypeStruct((T, GROUP_W), BF16),
                   jax.ShapeDtypeStruct((T, D), BF16), jax.ShapeDtypeStruct((T, D), BF16),
                   jax.ShapeDtypeStruct((T, D), BF16), jax.ShapeDtypeStruct((T, D), BF16),
                   jax.ShapeDtypeStruct((T, D), BF16)],
        scratch_shapes=_group_scratch(tm, 2),
    )(x, zr, zr, zr, zr, *grp_views(o_l), *grp_views(lse_l), w_ya, pool_w, ps, w_yb, w_o)


def _up_proj(name, h2, w_up, F):
    T, D = h2.shape
    tm, tn, tk = _tile(T, 1024, 8), _tile(F, 1408), _tile(D, 1024)
    njh = F // tn
    return _mm(name, h2, w_up, grid=(T // tm, 2 * njh, D // tk),
               a_block=(tm, tk), a_map=lambda i, j, k: (i, k),
               b_block=(tk, tn), b_map=lambda i, j, k: (k, j),
               dims=NN_DIMS, acc_shape=(tm, tn),
               outs=[((2, T, F), BF16, (None, tm, tn), lambda i, j, k: (j // njh, i, j % njh))])[0]


def _conv_y(ext, w_ref, b_ref):
    return (b_ref[...] + w_ref[2:3, :] * ext + w_ref[1:2, :] * pltpu.roll(ext, 1, 0)
            + w_ref[0:1, :] * pltpu.roll(ext, 2, 0))


def _conv_params(conv_w, conv_b, F):
    cw = conv_w.reshape(3, 2, F).transpose(1, 0, 2)
    return cw, conv_b.reshape(2, 1, F)


def _ffn_act_fwd(name, u, conv_w, conv_b, S):
    _, T, F = u.shape
    tm = _tile(S, 512, CONV_HALO)
    tf = _tile(F, 1408)
    nst = S // tm
    hpt = tm // CONV_HALO
    cw, cb = _conv_params(conv_w, conv_b, F)

    def body(ug_ref, uv_ref, hg_ref, hv_ref, wg_ref, wv_ref, bg_ref, bv_ref, a_ref):
        keep = jnp.where(pl.program_id(0) % nst == 0, 0.0, 1.0)
        ext = lambda h_ref, u_ref: jnp.concatenate([h_ref[...].astype(F32) * keep, u_ref[...].astype(F32)], axis=0)
        yg = _conv_y(ext(hg_ref, ug_ref), wg_ref, bg_ref)[CONV_HALO:, :]
        yv = _conv_y(ext(hv_ref, uv_ref), wv_ref, bv_ref)[CONV_HALO:, :]
        a_ref[...] = (yg * _sigmoid(yg) * yv).astype(BF16)

    main = lambda h: pl.BlockSpec((None, tm, tf), lambda i, j: (h, i, j))
    halo = lambda h: pl.BlockSpec((None, CONV_HALO, tf), lambda i, j: (h, jnp.maximum(i * hpt - 1, 0), j))
    wsp = lambda h: pl.BlockSpec((None, 3, tf), lambda i, j: (h, 0, j))
    bsp = lambda h: pl.BlockSpec((None, 1, tf), lambda i, j: (h, 0, j))
    return pl.pallas_call(
        body, name=name, grid=(T // tm, F // tf),
        in_specs=[main(0), main(1), halo(0), halo(1), wsp(0), wsp(1), bsp(0), bsp(1)],
        out_specs=pl.BlockSpec((tm, tf), lambda i, j: (i, j)),
        out_shape=jax.ShapeDtypeStruct((T, F), BF16))(u, u, u, u, cw, cw, cb, cb)


def _ple_fwd(name, x2, p, g_ple, w_gate, w_ple):
    T, D = x2.shape
    P = p.shape[1]
    tm = _tile(T, 512, 8)

    def body(x_ref, p_ref, g_ref, wg_ref, wp_ref, x3_ref, e_ref, pg_ref, pbf_ref):
        xv = x_ref[...]
        h3 = xv * _rstd(xv) * g_ref[...]
        pg = _sigmoid(_dot(h3, wg_ref[...]))
        pb = p_ref[...].astype(BF16)
        e = _dot(pb, wp_ref[...])
        x3_ref[...] = xv + e * pg
        e_ref[...] = e.astype(BF16)
        pg_ref[...] = pg.astype(BF16)
        pbf_ref[...] = pb

    row = pl.BlockSpec((tm, D), lambda i: (i, 0))
    prow = pl.BlockSpec((tm, P), lambda i: (i, 0))
    g2 = g_ple.reshape(1, D)
    return pl.pallas_call(
        body, name=name, grid=(T // tm,),
        in_specs=[row, prow, _full(g2), _full(w_gate), _full(w_ple)],
        out_specs=[row, row, row, prow],
        out_shape=[jax.ShapeDtypeStruct((T, D), F32)] + [jax.ShapeDtypeStruct((T, D), BF16)] * 2
        + [jax.ShapeDtypeStruct((T, P), BF16)],
    )(x2, p, g2, w_gate, w_ple)


def _loss_bwd(name, xf, target, g_final):
    T, D = xf.shape
    tm = _tile(T, 512, 8)
    nt = T // tm

    def body(x_ref, t_ref, g_ref, dx_ref, loss_ref, dg_ref, lacc):
        i = pl.program_id(0)

        @pl.when(i == 0)
        def _():
            lacc[...] = jnp.zeros_like(lacc)
            dg_ref[...] = jnp.zeros_like(dg_ref)
            loss_ref[...] = jnp.zeros_like(loss_ref)

        xv = x_ref[...]
        g = g_ref[...]
        diff = xv * _rstd(xv) * g - t_ref[...]
        lacc[...] += jnp.sum(diff * diff, axis=0, keepdims=True)
        dx, dgr = _rms_bwd(xv, g, diff * (1.0 / D))
        dx_ref[...] = dx
        dg_ref[...] += jnp.sum(dgr, axis=0, keepdims=True)

        @pl.when(i == nt - 1)
        def _():
            tot = jnp.sum(lacc[...], axis=-1, keepdims=True) * (0.5 / D)
            loss_ref[...] = jnp.broadcast_to(tot, (1, LANE))

    row = pl.BlockSpec((tm, D), lambda i: (i, 0))
    vec = pl.BlockSpec((1, D), lambda i: (0, 0))
    return pl.pallas_call(
        body, name=name, grid=(nt,),
        in_specs=[row, row, vec],
        out_specs=[row, pl.BlockSpec((1, LANE), lambda i: (0, 0)), vec],
        out_shape=[jax.ShapeDtypeStruct((T, D), F32), jax.ShapeDtypeStruct((1, LANE), F32),
                   jax.ShapeDtypeStruct((1, D), F32)],
        scratch_shapes=[pltpu.VMEM((1, D), F32)])(xf, target, g_final.reshape(1, D))


def _ple_bwd(name, dx3, x2, e, pg, g_ple, w_gate):
    T, D = x2.shape
    tm = _tile(T, 512, 8)

    def body(dx3_ref, x_ref, e_ref, pg_ref, g_ref, wg_ref, dx2_ref, de_ref, ds_ref, h3_ref, dg_ref):
        @pl.when(pl.program_id(0) == 0)
        def _():
            dg_ref[...] = jnp.zeros_like(dg_ref)

        dx3v, xv, pgv, g = dx3_ref[...], x_ref[...], pg_ref[...].astype(F32), g_ref[...]
        de_ref[...] = (dx3v * pgv).astype(BF16)
        ds = (dx3v * e_ref[...].astype(F32) * pgv * (1.0 - pgv)).astype(BF16)
        ds_ref[...] = ds
        dh3 = _dot(ds, wg_ref[...], NT_DIMS)
        h3_ref[...] = (xv * _rstd(xv) * g).astype(BF16)
        dx, dgr = _rms_bwd(xv, g, dh3)
        dx2_ref[...] = dx3v + dx
        dg_ref[...] += jnp.sum(dgr, axis=0, keepdims=True)

    row = pl.BlockSpec((tm, D), lambda i: (i, 0))
    vec = pl.BlockSpec((1, D), lambda i: (0, 0))
    return pl.pallas_call(
        body, name=name, grid=(T // tm,),
        in_specs=[row, row, row, row, vec, _full(w_gate)],
        out_specs=[row, row, row, row, vec],
        out_shape=[jax.ShapeDtypeStruct((T, D), F32)] + [jax.ShapeDtypeStruct((T, D), BF16)] * 3
        + [jax.ShapeDtypeStruct((1, D), F32)])(dx3, x2, e, pg, g_ple.reshape(1, D), w_gate)


def _ffn_act_bwd(name, u, d_a, conv_w, conv_b, S):
    _, T, F = u.shape
    H = CONV_HALO
    tm = _tile(S, 512, H)
    tf = _tile(F, 1408)
    nst = S // tm
    hpt = tm // H
    last_halo = T // H - 1
    n_ext = tm + H
    cw, cb = _conv_params(conv_w, conv_b, F)

    def body(ug_ref, uv_ref, pg_ref, pv_ref, ng_ref, nv_ref, da_ref, dan_ref, wg_ref, wv_ref, bg_ref, bv_ref,
             du_ref, dw_ref, db_ref):
        i = pl.program_id(1)
        it = i % nst

        @pl.when(i == 0)
        def _():
            dw_ref[...] = jnp.zeros_like(dw_ref)
            db_ref[...] = jnp.zeros_like(db_ref)

        keep_prev = jnp.where(it == 0, 0.0, 1.0)
        keep_next = jnp.where(it == nst - 1, 0.0, 1.0)
        def shifted(p_ref, u_ref, n_ref):
            ext = jnp.concatenate([p_ref[...].astype(F32) * keep_prev, u_ref[...].astype(F32),
                                   n_ref[...].astype(F32)], axis=0)
            return ext[H:, :], pltpu.roll(ext, 1, 0)[H:, :], pltpu.roll(ext, 2, 0)[H:, :]

        us_g, us_v = shifted(pg_ref, ug_ref, ng_ref), shifted(pv_ref, uv_ref, nv_ref)
        conv = lambda us, w_ref, b_ref: b_ref[...] + w_ref[2:3, :] * us[0] + w_ref[1:2, :] * us[1] + w_ref[0:1, :] * us[2]
        yg, yv = conv(us_g, wg_ref, bg_ref), conv(us_v, wv_ref, bv_ref)
        da = jnp.concatenate([da_ref[...].astype(F32), dan_ref[...].astype(F32) * keep_next], axis=0)
        sg = _sigmoid(yg)
        silu = yg * sg
        dyv = da * silu
        dyg = (da * yv) * (sg + silu * (1.0 - sg))
        for half, (dy, us, w_ref) in enumerate(((dyg, us_g, wg_ref), (dyv, us_v, wv_ref))):
            du = (w_ref[2:3, :] * dy + w_ref[1:2, :] * pltpu.roll(dy, n_ext - 1, 0)
                  + w_ref[0:1, :] * pltpu.roll(dy, n_ext - 2, 0))
            du_ref[half] = du[:tm, :].astype(BF16)
            dym = dy[:tm, :]
            db_ref[half] += jnp.sum(dym, axis=0, keepdims=True)
            for tap in range(3):
                dw_ref[half, tap:tap + 1, :] += jnp.sum(dym * us[2 - tap][:tm, :], axis=0, keepdims=True)

    main = lambda h: pl.BlockSpec((None, tm, tf), lambda j, i: (h, i, j))
    prev = lambda h: pl.BlockSpec((None, H, tf), lambda j, i: (h, jnp.maximum(i * hpt - 1, 0), j))
    nxt = lambda h: pl.BlockSpec((None, H, tf), lambda j, i: (h, jnp.minimum((i + 1) * hpt, last_halo), j))
    wsp = lambda h: pl.BlockSpec((None, 3, tf), lambda j, i: (h, 0, j))
    bsp = lambda h: pl.BlockSpec((None, 1, tf), lambda j, i: (h, 0, j))
    return pl.pallas_call(
        body, name=name, grid=(F // tf, T // tm),
        in_specs=[main(0), main(1), prev(0), prev(1), nxt(0), nxt(1),
                  pl.BlockSpec((tm, tf), lambda j, i: (i, j)),
                  pl.BlockSpec((H, tf), lambda j, i: (jnp.minimum((i + 1) * hpt, last_halo), j)),
                  wsp(0), wsp(1), bsp(0), bsp(1)],
        out_specs=[pl.BlockSpec((2, tm, tf), lambda j, i: (0, i, j)),
                   pl.BlockSpec((2, 3, tf), lambda j, i: (0, 0, j)),
                   pl.BlockSpec((2, 1, tf), lambda j, i: (0, 0, j))],
        out_shape=[jax.ShapeDtypeStruct((2, T, F), BF16), jax.ShapeDtypeStruct((2, 3, F), F32),
                   jax.ShapeDtypeStruct((2, 1, F), F32)],
    )(u, u, u, u, u, u, d_a, d_a, cw, cw, cb, cb)


def _mix_out_bwd(name, dx1, zr, y_a, y_b, o_l, lse_l, pooled, w_o, w_ya, w_yb, pool_w, pool_scale, S):
    T, D = dx1.shape
    gw = D // len(POOL_WINDOWS)
    H = POOL_HALO
    tm = _tile(S, 256, H)
    nst = S // tm
    hpt = tm // H
    last_halo = T // H - 1
    n_ext = tm + H

    def body(dx_ref, dxn_ref, ga_ref, gb_ref, gbn_ref, ya_ref, yb_ref, o0, o1, o2, l0, l1, l2, pooled_ref,
             wo_ref, wya_ref, wyb_ref, pw_ref, ps_ref,
             dz_ref, dya_ref, dyb_ref, dpm_ref, do0, do1, do2, c0, c1, c2, dps_ref, *scs):
        i = pl.program_id(0)
        it = i % nst

        @pl.when(i == 0)
        def _():
            dps_ref[...] = jnp.zeros_like(dps_ref)

        keep_next = jnp.where(it == nst - 1, 0.0, 1.0)
        dm_e = _dot(jnp.concatenate([dx_ref[...], dxn_ref[...]], axis=0), wo_ref[...], NT_DIMS)
        sgb_e = _sigmoid(jnp.concatenate([gb_ref[...], gbn_ref[...]], axis=0))
        dyb_e = dm_e * sgb_e
        dm = dm_e[:tm, :]
        sga = _sigmoid(ga_ref[...])
        sgb = sgb_e[:tm, :]
        d_ga = dm * ya_ref[...].astype(F32) * (sga * (1.0 - sga))
        d_gb = dm * yb_ref[...].astype(F32) * (sgb * (1.0 - sgb))
        dya = (dm * sga).astype(BF16)
        dya_ref[...] = dya
        dyb_ref[...] = dyb_e[:tm, :].astype(BF16)
        dmixed_e = _dot(dyb_e, wyb_ref[...], NT_DIMS)

        rows = lax.broadcasted_iota(jnp.int32, (n_ext, 1), 0)
        tseq = it * tm + rows
        live = jnp.where(rows < tm, 1.0, keep_next)
        ps = ps_ref[...]
        du_parts = []
        for gi, w in enumerate(POOL_WINDOWS):
            cs = slice(gi * gw, (gi + 1) * gw)
            pm_g = _dot(pooled_ref[:, cs], pw_ref[gi])
            dps_ref[:, cs] += jnp.sum(dmixed_e[:tm, cs] * pm_g, axis=0, keepdims=True)
            dpm_e = (dmixed_e[:, cs] * ps[:, cs]).astype(BF16)
            dpm_ref[:, cs] = dpm_e[:tm, :]
            dpooled_e = _dot(dpm_e, pw_ref[gi], NT_DIMS)
            s = dpooled_e * (_pool_inv_count(tseq, w) * live)
            step = 1
            while step < w:
                s = s + pltpu.roll(s, n_ext - step, 0)
                step *= 2
            du_parts.append(s[:tm, :] - dpooled_e[:tm, :])
        dz_ref[...] = jnp.concatenate(du_parts + [d_ga, d_gb], axis=1).astype(BF16)

        d_attn = _dot(dya, wya_ref[...], NT_DIMS)
        ov, lv = _group_values((o0, o1, o2), (l0, l1, l2), scs[:n_in_sc])
        ws = _merge_weights(*lv)
        prod = d_attn * (ws[0] * ov[0] + ws[1] * ov[1] + ws[2] * ov[2])
        rs = jnp.concatenate(
            [jnp.broadcast_to(jnp.sum(prod[:, hh * HEAD_DIM:(hh + 1) * HEAD_DIM], axis=-1, keepdims=True),
                              (tm, HEAD_DIM)) for hh in range(HEADS)], axis=1)
        out_scs = list(scs[n_in_sc:])
        for d, wg, do_ref, c_ref in zip(DILATIONS, ws, (do0, do1, do2), (c0, c1, c2)):
            if d == 1:
                do_ref[...] = (wg * d_attn).astype(BF16)
                c_ref[...] = -wg * rs
            else:
                _store_residue_major(wg * d_attn, out_scs.pop(), do_ref, d, BF16)
                _store_residue_major(-wg * rs, out_scs.pop(), c_ref, d, F32)

    row = lambda c: pl.BlockSpec((tm, D), lambda i: (i, c))
    nxt = lambda c: pl.BlockSpec((H, D), lambda i: (jnp.minimum((i + 1) * hpt, last_halo), c))
    ps2 = pool_scale.reshape(1, D)
    bf = lambda w: jax.ShapeDtypeStruct((T, w), BF16)
    grp_specs, grp_views = _group_specs(tm, nst, S)
    grp_shape = lambda dt: [jax.ShapeDtypeStruct((T, GROUP_W) if d == 1 else (T // S, d, S // d, GROUP_W), dt)
                            for d in DILATIONS]
    n_in_sc = len(_group_scratch(tm, 2))
    return pl.pallas_call(
        body, name=name, grid=(T // tm,),
        in_specs=[row(0), nxt(0), row(1), row(2), nxt(2), row(0), row(0)] + grp_specs * 2 + [row(0)]
        + [_full(w_o), _full(w_ya), _full(w_yb), _full(pool_w), _full(ps2)],
        out_specs=[pl.BlockSpec((tm, 3 * D), lambda i: (i, 0)), row(0), row(0), row(0)] + grp_specs * 2
        + [pl.BlockSpec((1, D), lambda i: (0, 0))],
        out_shape=[bf(3 * D), bf(D), bf(D), bf(D)] + grp_shape(BF16) + grp_shape(F32)
        + [jax.ShapeDtypeStruct((1, D), F32)],
        scratch_shapes=_group_scratch(tm, 4),
    )(dx1, dx1, zr, zr, zr, y_a, y_b, *grp_views(o_l), *grp_views(lse_l), pooled, w_o, w_ya, w_yb, pool_w, ps2)


def _attn_bwd(name, qkv, d_o, lse, cst, rope, g, Bl, S):
    d = DILATIONS[g]
    L = S // d
    nb = L // BLOCK
    qv = qkv.reshape(Bl * d, L, GROUP_QKV_W)
    dov = d_o.reshape(Bl * d, L, GROUP_W)
    lv = lse.reshape(Bl * d, L, GROUP_W)
    cv = cst.reshape(Bl * d, L, GROUP_W)
    tabs = [t.reshape(d, L, HEAD_DIM) for t in rope]
    scale = HEAD_DIM ** -0.5

    cb, cs = _attn_chunking(nb)
    qc = cb * BLOCK
    lead = BLOCK if nb > 1 else 0

    def body(q_ref, qn_ref, kp_ref, kc_ref, vp_ref, vc_ref, do_ref, don_ref, l_ref, ln_ref, c_ref, cn_ref,
             cos_ref, sa_ref, sb_ref, out_ref):
        n = pl.program_id(1)
        valid = _attn_mask(qc, lead, n > 0)
        qi = lax.broadcasted_iota(jnp.int32, (BLOCK, BLOCK), 0)
        ki = lax.broadcasted_iota(jnp.int32, (BLOCK, BLOCK), 1)
        valid_n = (ki >= qi) & ((n + 1) * cb < nb)
        tail = slice(qc - BLOCK, qc)
        for si in range(cs):
            cos, sa, sb = cos_ref[si], -sa_ref[si], -sb_ref[si]
            for hh in range(HEADS):
                sl = slice(hh * HEAD_DIM, (hh + 1) * HEAD_DIM)
                col = slice(hh * HEAD_DIM, hh * HEAD_DIM + 1)
                q, kc, vc, do = q_ref[si, :, sl], kc_ref[si, :, sl], vc_ref[si, :, sl], do_ref[si, :, sl]
                kk, vv = kc, vc
                if lead:
                    kk = jnp.concatenate([kp_ref[si, :, sl], kc], axis=0)
                    vv = jnp.concatenate([vp_ref[si, :, sl], vc], axis=0)
                s = jnp.where(valid, _dot(q, kk, NT_DIMS) * scale, NEG_INF)
                p = jnp.exp(s - l_ref[si, :, col])
                ds = p * (_dot(do, vv, NT_DIMS) + c_ref[si, :, col])
                dq = _dot(ds, kk) * scale
                dk = _dot(ds[:, lead:], q, TN_DIMS)
                dv = _dot(p[:, lead:], do, TN_DIMS)
                if nb > cb:
                    qn, don = qn_ref[si, :, sl], don_ref[si, :, sl]
                    s2 = jnp.where(valid_n, _dot(qn, kc[tail], NT_DIMS) * scale, NEG_INF)
                    p2 = jnp.exp(s2 - ln_ref[si, :, col])
                    ds2 = p2 * (_dot(don, vc[tail], NT_DIMS) + cn_ref[si, :, col])
                    dk = jnp.concatenate([dk[:qc - BLOCK], dk[tail] + _dot(ds2, qn, TN_DIMS)], axis=0)
                    dv = jnp.concatenate([dv[:qc - BLOCK], dv[tail] + _dot(p2, don, TN_DIMS)], axis=0)
                out_ref[si, :, sl] = _rope(dq, cos, sa, sb).astype(BF16)
                out_ref[si, :, GROUP_W + hh * HEAD_DIM:GROUP_W + (hh + 1) * HEAD_DIM] = (
                    _rope(dk * scale, cos, sa, sb).astype(BF16))
                out_ref[si, :, 2 * GROUP_W + hh * HEAD_DIM:2 * GROUP_W + (hh + 1) * HEAD_DIM] = dv.astype(BF16)

    main = (cs, cb * BLOCK, GROUP_W)
    edge = (cs, BLOCK, GROUP_W)
    cur = lambda off: pl.BlockSpec(main, lambda s, n: (s, n, off))
    prv = lambda off: pl.BlockSpec(edge, lambda s, n: (s, jnp.maximum(n * cb - 1, 0), off))
    nxt = lambda off: pl.BlockSpec(edge, lambda s, n: (s, jnp.minimum((n + 1) * cb, nb - 1), off))
    tab = pl.BlockSpec((cs, cb * BLOCK, HEAD_DIM), lambda s, n: (s % (d // cs), n, 0))
    out = pl.pallas_call(
        body, name=name, grid=(Bl * d // cs, nb // cb),
        in_specs=[cur(0), nxt(0), prv(1), cur(1), prv(2), cur(2),
                  cur(0), nxt(0), cur(0), nxt(0), cur(0), nxt(0), tab, tab, tab],
        out_specs=pl.BlockSpec((cs, cb * BLOCK, GROUP_QKV_W), lambda s, n: (s, n, 0)),
        out_shape=jax.ShapeDtypeStruct((Bl * d, L, GROUP_QKV_W), BF16),
    )(qv, qv, qv, qv, qv, qv, dov, dov, lv, lv, cv, cv, *tabs)
    return out.reshape(Bl * S, GROUP_QKV_W)


def _pool_w_grad(name, pooled, d_pm, gw):
    T = pooled.shape[0]
    ng = len(POOL_WINDOWS)
    tk = _tile(T, 1024, 8)
    return _mm(name, pooled, d_pm, grid=(ng, 1, T // tk),
               a_block=(tk, gw), a_map=lambda i, j, k: (k, i),
               b_block=(tk, gw), b_map=lambda i, j, k: (k, i),
               dims=TN_DIMS, acc_shape=(gw, gw),
               outs=[((ng, gw, gw), F32, (None, gw, gw), lambda i, j, k: (i, 0, 0))])[0]


def _up_w_grad(name, h2, du):
    T, D = h2.shape
    F = du.shape[2]
    tm, tn, tk = _tile(D, 1024), _tile(F, 1408), _tile(T, 1024, 8)
    njh = F // tn
    return _mm(name, h2, du, grid=(D // tm, 2 * njh, T // tk),
               a_block=(tk, tm), a_map=lambda i, j, k: (k, i),
               b_block=(None, tk, tn), b_map=lambda i, j, k: (j // njh, k, j % njh),
               dims=TN_DIMS, acc_shape=(tm, tn),
               outs=[((D, 2 * F), F32, (tm, tn), lambda i, j, k: (i, j))])[0]


def _adamw(name, w, m, v, pieces):
    R, C = w.shape
    nl = len(pieces)
    rl = R // nl
    if nl > 1 and rl % 8:
        per = [_adamw(f"{name}_{l}", w[l * rl:(l + 1) * rl], m[l * rl:(l + 1) * rl], v[l * rl:(l + 1) * rl],
                      [pieces[l]]) for l in range(nl)]
        return [jnp.concatenate([p[o] for p in per], axis=0) for o in range(4)]
    tr = _tile(rl, max(PACK_ROWS, (1 << 18) // C // PACK_ROWS * PACK_ROWS), PACK_ROWS)
    nbl = rl // tr
    c1 = 1.0 - ADAM_B1 ** ADAM_STEP
    c2 = 1.0 - ADAM_B2 ** ADAM_STEP

    def body(w_ref, m_ref, v_ref, *rest):
        p_refs = rest[:nl]
        g_ref, d_ref, mo_ref, vo_ref = rest[nl:]
        i = pl.program_id(0)
        for l in range(nl):
            @pl.when((i >= l * nbl) & (i < (l + 1) * nbl))
            def _():
                g = p_refs[l][0].astype(F32)
                for dev in range(1, N_DEV):
                    g = g + p_refs[l][dev].astype(F32)
                mn = ADAM_B1 * m_ref[...] + (1.0 - ADAM_B1) * g
                vn = ADAM_B2 * v_ref[...] + (1.0 - ADAM_B2) * (g * g)
                g_ref[...] = g
                mo_ref[...] = mn
                vo_ref[...] = vn
                d_ref[...] = -ADAM_LR * ((mn / c1) / (jnp.sqrt(vn / c2) + ADAM_EPS) + ADAM_WD * w_ref[...])

    row = pl.BlockSpec((tr, C), lambda i: (i, 0))
    piece = lambda l: pl.BlockSpec((N_DEV, tr, C), lambda i: (0, jnp.clip(i - l * nbl, 0, nbl - 1), 0))
    return pl.pallas_call(
        body, name=name, grid=(R // tr,),
        in_specs=[row, row, row] + [piece(l) for l in range(nl)],
        out_specs=[row] * 4,
        out_shape=[jax.ShapeDtypeStruct((R, C), F32)] * 4)(w, m, v, *pieces)


def _my_index():
    return 4 * lax.axis_index("x") + 2 * lax.axis_index("y") + lax.axis_index("c")


def _all_gather(name, mine):
    na = len(mine)

    def body(*refs):
        x_refs, out_refs, token = refs[:na], refs[na:2 * na], refs[2 * na]
        send_sems, recv_sems, local_sems = refs[2 * na + 1:]
        token[...] = jnp.zeros_like(token)
        x, y, c = lax.axis_index("x"), lax.axis_index("y"), lax.axis_index("c")
        me, sibling = (x, y, c), (x, y, 1 - c)
        chips = [(1 - x, y), (x, 1 - y), (1 - x, 1 - y)]

        def slot(a, px, py, pc):
            return out_refs[a].at[4 * px + 2 * py + pc]

        def copy(a, k, block, to, src=None):
            return pltpu.make_async_remote_copy(
                src_ref=slot(a, *block) if src is None else src, dst_ref=slot(a, *block),
                send_sem=send_sems.at[7 * a + k], recv_sem=recv_sems.at[7 * a + k],
                device_id=to, device_id_type=MESH_ID)

        own = [pltpu.make_async_copy(x_refs[a], slot(a, *me), local_sems.at[a]) for a in range(na)]
        for cp in own:
            cp.start()
        first = []
        for a in range(na):
            first.append(copy(a, 0, me, sibling, src=x_refs[a]))
            first += [copy(a, 1 + j, me, (*chip, c), src=x_refs[a]) for j, chip in enumerate(chips)]
        for cp in first:
            cp.start()
        passed = []
        for j, chip in enumerate(chips):
            for a in range(na):
                copy(a, 1 + j, (*chip, c), me).wait_recv()
                fwd = copy(a, 4 + j, (*chip, c), sibling)
                fwd.start()
                passed.append(fwd)
        for a in range(na):
            copy(a, 0, sibling, me).wait_recv()
            for j, chip in enumerate(chips):
                copy(a, 4 + j, (*chip, 1 - c), me).wait_recv()
        for cp in first + passed:
            cp.wait_send()
        for cp in own:
            cp.wait()

    res = pl.pallas_call(
        body, name=name,
        in_specs=[pl.BlockSpec(memory_space=pl.ANY)] * na,
        out_specs=[pl.BlockSpec(memory_space=pl.ANY)] * na + [pl.BlockSpec(memory_space=pltpu.VMEM)],
        out_shape=[jax.ShapeDtypeStruct((N_DEV,) + m.shape, m.dtype) for m in mine]
        + [jax.ShapeDtypeStruct((8, LANE), F32)],
        scratch_shapes=[pltpu.SemaphoreType.DMA((7 * na,)), pltpu.SemaphoreType.DMA((7 * na,)),
                        pltpu.SemaphoreType.DMA((na,))],
    )(*mine)
    return res[:na], res[na]


_HBM_SPEC = pl.BlockSpec(memory_space=pltpu.HBM)
_SEM_SPEC = pl.BlockSpec(memory_space=pltpu.SEMAPHORE)
_SIDE_EFFECT = pltpu.SideEffectType.DATAFLOW_SIDE_EFFECTING


def _peer_of(k):
    x, y, c = lax.axis_index("x"), lax.axis_index("y"), lax.axis_index("c")
    px = 1 - x if k & 4 else x
    py = 1 - y if k & 2 else y
    pc = 1 - c if k & 1 else c
    return (px, py, pc), 4 * px + 2 * py + pc


def _send_start(name, srcs, pieces):
    na = len(srcs)
    land_shapes = [s.shape if pieces else (N_DEV,) + s.shape for s in srcs]
    lands = [lax.empty(shp, s.dtype) for shp, s in zip(land_shapes, srcs)]

    def body(*refs):
        src_refs, land_refs = refs[:na], refs[na:2 * na]
        send_sems, recv_sems, token = refs[2 * na], refs[2 * na + 1], refs[4 * na + 2]
        me = 4 * lax.axis_index("x") + 2 * lax.axis_index("y") + lax.axis_index("c")
        for k in range(1, N_DEV):
            to, pidx = _peer_of(k)
            for a in range(na):
                pltpu.make_async_remote_copy(
                    src_ref=src_refs[a].at[pidx] if pieces else src_refs[a], dst_ref=land_refs[a].at[me],
                    send_sem=send_sems.at[7 * a + k - 1], recv_sem=recv_sems.at[7 * a + k - 1],
                    device_id=to, device_id_type=MESH_ID).start()
        token[...] = jnp.zeros_like(token)

    hbm = lambda arrs: [pltpu.HBM(a.shape, a.dtype) for a in arrs]
    outs = pl.pallas_call(
        body, name=name,
        out_shape=(pltpu.SemaphoreType.DMA((7 * na,)), pltpu.SemaphoreType.DMA((7 * na,)), *hbm(srcs), *hbm(lands),
                   jax.ShapeDtypeStruct((8, LANE), F32)),
        in_specs=[_HBM_SPEC] * (2 * na),
        out_specs=(_SEM_SPEC, _SEM_SPEC, *([_HBM_SPEC] * (2 * na)), pl.BlockSpec(memory_space=pltpu.VMEM)),
        input_output_aliases={i: 2 + i for i in range(2 * na)},
        compiler_params=pltpu.CompilerParams(has_side_effects=_SIDE_EFFECT),
    )(*[pltpu.with_memory_space_constraint(s, pltpu.HBM) for s in srcs],
      *[pltpu.with_memory_space_constraint(l, pltpu.HBM) for l in lands])
    return outs[0], outs[1], outs[2:2 + na], outs[2 + na:2 + 2 * na], outs[-1]


def _send_wait(name, send_sems, recv_sems, srcs, lands, pieces, after):
    na = len(srcs)

    def body(*refs):
        src_refs, land_refs = refs[:na], refs[na:2 * na]
        send_sems, recv_sems = refs[2 * na], refs[2 * na + 1]
        for k in range(1, N_DEV):
            to, pidx = _peer_of(k)
            for a in range(na):
                cp = pltpu.make_async_remote_copy(
                    src_ref=src_refs[a].at[pidx] if pieces else src_refs[a], dst_ref=land_refs[a].at[pidx],
                    send_sem=send_sems.at[7 * a + k - 1], recv_sem=recv_sems.at[7 * a + k - 1],
                    device_id=to, device_id_type=MESH_ID)
                cp.wait_send()
                cp.wait_recv()

    hbm = lambda arrs: [pltpu.HBM(a.shape, a.dtype) for a in arrs]
    outs = pl.pallas_call(
        body, name=name,
        out_shape=tuple(hbm(srcs) + hbm(lands)),
        in_specs=[_HBM_SPEC] * (2 * na) + [_SEM_SPEC, _SEM_SPEC, pl.BlockSpec(memory_space=pl.ANY)],
        out_specs=tuple([_HBM_SPEC] * (2 * na)),
        input_output_aliases={i: i for i in range(2 * na)},
        compiler_params=pltpu.CompilerParams(has_side_effects=_SIDE_EFFECT),
    )(*srcs, *lands, send_sems, recv_sems, after)
    return outs[:na], outs[na:]


def _own_slot(land, own):
    me = 4 * lax.axis_index("x") + 2 * lax.axis_index("y") + lax.axis_index("c")
    mine = lax.broadcasted_iota(jnp.int32, land.shape, 0) == me
    return jnp.where(mine, jnp.broadcast_to(own, land.shape), land)


def _exchange(name, pieces, bcast):
    n_p, n_b = len(pieces), len(bcast)
    na = n_p + n_b

    def body(*refs):
        src_refs, dst_refs = refs[:na], refs[na:2 * na]
        send_sems, recv_sems, local_sems = refs[2 * na:]
        x, y, c = lax.axis_index("x"), lax.axis_index("y"), lax.axis_index("c")
        me = 4 * x + 2 * y + c

        def src(a, slot):
            return src_refs[a].at[slot] if a < n_p else src_refs[a]

        own = [pltpu.make_async_copy(src(a, me), dst_refs[a].at[me], local_sems.at[a]) for a in range(na)]
        for cp in own:
            cp.start()

        def peer_of(k):
            px = 1 - x if k & 4 else x
            py = 1 - y if k & 2 else y
            pc = 1 - c if k & 1 else c
            return (px, py, pc), 4 * px + 2 * py + pc

        def copy(a, k, src_slot, dst_slot, to):
            return pltpu.make_async_remote_copy(
                src_ref=src(a, src_slot), dst_ref=dst_refs[a].at[dst_slot],
                send_sem=send_sems.at[7 * a + k - 1], recv_sem=recv_sems.at[7 * a + k - 1],
                device_id=to, device_id_type=MESH_ID)

        sent = []
        for k in range(1, N_DEV):
            to, pidx = peer_of(k)
            for a in range(na):
                cp = copy(a, k, pidx, me, to)
                cp.start()
                sent.append(cp)
        for k in range(1, N_DEV):
            to, pidx = peer_of(k)
            for a in range(na):
                copy(a, k, me, pidx, to).wait_recv()
        for cp in sent:
            cp.wait_send()
        for cp in own:
            cp.wait()

    arrays = list(pieces) + list(bcast)
    out_shape = [jax.ShapeDtypeStruct(p.shape, p.dtype) for p in pieces]
    out_shape += [jax.ShapeDtypeStruct((N_DEV,) + b.shape, b.dtype) for b in bcast]
    res = pl.pallas_call(
        body, name=name,
        in_specs=[pl.BlockSpec(memory_space=pl.ANY)] * na, out_specs=[pl.BlockSpec(memory_space=pl.ANY)] * na,
        out_shape=out_shape,
        scratch_shapes=[pltpu.SemaphoreType.DMA((7 * na,)), pltpu.SemaphoreType.DMA((7 * na,)),
                        pltpu.SemaphoreType.DMA((na,))],
    )(*arrays)
    return res[:n_p], res[n_p:]


def _pad_rows(flat, cols, row_mult):
    n = flat.shape[-1]
    unit = cols * row_mult
    padded = -(-n // unit) * unit
    pad = [(0, 0)] * (flat.ndim - 1) + [(0, padded - n)]
    return jnp.pad(flat, pad).reshape(flat.shape[:-1] + (padded // cols, cols))


def _perm_cols(w):
    aw = N_GROUPS * GROUP_W
    parts = [w[..., QKV_W:]]
    parts += [w[..., a * aw + g * GROUP_W:a * aw + (g + 1) * GROUP_W] for g in range(N_GROUPS) for a in range(3)]
    return jnp.concatenate(parts, axis=-1)


def _unperm_cols(wp, rest_w):
    qkv = wp[..., rest_w:]
    parts = [qkv[..., g * GROUP_QKV_W + a * GROUP_W:g * GROUP_QKV_W + (a + 1) * GROUP_W]
             for a in range(3) for g in range(N_GROUPS)]
    return jnp.concatenate(parts + [wp[..., :rest_w]], axis=-1)


SHARD_AXIS = dict(SHARDED)


def _layer_shards(wts, li, names, zero):
    out = []
    for n in names:
        w = wts[n][li] if zero is None else wts[n][li] + zero
        out.append(w if n in EXACT_F32 else w.astype(BF16))
    return out


def _assemble_weights(names, segs):
    W = {}
    for n, seg in zip(names, segs):
        ax = SHARD_AXIS[n]
        shp = seg.shape[1:]
        seg = jnp.moveaxis(seg, 0, ax)
        W[n] = seg.reshape(shp[:ax] + (N_DEV * shp[ax],) + shp[ax + 1:])
    if "w_in" in W:
        W["w_in"] = _perm_cols(W["w_in"])
    return W


def _grad_pieces(gr):
    out = []
    for n in gr:
        ax = SHARD_AXIS[n]
        shp = gr[n].shape
        g = gr[n].reshape(shp[:ax] + (N_DEV, shp[ax] // N_DEV) + shp[ax + 1:])
        out.append(jnp.moveaxis(g, ax, 0).astype(BF16))
    return out


def _pack_small(vals):
    flat = jnp.concatenate([vals[n].astype(F32).reshape(-1) for n in REPLICATED])
    return _pad_rows(flat, LANE, 8)


def _layer_fwd(li, x, p_l, w_in, other_weights, G, rope, Bl, S, F):
    T, D = x.shape
    rest_w = 3 * D
    sv = {"x0": x}
    W = {"w_in": w_in}
    hs = _rms_mix_fwd(f"rms_mix_{li}", x, G["g_mix"], S)
    h = hs[0]
    h_g = [h] + [a.reshape(T, D) for a in hs[1:]]
    sv["h_g"] = h_g
    zr = _mm_nn(f"rest_proj_{li}", h, W["w_in"], n_cols=rest_w, tn=1024, tk=_tile(D, 1024))
    sv["zr"] = zr
    qkv_l, o_l, lse_l = [], [], []
    for g in range(N_GROUPS):
        qkv = _qkv_proj(f"qkv_proj_{li}_{g}", h_g[g], W["w_in"], rope[g], S, rest_w + g * GROUP_QKV_W)
        o, lse = _attn_fwd(f"attn_fwd_{li}_{g}", qkv, g, Bl, S)
        qkv_l.append(qkv)
        o_l.append(o)
        lse_l.append(lse)
    sv["qkv"], sv["o"], sv["lse"] = qkv_l, o_l, lse_l
    W.update(other_weights(o_l[-1]))
    x1, attn, pooled, mixed, y_a, y_b, merged = _mix_out_fwd(
        f"mix_out_fwd_{li}", x, zr, o_l, lse_l, W["w_ya"], W["pool_w"], G["pool_scale"], W["w_yb"], W["w_o"], S)
    sv.update(x1=x1, attn=attn, pooled=pooled, mixed=mixed, y_a=y_a, y_b=y_b, merged=merged)
    h2 = _rms_fwd(f"rms_ffn_{li}", x1, G["g_ffn"])
    u = _up_proj(f"up_proj_{li}", h2, W["w_up"], F)
    a = _ffn_act_fwd(f"ffn_act_fwd_{li}", u, W["conv_w"], G["conv_b"], S)
    x2 = _mm_nn(f"down_proj_{li}", a, W["w_down"], add=x1, tk=_tile(F, 1408))
    sv.update(h2=h2, u=u, a=a, x2=x2)
    x3, e, pg, p_bf = _ple_fwd(f"ple_fwd_{li}", x2, p_l, G["g_ple"], W["w_ple_gate"], W["w_ple"])
    sv.update(e=e, pg=pg, p_bf=p_bf)
    return x3, sv, W


EARLY_GRADS = ("w_ple", "w_ple_gate", "w_down", "conv_w", "w_up")
LATE_GRADS = ("w_in", "w_ya", "w_yb", "pool_w", "w_o")


def _layer_bwd(li, dx3, sv, W, G, rope, Bl, S, F, send):
    T, D = dx3.shape
    rest_w = 3 * D
    gr = {}
    dx2, d_e, d_s, h3, dg = _ple_bwd(f"ple_bwd_{li}", dx3, sv["x2"], sv["e"], sv["pg"], G["g_ple"], W["w_ple_gate"])
    gr["g_ple"] = dg[0]
    gr["w_ple"] = _mm_tn(f"w_ple_grad_{li}", sv["p_bf"], d_e)
    gr["w_ple_gate"] = _mm_tn(f"w_ple_gate_grad_{li}", h3, d_s)

    d_a = _mm_nt(f"down_bwd_{li}", dx2, W["w_down"], out_dtype=BF16, tn=1408, tk=_tile(D, 1024))
    gr["w_down"] = _mm_tn(f"w_down_grad_{li}", sv["a"], dx2, tm=1408)
    du, d_cw, d_cb = _ffn_act_bwd(f"ffn_act_bwd_{li}", sv["u"], d_a, W["conv_w"], G["conv_b"], S)
    gr["conv_w"] = d_cw.transpose(1, 0, 2).reshape(3, 2 * F)
    gr["conv_b"] = d_cb.reshape(2 * F)
    tk_f = _tile(F, 1408)
    nkh = F // tk_f
    tm_r = _tile(T, 1024, 8)
    dx1, dg = _mm_nt_rmsbwd(f"up_bwd_{li}", du, (None, tm_r, tk_f), lambda i, j, k: (k // nkh, i, k % nkh),
                            2 * nkh, tk_f, W["w_up"], sv["x1"], G["g_ffn"], dx2)
    gr["g_ffn"] = dg[0]
    gr["w_up"] = _up_w_grad(f"w_up_grad_{li}", sv["h2"], du)
    zero = send("a", {n: gr[n] for n in EARLY_GRADS})

    (dz_rest, d_ya, d_yb, d_pm, do0, do1, do2, c0, c1, c2, dps) = _mix_out_bwd(
        f"mix_out_bwd_{li}", dx1, sv["zr"], sv["y_a"], sv["y_b"], sv["o"], sv["lse"], sv["pooled"],
        W["w_o"], W["w_ya"], W["w_yb"], W["pool_w"], G["pool_scale"] + zero, S)
    gr["pool_scale"] = dps[0]
    gr["w_o"] = _mm_tn(f"w_o_grad_{li}", sv["merged"], dx1)
    gr["w_ya"] = _mm_tn(f"w_ya_grad_{li}", sv["attn"], d_ya)
    gr["w_yb"] = _mm_tn(f"w_yb_grad_{li}", sv["mixed"], d_yb)
    gr["pool_w"] = _pool_w_grad(f"pool_w_grad_{li}", sv["pooled"], d_pm, D // len(POOL_WINDOWS))
    segs = [(dz_rest, 1)]
    for g, (do, cst) in enumerate(((do0, c0), (do1, c1), (do2, c2))):
        dqkv = _attn_bwd(f"attn_bwd_{li}_{g}", sv["qkv"][g], do, sv["lse"][g], cst, rope[g], g, Bl, S)
        segs.append((dqkv, DILATIONS[g]))

    h_rows = [sv["h_g"][0]] + sv["h_g"]
    w_in_parts = [_mm_tn(f"w_in_grad_{li}_{s}", h_rows[s], seg, tn=1536) for s, (seg, _) in enumerate(segs)]
    gr["w_in"] = _unperm_cols(jnp.concatenate(w_in_parts, axis=1), rest_w)
    zero = send("b", {n: gr[n] for n in LATE_GRADS})
    dx0, dg = _in_bwd(f"in_bwd_{li}", segs, W["w_in"], sv["x0"], G["g_mix"] + zero, dx1, S)
    gr["g_mix"] = dg[0]
    return dx0, gr


def kernel(x, p, g_mix, w_in, w_ya, w_yb, pool_w, pool_scale, w_o, g_ffn, w_up, conv_w, conv_b, w_down, g_ple, w_ple, w_ple_gate, g_final, loss_target, m_g_mix, m_w_in, m_w_ya, m_w_yb, m_pool_w, m_pool_scale, m_w_o, m_g_ffn, m_w_up, m_conv_w, m_conv_b, m_w_down, m_g_ple, m_w_ple, m_w_ple_gate, m_g_final, v_g_mix, v_w_in, v_w_ya, v_w_yb, v_pool_w, v_pool_scale, v_w_o, v_g_ffn, v_w_up, v_conv_w, v_conv_b, v_w_down, v_g_ple, v_w_ple, v_w_ple_gate, v_g_final):
    wts = dict(g_mix=g_mix, w_in=w_in, w_ya=w_ya, w_yb=w_yb, pool_w=pool_w, pool_scale=pool_scale, w_o=w_o,
               g_ffn=g_ffn, w_up=w_up, conv_w=conv_w, conv_b=conv_b, w_down=w_down, g_ple=g_ple, w_ple=w_ple,
               w_ple_gate=w_ple_gate, g_final=g_final)
    mom = dict(g_mix=m_g_mix, w_in=m_w_in, w_ya=m_w_ya, w_yb=m_w_yb, pool_w=m_pool_w, pool_scale=m_pool_scale,
               w_o=m_w_o, g_ffn=m_g_ffn, w_up=m_w_up, conv_w=m_conv_w, conv_b=m_conv_b, w_down=m_w_down,
               g_ple=m_g_ple, w_ple=m_w_ple, w_ple_gate=m_w_ple_gate, g_final=m_g_final)
    var = dict(g_mix=v_g_mix, w_in=v_w_in, w_ya=v_w_ya, w_yb=v_w_yb, pool_w=v_pool_w, pool_scale=v_pool_scale,
               w_o=v_w_o, g_ffn=v_g_ffn, w_up=v_w_up, conv_w=v_conv_w, conv_b=v_conv_b, w_down=v_w_down,
               g_ple=v_g_ple, w_ple=v_w_ple, w_ple_gate=v_w_ple_gate, g_final=v_g_final)
    Bl, S, D = x.shape
    depth = g_mix.shape[0]
    F = w_down.shape[1] * N_DEV
    T = Bl * S
    assert S % (BLOCK * DILATIONS[-1]) == 0 and D % GROUP_W == 0 and F % LANE == 0
    rope = [tuple(t if d == 1 else t.reshape(S // d, d, HEAD_DIM).transpose(1, 0, 2).reshape(S, HEAD_DIM)
                  for t in _rope_tables(S)) for d in DILATIONS]

    first, others = ("w_in",), tuple(n for n, _ in SHARDED if n != "w_in")
    got_in, tok = _all_gather("gather_w0_in", _layer_shards(wts, 0, first, None))
    gathers = {}
    for li in range(depth):
        names = others if li == 0 else first + others
        *gathers[li], tok = _send_start(f"gather_w{li}_start", _layer_shards(wts, li, names, tok[0, 0]), False)

    def gathered(li, names, after):
        shards, lands = _send_wait(f"gather_w{li}_wait", *gathers[li], False, after)
        return _assemble_weights(names, [_own_slot(l, s) for l, s in zip(lands, shards)])

    xs = x.reshape(T, D)
    saved = []
    for li in range(depth):
        G = {n: wts[n][li] for n in REPLICATED if n != "g_final"}
        if li == 0:
            G["g_mix"] = G["g_mix"] + tok[0, 0]
            w_in_full = _assemble_weights(first, got_in)["w_in"]
            rest_of = lambda after: gathered(0, others, after)
        else:
            W_all = gathered(li, first + others, xs)
            w_in_full = W_all["w_in"]
            rest_of = lambda after, W_all=W_all: W_all
        xs, sv, W = _layer_fwd(li, xs, p[li].reshape(T, -1), w_in_full, rest_of, G, rope, Bl, S, F)
        saved.append((sv, W, G))

    dx, loss_row, dg_final = _loss_bwd("loss_bwd", xs, loss_target.reshape(T, D), g_final)
    layer_grads = [None] * depth
    sends = []
    zero = [None]
    for li in reversed(range(depth)):
        sv, W, G = saved[li]
        if zero[0] is not None:
            G = dict(G, g_ple=G["g_ple"] + zero[0])

        def send(tag, group, li=li):
            *handles, tok = _send_start(f"exchange_g{li}{tag}_start", _grad_pieces(group), True)
            sends.append((li, tag, tuple(group), handles))
            zero[0] = tok[0, 0]
            return zero[0]

        dx, layer_grads[li] = _layer_bwd(li, dx, sv, W, G, rope, Bl, S, F, send)

    recv = {}
    for li, tag, names, handles in sends:
        pcs, lands = _send_wait(f"exchange_g{li}{tag}_wait", *handles, True, dx)
        for n, l, s in zip(names, lands, pcs):
            recv[(li, n)] = _own_slot(l, s)
    grads = {n: jnp.stack([layer_grads[li][n] for li in range(depth)]) for n in REPLICATED if n != "g_final"}
    grads["g_final"] = dg_final[0]
    _, (small_all,) = _exchange("exchange_small", [], [_pack_small(grads)])

    out_g, out_d, out_m, out_v = {}, {}, {}, {}
    for n, _ in SHARDED:
        shp = wts[n].shape
        two_d = (math.prod(shp[:-1]), shp[-1])
        pieces = [recv[(li, n)].reshape(N_DEV, two_d[0] // depth, two_d[1]) for li in range(depth)]
        res = _adamw(f"adamw_{n}", wts[n].reshape(two_d), mom[n].reshape(two_d), var[n].reshape(two_d), pieces)
        out_g[n], out_d[n], out_m[n], out_v[n] = [r.reshape(shp) for r in res]
    res = _adamw("adamw_replicated", _pack_small(wts), _pack_small(mom), _pack_small(var), [small_all])
    off = 0
    for n in REPLICATED:
        shp = wts[n].shape
        size = math.prod(shp)
        for dst, r in zip((out_g, out_d, out_m, out_v), res):
            dst[n] = r.reshape(-1)[off:off + size].reshape(shp)
        off += size

    loss = lax.psum(loss_row[0, 0], MESH_AXES)
    outs = [loss, dx.reshape(Bl, S, D)]
    for dct in (out_g, out_d, out_m, out_v):
        outs += [dct[n] for n in WEIGHT_ORDER]
    return tuple(outs)
```

```python
import math

import jax
import jax.numpy as jnp
from jax import lax
from jax.experimental import pallas as pl
from jax.experimental.pallas import tpu as pltpu

F32 = jnp.float32
BF16 = jnp.bfloat16

N_DEV = 8
HEAD_DIM = 128
HEADS = 4
GROUP_W = HEADS * HEAD_DIM
DILATIONS = (1, 4, 16)
N_GROUPS = len(DILATIONS)
QKV_W = 3 * N_GROUPS * GROUP_W
GROUP_QKV_W = 3 * GROUP_W
BLOCK = 128
ROPE_DIM = HEAD_DIM // 4
ROPE_HALF = ROPE_DIM // 2
ROPE_THETA = 500000.0
NEG_INF = -1e30
POOL_WINDOWS = (2, 4, 8, 16)
POOL_HALO = 16
CONV_HALO = 16
RMS_EPS = 1e-6
ADAM_LR = 0.001
ADAM_B1 = 0.9
ADAM_B2 = 0.999
ADAM_EPS = 1e-08
ADAM_WD = 0.01
ADAM_STEP = 10
LANE = 128
PACK_COLS = 1024
PACK_ROWS = 16
MESH_ID = pl.DeviceIdType.MESH
MESH_AXES = ("x", "y", "c")

NT_DIMS = (((1,), (1,)), ((), ()))
TN_DIMS = (((0,), (0,)), ((), ()))
NN_DIMS = (((1,), (0,)), ((), ()))

SHARDED = (("w_in", 1), ("w_ya", 1), ("w_yb", 0), ("pool_w", 1), ("w_o", 0), ("w_up", 1), ("conv_w", 1),
           ("w_down", 0), ("w_ple", 1), ("w_ple_gate", 0))
EXACT_F32 = ("conv_w",)
REPLICATED = ("g_mix", "pool_scale", "g_ffn", "conv_b", "g_ple", "g_final")
WEIGHT_ORDER = ("g_mix", "w_in", "w_ya", "w_yb", "pool_w", "pool_scale", "w_o", "g_ffn", "w_up", "conv_w", "conv_b",
                "w_down", "g_ple", "w_ple", "w_ple_gate", "g_final")


def _tile(n, pref, mult=LANE):
    if n <= pref:
        return n
    t = (pref // mult) * mult
    while t >= mult:
        if n % t == 0:
            return t
        t -= mult
    return n


def _sigmoid(x):
    return 1.0 / (1.0 + jnp.exp(-x))


def _dot(a, b, dims=NN_DIMS):
    return lax.dot_general(a.astype(BF16), b.astype(BF16), dims, preferred_element_type=F32)


def _rstd(x):
    return lax.rsqrt(jnp.mean(x * x, axis=-1, keepdims=True) + RMS_EPS)


def _rms_bwd(x, g, dh):
    r = _rstd(x)
    u = dh * g
    dx = r * u - x * (r * r * r) * jnp.mean(x * u, axis=-1, keepdims=True)
    return dx, dh * x * r


def _full(a):
    return pl.BlockSpec(a.shape, lambda *_: (0,) * a.ndim)


def _mm(name, a, b, *, grid, a_block, a_map, b_block, b_map, dims, acc_shape, outs, extras=(), epi=None):
    nk = grid[2]
    n_ex = len(extras)
    n_out = len(outs)

    def body(*refs):
        a_ref, b_ref = refs[0], refs[1]
        ex = refs[2:2 + n_ex]
        o = refs[2 + n_ex:2 + n_ex + n_out]
        acc = refs[2 + n_ex + n_out]
        i, j, k = pl.program_id(0), pl.program_id(1), pl.program_id(2)

        @pl.when(k == 0)
        def _():
            acc[...] = jnp.zeros_like(acc)

        acc[...] += _dot(a_ref[...], b_ref[...], dims)

        @pl.when(k == nk - 1)
        def _():
            if epi is None:
                o[0][...] = acc[...].astype(o[0].dtype)
            else:
                epi(acc[...], ex, o, i, j)

    in_specs = [pl.BlockSpec(a_block, a_map), pl.BlockSpec(b_block, b_map)]
    in_specs += [pl.BlockSpec(blk, mp) for (_, blk, mp) in extras]
    out_specs = [pl.BlockSpec(blk, mp) for (_, _, blk, mp) in outs]
    out_shape = [jax.ShapeDtypeStruct(s, d) for (s, d, _, _) in outs]
    return pl.pallas_call(
        body, name=name, grid=grid, in_specs=in_specs, out_specs=out_specs, out_shape=out_shape,
        scratch_shapes=[pltpu.VMEM(acc_shape, F32)],
    )(a, b, *[e[0] for e in extras])


def _mm_nn(name, a, b, *, out_dtype=F32, tm=1024, tn=1024, tk=512, b_col_off=0, n_cols=None, add=None):
    M, K = a.shape
    N = n_cols if n_cols is not None else b.shape[1]
    tm, tn, tk = _tile(M, tm, 8), _tile(N, tn), _tile(K, tk)
    assert b_col_off % tn == 0
    joff = b_col_off // tn
    extras, epi = (), None
    if add is not None:
        extras = ((add, (tm, tn), lambda i, j, k: (i, j)),)

        def epi(acc, ex, o, i, j):
            o[0][...] = (acc + ex[0][...]).astype(o[0].dtype)

    return _mm(name, a, b, grid=(M // tm, N // tn, K // tk),
               a_block=(tm, tk), a_map=lambda i, j, k: (i, k),
               b_block=(tk, tn), b_map=lambda i, j, k: (k, j + joff),
               dims=NN_DIMS, acc_shape=(tm, tn),
               outs=[((M, N), out_dtype, (tm, tn), lambda i, j, k: (i, j))], extras=extras, epi=epi)[0]


def _mm_nt(name, a, b, *, out_dtype=F32, tm=1024, tn=1024, tk=512):
    M, K = a.shape
    N = b.shape[0]
    tm, tn, tk = _tile(M, tm, 8), _tile(N, tn), _tile(K, tk)
    return _mm(name, a, b, grid=(M // tm, N // tn, K // tk),
               a_block=(tm, tk), a_map=lambda i, j, k: (i, k),
               b_block=(tn, tk), b_map=lambda i, j, k: (j, k),
               dims=NT_DIMS, acc_shape=(tm, tn),
               outs=[((M, N), out_dtype, (tm, tn), lambda i, j, k: (i, j))])[0]


def _mm_tn(name, a, b, *, tm=1024, tn=1024, tk=1024):
    K, M = a.shape
    N = b.shape[1]
    tm, tn, tk = _tile(M, tm), _tile(N, tn), _tile(K, tk, 8)
    return _mm(name, a, b, grid=(M // tm, N // tn, K // tk),
               a_block=(tk, tm), a_map=lambda i, j, k: (k, i),
               b_block=(tk, tn), b_map=lambda i, j, k: (k, j),
               dims=TN_DIMS, acc_shape=(tm, tn),
               outs=[((M, N), F32, (tm, tn), lambda i, j, k: (i, j))])[0]


def _mm_nt_rmsbwd(name, a, a_block, a_map, nk, tk, w, x, g, dres):
    T, D = x.shape
    tm = a_block[-2]

    def epi(acc, ex, o, i, j):
        @pl.when(i == 0)
        def _():
            o[1][...] = jnp.zeros_like(o[1])

        dx, dgr = _rms_bwd(ex[0][...], ex[1][...], acc)
        o[0][...] = ex[2][...] + dx
        o[1][...] += jnp.sum(dgr, axis=0, keepdims=True)

    row = lambda i, j, k: (i, 0)
    vec = lambda i, j, k: (0, 0)
    return _mm(name, a, w, grid=(T // tm, 1, nk),
               a_block=a_block, a_map=a_map,
               b_block=(D, tk), b_map=lambda i, j, k: (0, k),
               dims=NT_DIMS, acc_shape=(tm, D),
               outs=[((T, D), F32, (tm, D), row), ((1, D), F32, (1, D), vec)],
               extras=[(x, (tm, D), row), (g.reshape(1, D), (1, D), vec), (dres, (tm, D), row)], epi=epi)


def _in_bwd(name, segs, w_perm, x, g, dres, S):
    T, D = x.shape
    tm = _tile(S, 512, 256)
    nst = S // tm
    tk = GROUP_QKV_W
    steps = [a.shape[1] // tk for a, _ in segs]
    starts = [sum(steps[:s]) for s in range(len(segs))]
    nk = sum(steps)
    ns = len(segs)
    cols = _chunks(D)
    assert all(a.shape[1] % tk == 0 for a, _ in segs) and nk * tk == w_perm.shape[1]

    def body(*refs):
        a_refs = refs[:ns]
        w_ref, x_ref, g_ref, dres_ref, dx_ref, dg_ref, acc, acc_tok = refs[ns:]
        i, k = pl.program_id(0), pl.program_id(1)

        @pl.when(k == 0)
        def _():
            acc[...] = jnp.zeros_like(acc)
            acc_tok[...] = jnp.zeros_like(acc_tok)

        for s in range(ns):
            d = segs[s][1]

            @pl.when((k >= starts[s]) & (k < starts[s] + steps[s]))
            def _():
                if d == 1:
                    acc_tok[...] += _dot(a_refs[s][...], w_ref[...], NT_DIMS)
                else:
                    prod = _dot(a_refs[s][...].reshape(tm, tk), w_ref[...], NT_DIMS)
                    q = tm // d
                    for c, cs in enumerate(cols):
                        for r in range(d):
                            acc[c, pl.ds(r, q, stride=d), :] += prod[r * q:(r + 1) * q, cs]

        @pl.when(k == nk - 1)
        def _():
            @pl.when(i == 0)
            def _():
                dg_ref[...] = jnp.zeros_like(dg_ref)

            dh = acc_tok[...] + jnp.concatenate([acc[c] for c in range(len(cols))], axis=1)
            dx, dgr = _rms_bwd(x_ref[...], g_ref[...], dh)
            dx_ref[...] = dres_ref[...] + dx
            dg_ref[...] += jnp.sum(dgr, axis=0, keepdims=True)

    def seg_spec(s):
        kmap = lambda k: jnp.clip(k - starts[s], 0, steps[s] - 1)
        d = segs[s][1]
        if d == 1:
            return pl.BlockSpec((tm, tk), lambda i, k: (i, kmap(k)))
        return pl.BlockSpec((None, d, tm // d, tk), lambda i, k: (i // nst, 0, i % nst, kmap(k)))

    views = [a if d == 1 else a.reshape(T // S, d, S // d, a.shape[1]) for a, d in segs]
    row = pl.BlockSpec((tm, D), lambda i, k: (i, 0))
    vec = pl.BlockSpec((1, D), lambda i, k: (0, 0))
    return pl.pallas_call(
        body, name=name, grid=(T // tm, nk),
        in_specs=[seg_spec(s) for s in range(ns)] + [pl.BlockSpec((D, tk), lambda i, k: (0, k)), row, vec, row],
        out_specs=[row, vec],
        out_shape=[jax.ShapeDtypeStruct((T, D), F32), jax.ShapeDtypeStruct((1, D), F32)],
        scratch_shapes=[pltpu.VMEM((D // LANE, tm, LANE), F32), pltpu.VMEM((tm, D), F32)],
    )(*views, w_perm, x, g.reshape(1, D), dres)


def _rope_tables(S):
    pos = jnp.arange(S, dtype=F32)
    inv_freq = jnp.exp(jnp.arange(0, ROPE_DIM, 2, dtype=F32) * (-math.log(ROPE_THETA) / ROPE_DIM))
    ang = pos[:, None] * inv_freq[None, :]
    cos, sin = jnp.cos(ang), jnp.sin(ang)
    ones = jnp.ones((S, HEAD_DIM - ROPE_DIM), F32)
    zeros_h = jnp.zeros((S, ROPE_HALF), F32)
    zeros_r = jnp.zeros((S, HEAD_DIM - ROPE_DIM), F32)
    c = jnp.concatenate([cos, cos, ones], axis=1)
    sa = jnp.concatenate([-sin, zeros_h, zeros_r], axis=1)
    sb = jnp.concatenate([zeros_h, sin, zeros_r], axis=1)
    return c, sa, sb


def _rope(t, c, sa, sb):
    return t * c + pltpu.roll(t, HEAD_DIM - ROPE_HALF, 1) * sa + pltpu.roll(t, ROPE_HALF, 1) * sb


def _rms_fwd(name, x, g):
    T, D = x.shape
    tm = _tile(T, 512, 8)

    def body(x_ref, g_ref, h_ref):
        xv = x_ref[...]
        h_ref[...] = (xv * _rstd(xv) * g_ref[...]).astype(BF16)

    return pl.pallas_call(
        body, name=name, grid=(T // tm,),
        in_specs=[pl.BlockSpec((tm, D), lambda i: (i, 0)), pl.BlockSpec((1, D), lambda i: (0, 0))],
        out_specs=pl.BlockSpec((tm, D), lambda i: (i, 0)),
        out_shape=jax.ShapeDtypeStruct((T, D), BF16))(x, g.reshape(1, D))


def _chunks(width):
    return [slice(c * LANE, (c + 1) * LANE) for c in range(width // LANE)]


def _store_residue_major(val, sc, out_ref, d, dtype):
    rows = val.shape[0]
    for c, cs in enumerate(_chunks(val.shape[1])):
        sc[c] = val[:, cs]
    for r in range(d):
        for c, cs in enumerate(_chunks(val.shape[1])):
            out_ref[r, :, cs] = sc[c, pl.ds(r, rows // d, stride=d), :].astype(dtype)


def _load_token_order(blk_ref, sc, d):
    _, q, width = blk_ref.shape
    for r in range(d):
        for c, cs in enumerate(_chunks(width)):
            sc[c, pl.ds(r, q, stride=d), :] = blk_ref[r, :, cs]
    return jnp.concatenate([sc[c] for c in range(width // LANE)], axis=1)


def _residue_major_spec(d, q, width, nst):
    return pl.BlockSpec((None, d, q, width), lambda i, *_: (i // nst, 0, i % nst, 0))


def _rms_mix_fwd(name, x, g, S):
    T, D = x.shape
    Bl = T // S
    tm = _tile(S, 512, 256)
    nst = S // tm
    dils = [d for d in DILATIONS if d > 1]

    def body(x_ref, g_ref, h_ref, *rest):
        rm_refs, sc = rest[:len(dils)], rest[len(dils)]
        xv = x_ref[...]
        hv = xv * _rstd(xv) * g_ref[...]
        h_ref[...] = hv.astype(BF16)
        for d, o_ref in zip(dils, rm_refs):
            _store_residue_major(hv, sc, o_ref, d, BF16)

    row = pl.BlockSpec((tm, D), lambda i: (i, 0))
    return pl.pallas_call(
        body, name=name, grid=(T // tm,),
        in_specs=[row, pl.BlockSpec((1, D), lambda i: (0, 0))],
        out_specs=[row] + [_residue_major_spec(d, tm // d, D, nst) for d in dils],
        out_shape=[jax.ShapeDtypeStruct((T, D), BF16)]
        + [jax.ShapeDtypeStruct((Bl, d, S // d, D), BF16) for d in dils],
        scratch_shapes=[pltpu.VMEM((D // LANE, tm, LANE), F32)])(x, g.reshape(1, D))


def _qkv_proj(name, h, w_perm, rope, S, col_off):
    T, D = h.shape
    tm = _tile(S, 512, 8)
    c_t, sa_t, sb_t = rope
    n_seq_tiles = S // tm
    jblk = col_off // GROUP_QKV_W

    def body(h_ref, w_ref, c_ref, sa_ref, sb_ref, o_ref):
        hv = h_ref[...]
        c, sa, sb = c_ref[...], sa_ref[...], sb_ref[...]
        for chunk in range(3):
            cols = slice(chunk * GROUP_W, (chunk + 1) * GROUP_W)
            acc = _dot(hv, w_ref[:, cols])
            if chunk < 2:
                for hh in range(HEADS):
                    sl = slice(hh * HEAD_DIM, (hh + 1) * HEAD_DIM)
                    o_ref[:, chunk * GROUP_W + hh * HEAD_DIM:chunk * GROUP_W + (hh + 1) * HEAD_DIM] = (
                        _rope(acc[:, sl], c, sa, sb).astype(BF16))
            else:
                o_ref[:, cols] = acc.astype(BF16)

    tab = pl.BlockSpec((tm, HEAD_DIM), lambda i: (i % n_seq_tiles, 0))
    return pl.pallas_call(
        body, name=name, grid=(T // tm,),
        in_specs=[pl.BlockSpec((tm, D), lambda i: (i, 0)), pl.BlockSpec((D, GROUP_QKV_W), lambda i: (0, jblk)),
                  tab, tab, tab],
        out_specs=pl.BlockSpec((tm, GROUP_QKV_W), lambda i: (i, 0)),
        out_shape=jax.ShapeDtypeStruct((T, GROUP_QKV_W), BF16))(h, w_perm, c_t, sa_t, sb_t)


ATTN_BLOCKS_PER_STEP = 4


def _attn_mask(rows, lead, has_prev):
    qi = lax.broadcasted_iota(jnp.int32, (rows, lead + rows), 0)
    ki = lax.broadcasted_iota(jnp.int32, (rows, lead + rows), 1)
    diff = lead + qi - ki
    band = (diff >= 0) & (diff <= BLOCK)
    return band & (has_prev | (ki >= lead)) if lead else band


def _attn_chunking(nb):
    cb = min(ATTN_BLOCKS_PER_STEP, nb)
    return cb, ATTN_BLOCKS_PER_STEP // cb


def _attn_fwd(name, qkv, g, Bl, S):
    d = DILATIONS[g]
    L = S // d
    nb = L // BLOCK
    cb, cs = _attn_chunking(nb)
    qv = qkv.reshape(Bl * d, L, GROUP_QKV_W)
    scale = HEAD_DIM ** -0.5

    def body(q_ref, kc_ref, vc_ref, kp_ref, vp_ref, o_ref, l_ref):
        n = pl.program_id(1)
        for si in range(cs):
            for bi in range(cb):
                rows = slice(bi * BLOCK, (bi + 1) * BLOCK)
                before = slice((bi - 1) * BLOCK, bi * BLOCK)
                valid = _attn_mask(BLOCK, BLOCK, n > 0 if bi == 0 else True)
                for hh in range(HEADS):
                    sl = slice(hh * HEAD_DIM, (hh + 1) * HEAD_DIM)
                    kp = kp_ref[si, :, sl] if bi == 0 else kc_ref[si, before, sl]
                    vp = vp_ref[si, :, sl] if bi == 0 else vc_ref[si, before, sl]
                    kk = jnp.concatenate([kp, kc_ref[si, rows, sl]], axis=0)
                    vv = jnp.concatenate([vp, vc_ref[si, rows, sl]], axis=0)
                    s = jnp.where(valid, _dot(q_ref[si, rows, sl], kk, NT_DIMS) * scale, NEG_INF)
                    m = jnp.max(s, axis=-1, keepdims=True)
                    p = jnp.exp(s - m)
                    l = jnp.sum(p, axis=-1, keepdims=True)
                    o_ref[si, rows, sl] = _dot(p, vv) / l
                    l_ref[si, rows, sl] = jnp.broadcast_to(m + jnp.log(l), (BLOCK, HEAD_DIM))

    main = (cs, cb * BLOCK, GROUP_W)
    edge = (cs, BLOCK, GROUP_W)
    cur = lambda off: pl.BlockSpec(main, lambda s, n: (s, n, off))
    prev = lambda off: pl.BlockSpec(edge, lambda s, n: (s, jnp.maximum(n * cb - 1, 0), off))
    out = pl.BlockSpec(main, lambda s, n: (s, n, 0))
    return pl.pallas_call(
        body, name=name, grid=(Bl * d // cs, nb // cb),
        in_specs=[cur(0), cur(1), cur(2), prev(1), prev(2)],
        out_specs=[out, out],
        out_shape=[jax.ShapeDtypeStruct((Bl * d, L, GROUP_W), F32)] * 2)(qv, qv, qv, qv, qv)


def _merge_weights(l0, l1, l2):
    mx = jnp.maximum(jnp.maximum(l0, l1), l2)
    e0, e1, e2 = jnp.exp(l0 - mx), jnp.exp(l1 - mx), jnp.exp(l2 - mx)
    inv = 1.0 / (e0 + e1 + e2)
    return e0 * inv, e1 * inv, e2 * inv


def _group_specs(tm, nst, S):
    specs = []
    for d in DILATIONS:
        specs.append(pl.BlockSpec((tm, GROUP_W), lambda i: (i, 0)) if d == 1
                     else _residue_major_spec(d, tm // d, GROUP_W, nst))

    def views(arrs):
        out = []
        for d, a in zip(DILATIONS, arrs):
            out.append(a.reshape(-1, GROUP_W) if d == 1 else a.reshape(-1, d, S // d, GROUP_W))
        return out

    return specs, views


def _group_scratch(tm, per_group):
    n = per_group * sum(1 for d in DILATIONS if d > 1)
    return [pltpu.VMEM((GROUP_W // LANE, tm, LANE), F32) for _ in range(n)]


def _group_values(o_refs, l_refs, scs):
    scs = list(scs)
    ov, lv = [], []
    for d, o_ref, l_ref in zip(DILATIONS, o_refs, l_refs):
        if d == 1:
            ov.append(o_ref[...])
            lv.append(l_ref[...])
        else:
            ov.append(_load_token_order(o_ref, scs.pop(), d))
            lv.append(_load_token_order(l_ref, scs.pop(), d))
    return ov, lv


def _pool_inv_count(tseq, w):
    return 1.0 / jnp.minimum(tseq + 1, w).astype(F32)


def _mix_out_fwd(name, x, zr, o_l, lse_l, w_ya, pool_w, pool_scale, w_yb, w_o, S):
    T, D = x.shape
    gw = D // len(POOL_WINDOWS)
    tm = _tile(S, 256, POOL_HALO)
    nst = S // tm
    hpt = tm // POOL_HALO

    def body(x_ref, u_ref, uh_ref, ga_ref, gb_ref, o0, o1, o2, l0, l1, l2, wya_ref, pw_ref, ps_ref, wyb_ref, wo_ref,
             x1_ref, attn_ref, pooled_ref, mixed_ref, ya_ref, yb_ref, merged_ref, *scs):
        it = pl.program_id(0) % nst
        ov, lv = _group_values((o0, o1, o2), (l0, l1, l2), scs)
        w0, w1, w2 = _merge_weights(*lv)
        attn = w0 * ov[0] + w1 * ov[1] + w2 * ov[2]
        attn_ref[...] = attn.astype(BF16)
        y_a = _dot(attn, wya_ref[...])

        u = u_ref[...]
        halo = uh_ref[...] * jnp.where(it == 0, 0.0, 1.0)
        ext = jnp.concatenate([halo, u], axis=0)
        tseq = it * tm + lax.broadcasted_iota(jnp.int32, (tm, 1), 0)
        pm_parts = []
        for gi, w in enumerate(POOL_WINDOWS):
            cs = slice(gi * gw, (gi + 1) * gw)
            s = ext[:, cs]
            step = 1
            while step < w:
                s = s + pltpu.roll(s, step, 0)
                step *= 2
            pooled_g = (s[POOL_HALO:, :] * _pool_inv_count(tseq, w) - u[:, cs]).astype(BF16)
            pooled_ref[:, cs] = pooled_g
            pm_parts.append(_dot(pooled_g, pw_ref[gi]))
        mixed = (jnp.concatenate(pm_parts, axis=1) * ps_ref[...]).astype(BF16)
        mixed_ref[...] = mixed
        y_b = _dot(mixed, wyb_ref[...])
        merged = (_sigmoid(ga_ref[...]) * y_a + _sigmoid(gb_ref[...]) * y_b).astype(BF16)
        ya_ref[...] = y_a.astype(BF16)
        yb_ref[...] = y_b.astype(BF16)
        merged_ref[...] = merged
        x1_ref[...] = x_ref[...] + _dot(merged, wo_ref[...])

    row = lambda c: pl.BlockSpec((tm, D), lambda i: (i, c))
    row512 = pl.BlockSpec((tm, GROUP_W), lambda i: (i, 0))
    ps = pool_scale.reshape(1, D)
    halo_spec = pl.BlockSpec((POOL_HALO, D), lambda i: (jnp.maximum(i * hpt - 1, 0), 0))
    grp_specs, grp_views = _group_specs(tm, nst, S)
    return pl.pallas_call(
        body, name=name, grid=(T // tm,),
        in_specs=[row(0), row(0), halo_spec, row(1), row(2)] + grp_specs * 2
        + [_full(w_ya), _full(pool_w), _full(ps), _full(w_yb), _full(w_o)],
        out_specs=[row(0), row512, row(0), row(0), row(0), row(0), row(0)],
        out_shape=[jax.ShapeDtypeStruct((T, D), F32), jax.ShapeDtypeStruct((T, GROUP_W), BF16),
                   jax.ShapeDtypeStruct((T, D), BF16), jax.ShapeDtypeStruct((T, D), BF16),
                   jax.ShapeDtypeStruct((T, D), BF16), jax.ShapeDtypeStruct((T, D), BF16),
                   jax.ShapeDtypeStruct((T, D), BF16)],
        scratch_shapes=_group_scratch(tm, 2),
    )(x, zr, zr, zr, zr, *grp_views(o_l), *grp_views(lse_l), w_ya, pool_w, ps, w_yb, w_o)


def _up_proj(name, h2, w_up, F):
    T, D = h2.shape
    tm, tn, tk = _tile(T, 1024, 8), _tile(F, 1408), _tile(D, 1024)
    njh = F // tn
    return _mm(name, h2, w_up, grid=(T // tm, 2 * njh, D // tk),
               a_block=(tm, tk), a_map=lambda i, j, k: (i, k),
               b_block=(tk, tn), b_map=lambda i, j, k: (k, j),
               dims=NN_DIMS, acc_shape=(tm, tn),
               outs=[((2, T, F), BF16, (None, tm, tn), lambda i, j, k: (j // njh, i, j % njh))])[0]


def _conv_y(ext, w_ref, b_ref):
    return (b_ref[...] + w_ref[2:3, :] * ext + w_ref[1:2, :] * pltpu.roll(ext, 1, 0)
            + w_ref[0:1, :] * pltpu.roll(ext, 2, 0))


def _conv_params(conv_w, conv_b, F):
    cw = conv_w.reshape(3, 2, F).transpose(1, 0, 2)
    return cw, conv_b.reshape(2, 1, F)


def _ffn_act_fwd(name, u, conv_w, conv_b, S):
    _, T, F = u.shape
    tm = _tile(S, 512, CONV_HALO)
    tf = _tile(F, 1408)
    nst = S // tm
    hpt = tm // CONV_HALO
    cw, cb = _conv_params(conv_w, conv_b, F)

    def body(ug_ref, uv_ref, hg_ref, hv_ref, wg_ref, wv_ref, bg_ref, bv_ref, a_ref):
        keep = jnp.where(pl.program_id(0) % nst == 0, 0.0, 1.0)
        ext = lambda h_ref, u_ref: jnp.concatenate([h_ref[...].astype(F32) * keep, u_ref[...].astype(F32)], axis=0)
        yg = _conv_y(ext(hg_ref, ug_ref), wg_ref, bg_ref)[CONV_HALO:, :]
        yv = _conv_y(ext(hv_ref, uv_ref), wv_ref, bv_ref)[CONV_HALO:, :]
        a_ref[...] = (yg * _sigmoid(yg) * yv).astype(BF16)

    main = lambda h: pl.BlockSpec((None, tm, tf), lambda i, j: (h, i, j))
    halo = lambda h: pl.BlockSpec((None, CONV_HALO, tf), lambda i, j: (h, jnp.maximum(i * hpt - 1, 0), j))
    wsp = lambda h: pl.BlockSpec((None, 3, tf), lambda i, j: (h, 0, j))
    bsp = lambda h: pl.BlockSpec((None, 1, tf), lambda i, j: (h, 0, j))
    return pl.pallas_call(
        body, name=name, grid=(T // tm, F // tf),
        in_specs=[main(0), main(1), halo(0), halo(1), wsp(0), wsp(1), bsp(0), bsp(1)],
        out_specs=pl.BlockSpec((tm, tf), lambda i, j: (i, j)),
        out_shape=jax.ShapeDtypeStruct((T, F), BF16))(u, u, u, u, cw, cw, cb, cb)


def _ple_fwd(name, x2, p, g_ple, w_gate, w_ple):
    T, D = x2.shape
    P = p.shape[1]
    tm = _tile(T, 512, 8)

    def body(x_ref, p_ref, g_ref, wg_ref, wp_ref, x3_ref, e_ref, pg_ref, pbf_ref):
        xv = x_ref[...]
        h3 = xv * _rstd(xv) * g_ref[...]
        pg = _sigmoid(_dot(h3, wg_ref[...]))
        pb = p_ref[...].astype(BF16)
        e = _dot(pb, wp_ref[...])
        x3_ref[...] = xv + e * pg
        e_ref[...] = e.astype(BF16)
        pg_ref[...] = pg.astype(BF16)
        pbf_ref[...] = pb

    row = pl.BlockSpec((tm, D), lambda i: (i, 0))
    prow = pl.BlockSpec((tm, P), lambda i: (i, 0))
    g2 = g_ple.reshape(1, D)
    return pl.pallas_call(
        body, name=name, grid=(T // tm,),
        in_specs=[row, prow, _full(g2), _full(w_gate), _full(w_ple)],
        out_specs=[row, row, row, prow],
        out_shape=[jax.ShapeDtypeStruct((T, D), F32)] + [jax.ShapeDtypeStruct((T, D), BF16)] * 2
        + [jax.ShapeDtypeStruct((T, P), BF16)],
    )(x2, p, g2, w_gate, w_ple)


def _loss_bwd(name, xf, target, g_final):
    T, D = xf.shape
    tm = _tile(T, 512, 8)
    nt = T // tm

    def body(x_ref, t_ref, g_ref, dx_ref, loss_ref, dg_ref, lacc):
        i = pl.program_id(0)

        @pl.when(i == 0)
        def _():
            lacc[...] = jnp.zeros_like(lacc)
            dg_ref[...] = jnp.zeros_like(dg_ref)
            loss_ref[...] = jnp.zeros_like(loss_ref)

        xv = x_ref[...]
        g = g_ref[...]
        diff = xv * _rstd(xv) * g - t_ref[...]
        lacc[...] += jnp.sum(diff * diff, axis=0, keepdims=True)
        dx, dgr = _rms_bwd(xv, g, diff * (1.0 / D))
        dx_ref[...] = dx
        dg_ref[...] += jnp.sum(dgr, axis=0, keepdims=True)

        @pl.when(i == nt - 1)
        def _():
            tot = jnp.sum(lacc[...], axis=-1, keepdims=True) * (0.5 / D)
            loss_ref[...] = jnp.broadcast_to(tot, (1, LANE))

    row = pl.BlockSpec((tm, D), lambda i: (i, 0))
    vec = pl.BlockSpec((1, D), lambda i: (0, 0))
    return pl.pallas_call(
        body, name=name, grid=(nt,),
        in_specs=[row, row, vec],
        out_specs=[row, pl.BlockSpec((1, LANE), lambda i: (0, 0)), vec],
        out_shape=[jax.ShapeDtypeStruct((T, D), F32), jax.ShapeDtypeStruct((1, LANE), F32),
                   jax.ShapeDtypeStruct((1, D), F32)],
        scratch_shapes=[pltpu.VMEM((1, D), F32)])(xf, target, g_final.reshape(1, D))


def _ple_bwd(name, dx3, x2, e, pg, g_ple, w_gate):
    T, D = x2.shape
    tm = _tile(T, 512, 8)

    def body(dx3_ref, x_ref, e_ref, pg_ref, g_ref, wg_ref, dx2_ref, de_ref, ds_ref, h3_ref, dg_ref):
        @pl.when(pl.program_id(0) == 0)
        def _():
            dg_ref[...] = jnp.zeros_like(dg_ref)

        dx3v, xv, pgv, g = dx3_ref[...], x_ref[...], pg_ref[...].astype(F32), g_ref[...]
        de_ref[...] = (dx3v * pgv).astype(BF16)
        ds = (dx3v * e_ref[...].astype(F32) * pgv * (1.0 - pgv)).astype(BF16)
        ds_ref[...] = ds
        dh3 = _dot(ds, wg_ref[...], NT_DIMS)
        h3_ref[...] = (xv * _rstd(xv) * g).astype(BF16)
        dx, dgr = _rms_bwd(xv, g, dh3)
        dx2_ref[...] = dx3v + dx
        dg_ref[...] += jnp.sum(dgr, axis=0, keepdims=True)

    row = pl.BlockSpec((tm, D), lambda i: (i, 0))
    vec = pl.BlockSpec((1, D), lambda i: (0, 0))
    return pl.pallas_call(
        body, name=name, grid=(T // tm,),
        in_specs=[row, row, row, row, vec, _full(w_gate)],
        out_specs=[row, row, row, row, vec],
        out_shape=[jax.ShapeDtypeStruct((T, D), F32)] + [jax.ShapeDtypeStruct((T, D), BF16)] * 3
        + [jax.ShapeDtypeStruct((1, D), F32)])(dx3, x2, e, pg, g_ple.reshape(1, D), w_gate)


def _ffn_act_bwd(name, u, d_a, conv_w, conv_b, S):
    _, T, F = u.shape
    H = CONV_HALO
    tm = _tile(S, 512, H)
    tf = _tile(F, 1408)
    nst = S // tm
    hpt = tm // H
    last_halo = T // H - 1
    n_ext = tm + H
    cw, cb = _conv_params(conv_w, conv_b, F)

    def body(ug_ref, uv_ref, pg_ref, pv_ref, ng_ref, nv_ref, da_ref, dan_ref, wg_ref, wv_ref, bg_ref, bv_ref,
             du_ref, dw_ref, db_ref):
        i = pl.program_id(1)
        it = i % nst

        @pl.when(i == 0)
        def _():
            dw_ref[...] = jnp.zeros_like(dw_ref)
            db_ref[...] = jnp.zeros_like(db_ref)

        keep_prev = jnp.where(it == 0, 0.0, 1.0)
        keep_next = jnp.where(it == nst - 1, 0.0, 1.0)
        def shifted(p_ref, u_ref, n_ref):
            ext = jnp.concatenate([p_ref[...].astype(F32) * keep_prev, u_ref[...].astype(F32),
                                   n_ref[...].astype(F32)], axis=0)
            return ext[H:, :], pltpu.roll(ext, 1, 0)[H:, :], pltpu.roll(ext, 2, 0)[H:, :]

        us_g, us_v = shifted(pg_ref, ug_ref, ng_ref), shifted(pv_ref, uv_ref, nv_ref)
        conv = lambda us, w_ref, b_ref: b_ref[...] + w_ref[2:3, :] * us[0] + w_ref[1:2, :] * us[1] + w_ref[0:1, :] * us[2]
        yg, yv = conv(us_g, wg_ref, bg_ref), conv(us_v, wv_ref, bv_ref)
        da = jnp.concatenate([da_ref[...].astype(F32), dan_ref[...].astype(F32) * keep_next], axis=0)
        sg = _sigmoid(yg)
        silu = yg * sg
        dyv = da * silu
        dyg = (da * yv) * (sg + silu * (1.0 - sg))
        for half, (dy, us, w_ref) in enumerate(((dyg, us_g, wg_ref), (dyv, us_v, wv_ref))):
            du = (w_ref[2:3, :] * dy + w_ref[1:2, :] * pltpu.roll(dy, n_ext - 1, 0)
                  + w_ref[0:1, :] * pltpu.roll(dy, n_ext - 2, 0))
            du_ref[half] = du[:tm, :].astype(BF16)
            dym = dy[:tm, :]
            db_ref[half] += jnp.sum(dym, axis=0, keepdims=True)
            for tap in range(3):
                dw_ref[half, tap:tap + 1, :] += jnp.sum(dym * us[2 - tap][:tm, :], axis=0, keepdims=True)

    main = lambda h: pl.BlockSpec((None, tm, tf), lambda j, i: (h, i, j))
    prev = lambda h: pl.BlockSpec((None, H, tf), lambda j, i: (h, jnp.maximum(i * hpt - 1, 0), j))
    nxt = lambda h: pl.BlockSpec((None, H, tf), lambda j, i: (h, jnp.minimum((i + 1) * hpt, last_halo), j))
    wsp = lambda h: pl.BlockSpec((None, 3, tf), lambda j, i: (h, 0, j))
    bsp = lambda h: pl.BlockSpec((None, 1, tf), lambda j, i: (h, 0, j))
    return pl.pallas_call(
        body, name=name, grid=(F // tf, T // tm),
        in_specs=[main(0), main(1), prev(0), prev(1), nxt(0), nxt(1),
                  pl.BlockSpec((tm, tf), lambda j, i: (i, j)),
                  pl.BlockSpec((H, tf), lambda j, i: (jnp.minimum((i + 1) * hpt, last_halo), j)),
                  wsp(0), wsp(1), bsp(0), bsp(1)],
        out_specs=[pl.BlockSpec((2, tm, tf), lambda j, i: (0, i, j)),
                   pl.BlockSpec((2, 3, tf), lambda j, i: (0, 0, j)),
                   pl.BlockSpec((2, 1, tf), lambda j, i: (0, 0, j))],
        out_shape=[jax.ShapeDtypeStruct((2, T, F), BF16), jax.ShapeDtypeStruct((2, 3, F), F32),
                   jax.ShapeDtypeStruct((2, 1, F), F32)],
    )(u, u, u, u, u, u, d_a, d_a, cw, cw, cb, cb)


def _mix_out_bwd(name, dx1, zr, y_a, y_b, o_l, lse_l, pooled, w_o, w_ya, w_yb, pool_w, pool_scale, S):
    T, D = dx1.shape
    gw = D // len(POOL_WINDOWS)
    H = POOL_HALO
    tm = _tile(S, 256, H)
    nst = S // tm
    hpt = tm // H
    last_halo = T // H - 1
    n_ext = tm + H

    def body(dx_ref, dxn_ref, ga_ref, gb_ref, gbn_ref, ya_ref, yb_ref, o0, o1, o2, l0, l1, l2, pooled_ref,
             wo_ref, wya_ref, wyb_ref, pw_ref, ps_ref,
             dz_ref, dya_ref, dyb_ref, dpm_ref, do0, do1, do2, c0, c1, c2, dps_ref, *scs):
        i = pl.program_id(0)
        it = i % nst

        @pl.when(i == 0)
        def _():
            dps_ref[...] = jnp.zeros_like(dps_ref)

        keep_next = jnp.where(it == nst - 1, 0.0, 1.0)
        dm_e = _dot(jnp.concatenate([dx_ref[...], dxn_ref[...]], axis=0), wo_ref[...], NT_DIMS)
        sgb_e = _sigmoid(jnp.concatenate([gb_ref[...], gbn_ref[...]], axis=0))
        dyb_e = dm_e * sgb_e
        dm = dm_e[:tm, :]
        sga = _sigmoid(ga_ref[...])
        sgb = sgb_e[:tm, :]
        d_ga = dm * ya_ref[...].astype(F32) * (sga * (1.0 - sga))
        d_gb = dm * yb_ref[...].astype(F32) * (sgb * (1.0 - sgb))
        dya = (dm * sga).astype(BF16)
        dya_ref[...] = dya
        dyb_ref[...] = dyb_e[:tm, :].astype(BF16)
        dmixed_e = _dot(dyb_e, wyb_ref[...], NT_DIMS)

        rows = lax.broadcasted_iota(jnp.int32, (n_ext, 1), 0)
        tseq = it * tm + rows
        live = jnp.where(rows < tm, 1.0, keep_next)
        ps = ps_ref[...]
        du_parts = []
        for gi, w in enumerate(POOL_WINDOWS):
            cs = slice(gi * gw, (gi + 1) * gw)
            pm_g = _dot(pooled_ref[:, cs], pw_ref[gi])
            dps_ref[:, cs] += jnp.sum(dmixed_e[:tm, cs] * pm_g, axis=0, keepdims=True)
            dpm_e = (dmixed_e[:, cs] * ps[:, cs]).astype(BF16)
            dpm_ref[:, cs] = dpm_e[:tm, :]
            dpooled_e = _dot(dpm_e, pw_ref[gi], NT_DIMS)
            s = dpooled_e * (_pool_inv_count(tseq, w) * live)
            step = 1
            while step < w:
                s = s + pltpu.roll(s, n_ext - step, 0)
                step *= 2
            du_parts.append(s[:tm, :] - dpooled_e[:tm, :])
        dz_ref[...] = jnp.concatenate(du_parts + [d_ga, d_gb], axis=1).astype(BF16)

        d_attn = _dot(dya, wya_ref[...], NT_DIMS)
        ov, lv = _group_values((o0, o1, o2), (l0, l1, l2), scs[:n_in_sc])
        ws = _merge_weights(*lv)
        prod = d_attn * (ws[0] * ov[0] + ws[1] * ov[1] + ws[2] * ov[2])
        rs = jnp.concatenate(
            [jnp.broadcast_to(jnp.sum(prod[:, hh * HEAD_DIM:(hh + 1) * HEAD_DIM], axis=-1, keepdims=True),
                              (tm, HEAD_DIM)) for hh in range(HEADS)], axis=1)
        out_scs = list(scs[n_in_sc:])
        for d, wg, do_ref, c_ref in zip(DILATIONS, ws, (do0, do1, do2), (c0, c1, c2)):
            if d == 1:
                do_ref[...] = (wg * d_attn).astype(BF16)
                c_ref[...] = -wg * rs
            else:
                _store_residue_major(wg * d_attn, out_scs.pop(), do_ref, d, BF16)
                _store_residue_major(-wg * rs, out_scs.pop(), c_ref, d, F32)

    row = lambda c: pl.BlockSpec((tm, D), lambda i: (i, c))
    nxt = lambda c: pl.BlockSpec((H, D), lambda i: (jnp.minimum((i + 1) * hpt, last_halo), c))
    ps2 = pool_scale.reshape(1, D)
    bf = lambda w: jax.ShapeDtypeStruct((T, w), BF16)
    grp_specs, grp_views = _group_specs(tm, nst, S)
    grp_shape = lambda dt: [jax.ShapeDtypeStruct((T, GROUP_W) if d == 1 else (T // S, d, S // d, GROUP_W), dt)
                            for d in DILATIONS]
    n_in_sc = len(_group_scratch(tm, 2))
    return pl.pallas_call(
        body, name=name, grid=(T // tm,),
        in_specs=[row(0), nxt(0), row(1), row(2), nxt(2), row(0), row(0)] + grp_specs * 2 + [row(0)]
        + [_full(w_o), _full(w_ya), _full(w_yb), _full(pool_w), _full(ps2)],
        out_specs=[pl.BlockSpec((tm, 3 * D), lambda i: (i, 0)), row(0), row(0), row(0)] + grp_specs * 2
        + [pl.BlockSpec((1, D), lambda i: (0, 0))],
        out_shape=[bf(3 * D), bf(D), bf(D), bf(D)] + grp_shape(BF16) + grp_shape(F32)
        + [jax.ShapeDtypeStruct((1, D), F32)],
        scratch_shapes=_group_scratch(tm, 4),
    )(dx1, dx1, zr, zr, zr, y_a, y_b, *grp_views(o_l), *grp_views(lse_l), pooled, w_o, w_ya, w_yb, pool_w, ps2)


def _attn_bwd(name, qkv, d_o, lse, cst, rope, g, Bl, S):
    d = DILATIONS[g]
    L = S // d
    nb = L // BLOCK
    qv = qkv.reshape(Bl * d, L, GROUP_QKV_W)
    dov = d_o.reshape(Bl * d, L, GROUP_W)
    lv = lse.reshape(Bl * d, L, GROUP_W)
    cv = cst.reshape(Bl * d, L, GROUP_W)
    tabs = [t.reshape(d, L, HEAD_DIM) for t in rope]
    scale = HEAD_DIM ** -0.5

    cb, cs = _attn_chunking(nb)
    qc = cb * BLOCK
    lead = BLOCK if nb > 1 else 0

    def body(q_ref, qn_ref, kp_ref, kc_ref, vp_ref, vc_ref, do_ref, don_ref, l_ref, ln_ref, c_ref, cn_ref,
             cos_ref, sa_ref, sb_ref, out_ref):
        n = pl.program_id(1)
        valid = _attn_mask(qc, lead, n > 0)
        qi = lax.broadcasted_iota(jnp.int32, (BLOCK, BLOCK), 0)
        ki = lax.broadcasted_iota(jnp.int32, (BLOCK, BLOCK), 1)
        valid_n = (ki >= qi) & ((n + 1) * cb < nb)
        tail = slice(qc - BLOCK, qc)
        for si in range(cs):
            cos, sa, sb = cos_ref[si], -sa_ref[si], -sb_ref[si]
            for hh in range(HEADS):
                sl = slice(hh * HEAD_DIM, (hh + 1) * HEAD_DIM)
                col = slice(hh * HEAD_DIM, hh * HEAD_DIM + 1)
                q, kc, vc, do = q_ref[si, :, sl], kc_ref[si, :, sl], vc_ref[si, :, sl], do_ref[si, :, sl]
                kk, vv = kc, vc
                if lead:
                    kk = jnp.concatenate([kp_ref[si, :, sl], kc], axis=0)
                    vv = jnp.concatenate([vp_ref[si, :, sl], vc], axis=0)
                s = jnp.where(valid, _dot(q, kk, NT_DIMS) * scale, NEG_INF)
                p = jnp.exp(s - l_ref[si, :, col])
                ds = p * (_dot(do, vv, NT_DIMS) + c_ref[si, :, col])
                dq = _dot(ds, kk) * scale
                dk = _dot(ds[:, lead:], q, TN_DIMS)
                dv = _dot(p[:, lead:], do, TN_DIMS)
                if nb > cb:
                    qn, don = qn_ref[si, :, sl], don_ref[si, :, sl]
                    s2 = jnp.where(valid_n, _dot(qn, kc[tail], NT_DIMS) * scale, NEG_INF)
                    p2 = jnp.exp(s2 - ln_ref[si, :, col])
                    ds2 = p2 * (_dot(don, vc[tail], NT_DIMS) + cn_ref[si, :, col])
                    dk = jnp.concatenate([dk[:qc - BLOCK], dk[tail] + _dot(ds2, qn, TN_DIMS)], axis=0)
                    dv = jnp.concatenate([dv[:qc - BLOCK], dv[tail] + _dot(p2, don, TN_DIMS)], axis=0)
                out_ref[si, :, sl] = _rope(dq, cos, sa, sb).astype(BF16)
                out_ref[si, :, GROUP_W + hh * HEAD_DIM:GROUP_W + (hh + 1) * HEAD_DIM] = (
                    _rope(dk * scale, cos, sa, sb).astype(BF16))
                out_ref[si, :, 2 * GROUP_W + hh * HEAD_DIM:2 * GROUP_W + (hh + 1) * HEAD_DIM] = dv.astype(BF16)

    main = (cs, cb * BLOCK, GROUP_W)
    edge = (cs, BLOCK, GROUP_W)
    cur = lambda off: pl.BlockSpec(main, lambda s, n: (s, n, off))
    prv = lambda off: pl.BlockSpec(edge, lambda s, n: (s, jnp.maximum(n * cb - 1, 0), off))
    nxt = lambda off: pl.BlockSpec(edge, lambda s, n: (s, jnp.minimum((n + 1) * cb, nb - 1), off))
    tab = pl.BlockSpec((cs, cb * BLOCK, HEAD_DIM), lambda s, n: (s % (d // cs), n, 0))
    out = pl.pallas_call(
        body, name=name, grid=(Bl * d // cs, nb // cb),
        in_specs=[cur(0), nxt(0), prv(1), cur(1), prv(2), cur(2),
                  cur(0), nxt(0), cur(0), nxt(0), cur(0), nxt(0), tab, tab, tab],
        out_specs=pl.BlockSpec((cs, cb * BLOCK, GROUP_QKV_W), lambda s, n: (s, n, 0)),
        out_shape=jax.ShapeDtypeStruct((Bl * d, L, GROUP_QKV_W), BF16),
    )(qv, qv, qv, qv, qv, qv, dov, dov, lv, lv, cv, cv, *tabs)
    return out.reshape(Bl * S, GROUP_QKV_W)


def _pool_w_grad(name, pooled, d_pm, gw):
    T = pooled.shape[0]
    ng = len(POOL_WINDOWS)
    tk = _tile(T, 1024, 8)
    return _mm(name, pooled, d_pm, grid=(ng, 1, T // tk),
               a_block=(tk, gw), a_map=lambda i, j, k: (k, i),
               b_block=(tk, gw), b_map=lambda i, j, k: (k, i),
               dims=TN_DIMS, acc_shape=(gw, gw),
               outs=[((ng, gw, gw), F32, (None, gw, gw), lambda i, j, k: (i, 0, 0))])[0]


def _up_w_grad(name, h2, du):
    T, D = h2.shape
    F = du.shape[2]
    tm, tn, tk = _tile(D, 1024), _tile(F, 1408), _tile(T, 1024, 8)
    njh = F // tn
    return _mm(name, h2, du, grid=(D // tm, 2 * njh, T // tk),
               a_block=(tk, tm), a_map=lambda i, j, k: (k, i),
               b_block=(None, tk, tn), b_map=lambda i, j, k: (j // njh, k, j % njh),
               dims=TN_DIMS, acc_shape=(tm, tn),
               outs=[((D, 2 * F), F32, (tm, tn), lambda i, j, k: (i, j))])[0]


def _adamw(name, w, m, v, pieces):
    R, C = w.shape
    nl = len(pieces)
    rl = R // nl
    if nl > 1 and rl % 8:
        per = [_adamw(f"{name}_{l}", w[l * rl:(l + 1) * rl], m[l * rl:(l + 1) * rl], v[l * rl:(l + 1) * rl],
                      [pieces[l]]) for l in range(nl)]
        return [jnp.concatenate([p[o] for p in per], axis=0) for o in range(4)]
    tr = _tile(rl, max(PACK_ROWS, (1 << 18) // C // PACK_ROWS * PACK_ROWS), PACK_ROWS)
    nbl = rl // tr
    c1 = 1.0 - ADAM_B1 ** ADAM_STEP
    c2 = 1.0 - ADAM_B2 ** ADAM_STEP

    def body(w_ref, m_ref, v_ref, *rest):
        p_refs = rest[:nl]
        g_ref, d_ref, mo_ref, vo_ref = rest[nl:]
        i = pl.program_id(0)
        for l in range(nl):
            @pl.when((i >= l * nbl) & (i < (l + 1) * nbl))
            def _():
                g = p_refs[l][0].astype(F32)
                for dev in range(1, N_DEV):
                    g = g + p_refs[l][dev].astype(F32)
                mn = ADAM_B1 * m_ref[...] + (1.0 - ADAM_B1) * g
                vn = ADAM_B2 * v_ref[...] + (1.0 - ADAM_B2) * (g * g)
                g_ref[...] = g
                mo_ref[...] = mn
                vo_ref[...] = vn
                d_ref[...] = -ADAM_LR * ((mn / c1) / (jnp.sqrt(vn / c2) + ADAM_EPS) + ADAM_WD * w_ref[...])

    row = pl.BlockSpec((tr, C), lambda i: (i, 0))
    piece = lambda l: pl.BlockSpec((N_DEV, tr, C), lambda i: (0, jnp.clip(i - l * nbl, 0, nbl - 1), 0))
    return pl.pallas_call(
        body, name=name, grid=(R // tr,),
        in_specs=[row, row, row] + [piece(l) for l in range(nl)],
        out_specs=[row] * 4,
        out_shape=[jax.ShapeDtypeStruct((R, C), F32)] * 4)(w, m, v, *pieces)


def _my_index():
    return 4 * lax.axis_index("x") + 2 * lax.axis_index("y") + lax.axis_index("c")


def _all_gather(name, mine):
    na = len(mine)

    def body(*refs):
        x_refs, out_refs, token = refs[:na], refs[na:2 * na], refs[2 * na]
        send_sems, recv_sems, local_sems = refs[2 * na + 1:]
        token[...] = jnp.zeros_like(token)
        x, y, c = lax.axis_index("x"), lax.axis_index("y"), lax.axis_index("c")
        me, sibling = (x, y, c), (x, y, 1 - c)
        chips = [(1 - x, y), (x, 1 - y), (1 - x, 1 - y)]

        def slot(a, px, py, pc):
            return out_refs[a].at[4 * px + 2 * py + pc]

        def copy(a, k, block, to, src=None):
            return pltpu.make_async_remote_copy(
                src_ref=slot(a, *block) if src is None else src, dst_ref=slot(a, *block),
                send_sem=send_sems.at[7 * a + k], recv_sem=recv_sems.at[7 * a + k],
                device_id=to, device_id_type=MESH_ID)

        own = [pltpu.make_async_copy(x_refs[a], slot(a, *me), local_sems.at[a]) for a in range(na)]
        for cp in own:
            cp.start()
        first = []
        for a in range(na):
            first.append(copy(a, 0, me, sibling, src=x_refs[a]))
            first += [copy(a, 1 + j, me, (*chip, c), src=x_refs[a]) for j, chip in enumerate(chips)]
        for cp in first:
            cp.start()
        passed = []
        for j, chip in enumerate(chips):
            for a in range(na):
                copy(a, 1 + j, (*chip, c), me).wait_recv()
                fwd = copy(a, 4 + j, (*chip, c), sibling)
                fwd.start()
                passed.append(fwd)
        for a in range(na):
            copy(a, 0, sibling, me).wait_recv()
            for j, chip in enumerate(chips):
                copy(a, 4 + j, (*chip, 1 - c), me).wait_recv()
        for cp in first + passed:
            cp.wait_send()
        for cp in own:
            cp.wait()

    res = pl.pallas_call(
        body, name=name,
        in_specs=[pl.BlockSpec(memory_space=pl.ANY)] * na,
        out_specs=[pl.BlockSpec(memory_space=pl.ANY)] * na + [pl.BlockSpec(memory_space=pltpu.VMEM)],
        out_shape=[jax.ShapeDtypeStruct((N_DEV,) + m.shape, m.dtype) for m in mine]
        + [jax.ShapeDtypeStruct((8, LANE), F32)],
        scratch_shapes=[pltpu.SemaphoreType.DMA((7 * na,)), pltpu.SemaphoreType.DMA((7 * na,)),
                        pltpu.SemaphoreType.DMA((na,))],
    )(*mine)
    return res[:na], res[na]


_HBM_SPEC = pl.BlockSpec(memory_space=pltpu.HBM)
_SEM_SPEC = pl.BlockSpec(memory_space=pltpu.SEMAPHORE)
_SIDE_EFFECT = pltpu.SideEffectType.DATAFLOW_SIDE_EFFECTING


def _peer_of(k):
    x, y, c = lax.axis_index("x"), lax.axis_index("y"), lax.axis_index("c")
    px = 1 - x if k & 4 else x
    py = 1 - y if k & 2 else y
    pc = 1 - c if k & 1 else c
    return (px, py, pc), 4 * px + 2 * py + pc


def _send_start(name, srcs, pieces):
    na = len(srcs)
    land_shapes = [s.shape if pieces else (N_DEV,) + s.shape for s in srcs]
    lands = [lax.empty(shp, s.dtype) for shp, s in zip(land_shapes, srcs)]

    def body(*refs):
        src_refs, land_refs = refs[:na], refs[na:2 * na]
        send_sems, recv_sems, token = refs[2 * na], refs[2 * na + 1], refs[4 * na + 2]
        me = 4 * lax.axis_index("x") + 2 * lax.axis_index("y") + lax.axis_index("c")
        for k in range(1, N_DEV):
            to, pidx = _peer_of(k)
            for a in range(na):
                pltpu.make_async_remote_copy(
                    src_ref=src_refs[a].at[pidx] if pieces else src_refs[a], dst_ref=land_refs[a].at[me],
                    send_sem=send_sems.at[7 * a + k - 1], recv_sem=recv_sems.at[7 * a + k - 1],
                    device_id=to, device_id_type=MESH_ID).start()
        token[...] = jnp.zeros_like(token)

    hbm = lambda arrs: [pltpu.HBM(a.shape, a.dtype) for a in arrs]
    outs = pl.pallas_call(
        body, name=name,
        out_shape=(pltpu.SemaphoreType.DMA((7 * na,)), pltpu.SemaphoreType.DMA((7 * na,)), *hbm(srcs), *hbm(lands),
                   jax.ShapeDtypeStruct((8, LANE), F32)),
        in_specs=[_HBM_SPEC] * (2 * na),
        out_specs=(_SEM_SPEC, _SEM_SPEC, *([_HBM_SPEC] * (2 * na)), pl.BlockSpec(memory_space=pltpu.VMEM)),
        input_output_aliases={i: 2 + i for i in range(2 * na)},
        compiler_params=pltpu.CompilerParams(has_side_effects=_SIDE_EFFECT),
    )(*[pltpu.with_memory_space_constraint(s, pltpu.HBM) for s in srcs],
      *[pltpu.with_memory_space_constraint(l, pltpu.HBM) for l in lands])
    return outs[0], outs[1], outs[2:2 + na], outs[2 + na:2 + 2 * na], outs[-1]


def _send_wait(name, send_sems, recv_sems, srcs, lands, pieces, after):
    na = len(srcs)

    def body(*refs):
        src_refs, land_refs = refs[:na], refs[na:2 * na]
        send_sems, recv_sems = refs[2 * na], refs[2 * na + 1]
        for k in range(1, N_DEV):
            to, pidx = _peer_of(k)
            for a in range(na):
                cp = pltpu.make_async_remote_copy(
                    src_ref=src_refs[a].at[pidx] if pieces else src_refs[a], dst_ref=land_refs[a].at[pidx],
                    send_sem=send_sems.at[7 * a + k - 1], recv_sem=recv_sems.at[7 * a + k - 1],
                    device_id=to, device_id_type=MESH_ID)
                cp.wait_send()
                cp.wait_recv()

    hbm = lambda arrs: [pltpu.HBM(a.shape, a.dtype) for a in arrs]
    outs = pl.pallas_call(
        body, name=name,
        out_shape=tuple(hbm(srcs) + hbm(lands)),
        in_specs=[_HBM_SPEC] * (2 * na) + [_SEM_SPEC, _SEM_SPEC, pl.BlockSpec(memory_space=pl.ANY)],
        out_specs=tuple([_HBM_SPEC] * (2 * na)),
        input_output_aliases={i: i for i in range(2 * na)},
        compiler_params=pltpu.CompilerParams(has_side_effects=_SIDE_EFFECT),
    )(*srcs, *lands, send_sems, recv_sems, after)
    return outs[:na], outs[na:]


def _own_slot(land, own):
    me = 4 * lax.axis_index("x") + 2 * lax.axis_index("y") + lax.axis_index("c")
    mine = lax.broadcasted_iota(jnp.int32, land.shape, 0) == me
    return jnp.where(mine, jnp.broadcast_to(own, land.shape), land)


def _exchange(name, pieces, bcast):
    n_p, n_b = len(pieces), len(bcast)
    na = n_p + n_b

    def body(*refs):
        src_refs, dst_refs = refs[:na], refs[na:2 * na]
        send_sems, recv_sems, local_sems = refs[2 * na:]
        x, y, c = lax.axis_index("x"), lax.axis_index("y"), lax.axis_index("c")
        me = 4 * x + 2 * y + c

        def src(a, slot):
            return src_refs[a].at[slot] if a < n_p else src_refs[a]

        own = [pltpu.make_async_copy(src(a, me), dst_refs[a].at[me], local_sems.at[a]) for a in range(na)]
        for cp in own:
            cp.start()

        def peer_of(k):
            px = 1 - x if k & 4 else x
            py = 1 - y if k & 2 else y
            pc = 1 - c if k & 1 else c
            return (px, py, pc), 4 * px + 2 * py + pc

        def copy(a, k, src_slot, dst_slot, to):
            return pltpu.make_async_remote_copy(
                src_ref=src(a, src_slot), dst_ref=dst_refs[a].at[dst_slot],
                send_sem=send_sems.at[7 * a + k - 1], recv_sem=recv_sems.at[7 * a + k - 1],
                device_id=to, device_id_type=MESH_ID)

        sent = []
        for k in range(1, N_DEV):
            to, pidx = peer_of(k)
            for a in range(na):
                cp = copy(a, k, pidx, me, to)
                cp.start()
                sent.append(cp)
        for k in range(1, N_DEV):
            to, pidx = peer_of(k)
            for a in range(na):
                copy(a, k, me, pidx, to).wait_recv()
        for cp in sent:
            cp.wait_send()
        for cp in own:
            cp.wait()

    arrays = list(pieces) + list(bcast)
    out_shape = [jax.ShapeDtypeStruct(p.shape, p.dtype) for p in pieces]
    out_shape += [jax.ShapeDtypeStruct((N_DEV,) + b.shape, b.dtype) for b in bcast]
    res = pl.pallas_call(
        body, name=name,
        in_specs=[pl.BlockSpec(memory_space=pl.ANY)] * na, out_specs=[pl.BlockSpec(memory_space=pl.ANY)] * na,
        out_shape=out_shape,
        scratch_shapes=[pltpu.SemaphoreType.DMA((7 * na,)), pltpu.SemaphoreType.DMA((7 * na,)),
                        pltpu.SemaphoreType.DMA((na,))],
    )(*arrays)
    return res[:n_p], res[n_p:]


def _pad_rows(flat, cols, row_mult):
    n = flat.shape[-1]
    unit = cols * row_mult
    padded = -(-n // unit) * unit
    pad = [(0, 0)] * (flat.ndim - 1) + [(0, padded - n)]
    return jnp.pad(flat, pad).reshape(flat.shape[:-1] + (padded // cols, cols))


def _perm_cols(w):
    aw = N_GROUPS * GROUP_W
    parts = [w[..., QKV_W:]]
    parts += [w[..., a * aw + g * GROUP_W:a * aw + (g + 1) * GROUP_W] for g in range(N_GROUPS) for a in range(3)]
    return jnp.concatenate(parts, axis=-1)


def _unperm_cols(wp, rest_w):
    qkv = wp[..., rest_w:]
    parts = [qkv[..., g * GROUP_QKV_W + a * GROUP_W:g * GROUP_QKV_W + (a + 1) * GROUP_W]
             for a in range(3) for g in range(N_GROUPS)]
    return jnp.concatenate(parts + [wp[..., :rest_w]], axis=-1)


SHARD_AXIS = dict(SHARDED)


def _layer_shards(wts, li, names, zero):
    out = []
    for n in names:
        w = wts[n][li] if zero is None else wts[n][li] + zero
        out.append(w if n in EXACT_F32 else w.astype(BF16))
    return out


def _assemble_weights(names, segs):
    W = {}
    for n, seg in zip(names, segs):
        ax = SHARD_AXIS[n]
        shp = seg.shape[1:]
        seg = jnp.moveaxis(seg, 0, ax)
        W[n] = seg.reshape(shp[:ax] + (N_DEV * shp[ax],) + shp[ax + 1:])
    if "w_in" in W:
        W["w_in"] = _perm_cols(W["w_in"])
    return W


def _grad_pieces(gr):
    out = []
    for n in gr:
        ax = SHARD_AXIS[n]
        shp = gr[n].shape
        g = gr[n].reshape(shp[:ax] + (N_DEV, shp[ax] // N_DEV) + shp[ax + 1:])
        out.append(jnp.moveaxis(g, ax, 0).astype(BF16))
    return out


def _pack_small(vals):
    flat = jnp.concatenate([vals[n].astype(F32).reshape(-1) for n in REPLICATED])
    return _pad_rows(flat, LANE, 8)


def _layer_fwd(li, x, p_l, w_in, other_weights, G, rope, Bl, S, F):
    T, D = x.shape
    rest_w = 3 * D
    sv = {"x0": x}
    W = {"w_in": w_in}
    hs = _rms_mix_fwd(f"rms_mix_{li}", x, G["g_mix"], S)
    h = hs[0]
    h_g = [h] + [a.reshape(T, D) for a in hs[1:]]
    sv["h_g"] = h_g
    zr = _mm_nn(f"rest_proj_{li}", h, W["w_in"], n_cols=rest_w, tn=1024, tk=_tile(D, 1024))
    sv["zr"] = zr
    qkv_l, o_l, lse_l = [], [], []
    for g in range(N_GROUPS):
        qkv = _qkv_proj(f"qkv_proj_{li}_{g}", h_g[g], W["w_in"], rope[g], S, rest_w + g * GROUP_QKV_W)
        o, lse = _attn_fwd(f"attn_fwd_{li}_{g}", qkv, g, Bl, S)
        qkv_l.append(qkv)
        o_l.append(o)
        lse_l.append(lse)
    sv["qkv"], sv["o"], sv["lse"] = qkv_l, o_l, lse_l
    W.update(other_weights(o_l[-1]))
    x1, attn, pooled, mixed, y_a, y_b, merged = _mix_out_fwd(
        f"mix_out_fwd_{li}", x, zr, o_l, lse_l, W["w_ya"], W["pool_w"], G["pool_scale"], W["w_yb"], W["w_o"], S)
    sv.update(x1=x1, attn=attn, pooled=pooled, mixed=mixed, y_a=y_a, y_b=y_b, merged=merged)
    h2 = _rms_fwd(f"rms_ffn_{li}", x1, G["g_ffn"])
    u = _up_proj(f"up_proj_{li}", h2, W["w_up"], F)
    a = _ffn_act_fwd(f"ffn_act_fwd_{li}", u, W["conv_w"], G["conv_b"], S)
    x2 = _mm_nn(f"down_proj_{li}", a, W["w_down"], add=x1, tk=_tile(F, 1408))
    sv.update(h2=h2, u=u, a=a, x2=x2)
    x3, e, pg, p_bf = _ple_fwd(f"ple_fwd_{li}", x2, p_l, G["g_ple"], W["w_ple_gate"], W["w_ple"])
    sv.update(e=e, pg=pg, p_bf=p_bf)
    return x3, sv, W


EARLY_GRADS = ("w_ple", "w_ple_gate", "w_down", "conv_w", "w_up")
LATE_GRADS = ("w_in", "w_ya", "w_yb", "pool_w", "w_o")


def _layer_bwd(li, dx3, sv, W, G, rope, Bl, S, F, send):
    T, D = dx3.shape
    rest_w = 3 * D
    gr = {}
    dx2, d_e, d_s, h3, dg = _ple_bwd(f"ple_bwd_{li}", dx3, sv["x2"], sv["e"], sv["pg"], G["g_ple"], W["w_ple_gate"])
    gr["g_ple"] = dg[0]
    gr["w_ple"] = _mm_tn(f"w_ple_grad_{li}", sv["p_bf"], d_e)
    gr["w_ple_gate"] = _mm_tn(f"w_ple_gate_grad_{li}", h3, d_s)

    d_a = _mm_nt(f"down_bwd_{li}", dx2, W["w_down"], out_dtype=BF16, tn=1408, tk=_tile(D, 1024))
    gr["w_down"] = _mm_tn(f"w_down_grad_{li}", sv["a"], dx2, tm=1408)
    du, d_cw, d_cb = _ffn_act_bwd(f"ffn_act_bwd_{li}", sv["u"], d_a, W["conv_w"], G["conv_b"], S)
    gr["conv_w"] = d_cw.transpose(1, 0, 2).reshape(3, 2 * F)
    gr["conv_b"] = d_cb.reshape(2 * F)
    tk_f = _tile(F, 1408)
    nkh = F // tk_f
    tm_r = _tile(T, 1024, 8)
    dx1, dg = _mm_nt_rmsbwd(f"up_bwd_{li}", du, (None, tm_r, tk_f), lambda i, j, k: (k // nkh, i, k % nkh),
                            2 * nkh, tk_f, W["w_up"], sv["x1"], G["g_ffn"], dx2)
    gr["g_ffn"] = dg[0]
    gr["w_up"] = _up_w_grad(f"w_up_grad_{li}", sv["h2"], du)
    zero = send("a", {n: gr[n] for n in EARLY_GRADS})

    (dz_rest, d_ya, d_yb, d_pm, do0, do1, do2, c0, c1, c2, dps) = _mix_out_bwd(
        f"mix_out_bwd_{li}", dx1, sv["zr"], sv["y_a"], sv["y_b"], sv["o"], sv["lse"], sv["pooled"],
        W["w_o"], W["w_ya"], W["w_yb"], W["pool_w"], G["pool_scale"] + zero, S)
    gr["pool_scale"] = dps[0]
    gr["w_o"] = _mm_tn(f"w_o_grad_{li}", sv["merged"], dx1)
    gr["w_ya"] = _mm_tn(f"w_ya_grad_{li}", sv["attn"], d_ya)
    gr["w_yb"] = _mm_tn(f"w_yb_grad_{li}", sv["mixed"], d_yb)
    gr["pool_w"] = _pool_w_grad(f"pool_w_grad_{li}", sv["pooled"], d_pm, D // len(POOL_WINDOWS))
    segs = [(dz_rest, 1)]
    for g, (do, cst) in enumerate(((do0, c0), (do1, c1), (do2, c2))):
        dqkv = _attn_bwd(f"attn_bwd_{li}_{g}", sv["qkv"][g], do, sv["lse"][g], cst, rope[g], g, Bl, S)
        segs.append((dqkv, DILATIONS[g]))

    h_rows = [sv["h_g"][0]] + sv["h_g"]
    w_in_parts = [_mm_tn(f"w_in_grad_{li}_{s}", h_rows[s], seg, tn=1536) for s, (seg, _) in enumerate(segs)]
    gr["w_in"] = _unperm_cols(jnp.concatenate(w_in_parts, axis=1), rest_w)
    zero = send("b", {n: gr[n] for n in LATE_GRADS})
    dx0, dg = _in_bwd(f"in_bwd_{li}", segs, W["w_in"], sv["x0"], G["g_mix"] + zero, dx1, S)
    gr["g_mix"] = dg[0]
    return dx0, gr


def kernel(x, p, g_mix, w_in, w_ya, w_yb, pool_w, pool_scale, w_o, g_ffn, w_up, conv_w, conv_b, w_down, g_ple, w_ple, w_ple_gate, g_final, loss_target, m_g_mix, m_w_in, m_w_ya, m_w_yb, m_pool_w, m_pool_scale, m_w_o, m_g_ffn, m_w_up, m_conv_w, m_conv_b, m_w_down, m_g_ple, m_w_ple, m_w_ple_gate, m_g_final, v_g_mix, v_w_in, v_w_ya, v_w_yb, v_pool_w, v_pool_scale, v_w_o, v_g_ffn, v_w_up, v_conv_w, v_conv_b, v_w_down, v_g_ple, v_w_ple, v_w_ple_gate, v_g_final):
    wts = dict(g_mix=g_mix, w_in=w_in, w_ya=w_ya, w_yb=w_yb, pool_w=pool_w, pool_scale=pool_scale, w_o=w_o,
               g_ffn=g_ffn, w_up=w_up, conv_w=conv_w, conv_b=conv_b, w_down=w_down, g_ple=g_ple, w_ple=w_ple,
               w_ple_gate=w_ple_gate, g_final=g_final)
    mom = dict(g_mix=m_g_mix, w_in=m_w_in, w_ya=m_w_ya, w_yb=m_w_yb, pool_w=m_pool_w, pool_scale=m_pool_scale,
               w_o=m_w_o, g_ffn=m_g_ffn, w_up=m_w_up, conv_w=m_conv_w, conv_b=m_conv_b, w_down=m_w_down,
               g_ple=m_g_ple, w_ple=m_w_ple, w_ple_gate=m_w_ple_gate, g_final=m_g_final)
    var = dict(g_mix=v_g_mix, w_in=v_w_in, w_ya=v_w_ya, w_yb=v_w_yb, pool_w=v_pool_w, pool_scale=v_pool_scale,
               w_o=v_w_o, g_ffn=v_g_ffn, w_up=v_w_up, conv_w=v_conv_w, conv_b=v_conv_b, w_down=v_w_down,
               g_ple=v_g_ple, w_ple=v_w_ple, w_ple_gate=v_w_ple_gate, g_final=v_g_final)
    Bl, S, D = x.shape
    depth = g_mix.shape[0]
    F = w_down.shape[1] * N_DEV
    T = Bl * S
    assert S % (BLOCK * DILATIONS[-1]) == 0 and D % GROUP_W == 0 and F % LANE == 0
    rope = [tuple(t if d == 1 else t.reshape(S // d, d, HEAD_DIM).transpose(1, 0, 2).reshape(S, HEAD_DIM)
                  for t in _rope_tables(S)) for d in DILATIONS]

    first, others = ("w_in",), tuple(n for n, _ in SHARDED if n != "w_in")
    got_in, tok = _all_gather("gather_w0_in", _layer_shards(wts, 0, first, None))
    gathers = {}
    for li in range(depth):
        names = others if li == 0 else first + others
        *gathers[li], tok = _send_start(f"gather_w{li}_start", _layer_shards(wts, li, names, tok[0, 0]), False)

    def gathered(li, names, after):
        shards, lands = _send_wait(f"gather_w{li}_wait", *gathers[li], False, after)
        return _assemble_weights(names, [_own_slot(l, s) for l, s in zip(lands, shards)])

    xs = x.reshape(T, D)
    saved = []
    for li in range(depth):
        G = {n: wts[n][li] for n in REPLICATED if n != "g_final"}
        if li == 0:
            G["g_mix"] = G["g_mix"] + tok[0, 0]
            w_in_full = _assemble_weights(first, got_in)["w_in"]
            rest_of = lambda after: gathered(0, others, after)
        else:
            W_all = gathered(li, first + others, xs)
            w_in_full = W_all["w_in"]
            rest_of = lambda after, W_all=W_all: W_all
        xs, sv, W = _layer_fwd(li, xs, p[li].reshape(T, -1), w_in_full, rest_of, G, rope, Bl, S, F)
        saved.append((sv, W, G))

    dx, loss_row, dg_final = _loss_bwd("loss_bwd", xs, loss_target.reshape(T, D), g_final)
    layer_grads = [None] * depth
    sends = []
    zero = [None]
    for li in reversed(range(depth)):
        sv, W, G = saved[li]
        if zero[0] is not None:
            G = dict(G, g_ple=G["g_ple"] + zero[0])

        def send(tag, group, li=li):
            *handles, tok = _send_start(f"exchange_g{li}{tag}_start", _grad_pieces(group), True)
            sends.append((li, tag, tuple(group), handles))
            zero[0] = tok[0, 0]
            return zero[0]

        dx, layer_grads[li] = _layer_bwd(li, dx, sv, W, G, rope, Bl, S, F, send)

    recv = {}
    for li, tag, names, handles in sends:
        pcs, lands = _send_wait(f"exchange_g{li}{tag}_wait", *handles, True, dx)
        for n, l, s in zip(names, lands, pcs):
            recv[(li, n)] = _own_slot(l, s)
    grads = {n: jnp.stack([layer_grads[li][n] for li in range(depth)]) for n in REPLICATED if n != "g_final"}
    grads["g_final"] = dg_final[0]
    _, (small_all,) = _exchange("exchange_small", [], [_pack_small(grads)])

    out_g, out_d, out_m, out_v = {}, {}, {}, {}
    for n, _ in SHARDED:
        shp = wts[n].shape
        two_d = (math.prod(shp[:-1]), shp[-1])
        pieces = [recv[(li, n)].reshape(N_DEV, two_d[0] // depth, two_d[1]) for li in range(depth)]
        res = _adamw(f"adamw_{n}", wts[n].reshape(two_d), mom[n].reshape(two_d), var[n].reshape(two_d), pieces)
        out_g[n], out_d[n], out_m[n], out_v[n] = [r.reshape(shp) for r in res]
    res = _adamw("adamw_replicated", _pack_small(wts), _pack_small(mom), _pack_small(var), [small_all])
    off = 0
    for n in REPLICATED:
        shp = wts[n].shape
        size = math.prod(shp)
        for dst, r in zip((out_g, out_d, out_m, out_v), res):
            dst[n] = r.reshape(-1)[off:off + size].reshape(shp)
        off += size

    loss = lax.psum(loss_row[0, 0], MESH_AXES)
    outs = [loss, dx.reshape(Bl, S, D)]
    for dct in (out_g, out_d, out_m, out_v):
        outs += [dct[n] for n in WEIGHT_ORDER]
    return tuple(outs)
```

```python
import math

import jax
import jax.numpy as jnp
from jax import lax
from jax.experimental import pallas as pl
from jax.experimental.pallas import tpu as pltpu

F32 = jnp.float32
BF16 = jnp.bfloat16

N_DEV = 8
HEAD_DIM = 128
HEADS = 4
GROUP_W = HEADS * HEAD_DIM
DILATIONS = (1, 4, 16)
N_GROUPS = len(DILATIONS)
QKV_W = 3 * N_GROUPS * GROUP_W
GROUP_QKV_W = 3 * GROUP_W
BLOCK = 128
ROPE_DIM = HEAD_DIM // 4
ROPE_HALF = ROPE_DIM // 2
ROPE_THETA = 500000.0
NEG_INF = -1e30
POOL_WINDOWS = (2, 4, 8, 16)
POOL_HALO = 16
CONV_HALO = 16
RMS_EPS = 1e-6
ADAM_LR = 0.001
ADAM_B1 = 0.9
ADAM_B2 = 0.999
ADAM_EPS = 1e-08
ADAM_WD = 0.01
ADAM_STEP = 10
LANE = 128
PACK_ROWS = 16
MESH_ID = pl.DeviceIdType.MESH
MESH_AXES = ("x", "y", "c")

NT_DIMS = (((1,), (1,)), ((), ()))
TN_DIMS = (((0,), (0,)), ((), ()))
NN_DIMS = (((1,), (0,)), ((), ()))

SHARDED = (("w_in", 1), ("w_ya", 1), ("w_yb", 0), ("pool_w", 1), ("w_o", 0), ("w_up", 1), ("conv_w", 1),
           ("w_down", 0), ("w_ple", 1), ("w_ple_gate", 0))
EXACT_F32 = ("conv_w",)
REPLICATED = ("g_mix", "pool_scale", "g_ffn", "conv_b", "g_ple", "g_final")
WEIGHT_ORDER = ("g_mix", "w_in", "w_ya", "w_yb", "pool_w", "pool_scale", "w_o", "g_ffn", "w_up", "conv_w", "conv_b",
                "w_down", "g_ple", "w_ple", "w_ple_gate", "g_final")


def _tile(n, pref, mult=LANE):
    if n <= pref:
        return n
    t = (pref // mult) * mult
    while t >= mult:
        if n % t == 0:
            return t
        t -= mult
    return n


def _sigmoid(x):
    return 1.0 / (1.0 + jnp.exp(-x))


def _dot(a, b, dims=NN_DIMS):
    return lax.dot_general(a.astype(BF16), b.astype(BF16), dims, preferred_element_type=F32)


def _rstd(x):
    return lax.rsqrt(jnp.mean(x * x, axis=-1, keepdims=True) + RMS_EPS)


def _rms_bwd(x, g, dh):
    r = _rstd(x)
    u = dh * g
    dx = r * u - x * (r * r * r) * jnp.mean(x * u, axis=-1, keepdims=True)
    return dx, dh * x * r


def _full(a):
    return pl.BlockSpec(a.shape, lambda *_: (0,) * a.ndim)


def _mm(name, a, b, *, grid, a_block, a_map, b_block, b_map, dims, acc_shape, outs, extras=(), epi=None):
    nk = grid[2]
    n_ex = len(extras)
    n_out = len(outs)

    def body(*refs):
        a_ref, b_ref = refs[0], refs[1]
        ex = refs[2:2 + n_ex]
        o = refs[2 + n_ex:2 + n_ex + n_out]
        acc = refs[2 + n_ex + n_out]
        i, j, k = pl.program_id(0), pl.program_id(1), pl.program_id(2)

        @pl.when(k == 0)
        def _():
            acc[...] = jnp.zeros_like(acc)

        acc[...] += _dot(a_ref[...], b_ref[...], dims)

        @pl.when(k == nk - 1)
        def _():
            if epi is None:
                o[0][...] = acc[...].astype(o[0].dtype)
            else:
                epi(acc[...], ex, o, i, j)

    in_specs = [pl.BlockSpec(a_block, a_map), pl.BlockSpec(b_block, b_map)]
    in_specs += [pl.BlockSpec(blk, mp) for (_, blk, mp) in extras]
    out_specs = [pl.BlockSpec(blk, mp) for (_, _, blk, mp) in outs]
    out_shape = [jax.ShapeDtypeStruct(s, d) for (s, d, _, _) in outs]
    return pl.pallas_call(
        body, name=name, grid=grid, in_specs=in_specs, out_specs=out_specs, out_shape=out_shape,
        scratch_shapes=[pltpu.VMEM(acc_shape, F32)],
    )(a, b, *[e[0] for e in extras])


def _mm_tn(name, a, b, *, tm=1024, tn=1024, tk=1024):
    K, M = a.shape
    N = b.shape[1]
    tm, tn, tk = _tile(M, tm), _tile(N, tn), _tile(K, tk, 8)
    return _mm(name, a, b, grid=(M // tm, N // tn, K // tk),
               a_block=(tk, tm), a_map=lambda i, j, k: (k, i),
               b_block=(tk, tn), b_map=lambda i, j, k: (k, j),
               dims=TN_DIMS, acc_shape=(tm, tn),
               outs=[((M, N), F32, (tm, tn), lambda i, j, k: (i, j))])[0]


def _mm_rows(name, a, w, *, transposed=False, n_cols=None, chunk=1024, out_dtype=F32, add=None, tm=512):
    M, K = a.shape
    N = w.shape[0] if transposed else (n_cols if n_cols is not None else w.shape[1])
    tm = _tile(M, tm, 8)
    chunk = _tile(N, chunk)
    w_block = (N, K) if transposed else (K, N)

    def body(*refs):
        a_ref, w_ref = refs[0], refs[1]
        o_ref = refs[-1]
        av = a_ref[...].astype(BF16)
        for c in range(N // chunk):
            cols = slice(c * chunk, (c + 1) * chunk)
            acc = _dot(av, w_ref[cols, :], NT_DIMS) if transposed else _dot(av, w_ref[:, cols])
            if add is not None:
                acc = acc + refs[2][:, cols]
            o_ref[:, cols] = acc.astype(out_dtype)

    row = lambda width: pl.BlockSpec((tm, width), lambda i: (i, 0))
    return pl.pallas_call(
        body, name=name, grid=(M // tm,),
        in_specs=[row(K), pl.BlockSpec(w_block, lambda i: (0, 0))] + ([row(N)] if add is not None else []),
        out_specs=row(N),
        out_shape=jax.ShapeDtypeStruct((M, N), out_dtype),
    )(a, w, *([add] if add is not None else []))


def _mm_nt_rmsbwd(name, a, a_block, a_map, nk, tk, w, x, g, dres):
    T, D = x.shape
    tm = a_block[-2]

    def epi(acc, ex, o, i, j):
        @pl.when(i == 0)
        def _():
            o[1][...] = jnp.zeros_like(o[1])

        dx, dgr = _rms_bwd(ex[0][...], ex[1][...], acc)
        o[0][...] = ex[2][...] + dx
        o[1][...] += jnp.sum(dgr, axis=0, keepdims=True)

    row = lambda i, j, k: (i, 0)
    vec = lambda i, j, k: (0, 0)
    return _mm(name, a, w, grid=(T // tm, 1, nk),
               a_block=a_block, a_map=a_map,
               b_block=(D, tk), b_map=lambda i, j, k: (0, k),
               dims=NT_DIMS, acc_shape=(tm, D),
               outs=[((T, D), F32, (tm, D), row), ((1, D), F32, (1, D), vec)],
               extras=[(x, (tm, D), row), (g.reshape(1, D), (1, D), vec), (dres, (tm, D), row)], epi=epi)


def _in_bwd(name, segs, w_perm, x, g, dres, S):
    T, D = x.shape
    tm = _tile(S, 512, 256)
    nst = S // tm
    tk = GROUP_QKV_W
    steps = [a.shape[1] // tk for a, _ in segs]
    starts = [sum(steps[:s]) for s in range(len(segs))]
    nk = sum(steps)
    ns = len(segs)
    cols = _chunks(D)
    assert all(a.shape[1] % tk == 0 for a, _ in segs) and nk * tk == w_perm.shape[1]

    def body(*refs):
        a_refs = refs[:ns]
        w_ref, x_ref, g_ref, dres_ref, dx_ref, dg_ref, acc, acc_tok = refs[ns:]
        i, k = pl.program_id(0), pl.program_id(1)

        @pl.when(k == 0)
        def _():
            acc[...] = jnp.zeros_like(acc)
            acc_tok[...] = jnp.zeros_like(acc_tok)

        for s in range(ns):
            d = segs[s][1]

            @pl.when((k >= starts[s]) & (k < starts[s] + steps[s]))
            def _():
                if d == 1:
                    acc_tok[...] += _dot(a_refs[s][...], w_ref[...], NT_DIMS)
                else:
                    prod = _dot(a_refs[s][...].reshape(tm, tk), w_ref[...], NT_DIMS)
                    q = tm // d
                    for c, cs in enumerate(cols):
                        for r in range(d):
                            acc[c, pl.ds(r, q, stride=d), :] += prod[r * q:(r + 1) * q, cs]

        @pl.when(k == nk - 1)
        def _():
            @pl.when(i == 0)
            def _():
                dg_ref[...] = jnp.zeros_like(dg_ref)

            dh = acc_tok[...] + jnp.concatenate([acc[c] for c in range(len(cols))], axis=1)
            dx, dgr = _rms_bwd(x_ref[...], g_ref[...], dh)
            dx_ref[...] = dres_ref[...] + dx
            dg_ref[...] += jnp.sum(dgr, axis=0, keepdims=True)

    def seg_spec(s):
        kmap = lambda k: jnp.clip(k - starts[s], 0, steps[s] - 1)
        d = segs[s][1]
        if d == 1:
            return pl.BlockSpec((tm, tk), lambda i, k: (i, kmap(k)))
        return pl.BlockSpec((None, d, tm // d, tk), lambda i, k: (i // nst, 0, i % nst, kmap(k)))

    views = [a if d == 1 else a.reshape(T // S, d, S // d, a.shape[1]) for a, d in segs]
    row = pl.BlockSpec((tm, D), lambda i, k: (i, 0))
    vec = pl.BlockSpec((1, D), lambda i, k: (0, 0))
    return pl.pallas_call(
        body, name=name, grid=(T // tm, nk),
        in_specs=[seg_spec(s) for s in range(ns)] + [pl.BlockSpec((D, tk), lambda i, k: (0, k)), row, vec, row],
        out_specs=[row, vec],
        out_shape=[jax.ShapeDtypeStruct((T, D), F32), jax.ShapeDtypeStruct((1, D), F32)],
        scratch_shapes=[pltpu.VMEM((D // LANE, tm, LANE), F32), pltpu.VMEM((tm, D), F32)],
    )(*views, w_perm, x, g.reshape(1, D), dres)


def _rope_tables(S):
    pos = jnp.arange(S, dtype=F32)
    inv_freq = jnp.exp(jnp.arange(0, ROPE_DIM, 2, dtype=F32) * (-math.log(ROPE_THETA) / ROPE_DIM))
    ang = pos[:, None] * inv_freq[None, :]
    cos, sin = jnp.cos(ang), jnp.sin(ang)
    ones = jnp.ones((S, HEAD_DIM - ROPE_DIM), F32)
    zeros_h = jnp.zeros((S, ROPE_HALF), F32)
    zeros_r = jnp.zeros((S, HEAD_DIM - ROPE_DIM), F32)
    c = jnp.concatenate([cos, cos, ones], axis=1)
    sa = jnp.concatenate([-sin, zeros_h, zeros_r], axis=1)
    sb = jnp.concatenate([zeros_h, sin, zeros_r], axis=1)
    return c, sa, sb


def _rope(t, c, sa, sb):
    return t * c + pltpu.roll(t, HEAD_DIM - ROPE_HALF, 1) * sa + pltpu.roll(t, ROPE_HALF, 1) * sb


def _rms_fwd(name, x, g):
    T, D = x.shape
    tm = _tile(T, 512, 8)

    def body(x_ref, g_ref, h_ref):
        xv = x_ref[...]
        h_ref[...] = (xv * _rstd(xv) * g_ref[...]).astype(BF16)

    return pl.pallas_call(
        body, name=name, grid=(T // tm,),
        in_specs=[pl.BlockSpec((tm, D), lambda i: (i, 0)), pl.BlockSpec((1, D), lambda i: (0, 0))],
        out_specs=pl.BlockSpec((tm, D), lambda i: (i, 0)),
        out_shape=jax.ShapeDtypeStruct((T, D), BF16))(x, g.reshape(1, D))


def _chunks(width):
    return [slice(c * LANE, (c + 1) * LANE) for c in range(width // LANE)]


def _store_residue_major(val, sc, out_ref, d, dtype):
    rows = val.shape[0]
    for c, cs in enumerate(_chunks(val.shape[1])):
        sc[c] = val[:, cs]
    for r in range(d):
        for c, cs in enumerate(_chunks(val.shape[1])):
            out_ref[r, :, cs] = sc[c, pl.ds(r, rows // d, stride=d), :].astype(dtype)


def _load_token_order(blk_ref, sc, d):
    _, q, width = blk_ref.shape
    for r in range(d):
        for c, cs in enumerate(_chunks(width)):
            sc[c, pl.ds(r, q, stride=d), :] = blk_ref[r, :, cs]
    return jnp.concatenate([sc[c] for c in range(width // LANE)], axis=1)


def _residue_major_spec(d, q, width, nst):
    return pl.BlockSpec((None, d, q, width), lambda i, *_: (i // nst, 0, i % nst, 0))


def _rms_mix_fwd(name, x, g, S):
    T, D = x.shape
    Bl = T // S
    tm = _tile(S, 512, 256)
    nst = S // tm
    dils = [d for d in DILATIONS if d > 1]

    def body(x_ref, g_ref, h_ref, *rest):
        rm_refs, sc = rest[:len(dils)], rest[len(dils)]
        xv = x_ref[...]
        hv = xv * _rstd(xv) * g_ref[...]
        h_ref[...] = hv.astype(BF16)
        for d, o_ref in zip(dils, rm_refs):
            _store_residue_major(hv, sc, o_ref, d, BF16)

    row = pl.BlockSpec((tm, D), lambda i: (i, 0))
    return pl.pallas_call(
        body, name=name, grid=(T // tm,),
        in_specs=[row, pl.BlockSpec((1, D), lambda i: (0, 0))],
        out_specs=[row] + [_residue_major_spec(d, tm // d, D, nst) for d in dils],
        out_shape=[jax.ShapeDtypeStruct((T, D), BF16)]
        + [jax.ShapeDtypeStruct((Bl, d, S // d, D), BF16) for d in dils],
        scratch_shapes=[pltpu.VMEM((D // LANE, tm, LANE), F32)])(x, g.reshape(1, D))


def _qkv_proj(name, h, w_perm, rope, S, col_off):
    T, D = h.shape
    tm = _tile(S, 512, 8)
    c_t, sa_t, sb_t = rope
    n_seq_tiles = S // tm
    jblk = col_off // GROUP_QKV_W

    def body(h_ref, w_ref, c_ref, sa_ref, sb_ref, o_ref):
        hv = h_ref[...]
        c, sa, sb = c_ref[...], sa_ref[...], sb_ref[...]
        for chunk in range(3):
            cols = slice(chunk * GROUP_W, (chunk + 1) * GROUP_W)
            acc = _dot(hv, w_ref[:, cols])
            if chunk < 2:
                for hh in range(HEADS):
                    sl = slice(hh * HEAD_DIM, (hh + 1) * HEAD_DIM)
                    o_ref[:, chunk * GROUP_W + hh * HEAD_DIM:chunk * GROUP_W + (hh + 1) * HEAD_DIM] = (
                        _rope(acc[:, sl], c, sa, sb).astype(BF16))
            else:
                o_ref[:, cols] = acc.astype(BF16)

    tab = pl.BlockSpec((tm, HEAD_DIM), lambda i: (i % n_seq_tiles, 0))
    return pl.pallas_call(
        body, name=name, grid=(T // tm,),
        in_specs=[pl.BlockSpec((tm, D), lambda i: (i, 0)), pl.BlockSpec((D, GROUP_QKV_W), lambda i: (0, jblk)),
                  tab, tab, tab],
        out_specs=pl.BlockSpec((tm, GROUP_QKV_W), lambda i: (i, 0)),
        out_shape=jax.ShapeDtypeStruct((T, GROUP_QKV_W), BF16))(h, w_perm, c_t, sa_t, sb_t)


ATTN_BLOCKS_PER_STEP = 4


def _attn_mask(rows, lead, has_prev):
    qi = lax.broadcasted_iota(jnp.int32, (rows, lead + rows), 0)
    ki = lax.broadcasted_iota(jnp.int32, (rows, lead + rows), 1)
    diff = lead + qi - ki
    band = (diff >= 0) & (diff <= BLOCK)
    return band & (has_prev | (ki >= lead)) if lead else band


def _attn_chunking(nb):
    cb = min(ATTN_BLOCKS_PER_STEP, nb)
    return cb, ATTN_BLOCKS_PER_STEP // cb


def _attn_fwd(name, qkv, g, Bl, S):
    d = DILATIONS[g]
    L = S // d
    nb = L // BLOCK
    cb, cs = _attn_chunking(nb)
    qv = qkv.reshape(Bl * d, L, GROUP_QKV_W)
    scale = HEAD_DIM ** -0.5

    def body(q_ref, kc_ref, vc_ref, kp_ref, vp_ref, o_ref, l_ref):
        n = pl.program_id(1)
        for si in range(cs):
            for bi in range(cb):
                rows = slice(bi * BLOCK, (bi + 1) * BLOCK)
                before = slice((bi - 1) * BLOCK, bi * BLOCK)
                valid = _attn_mask(BLOCK, BLOCK, n > 0 if bi == 0 else True)
                for hh in range(HEADS):
                    sl = slice(hh * HEAD_DIM, (hh + 1) * HEAD_DIM)
                    kp = kp_ref[si, :, sl] if bi == 0 else kc_ref[si, before, sl]
                    vp = vp_ref[si, :, sl] if bi == 0 else vc_ref[si, before, sl]
                    kk = jnp.concatenate([kp, kc_ref[si, rows, sl]], axis=0)
                    vv = jnp.concatenate([vp, vc_ref[si, rows, sl]], axis=0)
                    s = jnp.where(valid, _dot(q_ref[si, rows, sl], kk, NT_DIMS) * scale, NEG_INF)
                    m = jnp.max(s, axis=-1, keepdims=True)
                    p = jnp.exp(s - m)
                    l = jnp.sum(p, axis=-1, keepdims=True)
                    o_ref[si, rows, sl] = _dot(p, vv) / l
                    l_ref[si, rows, sl] = jnp.broadcast_to(m + jnp.log(l), (BLOCK, HEAD_DIM))

    main = (cs, cb * BLOCK, GROUP_W)
    edge = (cs, BLOCK, GROUP_W)
    cur = lambda off: pl.BlockSpec(main, lambda s, n: (s, n, off))
    prev = lambda off: pl.BlockSpec(edge, lambda s, n: (s, jnp.maximum(n * cb - 1, 0), off))
    out = pl.BlockSpec(main, lambda s, n: (s, n, 0))
    return pl.pallas_call(
        body, name=name, grid=(Bl * d // cs, nb // cb),
        in_specs=[cur(0), cur(1), cur(2), prev(1), prev(2)],
        out_specs=[out, out],
        out_shape=[jax.ShapeDtypeStruct((Bl * d, L, GROUP_W), F32)] * 2)(qv, qv, qv, qv, qv)


def _merge_weights(l0, l1, l2):
    mx = jnp.maximum(jnp.maximum(l0, l1), l2)
    e0, e1, e2 = jnp.exp(l0 - mx), jnp.exp(l1 - mx), jnp.exp(l2 - mx)
    inv = 1.0 / (e0 + e1 + e2)
    return e0 * inv, e1 * inv, e2 * inv


def _group_specs(tm, nst, S):
    specs = []
    for d in DILATIONS:
        specs.append(pl.BlockSpec((tm, GROUP_W), lambda i: (i, 0)) if d == 1
                     else _residue_major_spec(d, tm // d, GROUP_W, nst))

    def views(arrs):
        out = []
        for d, a in zip(DILATIONS, arrs):
            out.append(a.reshape(-1, GROUP_W) if d == 1 else a.reshape(-1, d, S // d, GROUP_W))
        return out

    return specs, views


def _group_scratch(tm, per_group):
    n = per_group * sum(1 for d in DILATIONS if d > 1)
    return [pltpu.VMEM((GROUP_W // LANE, tm, LANE), F32) for _ in range(n)]


def _group_values(o_refs, l_refs, scs):
    scs = list(scs)
    ov, lv = [], []
    for d, o_ref, l_ref in zip(DILATIONS, o_refs, l_refs):
        if d == 1:
            ov.append(o_ref[...])
            lv.append(l_ref[...])
        else:
            ov.append(_load_token_order(o_ref, scs.pop(), d))
            lv.append(_load_token_order(l_ref, scs.pop(), d))
    return ov, lv


def _pool_inv_count(tseq, w):
    return 1.0 / jnp.minimum(tseq + 1, w).astype(F32)


def _mix_out_fwd(name, x, zr, o_l, lse_l, w_ya, pool_w, pool_scale, w_yb, w_o, S):
    T, D = x.shape
    gw = D // len(POOL_WINDOWS)
    tm = _tile(S, 256, POOL_HALO)
    nst = S // tm
    hpt = tm // POOL_HALO

    def body(x_ref, u_ref, uh_ref, ga_ref, gb_ref, o0, o1, o2, l0, l1, l2, wya_ref, pw_ref, ps_ref, wyb_ref, wo_ref,
             x1_ref, attn_ref, pooled_ref, mixed_ref, ya_ref, yb_ref, merged_ref, *scs):
        it = pl.program_id(0) % nst
        ov, lv = _group_values((o0, o1, o2), (l0, l1, l2), scs)
        w0, w1, w2 = _merge_weights(*lv)
        attn = w0 * ov[0] + w1 * ov[1] + w2 * ov[2]
        attn_ref[...] = attn.astype(BF16)
        y_a = _dot(attn, wya_ref[...])

        u = u_ref[...]
        halo = uh_ref[...] * jnp.where(it == 0, 0.0, 1.0)
        ext = jnp.concatenate([halo, u], axis=0)
        tseq = it * tm + lax.broadcasted_iota(jnp.int32, (tm, 1), 0)
        pm_parts = []
        for gi, w in enumerate(POOL_WINDOWS):
            cs = slice(gi * gw, (gi + 1) * gw)
            s = ext[:, cs]
            step = 1
            while step < w:
                s = s + pltpu.roll(s, step, 0)
                step *= 2
            pooled_g = (s[POOL_HALO:, :] * _pool_inv_count(tseq, w) - u[:, cs]).astype(BF16)
            pooled_ref[:, cs] = pooled_g
            pm_parts.append(_dot(pooled_g, pw_ref[gi]))
        mixed = (jnp.concatenate(pm_parts, axis=1) * ps_ref[...]).astype(BF16)
        mixed_ref[...] = mixed
        y_b = _dot(mixed, wyb_ref[...])
        merged = (_sigmoid(ga_ref[...]) * y_a + _sigmoid(gb_ref[...]) * y_b).astype(BF16)
        ya_ref[...] = y_a.astype(BF16)
        yb_ref[...] = y_b.astype(BF16)
        merged_ref[...] = merged
        x1_ref[...] = x_ref[...] + _dot(merged, wo_ref[...])

    row = lambda c: pl.BlockSpec((tm, D), lambda i: (i, c))
    row512 = pl.BlockSpec((tm, GROUP_W), lambda i: (i, 0))
    ps = pool_scale.reshape(1, D)
    halo_spec = pl.BlockSpec((POOL_HALO, D), lambda i: (jnp.maximum(i * hpt - 1, 0), 0))
    grp_specs, grp_views = _group_specs(tm, nst, S)
    return pl.pallas_call(
        body, name=name, grid=(T // tm,),
        in_specs=[row(0), row(0), halo_spec, row(1), row(2)] + grp_specs * 2
        + [_full(w_ya), _full(pool_w), _full(ps), _full(w_yb), _full(w_o)],
        out_specs=[row(0), row512, row(0), row(0), row(0), row(0), row(0)],
        out_shape=[jax.ShapeDtypeStruct((T, D), F32), jax.ShapeDtypeStruct((T, GROUP_W), BF16),
                   jax.ShapeDtypeStruct((T, D), BF16), jax.ShapeDtypeStruct((T, D), BF16),
                   jax.ShapeDtypeStruct((T, D), BF16), jax.ShapeDtypeStruct((T, D), BF16),
                   jax.ShapeDtypeStruct((T, D), BF16)],
        scratch_shapes=_group_scratch(tm, 2),
    )(x, zr, zr, zr, zr, *grp_views(o_l), *grp_views(lse_l), w_ya, pool_w, ps, w_yb, w_o)


def _up_proj(name, h2, w_up, F):
    T, D = h2.shape
    tm, tn, tk = _tile(T, 1024, 8), _tile(F, 1408), _tile(D, 1024)
    njh = F // tn
    return _mm(name, h2, w_up, grid=(T // tm, 2 * njh, D // tk),
               a_block=(tm, tk), a_map=lambda i, j, k: (i, k),
               b_block=(tk, tn), b_map=lambda i, j, k: (k, j),
               dims=NN_DIMS, acc_shape=(tm, tn),
               outs=[((2, T, F), BF16, (None, tm, tn), lambda i, j, k: (j // njh, i, j % njh))])[0]


def _conv_y(ext, w_ref, b_ref):
    return (b_ref[...] + w_ref[2:3, :] * ext + w_ref[1:2, :] * pltpu.roll(ext, 1, 0)
            + w_ref[0:1, :] * pltpu.roll(ext, 2, 0))


def _conv_params(conv_w, conv_b, F):
    cw = conv_w.reshape(3, 2, F).transpose(1, 0, 2)
    return cw, conv_b.reshape(2, 1, F)


def _ffn_act_fwd(name, u, conv_w, conv_b, S):
    _, T, F = u.shape
    tm = _tile(S, 512, CONV_HALO)
    tf = _tile(F, 1408)
    nst = S // tm
    hpt = tm // CONV_HALO
    cw, cb = _conv_params(conv_w, conv_b, F)

    def body(ug_ref, uv_ref, hg_ref, hv_ref, wg_ref, wv_ref, bg_ref, bv_ref, a_ref):
        keep = jnp.where(pl.program_id(0) % nst == 0, 0.0, 1.0)
        ext = lambda h_ref, u_ref: jnp.concatenate([h_ref[...].astype(F32) * keep, u_ref[...].astype(F32)], axis=0)
        yg = _conv_y(ext(hg_ref, ug_ref), wg_ref, bg_ref)[CONV_HALO:, :]
        yv = _conv_y(ext(hv_ref, uv_ref), wv_ref, bv_ref)[CONV_HALO:, :]
        a_ref[...] = (yg * _sigmoid(yg) * yv).astype(BF16)

    main = lambda h: pl.BlockSpec((None, tm, tf), lambda i, j: (h, i, j))
    halo = lambda h: pl.BlockSpec((None, CONV_HALO, tf), lambda i, j: (h, jnp.maximum(i * hpt - 1, 0), j))
    wsp = lambda h: pl.BlockSpec((None, 3, tf), lambda i, j: (h, 0, j))
    bsp = lambda h: pl.BlockSpec((None, 1, tf), lambda i, j: (h, 0, j))
    return pl.pallas_call(
        body, name=name, grid=(T // tm, F // tf),
        in_specs=[main(0), main(1), halo(0), halo(1), wsp(0), wsp(1), bsp(0), bsp(1)],
        out_specs=pl.BlockSpec((tm, tf), lambda i, j: (i, j)),
        out_shape=jax.ShapeDtypeStruct((T, F), BF16))(u, u, u, u, cw, cw, cb, cb)


def _ple_fwd(name, x2, p, g_ple, w_gate, w_ple):
    T, D = x2.shape
    P = p.shape[1]
    tm = _tile(T, 512, 8)

    def body(x_ref, p_ref, g_ref, wg_ref, wp_ref, x3_ref, e_ref, pg_ref, pbf_ref):
        xv = x_ref[...]
        h3 = xv * _rstd(xv) * g_ref[...]
        pg = _sigmoid(_dot(h3, wg_ref[...]))
        pb = p_ref[...].astype(BF16)
        e = _dot(pb, wp_ref[...])
        x3_ref[...] = xv + e * pg
        e_ref[...] = e.astype(BF16)
        pg_ref[...] = pg.astype(BF16)
        pbf_ref[...] = pb

    row = pl.BlockSpec((tm, D), lambda i: (i, 0))
    prow = pl.BlockSpec((tm, P), lambda i: (i, 0))
    g2 = g_ple.reshape(1, D)
    return pl.pallas_call(
        body, name=name, grid=(T // tm,),
        in_specs=[row, prow, _full(g2), _full(w_gate), _full(w_ple)],
        out_specs=[row, row, row, prow],
        out_shape=[jax.ShapeDtypeStruct((T, D), F32)] + [jax.ShapeDtypeStruct((T, D), BF16)] * 2
        + [jax.ShapeDtypeStruct((T, P), BF16)],
    )(x2, p, g2, w_gate, w_ple)


def _loss_bwd(name, xf, target, g_final):
    T, D = xf.shape
    tm = _tile(T, 512, 8)
    nt = T // tm

    def body(x_ref, t_ref, g_ref, dx_ref, loss_ref, dg_ref, lacc):
        i = pl.program_id(0)

        @pl.when(i == 0)
        def _():
            lacc[...] = jnp.zeros_like(lacc)
            dg_ref[...] = jnp.zeros_like(dg_ref)
            loss_ref[...] = jnp.zeros_like(loss_ref)

        xv = x_ref[...]
        g = g_ref[...]
        diff = xv * _rstd(xv) * g - t_ref[...]
        lacc[...] += jnp.sum(diff * diff, axis=0, keepdims=True)
        dx, dgr = _rms_bwd(xv, g, diff * (1.0 / D))
        dx_ref[...] = dx
        dg_ref[...] += jnp.sum(dgr, axis=0, keepdims=True)

        @pl.when(i == nt - 1)
        def _():
            tot = jnp.sum(lacc[...], axis=-1, keepdims=True) * (0.5 / D)
            loss_ref[...] = jnp.broadcast_to(tot, (1, LANE))

    row = pl.BlockSpec((tm, D), lambda i: (i, 0))
    vec = pl.BlockSpec((1, D), lambda i: (0, 0))
    return pl.pallas_call(
        body, name=name, grid=(nt,),
        in_specs=[row, row, vec],
        out_specs=[row, pl.BlockSpec((1, LANE), lambda i: (0, 0)), vec],
        out_shape=[jax.ShapeDtypeStruct((T, D), F32), jax.ShapeDtypeStruct((1, LANE), F32),
                   jax.ShapeDtypeStruct((1, D), F32)],
        scratch_shapes=[pltpu.VMEM((1, D), F32)])(xf, target, g_final.reshape(1, D))


def _ple_bwd(name, dx3, x2, e, pg, g_ple, w_gate):
    T, D = x2.shape
    tm = _tile(T, 512, 8)

    def body(dx3_ref, x_ref, e_ref, pg_ref, g_ref, wg_ref, dx2_ref, de_ref, ds_ref, h3_ref, dg_ref):
        @pl.when(pl.program_id(0) == 0)
        def _():
            dg_ref[...] = jnp.zeros_like(dg_ref)

        dx3v, xv, pgv, g = dx3_ref[...], x_ref[...], pg_ref[...].astype(F32), g_ref[...]
        de_ref[...] = (dx3v * pgv).astype(BF16)
        ds = (dx3v * e_ref[...].astype(F32) * pgv * (1.0 - pgv)).astype(BF16)
        ds_ref[...] = ds
        dh3 = _dot(ds, wg_ref[...], NT_DIMS)
        h3_ref[...] = (xv * _rstd(xv) * g).astype(BF16)
        dx, dgr = _rms_bwd(xv, g, dh3)
        dx2_ref[...] = dx3v + dx
        dg_ref[...] += jnp.sum(dgr, axis=0, keepdims=True)

    row = pl.BlockSpec((tm, D), lambda i: (i, 0))
    vec = pl.BlockSpec((1, D), lambda i: (0, 0))
    return pl.pallas_call(
        body, name=name, grid=(T // tm,),
        in_specs=[row, row, row, row, vec, _full(w_gate)],
        out_specs=[row, row, row, row, vec],
        out_shape=[jax.ShapeDtypeStruct((T, D), F32)] + [jax.ShapeDtypeStruct((T, D), BF16)] * 3
        + [jax.ShapeDtypeStruct((1, D), F32)])(dx3, x2, e, pg, g_ple.reshape(1, D), w_gate)


def _ffn_act_bwd(name, u, d_a, conv_w, conv_b, S):
    _, T, F = u.shape
    H = CONV_HALO
    tm = _tile(S, 512, H)
    tf = _tile(F, 1408)
    nst = S // tm
    hpt = tm // H
    last_halo = T // H - 1
    n_ext = tm + H
    cw, cb = _conv_params(conv_w, conv_b, F)

    def body(ug_ref, uv_ref, pg_ref, pv_ref, ng_ref, nv_ref, da_ref, dan_ref, wg_ref, wv_ref, bg_ref, bv_ref,
             du_ref, dw_ref, db_ref):
        i = pl.program_id(1)
        it = i % nst

        @pl.when(i == 0)
        def _():
            dw_ref[...] = jnp.zeros_like(dw_ref)
            db_ref[...] = jnp.zeros_like(db_ref)

        keep_prev = jnp.where(it == 0, 0.0, 1.0)
        keep_next = jnp.where(it == nst - 1, 0.0, 1.0)
        def shifted(p_ref, u_ref, n_ref):
            ext = jnp.concatenate([p_ref[...].astype(F32) * keep_prev, u_ref[...].astype(F32),
                                   n_ref[...].astype(F32)], axis=0)
            return ext[H:, :], pltpu.roll(ext, 1, 0)[H:, :], pltpu.roll(ext, 2, 0)[H:, :]

        us_g, us_v = shifted(pg_ref, ug_ref, ng_ref), shifted(pv_ref, uv_ref, nv_ref)
        conv = lambda us, w_ref, b_ref: b_ref[...] + w_ref[2:3, :] * us[0] + w_ref[1:2, :] * us[1] + w_ref[0:1, :] * us[2]
        yg, yv = conv(us_g, wg_ref, bg_ref), conv(us_v, wv_ref, bv_ref)
        da = jnp.concatenate([da_ref[...].astype(F32), dan_ref[...].astype(F32) * keep_next], axis=0)
        sg = _sigmoid(yg)
        silu = yg * sg
        dyv = da * silu
        dyg = (da * yv) * (sg + silu * (1.0 - sg))
        for half, (dy, us, w_ref) in enumerate(((dyg, us_g, wg_ref), (dyv, us_v, wv_ref))):
            du = (w_ref[2:3, :] * dy + w_ref[1:2, :] * pltpu.roll(dy, n_ext - 1, 0)
                  + w_ref[0:1, :] * pltpu.roll(dy, n_ext - 2, 0))
            du_ref[half] = du[:tm, :].astype(BF16)
            dym = dy[:tm, :]
            db_ref[half] += jnp.sum(dym, axis=0, keepdims=True)
            for tap in range(3):
                dw_ref[half, tap:tap + 1, :] += jnp.sum(dym * us[2 - tap][:tm, :], axis=0, keepdims=True)

    main = lambda h: pl.BlockSpec((None, tm, tf), lambda j, i: (h, i, j))
    prev = lambda h: pl.BlockSpec((None, H, tf), lambda j, i: (h, jnp.maximum(i * hpt - 1, 0), j))
    nxt = lambda h: pl.BlockSpec((None, H, tf), lambda j, i: (h, jnp.minimum((i + 1) * hpt, last_halo), j))
    wsp = lambda h: pl.BlockSpec((None, 3, tf), lambda j, i: (h, 0, j))
    bsp = lambda h: pl.BlockSpec((None, 1, tf), lambda j, i: (h, 0, j))
    return pl.pallas_call(
        body, name=name, grid=(F // tf, T // tm),
        in_specs=[main(0), main(1), prev(0), prev(1), nxt(0), nxt(1),
                  pl.BlockSpec((tm, tf), lambda j, i: (i, j)),
                  pl.BlockSpec((H, tf), lambda j, i: (jnp.minimum((i + 1) * hpt, last_halo), j)),
                  wsp(0), wsp(1), bsp(0), bsp(1)],
        out_specs=[pl.BlockSpec((2, tm, tf), lambda j, i: (0, i, j)),
                   pl.BlockSpec((2, 3, tf), lambda j, i: (0, 0, j)),
                   pl.BlockSpec((2, 1, tf), lambda j, i: (0, 0, j))],
        out_shape=[jax.ShapeDtypeStruct((2, T, F), BF16), jax.ShapeDtypeStruct((2, 3, F), F32),
                   jax.ShapeDtypeStruct((2, 1, F), F32)],
    )(u, u, u, u, u, u, d_a, d_a, cw, cw, cb, cb)


def _mix_out_bwd(name, dx1, zr, y_a, y_b, o_l, lse_l, pooled, w_o, w_ya, w_yb, pool_w, pool_scale, S):
    T, D = dx1.shape
    gw = D // len(POOL_WINDOWS)
    H = POOL_HALO
    tm = _tile(S, 256, H)
    nst = S // tm
    hpt = tm // H
    last_halo = T // H - 1
    n_ext = tm + H

    def body(dx_ref, dxn_ref, ga_ref, gb_ref, gbn_ref, ya_ref, yb_ref, o0, o1, o2, l0, l1, l2, pooled_ref,
             wo_ref, wya_ref, wyb_ref, pw_ref, ps_ref,
             dz_ref, dya_ref, dyb_ref, dpm_ref, do0, do1, do2, c0, c1, c2, dps_ref, *scs):
        i = pl.program_id(0)
        it = i % nst

        @pl.when(i == 0)
        def _():
            dps_ref[...] = jnp.zeros_like(dps_ref)

        keep_next = jnp.where(it == nst - 1, 0.0, 1.0)
        dm_e = _dot(jnp.concatenate([dx_ref[...], dxn_ref[...]], axis=0), wo_ref[...], NT_DIMS)
        sgb_e = _sigmoid(jnp.concatenate([gb_ref[...], gbn_ref[...]], axis=0))
        dyb_e = dm_e * sgb_e
        dm = dm_e[:tm, :]
        sga = _sigmoid(ga_ref[...])
        sgb = sgb_e[:tm, :]
        d_ga = dm * ya_ref[...].astype(F32) * (sga * (1.0 - sga))
        d_gb = dm * yb_ref[...].astype(F32) * (sgb * (1.0 - sgb))
        dya = (dm * sga).astype(BF16)
        dya_ref[...] = dya
        dyb_ref[...] = dyb_e[:tm, :].astype(BF16)
        dmixed_e = _dot(dyb_e, wyb_ref[...], NT_DIMS)

        rows = lax.broadcasted_iota(jnp.int32, (n_ext, 1), 0)
        tseq = it * tm + rows
        live = jnp.where(rows < tm, 1.0, keep_next)
        ps = ps_ref[...]
        du_parts = []
        for gi, w in enumerate(POOL_WINDOWS):
            cs = slice(gi * gw, (gi + 1) * gw)
            pm_g = _dot(pooled_ref[:, cs], pw_ref[gi])
            dps_ref[:, cs] += jnp.sum(dmixed_e[:tm, cs] * pm_g, axis=0, keepdims=True)
            dpm_e = (dmixed_e[:, cs] * ps[:, cs]).astype(BF16)
            dpm_ref[:, cs] = dpm_e[:tm, :]
            dpooled_e = _dot(dpm_e, pw_ref[gi], NT_DIMS)
            s = dpooled_e * (_pool_inv_count(tseq, w) * live)
            step = 1
            while step < w:
                s = s + pltpu.roll(s, n_ext - step, 0)
                step *= 2
            du_parts.append(s[:tm, :] - dpooled_e[:tm, :])
        dz_ref[...] = jnp.concatenate(du_parts + [d_ga, d_gb], axis=1).astype(BF16)

        d_attn = _dot(dya, wya_ref[...], NT_DIMS)
        ov, lv = _group_values((o0, o1, o2), (l0, l1, l2), scs[:n_in_sc])
        ws = _merge_weights(*lv)
        prod = d_attn * (ws[0] * ov[0] + ws[1] * ov[1] + ws[2] * ov[2])
        rs = jnp.concatenate(
            [jnp.broadcast_to(jnp.sum(prod[:, hh * HEAD_DIM:(hh + 1) * HEAD_DIM], axis=-1, keepdims=True),
                              (tm, HEAD_DIM)) for hh in range(HEADS)], axis=1)
        out_scs = list(scs[n_in_sc:])
        for d, wg, do_ref, c_ref in zip(DILATIONS, ws, (do0, do1, do2), (c0, c1, c2)):
            if d == 1:
                do_ref[...] = (wg * d_attn).astype(BF16)
                c_ref[...] = -wg * rs
            else:
                _store_residue_major(wg * d_attn, out_scs.pop(), do_ref, d, BF16)
                _store_residue_major(-wg * rs, out_scs.pop(), c_ref, d, F32)

    row = lambda c: pl.BlockSpec((tm, D), lambda i: (i, c))
    nxt = lambda c: pl.BlockSpec((H, D), lambda i: (jnp.minimum((i + 1) * hpt, last_halo), c))
    ps2 = pool_scale.reshape(1, D)
    bf = lambda w: jax.ShapeDtypeStruct((T, w), BF16)
    grp_specs, grp_views = _group_specs(tm, nst, S)
    grp_shape = lambda dt: [jax.ShapeDtypeStruct((T, GROUP_W) if d == 1 else (T // S, d, S // d, GROUP_W), dt)
                            for d in DILATIONS]
    n_in_sc = len(_group_scratch(tm, 2))
    return pl.pallas_call(
        body, name=name, grid=(T // tm,),
        in_specs=[row(0), nxt(0), row(1), row(2), nxt(2), row(0), row(0)] + grp_specs * 2 + [row(0)]
        + [_full(w_o), _full(w_ya), _full(w_yb), _full(pool_w), _full(ps2)],
        out_specs=[pl.BlockSpec((tm, 3 * D), lambda i: (i, 0)), row(0), row(0), row(0)] + grp_specs * 2
        + [pl.BlockSpec((1, D), lambda i: (0, 0))],
        out_shape=[bf(3 * D), bf(D), bf(D), bf(D)] + grp_shape(BF16) + grp_shape(F32)
        + [jax.ShapeDtypeStruct((1, D), F32)],
        scratch_shapes=_group_scratch(tm, 4),
    )(dx1, dx1, zr, zr, zr, y_a, y_b, *grp_views(o_l), *grp_views(lse_l), pooled, w_o, w_ya, w_yb, pool_w, ps2)


def _attn_bwd(name, qkv, d_o, lse, cst, rope, g, Bl, S):
    d = DILATIONS[g]
    L = S // d
    nb = L // BLOCK
    qv = qkv.reshape(Bl * d, L, GROUP_QKV_W)
    dov = d_o.reshape(Bl * d, L, GROUP_W)
    lv = lse.reshape(Bl * d, L, GROUP_W)
    cv = cst.reshape(Bl * d, L, GROUP_W)
    tabs = [t.reshape(d, L, HEAD_DIM) for t in rope]
    scale = HEAD_DIM ** -0.5

    cb, cs = _attn_chunking(nb)
    qc = cb * BLOCK
    lead = BLOCK if nb > 1 else 0

    def body(q_ref, qn_ref, kp_ref, kc_ref, vp_ref, vc_ref, do_ref, don_ref, l_ref, ln_ref, c_ref, cn_ref,
             cos_ref, sa_ref, sb_ref, out_ref):
        n = pl.program_id(1)
        valid = _attn_mask(qc, lead, n > 0)
        qi = lax.broadcasted_iota(jnp.int32, (BLOCK, BLOCK), 0)
        ki = lax.broadcasted_iota(jnp.int32, (BLOCK, BLOCK), 1)
        valid_n = (ki >= qi) & ((n + 1) * cb < nb)
        tail = slice(qc - BLOCK, qc)
        for si in range(cs):
            cos, sa, sb = cos_ref[si], -sa_ref[si], -sb_ref[si]
            for hh in range(HEADS):
                sl = slice(hh * HEAD_DIM, (hh + 1) * HEAD_DIM)
                col = slice(hh * HEAD_DIM, hh * HEAD_DIM + 1)
                q, kc, vc, do = q_ref[si, :, sl], kc_ref[si, :, sl], vc_ref[si, :, sl], do_ref[si, :, sl]
                kk, vv = kc, vc
                if lead:
                    kk = jnp.concatenate([kp_ref[si, :, sl], kc], axis=0)
                    vv = jnp.concatenate([vp_ref[si, :, sl], vc], axis=0)
                s = jnp.where(valid, _dot(q, kk, NT_DIMS) * scale, NEG_INF)
                p = jnp.exp(s - l_ref[si, :, col])
                ds = p * (_dot(do, vv, NT_DIMS) + c_ref[si, :, col])
                dq = _dot(ds, kk) * scale
                dk = _dot(ds[:, lead:], q, TN_DIMS)
                dv = _dot(p[:, lead:], do, TN_DIMS)
                if nb > cb:
                    qn, don = qn_ref[si, :, sl], don_ref[si, :, sl]
                    s2 = jnp.where(valid_n, _dot(qn, kc[tail], NT_DIMS) * scale, NEG_INF)
                    p2 = jnp.exp(s2 - ln_ref[si, :, col])
                    ds2 = p2 * (_dot(don, vc[tail], NT_DIMS) + cn_ref[si, :, col])
                    dk = jnp.concatenate([dk[:qc - BLOCK], dk[tail] + _dot(ds2, qn, TN_DIMS)], axis=0)
                    dv = jnp.concatenate([dv[:qc - BLOCK], dv[tail] + _dot(p2, don, TN_DIMS)], axis=0)
                out_ref[si, :, sl] = _rope(dq, cos, sa, sb).astype(BF16)
                out_ref[si, :, GROUP_W + hh * HEAD_DIM:GROUP_W + (hh + 1) * HEAD_DIM] = (
                    _rope(dk * scale, cos, sa, sb).astype(BF16))
                out_ref[si, :, 2 * GROUP_W + hh * HEAD_DIM:2 * GROUP_W + (hh + 1) * HEAD_DIM] = dv.astype(BF16)

    main = (cs, cb * BLOCK, GROUP_W)
    edge = (cs, BLOCK, GROUP_W)
    cur = lambda off: pl.BlockSpec(main, lambda s, n: (s, n, off))
    prv = lambda off: pl.BlockSpec(edge, lambda s, n: (s, jnp.maximum(n * cb - 1, 0), off))
    nxt = lambda off: pl.BlockSpec(edge, lambda s, n: (s, jnp.minimum((n + 1) * cb, nb - 1), off))
    tab = pl.BlockSpec((cs, cb * BLOCK, HEAD_DIM), lambda s, n: (s % (d // cs), n, 0))
    out = pl.pallas_call(
        body, name=name, grid=(Bl * d // cs, nb // cb),
        in_specs=[cur(0), nxt(0), prv(1), cur(1), prv(2), cur(2),
                  cur(0), nxt(0), cur(0), nxt(0), cur(0), nxt(0), tab, tab, tab],
        out_specs=pl.BlockSpec((cs, cb * BLOCK, GROUP_QKV_W), lambda s, n: (s, n, 0)),
        out_shape=jax.ShapeDtypeStruct((Bl * d, L, GROUP_QKV_W), BF16),
    )(qv, qv, qv, qv, qv, qv, dov, dov, lv, lv, cv, cv, *tabs)
    return out.reshape(Bl * S, GROUP_QKV_W)


def _pool_w_grad(name, pooled, d_pm, gw):
    T = pooled.shape[0]
    ng = len(POOL_WINDOWS)
    tk = _tile(T, 1024, 8)
    return _mm(name, pooled, d_pm, grid=(ng, 1, T // tk),
               a_block=(tk, gw), a_map=lambda i, j, k: (k, i),
               b_block=(tk, gw), b_map=lambda i, j, k: (k, i),
               dims=TN_DIMS, acc_shape=(gw, gw),
               outs=[((ng, gw, gw), F32, (None, gw, gw), lambda i, j, k: (i, 0, 0))])[0]


def _up_w_grad(name, h2, du):
    T, D = h2.shape
    F = du.shape[2]
    tm, tn, tk = _tile(D, 1024), _tile(F, 1408), _tile(T, 1024, 8)
    njh = F // tn
    return _mm(name, h2, du, grid=(D // tm, 2 * njh, T // tk),
               a_block=(tk, tm), a_map=lambda i, j, k: (k, i),
               b_block=(None, tk, tn), b_map=lambda i, j, k: (j // njh, k, j % njh),
               dims=TN_DIMS, acc_shape=(tm, tn),
               outs=[((D, 2 * F), F32, (tm, tn), lambda i, j, k: (i, j))])[0]


def _adamw(name, w, m, v, pieces):
    R, C = w.shape
    nl = len(pieces)
    rl = R // nl
    if nl > 1 and rl % 8:
        per = [_adamw(f"{name}_{l}", w[l * rl:(l + 1) * rl], m[l * rl:(l + 1) * rl], v[l * rl:(l + 1) * rl],
                      [pieces[l]]) for l in range(nl)]
        return [jnp.concatenate([p[o] for p in per], axis=0) for o in range(4)]
    tr = _tile(rl, max(PACK_ROWS, (1 << 18) // C // PACK_ROWS * PACK_ROWS), PACK_ROWS)
    nbl = rl // tr
    c1 = 1.0 - ADAM_B1 ** ADAM_STEP
    c2 = 1.0 - ADAM_B2 ** ADAM_STEP

    def body(w_ref, m_ref, v_ref, *rest):
        p_refs = rest[:nl]
        g_ref, d_ref, mo_ref, vo_ref = rest[nl:]
        i = pl.program_id(0)
        for l in range(nl):
            @pl.when((i >= l * nbl) & (i < (l + 1) * nbl))
            def _():
                g = p_refs[l][0].astype(F32)
                for dev in range(1, N_DEV):
                    g = g + p_refs[l][dev].astype(F32)
                mn = ADAM_B1 * m_ref[...] + (1.0 - ADAM_B1) * g
                vn = ADAM_B2 * v_ref[...] + (1.0 - ADAM_B2) * (g * g)
                g_ref[...] = g
                mo_ref[...] = mn
                vo_ref[...] = vn
                d_ref[...] = -ADAM_LR * ((mn / c1) / (jnp.sqrt(vn / c2) + ADAM_EPS) + ADAM_WD * w_ref[...])

    row = pl.BlockSpec((tr, C), lambda i: (i, 0))
    piece = lambda l: pl.BlockSpec((N_DEV, tr, C), lambda i: (0, jnp.clip(i - l * nbl, 0, nbl - 1), 0))
    return pl.pallas_call(
        body, name=name, grid=(R // tr,),
        in_specs=[row, row, row] + [piece(l) for l in range(nl)],
        out_specs=[row] * 4,
        out_shape=[jax.ShapeDtypeStruct((R, C), F32)] * 4)(w, m, v, *pieces)


def _my_index():
    return 4 * lax.axis_index("x") + 2 * lax.axis_index("y") + lax.axis_index("c")


def _all_gather(name, mine):
    na = len(mine)

    def body(*refs):
        x_refs, out_refs, token = refs[:na], refs[na:2 * na], refs[2 * na]
        send_sems, recv_sems, local_sems = refs[2 * na + 1:]
        token[...] = jnp.zeros_like(token)
        x, y, c = lax.axis_index("x"), lax.axis_index("y"), lax.axis_index("c")
        me, sibling = (x, y, c), (x, y, 1 - c)
        chips = [(1 - x, y), (x, 1 - y), (1 - x, 1 - y)]

        def slot(a, px, py, pc):
            return out_refs[a].at[4 * px + 2 * py + pc]

        def copy(a, k, block, to, src=None):
            return pltpu.make_async_remote_copy(
                src_ref=slot(a, *block) if src is None else src, dst_ref=slot(a, *block),
                send_sem=send_sems.at[7 * a + k], recv_sem=recv_sems.at[7 * a + k],
                device_id=to, device_id_type=MESH_ID)

        own = [pltpu.make_async_copy(x_refs[a], slot(a, *me), local_sems.at[a]) for a in range(na)]
        for cp in own:
            cp.start()
        first = []
        for a in range(na):
            first.append(copy(a, 0, me, sibling, src=x_refs[a]))
            first += [copy(a, 1 + j, me, (*chip, c), src=x_refs[a]) for j, chip in enumerate(chips)]
        for cp in first:
            cp.start()
        passed = []
        for j, chip in enumerate(chips):
            for a in range(na):
                copy(a, 1 + j, (*chip, c), me).wait_recv()
                fwd = copy(a, 4 + j, (*chip, c), sibling)
                fwd.start()
                passed.append(fwd)
        for a in range(na):
            copy(a, 0, sibling, me).wait_recv()
            for j, chip in enumerate(chips):
                copy(a, 4 + j, (*chip, 1 - c), me).wait_recv()
        for cp in first + passed:
            cp.wait_send()
        for cp in own:
            cp.wait()

    res = pl.pallas_call(
        body, name=name,
        in_specs=[pl.BlockSpec(memory_space=pl.ANY)] * na,
        out_specs=[pl.BlockSpec(memory_space=pl.ANY)] * na + [pl.BlockSpec(memory_space=pltpu.VMEM)],
        out_shape=[jax.ShapeDtypeStruct((N_DEV,) + m.shape, m.dtype) for m in mine]
        + [jax.ShapeDtypeStruct((8, LANE), F32)],
        scratch_shapes=[pltpu.SemaphoreType.DMA((7 * na,)), pltpu.SemaphoreType.DMA((7 * na,)),
                        pltpu.SemaphoreType.DMA((na,))],
    )(*mine)
    return res[:na], res[na]


_HBM_SPEC = pl.BlockSpec(memory_space=pltpu.HBM)
_SEM_SPEC = pl.BlockSpec(memory_space=pltpu.SEMAPHORE)
_SIDE_EFFECT = pltpu.SideEffectType.DATAFLOW_SIDE_EFFECTING


def _peer_of(k):
    x, y, c = lax.axis_index("x"), lax.axis_index("y"), lax.axis_index("c")
    px = 1 - x if k & 4 else x
    py = 1 - y if k & 2 else y
    pc = 1 - c if k & 1 else c
    return (px, py, pc), 4 * px + 2 * py + pc


def _send_start(name, srcs, pieces):
    na = len(srcs)
    land_shapes = [s.shape if pieces else (N_DEV,) + s.shape for s in srcs]
    lands = [lax.empty(shp, s.dtype) for shp, s in zip(land_shapes, srcs)]

    def body(*refs):
        src_refs, land_refs = refs[:na], refs[na:2 * na]
        send_sems, recv_sems, token = refs[2 * na], refs[2 * na + 1], refs[4 * na + 2]
        me = 4 * lax.axis_index("x") + 2 * lax.axis_index("y") + lax.axis_index("c")
        for k in range(1, N_DEV):
            to, pidx = _peer_of(k)
            for a in range(na):
                pltpu.make_async_remote_copy(
                    src_ref=src_refs[a].at[pidx] if pieces else src_refs[a], dst_ref=land_refs[a].at[me],
                    send_sem=send_sems.at[7 * a + k - 1], recv_sem=recv_sems.at[7 * a + k - 1],
                    device_id=to, device_id_type=MESH_ID).start()
        token[...] = jnp.zeros_like(token)

    hbm = lambda arrs: [pltpu.HBM(a.shape, a.dtype) for a in arrs]
    outs = pl.pallas_call(
        body, name=name,
        out_shape=(pltpu.SemaphoreType.DMA((7 * na,)), pltpu.SemaphoreType.DMA((7 * na,)), *hbm(srcs), *hbm(lands),
                   jax.ShapeDtypeStruct((8, LANE), F32)),
        in_specs=[_HBM_SPEC] * (2 * na),
        out_specs=(_SEM_SPEC, _SEM_SPEC, *([_HBM_SPEC] * (2 * na)), pl.BlockSpec(memory_space=pltpu.VMEM)),
        input_output_aliases={i: 2 + i for i in range(2 * na)},
        compiler_params=pltpu.CompilerParams(has_side_effects=_SIDE_EFFECT),
    )(*[pltpu.with_memory_space_constraint(s, pltpu.HBM) for s in srcs],
      *[pltpu.with_memory_space_constraint(l, pltpu.HBM) for l in lands])
    return outs[0], outs[1], outs[2:2 + na], outs[2 + na:2 + 2 * na], outs[-1]


def _send_wait(name, send_sems, recv_sems, srcs, lands, pieces, after):
    na = len(srcs)

    def body(*refs):
        src_refs, land_refs = refs[:na], refs[na:2 * na]
        send_sems, recv_sems = refs[2 * na], refs[2 * na + 1]
        for k in range(1, N_DEV):
            to, pidx = _peer_of(k)
            for a in range(na):
                cp = pltpu.make_async_remote_copy(
                    src_ref=src_refs[a].at[pidx] if pieces else src_refs[a], dst_ref=land_refs[a].at[pidx],
                    send_sem=send_sems.at[7 * a + k - 1], recv_sem=recv_sems.at[7 * a + k - 1],
                    device_id=to, device_id_type=MESH_ID)
                cp.wait_send()
                cp.wait_recv()

    hbm = lambda arrs: [pltpu.HBM(a.shape, a.dtype) for a in arrs]
    outs = pl.pallas_call(
        body, name=name,
        out_shape=tuple(hbm(srcs) + hbm(lands)),
        in_specs=[_HBM_SPEC] * (2 * na) + [_SEM_SPEC, _SEM_SPEC, pl.BlockSpec(memory_space=pl.ANY)],
        out_specs=tuple([_HBM_SPEC] * (2 * na)),
        input_output_aliases={i: i for i in range(2 * na)},
        compiler_params=pltpu.CompilerParams(has_side_effects=_SIDE_EFFECT),
    )(*srcs, *lands, send_sems, recv_sems, after)
    return outs[:na], outs[na:]


def _own_slot(land, own):
    me = 4 * lax.axis_index("x") + 2 * lax.axis_index("y") + lax.axis_index("c")
    mine = lax.broadcasted_iota(jnp.int32, land.shape, 0) == me
    return jnp.where(mine, jnp.broadcast_to(own, land.shape), land)


def _exchange(name, pieces, bcast):
    n_p, n_b = len(pieces), len(bcast)
    na = n_p + n_b

    def body(*refs):
        src_refs, dst_refs = refs[:na], refs[na:2 * na]
        send_sems, recv_sems, local_sems = refs[2 * na:]
        x, y, c = lax.axis_index("x"), lax.axis_index("y"), lax.axis_index("c")
        me = 4 * x + 2 * y + c

        def src(a, slot):
            return src_refs[a].at[slot] if a < n_p else src_refs[a]

        own = [pltpu.make_async_copy(src(a, me), dst_refs[a].at[me], local_sems.at[a]) for a in range(na)]
        for cp in own:
            cp.start()

        def peer_of(k):
            px = 1 - x if k & 4 else x
            py = 1 - y if k & 2 else y
            pc = 1 - c if k & 1 else c
            return (px, py, pc), 4 * px + 2 * py + pc

        def copy(a, k, src_slot, dst_slot, to):
            return pltpu.make_async_remote_copy(
                src_ref=src(a, src_slot), dst_ref=dst_refs[a].at[dst_slot],
                send_sem=send_sems.at[7 * a + k - 1], recv_sem=recv_sems.at[7 * a + k - 1],
                device_id=to, device_id_type=MESH_ID)

        sent = []
        for k in range(1, N_DEV):
            to, pidx = peer_of(k)
            for a in range(na):
                cp = copy(a, k, pidx, me, to)
                cp.start()
                sent.append(cp)
        for k in range(1, N_DEV):
            to, pidx = peer_of(k)
            for a in range(na):
                copy(a, k, me, pidx, to).wait_recv()
        for cp in sent:
            cp.wait_send()
        for cp in own:
            cp.wait()

    arrays = list(pieces) + list(bcast)
    out_shape = [jax.ShapeDtypeStruct(p.shape, p.dtype) for p in pieces]
    out_shape += [jax.ShapeDtypeStruct((N_DEV,) + b.shape, b.dtype) for b in bcast]
    res = pl.pallas_call(
        body, name=name,
        in_specs=[pl.BlockSpec(memory_space=pl.ANY)] * na, out_specs=[pl.BlockSpec(memory_space=pl.ANY)] * na,
        out_shape=out_shape,
        scratch_shapes=[pltpu.SemaphoreType.DMA((7 * na,)), pltpu.SemaphoreType.DMA((7 * na,)),
                        pltpu.SemaphoreType.DMA((na,))],
    )(*arrays)
    return res[:n_p], res[n_p:]


def _pad_rows(flat, cols, row_mult):
    n = flat.shape[-1]
    unit = cols * row_mult
    padded = -(-n // unit) * unit
    pad = [(0, 0)] * (flat.ndim - 1) + [(0, padded - n)]
    return jnp.pad(flat, pad).reshape(flat.shape[:-1] + (padded // cols, cols))


def _perm_cols(w):
    aw = N_GROUPS * GROUP_W
    parts = [w[..., QKV_W:]]
    parts += [w[..., a * aw + g * GROUP_W:a * aw + (g + 1) * GROUP_W] for g in range(N_GROUPS) for a in range(3)]
    return jnp.concatenate(parts, axis=-1)


def _unperm_cols(wp, rest_w):
    qkv = wp[..., rest_w:]
    parts = [qkv[..., g * GROUP_QKV_W + a * GROUP_W:g * GROUP_QKV_W + (a + 1) * GROUP_W]
             for a in range(3) for g in range(N_GROUPS)]
    return jnp.concatenate(parts + [wp[..., :rest_w]], axis=-1)


SHARD_AXIS = dict(SHARDED)


def _layer_shards(wts, li, names, zero):
    out = []
    for n in names:
        w = wts[n][li] if zero is None else wts[n][li] + zero
        out.append(w if n in EXACT_F32 else w.astype(BF16))
    return out


def _assemble_weights(names, segs):
    W = {}
    for n, seg in zip(names, segs):
        ax = SHARD_AXIS[n]
        shp = seg.shape[1:]
        seg = jnp.moveaxis(seg, 0, ax)
        W[n] = seg.reshape(shp[:ax] + (N_DEV * shp[ax],) + shp[ax + 1:])
    if "w_in" in W:
        W["w_in"] = _perm_cols(W["w_in"])
    return W


def _grad_pieces(gr):
    out = []
    for n in gr:
        ax = SHARD_AXIS[n]
        shp = gr[n].shape
        g = gr[n].reshape(shp[:ax] + (N_DEV, shp[ax] // N_DEV) + shp[ax + 1:])
        out.append(jnp.moveaxis(g, ax, 0).astype(BF16))
    return out


def _pack_small(vals):
    flat = jnp.concatenate([vals[n].astype(F32).reshape(-1) for n in REPLICATED])
    return _pad_rows(flat, LANE, 8)


def _layer_fwd(li, x, p_l, w_in, other_weights, G, rope, Bl, S, F):
    T, D = x.shape
    rest_w = 3 * D
    sv = {"x0": x}
    W = {"w_in": w_in}
    hs = _rms_mix_fwd(f"rms_mix_{li}", x, G["g_mix"], S)
    h = hs[0]
    h_g = [h] + [a.reshape(T, D) for a in hs[1:]]
    sv["h_g"] = h_g
    zr = _mm_rows(f"rest_proj_{li}", h, W["w_in"], n_cols=rest_w)
    sv["zr"] = zr
    qkv_l, o_l, lse_l = [], [], []
    for g in range(N_GROUPS):
        qkv = _qkv_proj(f"qkv_proj_{li}_{g}", h_g[g], W["w_in"], rope[g], S, rest_w + g * GROUP_QKV_W)
        o, lse = _attn_fwd(f"attn_fwd_{li}_{g}", qkv, g, Bl, S)
        qkv_l.append(qkv)
        o_l.append(o)
        lse_l.append(lse)
    sv["qkv"], sv["o"], sv["lse"] = qkv_l, o_l, lse_l
    W.update(other_weights(o_l[-1]))
    x1, attn, pooled, mixed, y_a, y_b, merged = _mix_out_fwd(
        f"mix_out_fwd_{li}", x, zr, o_l, lse_l, W["w_ya"], W["pool_w"], G["pool_scale"], W["w_yb"], W["w_o"], S)
    sv.update(x1=x1, attn=attn, pooled=pooled, mixed=mixed, y_a=y_a, y_b=y_b, merged=merged)
    h2 = _rms_fwd(f"rms_ffn_{li}", x1, G["g_ffn"])
    u = _up_proj(f"up_proj_{li}", h2, W["w_up"], F)
    a = _ffn_act_fwd(f"ffn_act_fwd_{li}", u, W["conv_w"], G["conv_b"], S)
    x2 = _mm_rows(f"down_proj_{li}", a, W["w_down"], add=x1, chunk=512)
    sv.update(h2=h2, u=u, a=a, x2=x2)
    x3, e, pg, p_bf = _ple_fwd(f"ple_fwd_{li}", x2, p_l, G["g_ple"], W["w_ple_gate"], W["w_ple"])
    sv.update(e=e, pg=pg, p_bf=p_bf)
    return x3, sv, W


EARLY_GRADS = ("w_ple", "w_ple_gate", "w_down", "conv_w", "w_up")
LATE_GRADS = ("w_in", "w_ya", "w_yb", "pool_w", "w_o")


def _layer_bwd(li, dx3, sv, W, G, rope, Bl, S, F, send):
    T, D = dx3.shape
    rest_w = 3 * D
    gr = {}
    dx2, d_e, d_s, h3, dg = _ple_bwd(f"ple_bwd_{li}", dx3, sv["x2"], sv["e"], sv["pg"], G["g_ple"], W["w_ple_gate"])
    gr["g_ple"] = dg[0]
    gr["w_ple"] = _mm_tn(f"w_ple_grad_{li}", sv["p_bf"], d_e)
    gr["w_ple_gate"] = _mm_tn(f"w_ple_gate_grad_{li}", h3, d_s)

    d_a = _mm_rows(f"down_bwd_{li}", dx2, W["w_down"], transposed=True, chunk=1408, out_dtype=BF16)
    gr["w_down"] = _mm_tn(f"w_down_grad_{li}", sv["a"], dx2, tm=1408)
    du, d_cw, d_cb = _ffn_act_bwd(f"ffn_act_bwd_{li}", sv["u"], d_a, W["conv_w"], G["conv_b"], S)
    gr["conv_w"] = d_cw.transpose(1, 0, 2).reshape(3, 2 * F)
    gr["conv_b"] = d_cb.reshape(2 * F)
    tk_f = _tile(F, 1408)
    nkh = F // tk_f
    tm_r = _tile(T, 1024, 8)
    dx1, dg = _mm_nt_rmsbwd(f"up_bwd_{li}", du, (None, tm_r, tk_f), lambda i, j, k: (k // nkh, i, k % nkh),
                            2 * nkh, tk_f, W["w_up"], sv["x1"], G["g_ffn"], dx2)
    gr["g_ffn"] = dg[0]
    gr["w_up"] = _up_w_grad(f"w_up_grad_{li}", sv["h2"], du)
    zero = send("a", {n: gr[n] for n in EARLY_GRADS})

    (dz_rest, d_ya, d_yb, d_pm, do0, do1, do2, c0, c1, c2, dps) = _mix_out_bwd(
        f"mix_out_bwd_{li}", dx1, sv["zr"], sv["y_a"], sv["y_b"], sv["o"], sv["lse"], sv["pooled"],
        W["w_o"], W["w_ya"], W["w_yb"], W["pool_w"], G["pool_scale"] + zero, S)
    gr["pool_scale"] = dps[0]
    gr["w_o"] = _mm_tn(f"w_o_grad_{li}", sv["merged"], dx1)
    gr["w_ya"] = _mm_tn(f"w_ya_grad_{li}", sv["attn"], d_ya)
    gr["w_yb"] = _mm_tn(f"w_yb_grad_{li}", sv["mixed"], d_yb)
    gr["pool_w"] = _pool_w_grad(f"pool_w_grad_{li}", sv["pooled"], d_pm, D // len(POOL_WINDOWS))
    segs = [(dz_rest, 1)]
    for g, (do, cst) in enumerate(((do0, c0), (do1, c1), (do2, c2))):
        dqkv = _attn_bwd(f"attn_bwd_{li}_{g}", sv["qkv"][g], do, sv["lse"][g], cst, rope[g], g, Bl, S)
        segs.append((dqkv, DILATIONS[g]))

    h_rows = [sv["h_g"][0]] + sv["h_g"]
    w_in_parts = [_mm_tn(f"w_in_grad_{li}_{s}", h_rows[s], seg, tn=1536) for s, (seg, _) in enumerate(segs)]
    gr["w_in"] = _unperm_cols(jnp.concatenate(w_in_parts, axis=1), rest_w)
    zero = send("b", {n: gr[n] for n in LATE_GRADS})
    dx0, dg = _in_bwd(f"in_bwd_{li}", segs, W["w_in"], sv["x0"], G["g_mix"] + zero, dx1, S)
    gr["g_mix"] = dg[0]
    return dx0, gr


def kernel(x, p, g_mix, w_in, w_ya, w_yb, pool_w, pool_scale, w_o, g_ffn, w_up, conv_w, conv_b, w_down, g_ple, w_ple, w_ple_gate, g_final, loss_target, m_g_mix, m_w_in, m_w_ya, m_w_yb, m_pool_w, m_pool_scale, m_w_o, m_g_ffn, m_w_up, m_conv_w, m_conv_b, m_w_down, m_g_ple, m_w_ple, m_w_ple_gate, m_g_final, v_g_mix, v_w_in, v_w_ya, v_w_yb, v_pool_w, v_pool_scale, v_w_o, v_g_ffn, v_w_up, v_conv_w, v_conv_b, v_w_down, v_g_ple, v_w_ple, v_w_ple_gate, v_g_final):
    wts = dict(g_mix=g_mix, w_in=w_in, w_ya=w_ya, w_yb=w_yb, pool_w=pool_w, pool_scale=pool_scale, w_o=w_o,
               g_ffn=g_ffn, w_up=w_up, conv_w=conv_w, conv_b=conv_b, w_down=w_down, g_ple=g_ple, w_ple=w_ple,
               w_ple_gate=w_ple_gate, g_final=g_final)
    mom = dict(g_mix=m_g_mix, w_in=m_w_in, w_ya=m_w_ya, w_yb=m_w_yb, pool_w=m_pool_w, pool_scale=m_pool_scale,
               w_o=m_w_o, g_ffn=m_g_ffn, w_up=m_w_up, conv_w=m_conv_w, conv_b=m_conv_b, w_down=m_w_down,
               g_ple=m_g_ple, w_ple=m_w_ple, w_ple_gate=m_w_ple_gate, g_final=m_g_final)
    var = dict(g_mix=v_g_mix, w_in=v_w_in, w_ya=v_w_ya, w_yb=v_w_yb, pool_w=v_pool_w, pool_scale=v_pool_scale,
               w_o=v_w_o, g_ffn=v_g_ffn, w_up=v_w_up, conv_w=v_conv_w, conv_b=v_conv_b, w_down=v_w_down,
               g_ple=v_g_ple, w_ple=v_w_ple, w_ple_gate=v_w_ple_gate, g_final=v_g_final)
    Bl, S, D = x.shape
    depth = g_mix.shape[0]
    F = w_down.shape[1] * N_DEV
    T = Bl * S
    assert S % (BLOCK * DILATIONS[-1]) == 0 and D % GROUP_W == 0 and F % LANE == 0
    rope = [tuple(t if d == 1 else t.reshape(S // d, d, HEAD_DIM).transpose(1, 0, 2).reshape(S, HEAD_DIM)
                  for t in _rope_tables(S)) for d in DILATIONS]

    first, others = ("w_in",), tuple(n for n, _ in SHARDED if n != "w_in")
    got_in, tok = _all_gather("gather_w0_in", _layer_shards(wts, 0, first, None))
    gathers = {}
    for li in range(depth):
        names = others if li == 0 else first + others
        *gathers[li], tok = _send_start(f"gather_w{li}_start", _layer_shards(wts, li, names, tok[0, 0]), False)

    def gathered(li, names, after):
        shards, lands = _send_wait(f"gather_w{li}_wait", *gathers[li], False, after)
        return _assemble_weights(names, [_own_slot(l, s) for l, s in zip(lands, shards)])

    xs = x.reshape(T, D)
    saved = []
    for li in range(depth):
        G = {n: wts[n][li] for n in REPLICATED if n != "g_final"}
        if li == 0:
            G["g_mix"] = G["g_mix"] + tok[0, 0]
            w_in_full = _assemble_weights(first, got_in)["w_in"]
            rest_of = lambda after: gathered(0, others, after)
        else:
            W_all = gathered(li, first + others, xs)
            w_in_full = W_all["w_in"]
            rest_of = lambda after, W_all=W_all: W_all
        xs, sv, W = _layer_fwd(li, xs, p[li].reshape(T, -1), w_in_full, rest_of, G, rope, Bl, S, F)
        saved.append((sv, W, G))

    dx, loss_row, dg_final = _loss_bwd("loss_bwd", xs, loss_target.reshape(T, D), g_final)
    layer_grads = [None] * depth
    sends = []
    zero = [None]
    for li in reversed(range(depth)):
        sv, W, G = saved[li]
        if zero[0] is not None:
            G = dict(G, g_ple=G["g_ple"] + zero[0])

        def send(tag, group, li=li):
            *handles, tok = _send_start(f"exchange_g{li}{tag}_start", _grad_pieces(group), True)
            sends.append((li, tag, tuple(group), handles))
            zero[0] = tok[0, 0]
            return zero[0]

        dx, layer_grads[li] = _layer_bwd(li, dx, sv, W, G, rope, Bl, S, F, send)

    recv = {}
    for li, tag, names, handles in sends:
        pcs, lands = _send_wait(f"exchange_g{li}{tag}_wait", *handles, True, dx)
        for n, l, s in zip(names, lands, pcs):
            recv[(li, n)] = _own_slot(l, s)
    grads = {n: jnp.stack([layer_grads[li][n] for li in range(depth)]) for n in REPLICATED if n != "g_final"}
    grads["g_final"] = dg_final[0]
    _, (small_all,) = _exchange("exchange_small", [], [_pack_small(grads)])

    out_g, out_d, out_m, out_v = {}, {}, {}, {}
    for n, _ in SHARDED:
        shp = wts[n].shape
        two_d = (math.prod(shp[:-1]), shp[-1])
        pieces = [recv[(li, n)].reshape(N_DEV, two_d[0] // depth, two_d[1]) for li in range(depth)]
        res = _adamw(f"adamw_{n}", wts[n].reshape(two_d), mom[n].reshape(two_d), var[n].reshape(two_d), pieces)
        out_g[n], out_d[n], out_m[n], out_v[n] = [r.reshape(shp) for r in res]
    res = _adamw("adamw_replicated", _pack_small(wts), _pack_small(mom), _pack_small(var), [small_all])
    off = 0
    for n in REPLICATED:
        shp = wts[n].shape
        size = math.prod(shp)
        for dst, r in zip((out_g, out_d, out_m, out_v), res):
            dst[n] = r.reshape(-1)[off:off + size].reshape(shp)
        off += size

    loss = lax.psum(loss_row[0, 0], MESH_AXES)
    outs = [loss, dx.reshape(Bl, S, D)]
    for dct in (out_g, out_d, out_m, out_v):
        outs += [dct[n] for n in WEIGHT_ORDER]
    return tuple(outs)
```

```python
import math

import jax
import jax.numpy as jnp
from jax import lax
from jax.experimental import pallas as pl
from jax.experimental.pallas import tpu as pltpu

F32 = jnp.float32
BF16 = jnp.bfloat16

N_DEV = 8
HEAD_DIM = 128
HEADS = 4
GROUP_W = HEADS * HEAD_DIM
DILATIONS = (1, 4, 16)
N_GROUPS = len(DILATIONS)
QKV_W = 3 * N_GROUPS * GROUP_W
GROUP_QKV_W = 3 * GROUP_W
BLOCK = 128
ROPE_DIM = HEAD_DIM // 4
ROPE_HALF = ROPE_DIM // 2
ROPE_THETA = 500000.0
NEG_INF = -1e30
POOL_WINDOWS = (2, 4, 8, 16)
POOL_HALO = 16
CONV_HALO = 16
RMS_EPS = 1e-6
ADAM_LR = 0.001
ADAM_B1 = 0.9
ADAM_B2 = 0.999
ADAM_EPS = 1e-08
ADAM_WD = 0.01
ADAM_STEP = 10
LANE = 128
PACK_ROWS = 16
MESH_ID = pl.DeviceIdType.MESH
MESH_AXES = ("x", "y", "c")

NT_DIMS = (((1,), (1,)), ((), ()))
TN_DIMS = (((0,), (0,)), ((), ()))
NN_DIMS = (((1,), (0,)), ((), ()))

SHARDED = (("w_in", 1), ("w_ya", 1), ("w_yb", 0), ("pool_w", 1), ("w_o", 0), ("w_up", 1), ("conv_w", 1),
           ("w_down", 0), ("w_ple", 1), ("w_ple_gate", 0))
EXACT_F32 = ("conv_w",)
REPLICATED = ("g_mix", "pool_scale", "g_ffn", "conv_b", "g_ple", "g_final")
WEIGHT_ORDER = ("g_mix", "w_in", "w_ya", "w_yb", "pool_w", "pool_scale", "w_o", "g_ffn", "w_up", "conv_w", "conv_b",
                "w_down", "g_ple", "w_ple", "w_ple_gate", "g_final")


def _tile(n, pref, mult=LANE):
    if n <= pref:
        return n
    t = (pref // mult) * mult
    while t >= mult:
        if n % t == 0:
            return t
        t -= mult
    return n


def _sigmoid(x):
    return 1.0 / (1.0 + jnp.exp(-x))


def _dot(a, b, dims=NN_DIMS):
    return lax.dot_general(a.astype(BF16), b.astype(BF16), dims, preferred_element_type=F32)


def _rstd(x):
    return lax.rsqrt(jnp.mean(x * x, axis=-1, keepdims=True) + RMS_EPS)


def _rms_bwd(x, g, dh):
    r = _rstd(x)
    u = dh * g
    dx = r * u - x * (r * r * r) * jnp.mean(x * u, axis=-1, keepdims=True)
    return dx, dh * x * r


def _full(a):
    return pl.BlockSpec(a.shape, lambda *_: (0,) * a.ndim)


def _mm(name, a, b, *, grid, a_block, a_map, b_block, b_map, dims, acc_shape, outs, extras=(), epi=None):
    nk = grid[2]
    n_ex = len(extras)
    n_out = len(outs)

    def body(*refs):
        a_ref, b_ref = refs[0], refs[1]
        ex = refs[2:2 + n_ex]
        o = refs[2 + n_ex:2 + n_ex + n_out]
        acc = refs[2 + n_ex + n_out]
        i, j, k = pl.program_id(0), pl.program_id(1), pl.program_id(2)

        @pl.when(k == 0)
        def _():
            acc[...] = jnp.zeros_like(acc)

        acc[...] += _dot(a_ref[...], b_ref[...], dims)

        @pl.when(k == nk - 1)
        def _():
            if epi is None:
                o[0][...] = acc[...].astype(o[0].dtype)
            else:
                epi(acc[...], ex, o, i, j)

    in_specs = [pl.BlockSpec(a_block, a_map), pl.BlockSpec(b_block, b_map)]
    in_specs += [pl.BlockSpec(blk, mp) for (_, blk, mp) in extras]
    out_specs = [pl.BlockSpec(blk, mp) for (_, _, blk, mp) in outs]
    out_shape = [jax.ShapeDtypeStruct(s, d) for (s, d, _, _) in outs]
    return pl.pallas_call(
        body, name=name, grid=grid, in_specs=in_specs, out_specs=out_specs, out_shape=out_shape,
        scratch_shapes=[pltpu.VMEM(acc_shape, F32)],
    )(a, b, *[e[0] for e in extras])


def _mm_tn(name, a, b, *, tm=1024, tn=1024, tk=1024):
    K, M = a.shape
    N = b.shape[1]
    tm, tn, tk = _tile(M, tm), _tile(N, tn), _tile(K, tk, 8)
    return _mm(name, a, b, grid=(M // tm, N // tn, K // tk),
               a_block=(tk, tm), a_map=lambda i, j, k: (k, i),
               b_block=(tk, tn), b_map=lambda i, j, k: (k, j),
               dims=TN_DIMS, acc_shape=(tm, tn),
               outs=[((M, N), F32, (tm, tn), lambda i, j, k: (i, j))])[0]


def _mm_rows(name, a, w, *, transposed=False, n_cols=None, chunk=1024, out_dtype=F32, add=None, tm=512):
    M, K = a.shape
    N = w.shape[0] if transposed else (n_cols if n_cols is not None else w.shape[1])
    tm = _tile(M, tm, 8)
    chunk = _tile(N, chunk)
    w_block = (N, K) if transposed else (K, N)

    def body(*refs):
        a_ref, w_ref = refs[0], refs[1]
        o_ref = refs[-1]
        av = a_ref[...].astype(BF16)
        for c in range(N // chunk):
            cols = slice(c * chunk, (c + 1) * chunk)
            acc = _dot(av, w_ref[cols, :], NT_DIMS) if transposed else _dot(av, w_ref[:, cols])
            if add is not None:
                acc = acc + refs[2][:, cols]
            o_ref[:, cols] = acc.astype(out_dtype)

    row = lambda width: pl.BlockSpec((tm, width), lambda i: (i, 0))
    return pl.pallas_call(
        body, name=name, grid=(M // tm,),
        in_specs=[row(K), pl.BlockSpec(w_block, lambda i: (0, 0))] + ([row(N)] if add is not None else []),
        out_specs=row(N),
        out_shape=jax.ShapeDtypeStruct((M, N), out_dtype),
    )(a, w, *([add] if add is not None else []))


def _mm_nt_rmsbwd(name, a, a_block, a_map, nk, tk, w, x, g, dres):
    T, D = x.shape
    tm = a_block[-2]

    def epi(acc, ex, o, i, j):
        @pl.when(i == 0)
        def _():
            o[1][...] = jnp.zeros_like(o[1])

        dx, dgr = _rms_bwd(ex[0][...], ex[1][...], acc)
        o[0][...] = ex[2][...] + dx
        o[1][...] += jnp.sum(dgr, axis=0, keepdims=True)

    row = lambda i, j, k: (i, 0)
    vec = lambda i, j, k: (0, 0)
    return _mm(name, a, w, grid=(T // tm, 1, nk),
               a_block=a_block, a_map=a_map,
               b_block=(D, tk), b_map=lambda i, j, k: (0, k),
               dims=NT_DIMS, acc_shape=(tm, D),
               outs=[((T, D), F32, (tm, D), row), ((1, D), F32, (1, D), vec)],
               extras=[(x, (tm, D), row), (g.reshape(1, D), (1, D), vec), (dres, (tm, D), row)], epi=epi)


def _in_bwd(name, segs, w_perm, x, g, dres, S):
    T, D = x.shape
    tm = _tile(S, 512, 256)
    nst = S // tm
    tk = GROUP_QKV_W
    steps = [a.shape[1] // tk for a, _ in segs]
    starts = [sum(steps[:s]) for s in range(len(segs))]
    nk = sum(steps)
    ns = len(segs)
    cols = _chunks(D)
    assert all(a.shape[1] % tk == 0 for a, _ in segs) and nk * tk == w_perm.shape[1]

    def body(*refs):
        a_refs = refs[:ns]
        w_ref, x_ref, g_ref, dres_ref, dx_ref, dg_ref, acc, acc_tok = refs[ns:]
        i, k = pl.program_id(0), pl.program_id(1)

        @pl.when(k == 0)
        def _():
            acc[...] = jnp.zeros_like(acc)
            acc_tok[...] = jnp.zeros_like(acc_tok)

        for s in range(ns):
            d = segs[s][1]

            @pl.when((k >= starts[s]) & (k < starts[s] + steps[s]))
            def _():
                if d == 1:
                    acc_tok[...] += _dot(a_refs[s][...], w_ref[...], NT_DIMS)
                else:
                    prod = _dot(a_refs[s][...].reshape(tm, tk), w_ref[...], NT_DIMS)
                    q = tm // d
                    for c, cs in enumerate(cols):
                        for r in range(d):
                            acc[c, pl.ds(r, q, stride=d), :] += prod[r * q:(r + 1) * q, cs]

        @pl.when(k == nk - 1)
        def _():
            @pl.when(i == 0)
            def _():
                dg_ref[...] = jnp.zeros_like(dg_ref)

            dh = acc_tok[...] + jnp.concatenate([acc[c] for c in range(len(cols))], axis=1)
            dx, dgr = _rms_bwd(x_ref[...], g_ref[...], dh)
            dx_ref[...] = dres_ref[...] + dx
            dg_ref[...] += jnp.sum(dgr, axis=0, keepdims=True)

    def seg_spec(s):
        kmap = lambda k: jnp.clip(k - starts[s], 0, steps[s] - 1)
        d = segs[s][1]
        if d == 1:
            return pl.BlockSpec((tm, tk), lambda i, k: (i, kmap(k)))
        return pl.BlockSpec((None, d, tm // d, tk), lambda i, k: (i // nst, 0, i % nst, kmap(k)))

    views = [a if d == 1 else a.reshape(T // S, d, S // d, a.shape[1]) for a, d in segs]
    row = pl.BlockSpec((tm, D), lambda i, k: (i, 0))
    vec = pl.BlockSpec((1, D), lambda i, k: (0, 0))
    return pl.pallas_call(
        body, name=name, grid=(T // tm, nk),
        in_specs=[seg_spec(s) for s in range(ns)] + [pl.BlockSpec((D, tk), lambda i, k: (0, k)), row, vec, row],
        out_specs=[row, vec],
        out_shape=[jax.ShapeDtypeStruct((T, D), F32), jax.ShapeDtypeStruct((1, D), F32)],
        scratch_shapes=[pltpu.VMEM((D // LANE, tm, LANE), F32), pltpu.VMEM((tm, D), F32)],
    )(*views, w_perm, x, g.reshape(1, D), dres)


def _rope_tables(S):
    pos = jnp.arange(S, dtype=F32)
    inv_freq = jnp.exp(jnp.arange(0, ROPE_DIM, 2, dtype=F32) * (-math.log(ROPE_THETA) / ROPE_DIM))
    ang = pos[:, None] * inv_freq[None, :]
    cos, sin = jnp.cos(ang), jnp.sin(ang)
    ones = jnp.ones((S, HEAD_DIM - ROPE_DIM), F32)
    zeros_h = jnp.zeros((S, ROPE_HALF), F32)
    zeros_r = jnp.zeros((S, HEAD_DIM - ROPE_DIM), F32)
    c = jnp.concatenate([cos, cos, ones], axis=1)
    sa = jnp.concatenate([-sin, zeros_h, zeros_r], axis=1)
    sb = jnp.concatenate([zeros_h, sin, zeros_r], axis=1)
    return c, sa, sb


def _rope(t, c, sa, sb):
    return t * c + pltpu.roll(t, HEAD_DIM - ROPE_HALF, 1) * sa + pltpu.roll(t, ROPE_HALF, 1) * sb


def _rms_fwd(name, x, g):
    T, D = x.shape
    tm = _tile(T, 512, 8)

    def body(x_ref, g_ref, h_ref):
        xv = x_ref[...]
        h_ref[...] = (xv * _rstd(xv) * g_ref[...]).astype(BF16)

    return pl.pallas_call(
        body, name=name, grid=(T // tm,),
        in_specs=[pl.BlockSpec((tm, D), lambda i: (i, 0)), pl.BlockSpec((1, D), lambda i: (0, 0))],
        out_specs=pl.BlockSpec((tm, D), lambda i: (i, 0)),
        out_shape=jax.ShapeDtypeStruct((T, D), BF16))(x, g.reshape(1, D))


def _chunks(width):
    return [slice(c * LANE, (c + 1) * LANE) for c in range(width // LANE)]


def _store_residue_major(val, sc, out_ref, d, dtype):
    rows = val.shape[0]
    for c, cs in enumerate(_chunks(val.shape[1])):
        sc[c] = val[:, cs]
    for r in range(d):
        for c, cs in enumerate(_chunks(val.shape[1])):
            out_ref[r, :, cs] = sc[c, pl.ds(r, rows // d, stride=d), :].astype(dtype)


def _load_token_order(blk_ref, sc, d):
    _, q, width = blk_ref.shape
    for r in range(d):
        for c, cs in enumerate(_chunks(width)):
            sc[c, pl.ds(r, q, stride=d), :] = blk_ref[r, :, cs]
    return jnp.concatenate([sc[c] for c in range(width // LANE)], axis=1)


def _residue_major_spec(d, q, width, nst):
    return pl.BlockSpec((None, d, q, width), lambda i, *_: (i // nst, 0, i % nst, 0))


def _rms_mix_fwd(name, x, g, S):
    T, D = x.shape
    Bl = T // S
    tm = _tile(S, 512, 256)
    nst = S // tm
    dils = [d for d in DILATIONS if d > 1]

    def body(x_ref, g_ref, h_ref, *rest):
        rm_refs, sc = rest[:len(dils)], rest[len(dils)]
        xv = x_ref[...]
        hv = xv * _rstd(xv) * g_ref[...]
        h_ref[...] = hv.astype(BF16)
        for d, o_ref in zip(dils, rm_refs):
            _store_residue_major(hv, sc, o_ref, d, BF16)

    row = pl.BlockSpec((tm, D), lambda i: (i, 0))
    return pl.pallas_call(
        body, name=name, grid=(T // tm,),
        in_specs=[row, pl.BlockSpec((1, D), lambda i: (0, 0))],
        out_specs=[row] + [_residue_major_spec(d, tm // d, D, nst) for d in dils],
        out_shape=[jax.ShapeDtypeStruct((T, D), BF16)]
        + [jax.ShapeDtypeStruct((Bl, d, S // d, D), BF16) for d in dils],
        scratch_shapes=[pltpu.VMEM((D // LANE, tm, LANE), F32)])(x, g.reshape(1, D))


def _qkv_proj(name, h, w_perm, rope, S, col_off):
    T, D = h.shape
    tm = _tile(S, 512, 8)
    c_t, sa_t, sb_t = rope
    n_seq_tiles = S // tm
    jblk = col_off // GROUP_QKV_W

    def body(h_ref, w_ref, c_ref, sa_ref, sb_ref, o_ref):
        hv = h_ref[...]
        c, sa, sb = c_ref[...], sa_ref[...], sb_ref[...]
        for chunk in range(3):
            cols = slice(chunk * GROUP_W, (chunk + 1) * GROUP_W)
            acc = _dot(hv, w_ref[:, cols])
            if chunk < 2:
                for hh in range(HEADS):
                    sl = slice(hh * HEAD_DIM, (hh + 1) * HEAD_DIM)
                    o_ref[:, chunk * GROUP_W + hh * HEAD_DIM:chunk * GROUP_W + (hh + 1) * HEAD_DIM] = (
                        _rope(acc[:, sl], c, sa, sb).astype(BF16))
            else:
                o_ref[:, cols] = acc.astype(BF16)

    tab = pl.BlockSpec((tm, HEAD_DIM), lambda i: (i % n_seq_tiles, 0))
    return pl.pallas_call(
        body, name=name, grid=(T // tm,),
        in_specs=[pl.BlockSpec((tm, D), lambda i: (i, 0)), pl.BlockSpec((D, GROUP_QKV_W), lambda i: (0, jblk)),
                  tab, tab, tab],
        out_specs=pl.BlockSpec((tm, GROUP_QKV_W), lambda i: (i, 0)),
        out_shape=jax.ShapeDtypeStruct((T, GROUP_QKV_W), BF16))(h, w_perm, c_t, sa_t, sb_t)


ATTN_BLOCKS_PER_STEP = 4


def _attn_mask(rows, lead, has_prev):
    qi = lax.broadcasted_iota(jnp.int32, (rows, lead + rows), 0)
    ki = lax.broadcasted_iota(jnp.int32, (rows, lead + rows), 1)
    diff = lead + qi - ki
    band = (diff >= 0) & (diff <= BLOCK)
    return band & (has_prev | (ki >= lead)) if lead else band


def _attn_chunking(nb):
    cb = min(ATTN_BLOCKS_PER_STEP, nb)
    return cb, ATTN_BLOCKS_PER_STEP // cb


def _attn_fwd(name, qkv, g, Bl, S):
    d = DILATIONS[g]
    L = S // d
    nb = L // BLOCK
    cb, cs = _attn_chunking(nb)
    qv = qkv.reshape(Bl * d, L, GROUP_QKV_W)
    scale = HEAD_DIM ** -0.5

    def body(q_ref, kc_ref, vc_ref, kp_ref, vp_ref, o_ref, l_ref):
        n = pl.program_id(1)
        for si in range(cs):
            for bi in range(cb):
                rows = slice(bi * BLOCK, (bi + 1) * BLOCK)
                before = slice((bi - 1) * BLOCK, bi * BLOCK)
                valid = _attn_mask(BLOCK, BLOCK, n > 0 if bi == 0 else True)
                for hh in range(HEADS):
                    sl = slice(hh * HEAD_DIM, (hh + 1) * HEAD_DIM)
                    kp = kp_ref[si, :, sl] if bi == 0 else kc_ref[si, before, sl]
                    vp = vp_ref[si, :, sl] if bi == 0 else vc_ref[si, before, sl]
                    kk = jnp.concatenate([kp, kc_ref[si, rows, sl]], axis=0)
                    vv = jnp.concatenate([vp, vc_ref[si, rows, sl]], axis=0)
                    s = jnp.where(valid, _dot(q_ref[si, rows, sl], kk, NT_DIMS) * scale, NEG_INF)
                    m = jnp.max(s, axis=-1, keepdims=True)
                    p = jnp.exp(s - m)
                    l = jnp.sum(p, axis=-1, keepdims=True)
                    o_ref[si, rows, sl] = _dot(p, vv) / l
                    l_ref[si, rows, sl] = jnp.broadcast_to(m + jnp.log(l), (BLOCK, HEAD_DIM))

    main = (cs, cb * BLOCK, GROUP_W)
    edge = (cs, BLOCK, GROUP_W)
    cur = lambda off: pl.BlockSpec(main, lambda s, n: (s, n, off))
    prev = lambda off: pl.BlockSpec(edge, lambda s, n: (s, jnp.maximum(n * cb - 1, 0), off))
    out = pl.BlockSpec(main, lambda s, n: (s, n, 0))
    return pl.pallas_call(
        body, name=name, grid=(Bl * d // cs, nb // cb),
        in_specs=[cur(0), cur(1), cur(2), prev(1), prev(2)],
        out_specs=[out, out],
        out_shape=[jax.ShapeDtypeStruct((Bl * d, L, GROUP_W), F32)] * 2)(qv, qv, qv, qv, qv)


def _merge_weights(l0, l1, l2):
    mx = jnp.maximum(jnp.maximum(l0, l1), l2)
    e0, e1, e2 = jnp.exp(l0 - mx), jnp.exp(l1 - mx), jnp.exp(l2 - mx)
    inv = 1.0 / (e0 + e1 + e2)
    return e0 * inv, e1 * inv, e2 * inv


def _group_specs(tm, nst, S):
    specs = []
    for d in DILATIONS:
        specs.append(pl.BlockSpec((tm, GROUP_W), lambda i: (i, 0)) if d == 1
                     else _residue_major_spec(d, tm // d, GROUP_W, nst))

    def views(arrs):
        out = []
        for d, a in zip(DILATIONS, arrs):
            out.append(a.reshape(-1, GROUP_W) if d == 1 else a.reshape(-1, d, S // d, GROUP_W))
        return out

    return specs, views


def _group_scratch(tm, per_group):
    n = per_group * sum(1 for d in DILATIONS if d > 1)
    return [pltpu.VMEM((GROUP_W // LANE, tm, LANE), F32) for _ in range(n)]


def _group_values(o_refs, l_refs, scs):
    scs = list(scs)
    ov, lv = [], []
    for d, o_ref, l_ref in zip(DILATIONS, o_refs, l_refs):
        if d == 1:
            ov.append(o_ref[...])
            lv.append(l_ref[...])
        else:
            ov.append(_load_token_order(o_ref, scs.pop(), d))
            lv.append(_load_token_order(l_ref, scs.pop(), d))
    return ov, lv


def _pool_inv_count(tseq, w):
    return 1.0 / jnp.minimum(tseq + 1, w).astype(F32)


def _mix_out_fwd(name, x, zr, o_l, lse_l, w_ya, pool_w, pool_scale, w_yb, w_o, S):
    T, D = x.shape
    gw = D // len(POOL_WINDOWS)
    tm = _tile(S, 256, POOL_HALO)
    nst = S // tm
    hpt = tm // POOL_HALO

    def body(x_ref, u_ref, uh_ref, ga_ref, gb_ref, o0, o1, o2, l0, l1, l2, wya_ref, pw_ref, ps_ref, wyb_ref, wo_ref,
             x1_ref, attn_ref, pooled_ref, mixed_ref, ya_ref, yb_ref, merged_ref, *scs):
        it = pl.program_id(0) % nst
        ov, lv = _group_values((o0, o1, o2), (l0, l1, l2), scs)
        w0, w1, w2 = _merge_weights(*lv)
        attn = w0 * ov[0] + w1 * ov[1] + w2 * ov[2]
        attn_ref[...] = attn.astype(BF16)
        y_a = _dot(attn, wya_ref[...])

        u = u_ref[...]
        halo = uh_ref[...] * jnp.where(it == 0, 0.0, 1.0)
        ext = jnp.concatenate([halo, u], axis=0)
        tseq = it * tm + lax.broadcasted_iota(jnp.int32, (tm, 1), 0)
        pm_parts = []
        for gi, w in enumerate(POOL_WINDOWS):
            cs = slice(gi * gw, (gi + 1) * gw)
            s = ext[:, cs]
            step = 1
            while step < w:
                s = s + pltpu.roll(s, step, 0)
                step *= 2
            pooled_g = (s[POOL_HALO:, :] * _pool_inv_count(tseq, w) - u[:, cs]).astype(BF16)
            pooled_ref[:, cs] = pooled_g
            pm_parts.append(_dot(pooled_g, pw_ref[gi]))
        mixed = (jnp.concatenate(pm_parts, axis=1) * ps_ref[...]).astype(BF16)
        mixed_ref[...] = mixed
        y_b = _dot(mixed, wyb_ref[...])
        merged = (_sigmoid(ga_ref[...]) * y_a + _sigmoid(gb_ref[...]) * y_b).astype(BF16)
        ya_ref[...] = y_a.astype(BF16)
        yb_ref[...] = y_b.astype(BF16)
        merged_ref[...] = merged
        x1_ref[...] = x_ref[...] + _dot(merged, wo_ref[...])

    row = lambda c: pl.BlockSpec((tm, D), lambda i: (i, c))
    row512 = pl.BlockSpec((tm, GROUP_W), lambda i: (i, 0))
    ps = pool_scale.reshape(1, D)
    halo_spec = pl.BlockSpec((POOL_HALO, D), lambda i: (jnp.maximum(i * hpt - 1, 0), 0))
    grp_specs, grp_views = _group_specs(tm, nst, S)
    return pl.pallas_call(
        body, name=name, grid=(T // tm,),
        in_specs=[row(0), row(0), halo_spec, row(1), row(2)] + grp_specs * 2
        + [_full(w_ya), _full(pool_w), _full(ps), _full(w_yb), _full(w_o)],
        out_specs=[row(0), row512, row(0), row(0), row(0), row(0), row(0)],
        out_shape=[jax.ShapeDtypeStruct((T, D), F32), jax.ShapeDtypeStruct((T, GROUP_W), BF16),
                   jax.ShapeDtypeStruct((T, D), BF16), jax.ShapeDtypeStruct((T, D), BF16),
                   jax.ShapeDtypeStruct((T, D), BF16), jax.ShapeDtypeStruct((T, D), BF16),
                   jax.ShapeDtypeStruct((T, D), BF16)],
        scratch_shapes=_group_scratch(tm, 2),
    )(x, zr, zr, zr, zr, *grp_views(o_l), *grp_views(lse_l), w_ya, pool_w, ps, w_yb, w_o)


def _up_proj(name, h2, w_up, F):
    T, D = h2.shape
    tm = _tile(T, 512, 8)
    chunk = _tile(F, 1408)

    def body(h_ref, w_ref, u_ref):
        hv = h_ref[...]
        for half in range(2):
            for c in range(F // chunk):
                cols = slice(c * chunk, (c + 1) * chunk)
                u_ref[half, :, cols] = _dot(hv, w_ref[:, half * F + c * chunk:half * F + (c + 1) * chunk]).astype(BF16)

    return pl.pallas_call(
        body, name=name, grid=(T // tm,),
        in_specs=[pl.BlockSpec((tm, D), lambda i: (i, 0)), _full(w_up)],
        out_specs=pl.BlockSpec((2, tm, F), lambda i: (0, i, 0)),
        out_shape=jax.ShapeDtypeStruct((2, T, F), BF16))(h2, w_up)


def _conv_y(ext, w_ref, b_ref):
    return (b_ref[...] + w_ref[2:3, :] * ext + w_ref[1:2, :] * pltpu.roll(ext, 1, 0)
            + w_ref[0:1, :] * pltpu.roll(ext, 2, 0))


def _conv_params(conv_w, conv_b, F):
    cw = conv_w.reshape(3, 2, F).transpose(1, 0, 2)
    return cw, conv_b.reshape(2, 1, F)


def _ffn_act_fwd(name, u, conv_w, conv_b, S):
    _, T, F = u.shape
    tm = _tile(S, 512, CONV_HALO)
    tf = _tile(F, 1408)
    nst = S // tm
    hpt = tm // CONV_HALO
    cw, cb = _conv_params(conv_w, conv_b, F)

    def body(ug_ref, uv_ref, hg_ref, hv_ref, wg_ref, wv_ref, bg_ref, bv_ref, a_ref):
        keep = jnp.where(pl.program_id(0) % nst == 0, 0.0, 1.0)
        ext = lambda h_ref, u_ref: jnp.concatenate([h_ref[...].astype(F32) * keep, u_ref[...].astype(F32)], axis=0)
        yg = _conv_y(ext(hg_ref, ug_ref), wg_ref, bg_ref)[CONV_HALO:, :]
        yv = _conv_y(ext(hv_ref, uv_ref), wv_ref, bv_ref)[CONV_HALO:, :]
        a_ref[...] = (yg * _sigmoid(yg) * yv).astype(BF16)

    main = lambda h: pl.BlockSpec((None, tm, tf), lambda i, j: (h, i, j))
    halo = lambda h: pl.BlockSpec((None, CONV_HALO, tf), lambda i, j: (h, jnp.maximum(i * hpt - 1, 0), j))
    wsp = lambda h: pl.BlockSpec((None, 3, tf), lambda i, j: (h, 0, j))
    bsp = lambda h: pl.BlockSpec((None, 1, tf), lambda i, j: (h, 0, j))
    return pl.pallas_call(
        body, name=name, grid=(T // tm, F // tf),
        in_specs=[main(0), main(1), halo(0), halo(1), wsp(0), wsp(1), bsp(0), bsp(1)],
        out_specs=pl.BlockSpec((tm, tf), lambda i, j: (i, j)),
        out_shape=jax.ShapeDtypeStruct((T, F), BF16))(u, u, u, u, cw, cw, cb, cb)


def _ple_fwd(name, x2, p, g_ple, w_gate, w_ple):
    T, D = x2.shape
    P = p.shape[1]
    tm = _tile(T, 512, 8)

    def body(x_ref, p_ref, g_ref, wg_ref, wp_ref, x3_ref, e_ref, pg_ref, pbf_ref):
        xv = x_ref[...]
        h3 = xv * _rstd(xv) * g_ref[...]
        pg = _sigmoid(_dot(h3, wg_ref[...]))
        pb = p_ref[...].astype(BF16)
        e = _dot(pb, wp_ref[...])
        x3_ref[...] = xv + e * pg
        e_ref[...] = e.astype(BF16)
        pg_ref[...] = pg.astype(BF16)
        pbf_ref[...] = pb

    row = pl.BlockSpec((tm, D), lambda i: (i, 0))
    prow = pl.BlockSpec((tm, P), lambda i: (i, 0))
    g2 = g_ple.reshape(1, D)
    return pl.pallas_call(
        body, name=name, grid=(T // tm,),
        in_specs=[row, prow, _full(g2), _full(w_gate), _full(w_ple)],
        out_specs=[row, row, row, prow],
        out_shape=[jax.ShapeDtypeStruct((T, D), F32)] + [jax.ShapeDtypeStruct((T, D), BF16)] * 2
        + [jax.ShapeDtypeStruct((T, P), BF16)],
    )(x2, p, g2, w_gate, w_ple)


def _loss_bwd(name, xf, target, g_final):
    T, D = xf.shape
    tm = _tile(T, 512, 8)
    nt = T // tm

    def body(x_ref, t_ref, g_ref, dx_ref, loss_ref, dg_ref, lacc):
        i = pl.program_id(0)

        @pl.when(i == 0)
        def _():
            lacc[...] = jnp.zeros_like(lacc)
            dg_ref[...] = jnp.zeros_like(dg_ref)
            loss_ref[...] = jnp.zeros_like(loss_ref)

        xv = x_ref[...]
        g = g_ref[...]
        diff = xv * _rstd(xv) * g - t_ref[...]
        lacc[...] += jnp.sum(diff * diff, axis=0, keepdims=True)
        dx, dgr = _rms_bwd(xv, g, diff * (1.0 / D))
        dx_ref[...] = dx
        dg_ref[...] += jnp.sum(dgr, axis=0, keepdims=True)

        @pl.when(i == nt - 1)
        def _():
            tot = jnp.sum(lacc[...], axis=-1, keepdims=True) * (0.5 / D)
            loss_ref[...] = jnp.broadcast_to(tot, (1, LANE))

    row = pl.BlockSpec((tm, D), lambda i: (i, 0))
    vec = pl.BlockSpec((1, D), lambda i: (0, 0))
    return pl.pallas_call(
        body, name=name, grid=(nt,),
        in_specs=[row, row, vec],
        out_specs=[row, pl.BlockSpec((1, LANE), lambda i: (0, 0)), vec],
        out_shape=[jax.ShapeDtypeStruct((T, D), F32), jax.ShapeDtypeStruct((1, LANE), F32),
                   jax.ShapeDtypeStruct((1, D), F32)],
        scratch_shapes=[pltpu.VMEM((1, D), F32)])(xf, target, g_final.reshape(1, D))


def _ple_bwd(name, dx3, x2, e, pg, g_ple, w_gate):
    T, D = x2.shape
    tm = _tile(T, 512, 8)

    def body(dx3_ref, x_ref, e_ref, pg_ref, g_ref, wg_ref, dx2_ref, de_ref, ds_ref, h3_ref, dg_ref):
        @pl.when(pl.program_id(0) == 0)
        def _():
            dg_ref[...] = jnp.zeros_like(dg_ref)

        dx3v, xv, pgv, g = dx3_ref[...], x_ref[...], pg_ref[...].astype(F32), g_ref[...]
        de_ref[...] = (dx3v * pgv).astype(BF16)
        ds = (dx3v * e_ref[...].astype(F32) * pgv * (1.0 - pgv)).astype(BF16)
        ds_ref[...] = ds
        dh3 = _dot(ds, wg_ref[...], NT_DIMS)
        h3_ref[...] = (xv * _rstd(xv) * g).astype(BF16)
        dx, dgr = _rms_bwd(xv, g, dh3)
        dx2_ref[...] = dx3v + dx
        dg_ref[...] += jnp.sum(dgr, axis=0, keepdims=True)

    row = pl.BlockSpec((tm, D), lambda i: (i, 0))
    vec = pl.BlockSpec((1, D), lambda i: (0, 0))
    return pl.pallas_call(
        body, name=name, grid=(T // tm,),
        in_specs=[row, row, row, row, vec, _full(w_gate)],
        out_specs=[row, row, row, row, vec],
        out_shape=[jax.ShapeDtypeStruct((T, D), F32)] + [jax.ShapeDtypeStruct((T, D), BF16)] * 3
        + [jax.ShapeDtypeStruct((1, D), F32)])(dx3, x2, e, pg, g_ple.reshape(1, D), w_gate)


def _ffn_act_bwd(name, u, d_a, conv_w, conv_b, S):
    _, T, F = u.shape
    H = CONV_HALO
    tm = _tile(S, 512, H)
    tf = _tile(F, 1408)
    nst = S // tm
    hpt = tm // H
    last_halo = T // H - 1
    n_ext = tm + H
    cw, cb = _conv_params(conv_w, conv_b, F)

    def body(ug_ref, uv_ref, pg_ref, pv_ref, ng_ref, nv_ref, da_ref, dan_ref, wg_ref, wv_ref, bg_ref, bv_ref,
             du_ref, dw_ref, db_ref):
        i = pl.program_id(1)
        it = i % nst

        @pl.when(i == 0)
        def _():
            dw_ref[...] = jnp.zeros_like(dw_ref)
            db_ref[...] = jnp.zeros_like(db_ref)

        keep_prev = jnp.where(it == 0, 0.0, 1.0)
        keep_next = jnp.where(it == nst - 1, 0.0, 1.0)
        def shifted(p_ref, u_ref, n_ref):
            ext = jnp.concatenate([p_ref[...].astype(F32) * keep_prev, u_ref[...].astype(F32),
                                   n_ref[...].astype(F32)], axis=0)
            return ext[H:, :], pltpu.roll(ext, 1, 0)[H:, :], pltpu.roll(ext, 2, 0)[H:, :]

        us_g, us_v = shifted(pg_ref, ug_ref, ng_ref), shifted(pv_ref, uv_ref, nv_ref)
        conv = lambda us, w_ref, b_ref: b_ref[...] + w_ref[2:3, :] * us[0] + w_ref[1:2, :] * us[1] + w_ref[0:1, :] * us[2]
        yg, yv = conv(us_g, wg_ref, bg_ref), conv(us_v, wv_ref, bv_ref)
        da = jnp.concatenate([da_ref[...].astype(F32), dan_ref[...].astype(F32) * keep_next], axis=0)
        sg = _sigmoid(yg)
        silu = yg * sg
        dyv = da * silu
        dyg = (da * yv) * (sg + silu * (1.0 - sg))
        for half, (dy, us, w_ref) in enumerate(((dyg, us_g, wg_ref), (dyv, us_v, wv_ref))):
            du = (w_ref[2:3, :] * dy + w_ref[1:2, :] * pltpu.roll(dy, n_ext - 1, 0)
                  + w_ref[0:1, :] * pltpu.roll(dy, n_ext - 2, 0))
            du_ref[half] = du[:tm, :].astype(BF16)
            dym = dy[:tm, :]
            db_ref[half] += jnp.sum(dym, axis=0, keepdims=True)
            for tap in range(3):
                dw_ref[half, tap:tap + 1, :] += jnp.sum(dym * us[2 - tap][:tm, :], axis=0, keepdims=True)

    main = lambda h: pl.BlockSpec((None, tm, tf), lambda j, i: (h, i, j))
    prev = lambda h: pl.BlockSpec((None, H, tf), lambda j, i: (h, jnp.maximum(i * hpt - 1, 0), j))
    nxt = lambda h: pl.BlockSpec((None, H, tf), lambda j, i: (h, jnp.minimum((i + 1) * hpt, last_halo), j))
    wsp = lambda h: pl.BlockSpec((None, 3, tf), lambda j, i: (h, 0, j))
    bsp = lambda h: pl.BlockSpec((None, 1, tf), lambda j, i: (h, 0, j))
    return pl.pallas_call(
        body, name=name, grid=(F // tf, T // tm),
        in_specs=[main(0), main(1), prev(0), prev(1), nxt(0), nxt(1),
                  pl.BlockSpec((tm, tf), lambda j, i: (i, j)),
                  pl.BlockSpec((H, tf), lambda j, i: (jnp.minimum((i + 1) * hpt, last_halo), j)),
                  wsp(0), wsp(1), bsp(0), bsp(1)],
        out_specs=[pl.BlockSpec((2, tm, tf), lambda j, i: (0, i, j)),
                   pl.BlockSpec((2, 3, tf), lambda j, i: (0, 0, j)),
                   pl.BlockSpec((2, 1, tf), lambda j, i: (0, 0, j))],
        out_shape=[jax.ShapeDtypeStruct((2, T, F), BF16), jax.ShapeDtypeStruct((2, 3, F), F32),
                   jax.ShapeDtypeStruct((2, 1, F), F32)],
    )(u, u, u, u, u, u, d_a, d_a, cw, cw, cb, cb)


def _mix_out_bwd(name, dx1, zr, y_a, y_b, o_l, lse_l, pooled, w_o, w_ya, w_yb, pool_w, pool_scale, S):
    T, D = dx1.shape
    gw = D // len(POOL_WINDOWS)
    H = POOL_HALO
    tm = _tile(S, 256, H)
    nst = S // tm
    hpt = tm // H
    last_halo = T // H - 1
    n_ext = tm + H

    def body(dx_ref, dxn_ref, ga_ref, gb_ref, gbn_ref, ya_ref, yb_ref, o0, o1, o2, l0, l1, l2, pooled_ref,
             wo_ref, wya_ref, wyb_ref, pw_ref, ps_ref,
             dz_ref, dya_ref, dyb_ref, dpm_ref, do0, do1, do2, c0, c1, c2, dps_ref, *scs):
        i = pl.program_id(0)
        it = i % nst

        @pl.when(i == 0)
        def _():
            dps_ref[...] = jnp.zeros_like(dps_ref)

        keep_next = jnp.where(it == nst - 1, 0.0, 1.0)
        dm_e = _dot(jnp.concatenate([dx_ref[...], dxn_ref[...]], axis=0), wo_ref[...], NT_DIMS)
        sgb_e = _sigmoid(jnp.concatenate([gb_ref[...], gbn_ref[...]], axis=0))
        dyb_e = dm_e * sgb_e
        dm = dm_e[:tm, :]
        sga = _sigmoid(ga_ref[...])
        sgb = sgb_e[:tm, :]
        d_ga = dm * ya_ref[...].astype(F32) * (sga * (1.0 - sga))
        d_gb = dm * yb_ref[...].astype(F32) * (sgb * (1.0 - sgb))
        dya = (dm * sga).astype(BF16)
        dya_ref[...] = dya
        dyb_ref[...] = dyb_e[:tm, :].astype(BF16)
        dmixed_e = _dot(dyb_e, wyb_ref[...], NT_DIMS)

        rows = lax.broadcasted_iota(jnp.int32, (n_ext, 1), 0)
        tseq = it * tm + rows
        live = jnp.where(rows < tm, 1.0, keep_next)
        ps = ps_ref[...]
        du_parts = []
        for gi, w in enumerate(POOL_WINDOWS):
            cs = slice(gi * gw, (gi + 1) * gw)
            pm_g = _dot(pooled_ref[:, cs], pw_ref[gi])
            dps_ref[:, cs] += jnp.sum(dmixed_e[:tm, cs] * pm_g, axis=0, keepdims=True)
            dpm_e = (dmixed_e[:, cs] * ps[:, cs]).astype(BF16)
            dpm_ref[:, cs] = dpm_e[:tm, :]
            dpooled_e = _dot(dpm_e, pw_ref[gi], NT_DIMS)
            s = dpooled_e * (_pool_inv_count(tseq, w) * live)
            step = 1
            while step < w:
                s = s + pltpu.roll(s, n_ext - step, 0)
                step *= 2
            du_parts.append(s[:tm, :] - dpooled_e[:tm, :])
        dz_ref[...] = jnp.concatenate(du_parts + [d_ga, d_gb], axis=1).astype(BF16)

        d_attn = _dot(dya, wya_ref[...], NT_DIMS)
        ov, lv = _group_values((o0, o1, o2), (l0, l1, l2), scs[:n_in_sc])
        ws = _merge_weights(*lv)
        prod = d_attn * (ws[0] * ov[0] + ws[1] * ov[1] + ws[2] * ov[2])
        rs = jnp.concatenate(
            [jnp.broadcast_to(jnp.sum(prod[:, hh * HEAD_DIM:(hh + 1) * HEAD_DIM], axis=-1, keepdims=True),
                              (tm, HEAD_DIM)) for hh in range(HEADS)], axis=1)
        out_scs = list(scs[n_in_sc:])
        for d, wg, do_ref, c_ref in zip(DILATIONS, ws, (do0, do1, do2), (c0, c1, c2)):
            if d == 1:
                do_ref[...] = (wg * d_attn).astype(BF16)
                c_ref[...] = -wg * rs
            else:
                _store_residue_major(wg * d_attn, out_scs.pop(), do_ref, d, BF16)
                _store_residue_major(-wg * rs, out_scs.pop(), c_ref, d, F32)

    row = lambda c: pl.BlockSpec((tm, D), lambda i: (i, c))
    nxt = lambda c: pl.BlockSpec((H, D), lambda i: (jnp.minimum((i + 1) * hpt, last_halo), c))
    ps2 = pool_scale.reshape(1, D)
    bf = lambda w: jax.ShapeDtypeStruct((T, w), BF16)
    grp_specs, grp_views = _group_specs(tm, nst, S)
    grp_shape = lambda dt: [jax.ShapeDtypeStruct((T, GROUP_W) if d == 1 else (T // S, d, S // d, GROUP_W), dt)
                            for d in DILATIONS]
    n_in_sc = len(_group_scratch(tm, 2))
    return pl.pallas_call(
        body, name=name, grid=(T // tm,),
        in_specs=[row(0), nxt(0), row(1), row(2), nxt(2), row(0), row(0)] + grp_specs * 2 + [row(0)]
        + [_full(w_o), _full(w_ya), _full(w_yb), _full(pool_w), _full(ps2)],
        out_specs=[pl.BlockSpec((tm, 3 * D), lambda i: (i, 0)), row(0), row(0), row(0)] + grp_specs * 2
        + [pl.BlockSpec((1, D), lambda i: (0, 0))],
        out_shape=[bf(3 * D), bf(D), bf(D), bf(D)] + grp_shape(BF16) + grp_shape(F32)
        + [jax.ShapeDtypeStruct((1, D), F32)],
        scratch_shapes=_group_scratch(tm, 4),
    )(dx1, dx1, zr, zr, zr, y_a, y_b, *grp_views(o_l), *grp_views(lse_l), pooled, w_o, w_ya, w_yb, pool_w, ps2)


def _attn_bwd(name, qkv, d_o, lse, cst, rope, g, Bl, S):
    d = DILATIONS[g]
    L = S // d
    nb = L // BLOCK
    qv = qkv.reshape(Bl * d, L, GROUP_QKV_W)
    dov = d_o.reshape(Bl * d, L, GROUP_W)
    lv = lse.reshape(Bl * d, L, GROUP_W)
    cv = cst.reshape(Bl * d, L, GROUP_W)
    tabs = [t.reshape(d, L, HEAD_DIM) for t in rope]
    scale = HEAD_DIM ** -0.5

    cb, cs = _attn_chunking(nb)
    qc = cb * BLOCK
    lead = BLOCK if nb > 1 else 0

    def body(q_ref, qn_ref, kp_ref, kc_ref, vp_ref, vc_ref, do_ref, don_ref, l_ref, ln_ref, c_ref, cn_ref,
             cos_ref, sa_ref, sb_ref, out_ref):
        n = pl.program_id(1)
        valid = _attn_mask(qc, lead, n > 0)
        qi = lax.broadcasted_iota(jnp.int32, (BLOCK, BLOCK), 0)
        ki = lax.broadcasted_iota(jnp.int32, (BLOCK, BLOCK), 1)
        valid_n = (ki >= qi) & ((n + 1) * cb < nb)
        tail = slice(qc - BLOCK, qc)
        for si in range(cs):
            cos, sa, sb = cos_ref[si], -sa_ref[si], -sb_ref[si]
            for hh in range(HEADS):
                sl = slice(hh * HEAD_DIM, (hh + 1) * HEAD_DIM)
                col = slice(hh * HEAD_DIM, hh * HEAD_DIM + 1)
                q, kc, vc, do = q_ref[si, :, sl], kc_ref[si, :, sl], vc_ref[si, :, sl], do_ref[si, :, sl]
                kk, vv = kc, vc
                if lead:
                    kk = jnp.concatenate([kp_ref[si, :, sl], kc], axis=0)
                    vv = jnp.concatenate([vp_ref[si, :, sl], vc], axis=0)
                s = jnp.where(valid, _dot(q, kk, NT_DIMS) * scale, NEG_INF)
                p = jnp.exp(s - l_ref[si, :, col])
                ds = p * (_dot(do, vv, NT_DIMS) + c_ref[si, :, col])
                dq = _dot(ds, kk) * scale
                dk = _dot(ds[:, lead:], q, TN_DIMS)
                dv = _dot(p[:, lead:], do, TN_DIMS)
                if nb > cb:
                    qn, don = qn_ref[si, :, sl], don_ref[si, :, sl]
                    s2 = jnp.where(valid_n, _dot(qn, kc[tail], NT_DIMS) * scale, NEG_INF)
                    p2 = jnp.exp(s2 - ln_ref[si, :, col])
                    ds2 = p2 * (_dot(don, vc[tail], NT_DIMS) + cn_ref[si, :, col])
                    dk = jnp.concatenate([dk[:qc - BLOCK], dk[tail] + _dot(ds2, qn, TN_DIMS)], axis=0)
                    dv = jnp.concatenate([dv[:qc - BLOCK], dv[tail] + _dot(p2, don, TN_DIMS)], axis=0)
                out_ref[si, :, sl] = _rope(dq, cos, sa, sb).astype(BF16)
                out_ref[si, :, GROUP_W + hh * HEAD_DIM:GROUP_W + (hh + 1) * HEAD_DIM] = (
                    _rope(dk * scale, cos, sa, sb).astype(BF16))
                out_ref[si, :, 2 * GROUP_W + hh * HEAD_DIM:2 * GROUP_W + (hh + 1) * HEAD_DIM] = dv.astype(BF16)

    main = (cs, cb * BLOCK, GROUP_W)
    edge = (cs, BLOCK, GROUP_W)
    cur = lambda off: pl.BlockSpec(main, lambda s, n: (s, n, off))
    prv = lambda off: pl.BlockSpec(edge, lambda s, n: (s, jnp.maximum(n * cb - 1, 0), off))
    nxt = lambda off: pl.BlockSpec(edge, lambda s, n: (s, jnp.minimum((n + 1) * cb, nb - 1), off))
    tab = pl.BlockSpec((cs, cb * BLOCK, HEAD_DIM), lambda s, n: (s % (d // cs), n, 0))
    out = pl.pallas_call(
        body, name=name, grid=(Bl * d // cs, nb // cb),
        in_specs=[cur(0), nxt(0), prv(1), cur(1), prv(2), cur(2),
                  cur(0), nxt(0), cur(0), nxt(0), cur(0), nxt(0), tab, tab, tab],
        out_specs=pl.BlockSpec((cs, cb * BLOCK, GROUP_QKV_W), lambda s, n: (s, n, 0)),
        out_shape=jax.ShapeDtypeStruct((Bl * d, L, GROUP_QKV_W), BF16),
    )(qv, qv, qv, qv, qv, qv, dov, dov, lv, lv, cv, cv, *tabs)
    return out.reshape(Bl * S, GROUP_QKV_W)


def _pool_w_grad(name, pooled, d_pm, gw):
    T = pooled.shape[0]
    ng = len(POOL_WINDOWS)
    tk = _tile(T, 1024, 8)
    return _mm(name, pooled, d_pm, grid=(ng, 1, T // tk),
               a_block=(tk, gw), a_map=lambda i, j, k: (k, i),
               b_block=(tk, gw), b_map=lambda i, j, k: (k, i),
               dims=TN_DIMS, acc_shape=(gw, gw),
               outs=[((ng, gw, gw), F32, (None, gw, gw), lambda i, j, k: (i, 0, 0))])[0]


def _up_w_grad(name, h2, du):
    T, D = h2.shape
    F = du.shape[2]
    tm, tn, tk = _tile(D, 1024), _tile(F, 1408), _tile(T, 1024, 8)
    njh = F // tn
    return _mm(name, h2, du, grid=(D // tm, 2 * njh, T // tk),
               a_block=(tk, tm), a_map=lambda i, j, k: (k, i),
               b_block=(None, tk, tn), b_map=lambda i, j, k: (j // njh, k, j % njh),
               dims=TN_DIMS, acc_shape=(tm, tn),
               outs=[((D, 2 * F), F32, (tm, tn), lambda i, j, k: (i, j))])[0]


def _adamw(name, w, m, v, pieces):
    R, C = w.shape
    nl = len(pieces)
    rl = R // nl
    if nl > 1 and rl % 8:
        per = [_adamw(f"{name}_{l}", w[l * rl:(l + 1) * rl], m[l * rl:(l + 1) * rl], v[l * rl:(l + 1) * rl],
                      [pieces[l]]) for l in range(nl)]
        return [jnp.concatenate([p[o] for p in per], axis=0) for o in range(4)]
    tr = _tile(rl, max(PACK_ROWS, (1 << 18) // C // PACK_ROWS * PACK_ROWS), PACK_ROWS)
    nbl = rl // tr
    c1 = 1.0 - ADAM_B1 ** ADAM_STEP
    c2 = 1.0 - ADAM_B2 ** ADAM_STEP

    def body(w_ref, m_ref, v_ref, *rest):
        p_refs = rest[:nl]
        g_ref, d_ref, mo_ref, vo_ref = rest[nl:]
        i = pl.program_id(0)
        for l in range(nl):
            @pl.when((i >= l * nbl) & (i < (l + 1) * nbl))
            def _():
                g = p_refs[l][0].astype(F32)
                for dev in range(1, N_DEV):
                    g = g + p_refs[l][dev].astype(F32)
                mn = ADAM_B1 * m_ref[...] + (1.0 - ADAM_B1) * g
                vn = ADAM_B2 * v_ref[...] + (1.0 - ADAM_B2) * (g * g)
                g_ref[...] = g
                mo_ref[...] = mn
                vo_ref[...] = vn
                d_ref[...] = -ADAM_LR * ((mn / c1) / (jnp.sqrt(vn / c2) + ADAM_EPS) + ADAM_WD * w_ref[...])

    row = pl.BlockSpec((tr, C), lambda i: (i, 0))
    piece = lambda l: pl.BlockSpec((N_DEV, tr, C), lambda i: (0, jnp.clip(i - l * nbl, 0, nbl - 1), 0))
    return pl.pallas_call(
        body, name=name, grid=(R // tr,),
        in_specs=[row, row, row] + [piece(l) for l in range(nl)],
        out_specs=[row] * 4,
        out_shape=[jax.ShapeDtypeStruct((R, C), F32)] * 4)(w, m, v, *pieces)


def _my_index():
    return 4 * lax.axis_index("x") + 2 * lax.axis_index("y") + lax.axis_index("c")


def _all_gather(name, mine):
    na = len(mine)

    def body(*refs):
        x_refs, out_refs, token = refs[:na], refs[na:2 * na], refs[2 * na]
        send_sems, recv_sems, local_sems = refs[2 * na + 1:]
        token[...] = jnp.zeros_like(token)
        x, y, c = lax.axis_index("x"), lax.axis_index("y"), lax.axis_index("c")
        me, sibling = (x, y, c), (x, y, 1 - c)
        chips = [(1 - x, y), (x, 1 - y), (1 - x, 1 - y)]

        def slot(a, px, py, pc):
            return out_refs[a].at[4 * px + 2 * py + pc]

        def copy(a, k, block, to, src=None):
            return pltpu.make_async_remote_copy(
                src_ref=slot(a, *block) if src is None else src, dst_ref=slot(a, *block),
                send_sem=send_sems.at[7 * a + k], recv_sem=recv_sems.at[7 * a + k],
                device_id=to, device_id_type=MESH_ID)

        own = [pltpu.make_async_copy(x_refs[a], slot(a, *me), local_sems.at[a]) for a in range(na)]
        for cp in own:
            cp.start()
        first = []
        for a in range(na):
            first.append(copy(a, 0, me, sibling, src=x_refs[a]))
            first += [copy(a, 1 + j, me, (*chip, c), src=x_refs[a]) for j, chip in enumerate(chips)]
        for cp in first:
            cp.start()
        passed = []
        for j, chip in enumerate(chips):
            for a in range(na):
                copy(a, 1 + j, (*chip, c), me).wait_recv()
                fwd = copy(a, 4 + j, (*chip, c), sibling)
                fwd.start()
                passed.append(fwd)
        for a in range(na):
            copy(a, 0, sibling, me).wait_recv()
            for j, chip in enumerate(chips):
                copy(a, 4 + j, (*chip, 1 - c), me).wait_recv()
        for cp in first + passed:
            cp.wait_send()
        for cp in own:
            cp.wait()

    res = pl.pallas_call(
        body, name=name,
        in_specs=[pl.BlockSpec(memory_space=pl.ANY)] * na,
        out_specs=[pl.BlockSpec(memory_space=pl.ANY)] * na + [pl.BlockSpec(memory_space=pltpu.VMEM)],
        out_shape=[jax.ShapeDtypeStruct((N_DEV,) + m.shape, m.dtype) for m in mine]
        + [jax.ShapeDtypeStruct((8, LANE), F32)],
        scratch_shapes=[pltpu.SemaphoreType.DMA((7 * na,)), pltpu.SemaphoreType.DMA((7 * na,)),
                        pltpu.SemaphoreType.DMA((na,))],
    )(*mine)
    return res[:na], res[na]


_HBM_SPEC = pl.BlockSpec(memory_space=pltpu.HBM)
_SEM_SPEC = pl.BlockSpec(memory_space=pltpu.SEMAPHORE)
_SIDE_EFFECT = pltpu.SideEffectType.DATAFLOW_SIDE_EFFECTING


def _peer_of(k):
    x, y, c = lax.axis_index("x"), lax.axis_index("y"), lax.axis_index("c")
    px = 1 - x if k & 4 else x
    py = 1 - y if k & 2 else y
    pc = 1 - c if k & 1 else c
    return (px, py, pc), 4 * px + 2 * py + pc


def _send_start(name, srcs, pieces):
    na = len(srcs)
    land_shapes = [s.shape if pieces else (N_DEV,) + s.shape for s in srcs]
    lands = [lax.empty(shp, s.dtype) for shp, s in zip(land_shapes, srcs)]

    def body(*refs):
        src_refs, land_refs = refs[:na], refs[na:2 * na]
        send_sems, recv_sems, token = refs[2 * na], refs[2 * na + 1], refs[4 * na + 2]
        me = 4 * lax.axis_index("x") + 2 * lax.axis_index("y") + lax.axis_index("c")
        for k in range(1, N_DEV):
            to, pidx = _peer_of(k)
            for a in range(na):
                pltpu.make_async_remote_copy(
                    src_ref=src_refs[a].at[pidx] if pieces else src_refs[a], dst_ref=land_refs[a].at[me],
                    send_sem=send_sems.at[7 * a + k - 1], recv_sem=recv_sems.at[7 * a + k - 1],
                    device_id=to, device_id_type=MESH_ID).start()
        token[...] = jnp.zeros_like(token)

    hbm = lambda arrs: [pltpu.HBM(a.shape, a.dtype) for a in arrs]
    outs = pl.pallas_call(
        body, name=name,
        out_shape=(pltpu.SemaphoreType.DMA((7 * na,)), pltpu.SemaphoreType.DMA((7 * na,)), *hbm(srcs), *hbm(lands),
                   jax.ShapeDtypeStruct((8, LANE), F32)),
        in_specs=[_HBM_SPEC] * (2 * na),
        out_specs=(_SEM_SPEC, _SEM_SPEC, *([_HBM_SPEC] * (2 * na)), pl.BlockSpec(memory_space=pltpu.VMEM)),
        input_output_aliases={i: 2 + i for i in range(2 * na)},
        compiler_params=pltpu.CompilerParams(has_side_effects=_SIDE_EFFECT),
    )(*[pltpu.with_memory_space_constraint(s, pltpu.HBM) for s in srcs],
      *[pltpu.with_memory_space_constraint(l, pltpu.HBM) for l in lands])
    return outs[0], outs[1], outs[2:2 + na], outs[2 + na:2 + 2 * na], outs[-1]


def _send_wait(name, send_sems, recv_sems, srcs, lands, pieces, after):
    na = len(srcs)

    def body(*refs):
        src_refs, land_refs = refs[:na], refs[na:2 * na]
        send_sems, recv_sems = refs[2 * na], refs[2 * na + 1]
        for k in range(1, N_DEV):
            to, pidx = _peer_of(k)
            for a in range(na):
                cp = pltpu.make_async_remote_copy(
                    src_ref=src_refs[a].at[pidx] if pieces else src_refs[a], dst_ref=land_refs[a].at[pidx],
                    send_sem=send_sems.at[7 * a + k - 1], recv_sem=recv_sems.at[7 * a + k - 1],
                    device_id=to, device_id_type=MESH_ID)
                cp.wait_send()
                cp.wait_recv()

    hbm = lambda arrs: [pltpu.HBM(a.shape, a.dtype) for a in arrs]
    outs = pl.pallas_call(
        body, name=name,
        out_shape=tuple(hbm(srcs) + hbm(lands)),
        in_specs=[_HBM_SPEC] * (2 * na) + [_SEM_SPEC, _SEM_SPEC, pl.BlockSpec(memory_space=pl.ANY)],
        out_specs=tuple([_HBM_SPEC] * (2 * na)),
        input_output_aliases={i: i for i in range(2 * na)},
        compiler_params=pltpu.CompilerParams(has_side_effects=_SIDE_EFFECT),
    )(*srcs, *lands, send_sems, recv_sems, after)
    return outs[:na], outs[na:]


def _own_slot(land, own):
    me = 4 * lax.axis_index("x") + 2 * lax.axis_index("y") + lax.axis_index("c")
    mine = lax.broadcasted_iota(jnp.int32, land.shape, 0) == me
    return jnp.where(mine, jnp.broadcast_to(own, land.shape), land)


def _exchange(name, pieces, bcast):
    n_p, n_b = len(pieces), len(bcast)
    na = n_p + n_b

    def body(*refs):
        src_refs, dst_refs = refs[:na], refs[na:2 * na]
        send_sems, recv_sems, local_sems = refs[2 * na:]
        x, y, c = lax.axis_index("x"), lax.axis_index("y"), lax.axis_index("c")
        me = 4 * x + 2 * y + c

        def src(a, slot):
            return src_refs[a].at[slot] if a < n_p else src_refs[a]

        own = [pltpu.make_async_copy(src(a, me), dst_refs[a].at[me], local_sems.at[a]) for a in range(na)]
        for cp in own:
            cp.start()

        def peer_of(k):
            px = 1 - x if k & 4 else x
            py = 1 - y if k & 2 else y
            pc = 1 - c if k & 1 else c
            return (px, py, pc), 4 * px + 2 * py + pc

        def copy(a, k, src_slot, dst_slot, to):
            return pltpu.make_async_remote_copy(
                src_ref=src(a, src_slot), dst_ref=dst_refs[a].at[dst_slot],
                send_sem=send_sems.at[7 * a + k - 1], recv_sem=recv_sems.at[7 * a + k - 1],
                device_id=to, device_id_type=MESH_ID)

        sent = []
        for k in range(1, N_DEV):
            to, pidx = peer_of(k)
            for a in range(na):
                cp = copy(a, k, pidx, me, to)
                cp.start()
                sent.append(cp)
        for k in range(1, N_DEV):
            to, pidx = peer_of(k)
            for a in range(na):
                copy(a, k, me, pidx, to).wait_recv()
        for cp in sent:
            cp.wait_send()
        for cp in own:
            cp.wait()

    arrays = list(pieces) + list(bcast)
    out_shape = [jax.ShapeDtypeStruct(p.shape, p.dtype) for p in pieces]
    out_shape += [jax.ShapeDtypeStruct((N_DEV,) + b.shape, b.dtype) for b in bcast]
    res = pl.pallas_call(
        body, name=name,
        in_specs=[pl.BlockSpec(memory_space=pl.ANY)] * na, out_specs=[pl.BlockSpec(memory_space=pl.ANY)] * na,
        out_shape=out_shape,
        scratch_shapes=[pltpu.SemaphoreType.DMA((7 * na,)), pltpu.SemaphoreType.DMA((7 * na,)),
                        pltpu.SemaphoreType.DMA((na,))],
    )(*arrays)
    return res[:n_p], res[n_p:]


def _pad_rows(flat, cols, row_mult):
    n = flat.shape[-1]
    unit = cols * row_mult
    padded = -(-n // unit) * unit
    pad = [(0, 0)] * (flat.ndim - 1) + [(0, padded - n)]
    return jnp.pad(flat, pad).reshape(flat.shape[:-1] + (padded // cols, cols))


def _perm_cols(w):
    aw = N_GROUPS * GROUP_W
    parts = [w[..., QKV_W:]]
    parts += [w[..., a * aw + g * GROUP_W:a * aw + (g + 1) * GROUP_W] for g in range(N_GROUPS) for a in range(3)]
    return jnp.concatenate(parts, axis=-1)


def _unperm_cols(wp, rest_w):
    qkv = wp[..., rest_w:]
    parts = [qkv[..., g * GROUP_QKV_W + a * GROUP_W:g * GROUP_QKV_W + (a + 1) * GROUP_W]
             for a in range(3) for g in range(N_GROUPS)]
    return jnp.concatenate(parts + [wp[..., :rest_w]], axis=-1)


SHARD_AXIS = dict(SHARDED)


def _layer_shards(wts, li, names, zero):
    out = []
    for n in names:
        w = wts[n][li] if zero is None else wts[n][li] + zero
        out.append(w if n in EXACT_F32 else w.astype(BF16))
    return out


def _assemble_weights(names, segs):
    W = {}
    for n, seg in zip(names, segs):
        ax = SHARD_AXIS[n]
        shp = seg.shape[1:]
        seg = jnp.moveaxis(seg, 0, ax)
        W[n] = seg.reshape(shp[:ax] + (N_DEV * shp[ax],) + shp[ax + 1:])
    if "w_in" in W:
        W["w_in"] = _perm_cols(W["w_in"])
    return W


def _grad_pieces(gr):
    out = []
    for n in gr:
        ax = SHARD_AXIS[n]
        shp = gr[n].shape
        g = gr[n].reshape(shp[:ax] + (N_DEV, shp[ax] // N_DEV) + shp[ax + 1:])
        out.append(jnp.moveaxis(g, ax, 0).astype(BF16))
    return out


def _pack_small(vals):
    flat = jnp.concatenate([vals[n].astype(F32).reshape(-1) for n in REPLICATED])
    return _pad_rows(flat, LANE, 8)


def _layer_fwd(li, x, p_l, w_in, other_weights, G, rope, Bl, S, F):
    T, D = x.shape
    rest_w = 3 * D
    sv = {"x0": x}
    W = {"w_in": w_in}
    hs = _rms_mix_fwd(f"rms_mix_{li}", x, G["g_mix"], S)
    h = hs[0]
    h_g = [h] + [a.reshape(T, D) for a in hs[1:]]
    sv["h_g"] = h_g
    zr = _mm_rows(f"rest_proj_{li}", h, W["w_in"], n_cols=rest_w)
    sv["zr"] = zr
    qkv_l, o_l, lse_l = [], [], []
    for g in range(N_GROUPS):
        qkv = _qkv_proj(f"qkv_proj_{li}_{g}", h_g[g], W["w_in"], rope[g], S, rest_w + g * GROUP_QKV_W)
        o, lse = _attn_fwd(f"attn_fwd_{li}_{g}", qkv, g, Bl, S)
        qkv_l.append(qkv)
        o_l.append(o)
        lse_l.append(lse)
    sv["qkv"], sv["o"], sv["lse"] = qkv_l, o_l, lse_l
    W.update(other_weights(o_l[-1]))
    x1, attn, pooled, mixed, y_a, y_b, merged = _mix_out_fwd(
        f"mix_out_fwd_{li}", x, zr, o_l, lse_l, W["w_ya"], W["pool_w"], G["pool_scale"], W["w_yb"], W["w_o"], S)
    sv.update(x1=x1, attn=attn, pooled=pooled, mixed=mixed, y_a=y_a, y_b=y_b, merged=merged)
    h2 = _rms_fwd(f"rms_ffn_{li}", x1, G["g_ffn"])
    u = _up_proj(f"up_proj_{li}", h2, W["w_up"], F)
    a = _ffn_act_fwd(f"ffn_act_fwd_{li}", u, W["conv_w"], G["conv_b"], S)
    x2 = _mm_rows(f"down_proj_{li}", a, W["w_down"], add=x1, chunk=512)
    sv.update(h2=h2, u=u, a=a, x2=x2)
    x3, e, pg, p_bf = _ple_fwd(f"ple_fwd_{li}", x2, p_l, G["g_ple"], W["w_ple_gate"], W["w_ple"])
    sv.update(e=e, pg=pg, p_bf=p_bf)
    return x3, sv, W


EARLY_GRADS = ("w_ple", "w_ple_gate", "w_down", "conv_w", "w_up")
LATE_GRADS = ("w_in", "w_ya", "w_yb", "pool_w", "w_o")


def _layer_bwd(li, dx3, sv, W, G, rope, Bl, S, F, send):
    T, D = dx3.shape
    rest_w = 3 * D
    gr = {}
    dx2, d_e, d_s, h3, dg = _ple_bwd(f"ple_bwd_{li}", dx3, sv["x2"], sv["e"], sv["pg"], G["g_ple"], W["w_ple_gate"])
    gr["g_ple"] = dg[0]
    gr["w_ple"] = _mm_tn(f"w_ple_grad_{li}", sv["p_bf"], d_e)
    gr["w_ple_gate"] = _mm_tn(f"w_ple_gate_grad_{li}", h3, d_s)

    d_a = _mm_rows(f"down_bwd_{li}", dx2, W["w_down"], transposed=True, chunk=1408, out_dtype=BF16)
    gr["w_down"] = _mm_tn(f"w_down_grad_{li}", sv["a"], dx2, tm=1408)
    du, d_cw, d_cb = _ffn_act_bwd(f"ffn_act_bwd_{li}", sv["u"], d_a, W["conv_w"], G["conv_b"], S)
    gr["conv_w"] = d_cw.transpose(1, 0, 2).reshape(3, 2 * F)
    gr["conv_b"] = d_cb.reshape(2 * F)
    tk_f = _tile(F, 1408)
    nkh = F // tk_f
    tm_r = _tile(T, 1024, 8)
    dx1, dg = _mm_nt_rmsbwd(f"up_bwd_{li}", du, (None, tm_r, tk_f), lambda i, j, k: (k // nkh, i, k % nkh),
                            2 * nkh, tk_f, W["w_up"], sv["x1"], G["g_ffn"], dx2)
    gr["g_ffn"] = dg[0]
    gr["w_up"] = _up_w_grad(f"w_up_grad_{li}", sv["h2"], du)
    zero = send("a", {n: gr[n] for n in EARLY_GRADS})

    (dz_rest, d_ya, d_yb, d_pm, do0, do1, do2, c0, c1, c2, dps) = _mix_out_bwd(
        f"mix_out_bwd_{li}", dx1, sv["zr"], sv["y_a"], sv["y_b"], sv["o"], sv["lse"], sv["pooled"],
        W["w_o"], W["w_ya"], W["w_yb"], W["pool_w"], G["pool_scale"] + zero, S)
    gr["pool_scale"] = dps[0]
    gr["w_o"] = _mm_tn(f"w_o_grad_{li}", sv["merged"], dx1)
    gr["w_ya"] = _mm_tn(f"w_ya_grad_{li}", sv["attn"], d_ya)
    gr["w_yb"] = _mm_tn(f"w_yb_grad_{li}", sv["mixed"], d_yb)
    gr["pool_w"] = _pool_w_grad(f"pool_w_grad_{li}", sv["pooled"], d_pm, D // len(POOL_WINDOWS))
    segs = [(dz_rest, 1)]
    for g, (do, cst) in enumerate(((do0, c0), (do1, c1), (do2, c2))):
        dqkv = _attn_bwd(f"attn_bwd_{li}_{g}", sv["qkv"][g], do, sv["lse"][g], cst, rope[g], g, Bl, S)
        segs.append((dqkv, DILATIONS[g]))

    h_rows = [sv["h_g"][0]] + sv["h_g"]
    w_in_parts = [_mm_tn(f"w_in_grad_{li}_{s}", h_rows[s], seg, tn=1536) for s, (seg, _) in enumerate(segs)]
    gr["w_in"] = _unperm_cols(jnp.concatenate(w_in_parts, axis=1), rest_w)
    zero = send("b", {n: gr[n] for n in LATE_GRADS})
    dx0, dg = _in_bwd(f"in_bwd_{li}", segs, W["w_in"], sv["x0"], G["g_mix"] + zero, dx1, S)
    gr["g_mix"] = dg[0]
    return dx0, gr


def kernel(x, p, g_mix, w_in, w_ya, w_yb, pool_w, pool_scale, w_o, g_ffn, w_up, conv_w, conv_b, w_down, g_ple, w_ple, w_ple_gate, g_final, loss_target, m_g_mix, m_w_in, m_w_ya, m_w_yb, m_pool_w, m_pool_scale, m_w_o, m_g_ffn, m_w_up, m_conv_w, m_conv_b, m_w_down, m_g_ple, m_w_ple, m_w_ple_gate, m_g_final, v_g_mix, v_w_in, v_w_ya, v_w_yb, v_pool_w, v_pool_scale, v_w_o, v_g_ffn, v_w_up, v_conv_w, v_conv_b, v_w_down, v_g_ple, v_w_ple, v_w_ple_gate, v_g_final):
    wts = dict(g_mix=g_mix, w_in=w_in, w_ya=w_ya, w_yb=w_yb, pool_w=pool_w, pool_scale=pool_scale, w_o=w_o,
               g_ffn=g_ffn, w_up=w_up, conv_w=conv_w, conv_b=conv_b, w_down=w_down, g_ple=g_ple, w_ple=w_ple,
               w_ple_gate=w_ple_gate, g_final=g_final)
    mom = dict(g_mix=m_g_mix, w_in=m_w_in, w_ya=m_w_ya, w_yb=m_w_yb, pool_w=m_pool_w, pool_scale=m_pool_scale,
               w_o=m_w_o, g_ffn=m_g_ffn, w_up=m_w_up, conv_w=m_conv_w, conv_b=m_conv_b, w_down=m_w_down,
               g_ple=m_g_ple, w_ple=m_w_ple, w_ple_gate=m_w_ple_gate, g_final=m_g_final)
    var = dict(g_mix=v_g_mix, w_in=v_w_in, w_ya=v_w_ya, w_yb=v_w_yb, pool_w=v_pool_w, pool_scale=v_pool_scale,
               w_o=v_w_o, g_ffn=v_g_ffn, w_up=v_w_up, conv_w=v_conv_w, conv_b=v_conv_b, w_down=v_w_down,
               g_ple=v_g_ple, w_ple=v_w_ple, w_ple_gate=v_w_ple_gate, g_final=v_g_final)
    Bl, S, D = x.shape
    depth = g_mix.shape[0]
    F = w_down.shape[1] * N_DEV
    T = Bl * S
    assert S % (BLOCK * DILATIONS[-1]) == 0 and D % GROUP_W == 0 and F % LANE == 0
    rope = [tuple(t if d == 1 else t.reshape(S // d, d, HEAD_DIM).transpose(1, 0, 2).reshape(S, HEAD_DIM)
                  for t in _rope_tables(S)) for d in DILATIONS]

    first, others = ("w_in",), tuple(n for n, _ in SHARDED if n != "w_in")
    got_in, tok = _all_gather("gather_w0_in", _layer_shards(wts, 0, first, None))
    gathers = {}
    for li in range(depth):
        names = others if li == 0 else first + others
        *gathers[li], tok = _send_start(f"gather_w{li}_start", _layer_shards(wts, li, names, tok[0, 0]), False)

    def gathered(li, names, after):
        shards, lands = _send_wait(f"gather_w{li}_wait", *gathers[li], False, after)
        return _assemble_weights(names, [_own_slot(l, s) for l, s in zip(lands, shards)])

    xs = x.reshape(T, D)
    saved = []
    for li in range(depth):
        G = {n: wts[n][li] for n in REPLICATED if n != "g_final"}
        if li == 0:
            G["g_mix"] = G["g_mix"] + tok[0, 0]
            w_in_full = _assemble_weights(first, got_in)["w_in"]
            rest_of = lambda after: gathered(0, others, after)
        else:
            W_all = gathered(li, first + others, xs)
            w_in_full = W_all["w_in"]
            rest_of = lambda after, W_all=W_all: W_all
        xs, sv, W = _layer_fwd(li, xs, p[li].reshape(T, -1), w_in_full, rest_of, G, rope, Bl, S, F)
        saved.append((sv, W, G))

    dx, loss_row, dg_final = _loss_bwd("loss_bwd", xs, loss_target.reshape(T, D), g_final)
    layer_grads = [None] * depth
    sends = []
    zero = [None]
    for li in reversed(range(depth)):
        sv, W, G = saved[li]
        if zero[0] is not None:
            G = dict(G, g_ple=G["g_ple"] + zero[0])

        def send(tag, group, li=li):
            *handles, tok = _send_start(f"exchange_g{li}{tag}_start", _grad_pieces(group), True)
            sends.append((li, tag, tuple(group), handles))
            zero[0] = tok[0, 0]
            return zero[0]

        dx, layer_grads[li] = _layer_bwd(li, dx, sv, W, G, rope, Bl, S, F, send)

    recv = {}
    for li, tag, names, handles in sends:
        pcs, lands = _send_wait(f"exchange_g{li}{tag}_wait", *handles, True, dx)
        for n, l, s in zip(names, lands, pcs):
            recv[(li, n)] = _own_slot(l, s)
    grads = {n: jnp.stack([layer_grads[li][n] for li in range(depth)]) for n in REPLICATED if n != "g_final"}
    grads["g_final"] = dg_final[0]
    _, (small_all,) = _exchange("exchange_small", [], [_pack_small(grads)])

    out_g, out_d, out_m, out_v = {}, {}, {}, {}
    for n, _ in SHARDED:
        shp = wts[n].shape
        two_d = (math.prod(shp[:-1]), shp[-1])
        pieces = [recv[(li, n)].reshape(N_DEV, two_d[0] // depth, two_d[1]) for li in range(depth)]
        res = _adamw(f"adamw_{n}", wts[n].reshape(two_d), mom[n].reshape(two_d), var[n].reshape(two_d), pieces)
        out_g[n], out_d[n], out_m[n], out_v[n] = [r.reshape(shp) for r in res]
    res = _adamw("adamw_replicated", _pack_small(wts), _pack_small(mom), _pack_small(var), [small_all])
    off = 0
    for n in REPLICATED:
        shp = wts[n].shape
        size = math.prod(shp)
        for dst, r in zip((out_g, out_d, out_m, out_v), res):
            dst[n] = r.reshape(-1)[off:off + size].reshape(shp)
        off += size

    loss = lax.psum(loss_row[0, 0], MESH_AXES)
    outs = [loss, dx.reshape(Bl, S, D)]
    for dct in (out_g, out_d, out_m, out_v):
        outs += [dct[n] for n in WEIGHT_ORDER]
    return tuple(outs)
```

```python
import math

import jax
import jax.numpy as jnp
from jax import lax
from jax.experimental import pallas as pl
from jax.experimental.pallas import tpu as pltpu

F32 = jnp.float32
BF16 = jnp.bfloat16

N_DEV = 8
HEAD_DIM = 128
HEADS = 4
GROUP_W = HEADS * HEAD_DIM
DILATIONS = (1, 4, 16)
N_GROUPS = len(DILATIONS)
QKV_W = 3 * N_GROUPS * GROUP_W
GROUP_QKV_W = 3 * GROUP_W
BLOCK = 128
ROPE_DIM = HEAD_DIM // 4
ROPE_HALF = ROPE_DIM // 2
ROPE_THETA = 500000.0
NEG_INF = -1e30
POOL_WINDOWS = (2, 4, 8, 16)
POOL_HALO = 16
CONV_HALO = 16
RMS_EPS = 1e-6
ADAM_LR = 0.001
ADAM_B1 = 0.9
ADAM_B2 = 0.999
ADAM_EPS = 1e-08
ADAM_WD = 0.01
ADAM_STEP = 10
LANE = 128
PACK_ROWS = 16
MESH_ID = pl.DeviceIdType.MESH
MESH_AXES = ("x", "y", "c")

NT_DIMS = (((1,), (1,)), ((), ()))
TN_DIMS = (((0,), (0,)), ((), ()))
NN_DIMS = (((1,), (0,)), ((), ()))

SHARDED = (("w_in", 1), ("w_ya", 1), ("w_yb", 0), ("pool_w", 1), ("w_o", 0), ("w_up", 1), ("conv_w", 1),
           ("w_down", 0), ("w_ple", 1), ("w_ple_gate", 0))
EXACT_F32 = ("conv_w",)
REPLICATED = ("g_mix", "pool_scale", "g_ffn", "conv_b", "g_ple", "g_final")
WEIGHT_ORDER = ("g_mix", "w_in", "w_ya", "w_yb", "pool_w", "pool_scale", "w_o", "g_ffn", "w_up", "conv_w", "conv_b",
                "w_down", "g_ple", "w_ple", "w_ple_gate", "g_final")


def _tile(n, pref, mult=LANE):
    if n <= pref:
        return n
    t = (pref // mult) * mult
    while t >= mult:
        if n % t == 0:
            return t
        t -= mult
    return n


def _sigmoid(x):
    return 1.0 / (1.0 + jnp.exp(-x))


def _dot(a, b, dims=NN_DIMS):
    return lax.dot_general(a.astype(BF16), b.astype(BF16), dims, preferred_element_type=F32)


def _rstd(x):
    return lax.rsqrt(jnp.mean(x * x, axis=-1, keepdims=True) + RMS_EPS)


def _rms_bwd(x, g, dh):
    r = _rstd(x)
    u = dh * g
    dx = r * u - x * (r * r * r) * jnp.mean(x * u, axis=-1, keepdims=True)
    return dx, dh * x * r


def _full(a):
    return pl.BlockSpec(a.shape, lambda *_: (0,) * a.ndim)


def _mm(name, a, b, *, grid, a_block, a_map, b_block, b_map, dims, acc_shape, outs, extras=(), epi=None):
    nk = grid[2]
    n_ex = len(extras)
    n_out = len(outs)

    def body(*refs):
        a_ref, b_ref = refs[0], refs[1]
        ex = refs[2:2 + n_ex]
        o = refs[2 + n_ex:2 + n_ex + n_out]
        acc = refs[2 + n_ex + n_out]
        i, j, k = pl.program_id(0), pl.program_id(1), pl.program_id(2)

        @pl.when(k == 0)
        def _():
            acc[...] = jnp.zeros_like(acc)

        acc[...] += _dot(a_ref[...], b_ref[...], dims)

        @pl.when(k == nk - 1)
        def _():
            if epi is None:
                o[0][...] = acc[...].astype(o[0].dtype)
            else:
                epi(acc[...], ex, o, i, j)

    in_specs = [pl.BlockSpec(a_block, a_map), pl.BlockSpec(b_block, b_map)]
    in_specs += [pl.BlockSpec(blk, mp) for (_, blk, mp) in extras]
    out_specs = [pl.BlockSpec(blk, mp) for (_, _, blk, mp) in outs]
    out_shape = [jax.ShapeDtypeStruct(s, d) for (s, d, _, _) in outs]
    return pl.pallas_call(
        body, name=name, grid=grid, in_specs=in_specs, out_specs=out_specs, out_shape=out_shape,
        scratch_shapes=[pltpu.VMEM(acc_shape, F32)],
    )(a, b, *[e[0] for e in extras])


def _mm_tn(name, a, b, *, tm=1024, tn=1024, tk=1024):
    K, M = a.shape
    N = b.shape[1]
    tm, tn, tk = _tile(M, tm), _tile(N, tn), _tile(K, tk, 8)
    return _mm(name, a, b, grid=(M // tm, N // tn, K // tk),
               a_block=(tk, tm), a_map=lambda i, j, k: (k, i),
               b_block=(tk, tn), b_map=lambda i, j, k: (k, j),
               dims=TN_DIMS, acc_shape=(tm, tn),
               outs=[((M, N), F32, (tm, tn), lambda i, j, k: (i, j))])[0]


def _mm_rows(name, a, w, *, transposed=False, n_cols=None, chunk=1024, out_dtype=F32, add=None, tm=512):
    M, K = a.shape
    N = w.shape[0] if transposed else (n_cols if n_cols is not None else w.shape[1])
    tm = _tile(M, tm, 8)
    chunk = _tile(N, chunk)
    w_block = (N, K) if transposed else (K, N)

    def body(*refs):
        a_ref, w_ref = refs[0], refs[1]
        o_ref = refs[-1]
        av = a_ref[...].astype(BF16)
        for c in range(N // chunk):
            cols = slice(c * chunk, (c + 1) * chunk)
            acc = _dot(av, w_ref[cols, :], NT_DIMS) if transposed else _dot(av, w_ref[:, cols])
            if add is not None:
                acc = acc + refs[2][:, cols]
            o_ref[:, cols] = acc.astype(out_dtype)

    row = lambda width: pl.BlockSpec((tm, width), lambda i: (i, 0))
    return pl.pallas_call(
        body, name=name, grid=(M // tm,),
        in_specs=[row(K), pl.BlockSpec(w_block, lambda i: (0, 0))] + ([row(N)] if add is not None else []),
        out_specs=row(N),
        out_shape=jax.ShapeDtypeStruct((M, N), out_dtype),
    )(a, w, *([add] if add is not None else []))


def _mm_nt_rmsbwd(name, a, a_block, a_map, nk, tk, w, x, g, dres):
    T, D = x.shape
    tm = a_block[-2]

    def epi(acc, ex, o, i, j):
        @pl.when(i == 0)
        def _():
            o[1][...] = jnp.zeros_like(o[1])

        dx, dgr = _rms_bwd(ex[0][...], ex[1][...], acc)
        o[0][...] = ex[2][...] + dx
        o[1][...] += jnp.sum(dgr, axis=0, keepdims=True)

    row = lambda i, j, k: (i, 0)
    vec = lambda i, j, k: (0, 0)
    return _mm(name, a, w, grid=(T // tm, 1, nk),
               a_block=a_block, a_map=a_map,
               b_block=(D, tk), b_map=lambda i, j, k: (0, k),
               dims=NT_DIMS, acc_shape=(tm, D),
               outs=[((T, D), F32, (tm, D), row), ((1, D), F32, (1, D), vec)],
               extras=[(x, (tm, D), row), (g.reshape(1, D), (1, D), vec), (dres, (tm, D), row)], epi=epi)


def _in_bwd(name, segs, w_perm, x, g, dres, S):
    T, D = x.shape
    tm = _tile(S, 512, 256)
    nst = S // tm
    tk = GROUP_QKV_W
    steps = [a.shape[1] // tk for a, _ in segs]
    starts = [sum(steps[:s]) for s in range(len(segs))]
    nk = sum(steps)
    ns = len(segs)
    cols = _chunks(D)
    assert all(a.shape[1] % tk == 0 for a, _ in segs) and nk * tk == w_perm.shape[1]

    def body(*refs):
        a_refs = refs[:ns]
        w_ref, x_ref, g_ref, dres_ref, dx_ref, dg_ref, acc, acc_tok = refs[ns:]
        i, k = pl.program_id(0), pl.program_id(1)

        @pl.when(k == 0)
        def _():
            acc[...] = jnp.zeros_like(acc)
            acc_tok[...] = jnp.zeros_like(acc_tok)

        for s in range(ns):
            d = segs[s][1]

            @pl.when((k >= starts[s]) & (k < starts[s] + steps[s]))
            def _():
                if d == 1:
                    acc_tok[...] += _dot(a_refs[s][...], w_ref[...], NT_DIMS)
                else:
                    prod = _dot(a_refs[s][...].reshape(tm, tk), w_ref[...], NT_DIMS)
                    q = tm // d
                    for c, cs in enumerate(cols):
                        for r in range(d):
                            acc[c, pl.ds(r, q, stride=d), :] += prod[r * q:(r + 1) * q, cs]

        @pl.when(k == nk - 1)
        def _():
            @pl.when(i == 0)
            def _():
                dg_ref[...] = jnp.zeros_like(dg_ref)

            dh = acc_tok[...] + jnp.concatenate([acc[c] for c in range(len(cols))], axis=1)
            dx, dgr = _rms_bwd(x_ref[...], g_ref[...], dh)
            dx_ref[...] = dres_ref[...] + dx
            dg_ref[...] += jnp.sum(dgr, axis=0, keepdims=True)

    def seg_spec(s):
        kmap = lambda k: jnp.clip(k - starts[s], 0, steps[s] - 1)
        d = segs[s][1]
        if d == 1:
            return pl.BlockSpec((tm, tk), lambda i, k: (i, kmap(k)))
        return pl.BlockSpec((None, d, tm // d, tk), lambda i, k: (i // nst, 0, i % nst, kmap(k)))

    views = [a if d == 1 else a.reshape(T // S, d, S // d, a.shape[1]) for a, d in segs]
    row = pl.BlockSpec((tm, D), lambda i, k: (i, 0))
    vec = pl.BlockSpec((1, D), lambda i, k: (0, 0))
    return pl.pallas_call(
        body, name=name, grid=(T // tm, nk),
        in_specs=[seg_spec(s) for s in range(ns)] + [pl.BlockSpec((D, tk), lambda i, k: (0, k)), row, vec, row],
        out_specs=[row, vec],
        out_shape=[jax.ShapeDtypeStruct((T, D), F32), jax.ShapeDtypeStruct((1, D), F32)],
        scratch_shapes=[pltpu.VMEM((D // LANE, tm, LANE), F32), pltpu.VMEM((tm, D), F32)],
    )(*views, w_perm, x, g.reshape(1, D), dres)


def _rope_tables(S):
    pos = jnp.arange(S, dtype=F32)
    inv_freq = jnp.exp(jnp.arange(0, ROPE_DIM, 2, dtype=F32) * (-math.log(ROPE_THETA) / ROPE_DIM))
    ang = pos[:, None] * inv_freq[None, :]
    cos, sin = jnp.cos(ang), jnp.sin(ang)
    ones = jnp.ones((S, HEAD_DIM - ROPE_DIM), F32)
    zeros_h = jnp.zeros((S, ROPE_HALF), F32)
    zeros_r = jnp.zeros((S, HEAD_DIM - ROPE_DIM), F32)
    c = jnp.concatenate([cos, cos, ones], axis=1)
    sa = jnp.concatenate([-sin, zeros_h, zeros_r], axis=1)
    sb = jnp.concatenate([zeros_h, sin, zeros_r], axis=1)
    return c, sa, sb


def _rope(t, c, sa, sb):
    return t * c + pltpu.roll(t, HEAD_DIM - ROPE_HALF, 1) * sa + pltpu.roll(t, ROPE_HALF, 1) * sb


def _chunks(width):
    return [slice(c * LANE, (c + 1) * LANE) for c in range(width // LANE)]


def _store_residue_major(val, sc, out_ref, d, dtype):
    rows = val.shape[0]
    for c, cs in enumerate(_chunks(val.shape[1])):
        sc[c] = val[:, cs]
    for r in range(d):
        for c, cs in enumerate(_chunks(val.shape[1])):
            out_ref[r, :, cs] = sc[c, pl.ds(r, rows // d, stride=d), :].astype(dtype)


def _load_token_order(blk_ref, sc, d):
    _, q, width = blk_ref.shape
    for r in range(d):
        for c, cs in enumerate(_chunks(width)):
            sc[c, pl.ds(r, q, stride=d), :] = blk_ref[r, :, cs]
    return jnp.concatenate([sc[c] for c in range(width // LANE)], axis=1)


def _residue_major_spec(d, q, width, nst):
    return pl.BlockSpec((None, d, q, width), lambda i, *_: (i // nst, 0, i % nst, 0))


def _rms_mix_fwd(name, x, g, S):
    T, D = x.shape
    Bl = T // S
    tm = _tile(S, 512, 256)
    nst = S // tm
    dils = [d for d in DILATIONS if d > 1]

    def body(x_ref, g_ref, h_ref, *rest):
        rm_refs, sc = rest[:len(dils)], rest[len(dils)]
        xv = x_ref[...]
        hv = xv * _rstd(xv) * g_ref[...]
        h_ref[...] = hv.astype(BF16)
        for d, o_ref in zip(dils, rm_refs):
            _store_residue_major(hv, sc, o_ref, d, BF16)

    row = pl.BlockSpec((tm, D), lambda i: (i, 0))
    return pl.pallas_call(
        body, name=name, grid=(T // tm,),
        in_specs=[row, pl.BlockSpec((1, D), lambda i: (0, 0))],
        out_specs=[row] + [_residue_major_spec(d, tm // d, D, nst) for d in dils],
        out_shape=[jax.ShapeDtypeStruct((T, D), BF16)]
        + [jax.ShapeDtypeStruct((Bl, d, S // d, D), BF16) for d in dils],
        scratch_shapes=[pltpu.VMEM((D // LANE, tm, LANE), F32)])(x, g.reshape(1, D))


def _qkv_proj(name, h, w_perm, rope, S, col_off):
    T, D = h.shape
    tm = _tile(S, 512, 8)
    c_t, sa_t, sb_t = rope
    n_seq_tiles = S // tm
    jblk = col_off // GROUP_QKV_W

    def body(h_ref, w_ref, c_ref, sa_ref, sb_ref, o_ref):
        hv = h_ref[...]
        c, sa, sb = c_ref[...], sa_ref[...], sb_ref[...]
        for chunk in range(3):
            cols = slice(chunk * GROUP_W, (chunk + 1) * GROUP_W)
            acc = _dot(hv, w_ref[:, cols])
            if chunk < 2:
                for hh in range(HEADS):
                    sl = slice(hh * HEAD_DIM, (hh + 1) * HEAD_DIM)
                    o_ref[:, chunk * GROUP_W + hh * HEAD_DIM:chunk * GROUP_W + (hh + 1) * HEAD_DIM] = (
                        _rope(acc[:, sl], c, sa, sb).astype(BF16))
            else:
                o_ref[:, cols] = acc.astype(BF16)

    tab = pl.BlockSpec((tm, HEAD_DIM), lambda i: (i % n_seq_tiles, 0))
    return pl.pallas_call(
        body, name=name, grid=(T // tm,),
        in_specs=[pl.BlockSpec((tm, D), lambda i: (i, 0)), pl.BlockSpec((D, GROUP_QKV_W), lambda i: (0, jblk)),
                  tab, tab, tab],
        out_specs=pl.BlockSpec((tm, GROUP_QKV_W), lambda i: (i, 0)),
        out_shape=jax.ShapeDtypeStruct((T, GROUP_QKV_W), BF16))(h, w_perm, c_t, sa_t, sb_t)


ATTN_BLOCKS_PER_STEP = 4


def _attn_mask(rows, lead, has_prev):
    qi = lax.broadcasted_iota(jnp.int32, (rows, lead + rows), 0)
    ki = lax.broadcasted_iota(jnp.int32, (rows, lead + rows), 1)
    diff = lead + qi - ki
    band = (diff >= 0) & (diff <= BLOCK)
    return band & (has_prev | (ki >= lead)) if lead else band


def _attn_chunking(nb):
    cb = min(ATTN_BLOCKS_PER_STEP, nb)
    return cb, ATTN_BLOCKS_PER_STEP // cb


def _attn_fwd(name, qkv, g, Bl, S):
    d = DILATIONS[g]
    L = S // d
    nb = L // BLOCK
    cb, cs = _attn_chunking(nb)
    qv = qkv.reshape(Bl * d, L, GROUP_QKV_W)
    scale = HEAD_DIM ** -0.5

    def body(q_ref, kc_ref, vc_ref, kp_ref, vp_ref, o_ref, l_ref):
        n = pl.program_id(1)
        for si in range(cs):
            for bi in range(cb):
                rows = slice(bi * BLOCK, (bi + 1) * BLOCK)
                before = slice((bi - 1) * BLOCK, bi * BLOCK)
                valid = _attn_mask(BLOCK, BLOCK, n > 0 if bi == 0 else True)
                for hh in range(HEADS):
                    sl = slice(hh * HEAD_DIM, (hh + 1) * HEAD_DIM)
                    kp = kp_ref[si, :, sl] if bi == 0 else kc_ref[si, before, sl]
                    vp = vp_ref[si, :, sl] if bi == 0 else vc_ref[si, before, sl]
                    kk = jnp.concatenate([kp, kc_ref[si, rows, sl]], axis=0)
                    vv = jnp.concatenate([vp, vc_ref[si, rows, sl]], axis=0)
                    s = jnp.where(valid, _dot(q_ref[si, rows, sl], kk, NT_DIMS) * scale, NEG_INF)
                    m = jnp.max(s, axis=-1, keepdims=True)
                    p = jnp.exp(s - m)
                    l = jnp.sum(p, axis=-1, keepdims=True)
                    o_ref[si, rows, sl] = _dot(p, vv) / l
                    l_ref[si, rows, sl] = jnp.broadcast_to(m + jnp.log(l), (BLOCK, HEAD_DIM))

    main = (cs, cb * BLOCK, GROUP_W)
    edge = (cs, BLOCK, GROUP_W)
    cur = lambda off: pl.BlockSpec(main, lambda s, n: (s, n, off))
    prev = lambda off: pl.BlockSpec(edge, lambda s, n: (s, jnp.maximum(n * cb - 1, 0), off))
    out = pl.BlockSpec(main, lambda s, n: (s, n, 0))
    return pl.pallas_call(
        body, name=name, grid=(Bl * d // cs, nb // cb),
        in_specs=[cur(0), cur(1), cur(2), prev(1), prev(2)],
        out_specs=[out, out],
        out_shape=[jax.ShapeDtypeStruct((Bl * d, L, GROUP_W), F32)] * 2)(qv, qv, qv, qv, qv)


def _merge_weights(l0, l1, l2):
    mx = jnp.maximum(jnp.maximum(l0, l1), l2)
    e0, e1, e2 = jnp.exp(l0 - mx), jnp.exp(l1 - mx), jnp.exp(l2 - mx)
    inv = 1.0 / (e0 + e1 + e2)
    return e0 * inv, e1 * inv, e2 * inv


def _group_specs(tm, nst, S):
    specs = []
    for d in DILATIONS:
        specs.append(pl.BlockSpec((tm, GROUP_W), lambda i: (i, 0)) if d == 1
                     else _residue_major_spec(d, tm // d, GROUP_W, nst))

    def views(arrs):
        out = []
        for d, a in zip(DILATIONS, arrs):
            out.append(a.reshape(-1, GROUP_W) if d == 1 else a.reshape(-1, d, S // d, GROUP_W))
        return out

    return specs, views


def _group_scratch(tm, per_group):
    n = per_group * sum(1 for d in DILATIONS if d > 1)
    return [pltpu.VMEM((GROUP_W // LANE, tm, LANE), F32) for _ in range(n)]


def _group_values(o_refs, l_refs, scs):
    scs = list(scs)
    ov, lv = [], []
    for d, o_ref, l_ref in zip(DILATIONS, o_refs, l_refs):
        if d == 1:
            ov.append(o_ref[...])
            lv.append(l_ref[...])
        else:
            ov.append(_load_token_order(o_ref, scs.pop(), d))
            lv.append(_load_token_order(l_ref, scs.pop(), d))
    return ov, lv


def _pool_inv_count(tseq, w):
    return 1.0 / jnp.minimum(tseq + 1, w).astype(F32)


def _mix_out_fwd(name, x, zr, o_l, lse_l, w_ya, pool_w, pool_scale, w_yb, w_o, S):
    T, D = x.shape
    gw = D // len(POOL_WINDOWS)
    tm = _tile(S, 256, POOL_HALO)
    nst = S // tm
    hpt = tm // POOL_HALO

    def body(x_ref, u_ref, uh_ref, ga_ref, gb_ref, o0, o1, o2, l0, l1, l2, wya_ref, pw_ref, ps_ref, wyb_ref, wo_ref,
             x1_ref, attn_ref, pooled_ref, mixed_ref, ya_ref, yb_ref, merged_ref, *scs):
        it = pl.program_id(0) % nst
        ov, lv = _group_values((o0, o1, o2), (l0, l1, l2), scs)
        w0, w1, w2 = _merge_weights(*lv)
        attn = w0 * ov[0] + w1 * ov[1] + w2 * ov[2]
        attn_ref[...] = attn.astype(BF16)
        y_a = _dot(attn, wya_ref[...])

        u = u_ref[...]
        halo = uh_ref[...] * jnp.where(it == 0, 0.0, 1.0)
        ext = jnp.concatenate([halo, u], axis=0)
        tseq = it * tm + lax.broadcasted_iota(jnp.int32, (tm, 1), 0)
        pm_parts = []
        for gi, w in enumerate(POOL_WINDOWS):
            cs = slice(gi * gw, (gi + 1) * gw)
            s = ext[:, cs]
            step = 1
            while step < w:
                s = s + pltpu.roll(s, step, 0)
                step *= 2
            pooled_g = (s[POOL_HALO:, :] * _pool_inv_count(tseq, w) - u[:, cs]).astype(BF16)
            pooled_ref[:, cs] = pooled_g
            pm_parts.append(_dot(pooled_g, pw_ref[gi]))
        mixed = (jnp.concatenate(pm_parts, axis=1) * ps_ref[...]).astype(BF16)
        mixed_ref[...] = mixed
        y_b = _dot(mixed, wyb_ref[...])
        merged = (_sigmoid(ga_ref[...]) * y_a + _sigmoid(gb_ref[...]) * y_b).astype(BF16)
        ya_ref[...] = y_a.astype(BF16)
        yb_ref[...] = y_b.astype(BF16)
        merged_ref[...] = merged
        x1_ref[...] = x_ref[...] + _dot(merged, wo_ref[...])

    row = lambda c: pl.BlockSpec((tm, D), lambda i: (i, c))
    row512 = pl.BlockSpec((tm, GROUP_W), lambda i: (i, 0))
    ps = pool_scale.reshape(1, D)
    halo_spec = pl.BlockSpec((POOL_HALO, D), lambda i: (jnp.maximum(i * hpt - 1, 0), 0))
    grp_specs, grp_views = _group_specs(tm, nst, S)
    return pl.pallas_call(
        body, name=name, grid=(T // tm,),
        in_specs=[row(0), row(0), halo_spec, row(1), row(2)] + grp_specs * 2
        + [_full(w_ya), _full(pool_w), _full(ps), _full(w_yb), _full(w_o)],
        out_specs=[row(0), row512, row(0), row(0), row(0), row(0), row(0)],
        out_shape=[jax.ShapeDtypeStruct((T, D), F32), jax.ShapeDtypeStruct((T, GROUP_W), BF16),
                   jax.ShapeDtypeStruct((T, D), BF16), jax.ShapeDtypeStruct((T, D), BF16),
                   jax.ShapeDtypeStruct((T, D), BF16), jax.ShapeDtypeStruct((T, D), BF16),
                   jax.ShapeDtypeStruct((T, D), BF16)],
        scratch_shapes=_group_scratch(tm, 2),
    )(x, zr, zr, zr, zr, *grp_views(o_l), *grp_views(lse_l), w_ya, pool_w, ps, w_yb, w_o)


def _up_proj(name, x1, g, w_up, F):
    T, D = x1.shape
    tm = _tile(T, 512, 8)
    chunk = _tile(F, 1408)

    def body(x_ref, g_ref, w_ref, u_ref, h_ref):
        xv = x_ref[...]
        hv = (xv * _rstd(xv) * g_ref[...]).astype(BF16)
        h_ref[...] = hv
        for half in range(2):
            for c in range(F // chunk):
                cols = slice(c * chunk, (c + 1) * chunk)
                u_ref[half, :, cols] = _dot(hv, w_ref[:, half * F + c * chunk:half * F + (c + 1) * chunk]).astype(BF16)

    row = pl.BlockSpec((tm, D), lambda i: (i, 0))
    return pl.pallas_call(
        body, name=name, grid=(T // tm,),
        in_specs=[row, pl.BlockSpec((1, D), lambda i: (0, 0)), _full(w_up)],
        out_specs=[pl.BlockSpec((2, tm, F), lambda i: (0, i, 0)), row],
        out_shape=[jax.ShapeDtypeStruct((2, T, F), BF16), jax.ShapeDtypeStruct((T, D), BF16)],
    )(x1, g.reshape(1, D), w_up)


def _conv_y(ext, w_ref, b_ref):
    return (b_ref[...] + w_ref[2:3, :] * ext + w_ref[1:2, :] * pltpu.roll(ext, 1, 0)
            + w_ref[0:1, :] * pltpu.roll(ext, 2, 0))


def _conv_params(conv_w, conv_b, F):
    cw = conv_w.reshape(3, 2, F).transpose(1, 0, 2)
    return cw, conv_b.reshape(2, 1, F)


def _ffn_act_fwd(name, u, conv_w, conv_b, S):
    _, T, F = u.shape
    tm = _tile(S, 512, CONV_HALO)
    tf = _tile(F, 1408)
    nst = S // tm
    hpt = tm // CONV_HALO
    cw, cb = _conv_params(conv_w, conv_b, F)

    def body(ug_ref, uv_ref, hg_ref, hv_ref, wg_ref, wv_ref, bg_ref, bv_ref, a_ref):
        keep = jnp.where(pl.program_id(0) % nst == 0, 0.0, 1.0)
        ext = lambda h_ref, u_ref: jnp.concatenate([h_ref[...].astype(F32) * keep, u_ref[...].astype(F32)], axis=0)
        yg = _conv_y(ext(hg_ref, ug_ref), wg_ref, bg_ref)[CONV_HALO:, :]
        yv = _conv_y(ext(hv_ref, uv_ref), wv_ref, bv_ref)[CONV_HALO:, :]
        a_ref[...] = (yg * _sigmoid(yg) * yv).astype(BF16)

    main = lambda h: pl.BlockSpec((None, tm, tf), lambda i, j: (h, i, j))
    halo = lambda h: pl.BlockSpec((None, CONV_HALO, tf), lambda i, j: (h, jnp.maximum(i * hpt - 1, 0), j))
    wsp = lambda h: pl.BlockSpec((None, 3, tf), lambda i, j: (h, 0, j))
    bsp = lambda h: pl.BlockSpec((None, 1, tf), lambda i, j: (h, 0, j))
    return pl.pallas_call(
        body, name=name, grid=(T // tm, F // tf),
        in_specs=[main(0), main(1), halo(0), halo(1), wsp(0), wsp(1), bsp(0), bsp(1)],
        out_specs=pl.BlockSpec((tm, tf), lambda i, j: (i, j)),
        out_shape=jax.ShapeDtypeStruct((T, F), BF16))(u, u, u, u, cw, cw, cb, cb)


def _ple_fwd(name, x2, p, g_ple, w_gate, w_ple):
    T, D = x2.shape
    P = p.shape[1]
    tm = _tile(T, 512, 8)

    def body(x_ref, p_ref, g_ref, wg_ref, wp_ref, x3_ref, e_ref, pg_ref, pbf_ref):
        xv = x_ref[...]
        h3 = xv * _rstd(xv) * g_ref[...]
        pg = _sigmoid(_dot(h3, wg_ref[...]))
        pb = p_ref[...].astype(BF16)
        e = _dot(pb, wp_ref[...])
        x3_ref[...] = xv + e * pg
        e_ref[...] = e.astype(BF16)
        pg_ref[...] = pg.astype(BF16)
        pbf_ref[...] = pb

    row = pl.BlockSpec((tm, D), lambda i: (i, 0))
    prow = pl.BlockSpec((tm, P), lambda i: (i, 0))
    g2 = g_ple.reshape(1, D)
    return pl.pallas_call(
        body, name=name, grid=(T // tm,),
        in_specs=[row, prow, _full(g2), _full(w_gate), _full(w_ple)],
        out_specs=[row, row, row, prow],
        out_shape=[jax.ShapeDtypeStruct((T, D), F32)] + [jax.ShapeDtypeStruct((T, D), BF16)] * 2
        + [jax.ShapeDtypeStruct((T, P), BF16)],
    )(x2, p, g2, w_gate, w_ple)


def _loss_bwd(name, xf, target, g_final):
    T, D = xf.shape
    tm = _tile(T, 512, 8)
    nt = T // tm

    def body(x_ref, t_ref, g_ref, dx_ref, loss_ref, dg_ref, lacc):
        i = pl.program_id(0)

        @pl.when(i == 0)
        def _():
            lacc[...] = jnp.zeros_like(lacc)
            dg_ref[...] = jnp.zeros_like(dg_ref)
            loss_ref[...] = jnp.zeros_like(loss_ref)

        xv = x_ref[...]
        g = g_ref[...]
        diff = xv * _rstd(xv) * g - t_ref[...]
        lacc[...] += jnp.sum(diff * diff, axis=0, keepdims=True)
        dx, dgr = _rms_bwd(xv, g, diff * (1.0 / D))
        dx_ref[...] = dx
        dg_ref[...] += jnp.sum(dgr, axis=0, keepdims=True)

        @pl.when(i == nt - 1)
        def _():
            tot = jnp.sum(lacc[...], axis=-1, keepdims=True) * (0.5 / D)
            loss_ref[...] = jnp.broadcast_to(tot, (1, LANE))

    row = pl.BlockSpec((tm, D), lambda i: (i, 0))
    vec = pl.BlockSpec((1, D), lambda i: (0, 0))
    return pl.pallas_call(
        body, name=name, grid=(nt,),
        in_specs=[row, row, vec],
        out_specs=[row, pl.BlockSpec((1, LANE), lambda i: (0, 0)), vec],
        out_shape=[jax.ShapeDtypeStruct((T, D), F32), jax.ShapeDtypeStruct((1, LANE), F32),
                   jax.ShapeDtypeStruct((1, D), F32)],
        scratch_shapes=[pltpu.VMEM((1, D), F32)])(xf, target, g_final.reshape(1, D))


def _ple_bwd(name, dx3, x2, e, pg, g_ple, w_gate):
    T, D = x2.shape
    tm = _tile(T, 512, 8)

    def body(dx3_ref, x_ref, e_ref, pg_ref, g_ref, wg_ref, dx2_ref, de_ref, ds_ref, h3_ref, dg_ref):
        @pl.when(pl.program_id(0) == 0)
        def _():
            dg_ref[...] = jnp.zeros_like(dg_ref)

        dx3v, xv, pgv, g = dx3_ref[...], x_ref[...], pg_ref[...].astype(F32), g_ref[...]
        de_ref[...] = (dx3v * pgv).astype(BF16)
        ds = (dx3v * e_ref[...].astype(F32) * pgv * (1.0 - pgv)).astype(BF16)
        ds_ref[...] = ds
        dh3 = _dot(ds, wg_ref[...], NT_DIMS)
        h3_ref[...] = (xv * _rstd(xv) * g).astype(BF16)
        dx, dgr = _rms_bwd(xv, g, dh3)
        dx2_ref[...] = dx3v + dx
        dg_ref[...] += jnp.sum(dgr, axis=0, keepdims=True)

    row = pl.BlockSpec((tm, D), lambda i: (i, 0))
    vec = pl.BlockSpec((1, D), lambda i: (0, 0))
    return pl.pallas_call(
        body, name=name, grid=(T // tm,),
        in_specs=[row, row, row, row, vec, _full(w_gate)],
        out_specs=[row, row, row, row, vec],
        out_shape=[jax.ShapeDtypeStruct((T, D), F32)] + [jax.ShapeDtypeStruct((T, D), BF16)] * 3
        + [jax.ShapeDtypeStruct((1, D), F32)])(dx3, x2, e, pg, g_ple.reshape(1, D), w_gate)


def _ffn_act_bwd(name, u, d_a, conv_w, conv_b, S):
    _, T, F = u.shape
    H = CONV_HALO
    tm = _tile(S, 512, H)
    tf = _tile(F, 1408)
    nst = S // tm
    hpt = tm // H
    last_halo = T // H - 1
    n_ext = tm + H
    cw, cb = _conv_params(conv_w, conv_b, F)

    def body(ug_ref, uv_ref, pg_ref, pv_ref, ng_ref, nv_ref, da_ref, dan_ref, wg_ref, wv_ref, bg_ref, bv_ref,
             du_ref, dw_ref, db_ref):
        i = pl.program_id(1)
        it = i % nst

        @pl.when(i == 0)
        def _():
            dw_ref[...] = jnp.zeros_like(dw_ref)
            db_ref[...] = jnp.zeros_like(db_ref)

        keep_prev = jnp.where(it == 0, 0.0, 1.0)
        keep_next = jnp.where(it == nst - 1, 0.0, 1.0)
        def shifted(p_ref, u_ref, n_ref):
            ext = jnp.concatenate([p_ref[...].astype(F32) * keep_prev, u_ref[...].astype(F32),
                                   n_ref[...].astype(F32)], axis=0)
            return ext[H:, :], pltpu.roll(ext, 1, 0)[H:, :], pltpu.roll(ext, 2, 0)[H:, :]

        us_g, us_v = shifted(pg_ref, ug_ref, ng_ref), shifted(pv_ref, uv_ref, nv_ref)
        conv = lambda us, w_ref, b_ref: b_ref[...] + w_ref[2:3, :] * us[0] + w_ref[1:2, :] * us[1] + w_ref[0:1, :] * us[2]
        yg, yv = conv(us_g, wg_ref, bg_ref), conv(us_v, wv_ref, bv_ref)
        da = jnp.concatenate([da_ref[...].astype(F32), dan_ref[...].astype(F32) * keep_next], axis=0)
        sg = _sigmoid(yg)
        silu = yg * sg
        dyv = da * silu
        dyg = (da * yv) * (sg + silu * (1.0 - sg))
        for half, (dy, us, w_ref) in enumerate(((dyg, us_g, wg_ref), (dyv, us_v, wv_ref))):
            du = (w_ref[2:3, :] * dy + w_ref[1:2, :] * pltpu.roll(dy, n_ext - 1, 0)
                  + w_ref[0:1, :] * pltpu.roll(dy, n_ext - 2, 0))
            du_ref[half] = du[:tm, :].astype(BF16)
            dym = dy[:tm, :]
            db_ref[half] += jnp.sum(dym, axis=0, keepdims=True)
            for tap in range(3):
                dw_ref[half, tap:tap + 1, :] += jnp.sum(dym * us[2 - tap][:tm, :], axis=0, keepdims=True)

    main = lambda h: pl.BlockSpec((None, tm, tf), lambda j, i: (h, i, j))
    prev = lambda h: pl.BlockSpec((None, H, tf), lambda j, i: (h, jnp.maximum(i * hpt - 1, 0), j))
    nxt = lambda h: pl.BlockSpec((None, H, tf), lambda j, i: (h, jnp.minimum((i + 1) * hpt, last_halo), j))
    wsp = lambda h: pl.BlockSpec((None, 3, tf), lambda j, i: (h, 0, j))
    bsp = lambda h: pl.BlockSpec((None, 1, tf), lambda j, i: (h, 0, j))
    return pl.pallas_call(
        body, name=name, grid=(F // tf, T // tm),
        in_specs=[main(0), main(1), prev(0), prev(1), nxt(0), nxt(1),
                  pl.BlockSpec((tm, tf), lambda j, i: (i, j)),
                  pl.BlockSpec((H, tf), lambda j, i: (jnp.minimum((i + 1) * hpt, last_halo), j)),
                  wsp(0), wsp(1), bsp(0), bsp(1)],
        out_specs=[pl.BlockSpec((2, tm, tf), lambda j, i: (0, i, j)),
                   pl.BlockSpec((2, 3, tf), lambda j, i: (0, 0, j)),
                   pl.BlockSpec((2, 1, tf), lambda j, i: (0, 0, j))],
        out_shape=[jax.ShapeDtypeStruct((2, T, F), BF16), jax.ShapeDtypeStruct((2, 3, F), F32),
                   jax.ShapeDtypeStruct((2, 1, F), F32)],
    )(u, u, u, u, u, u, d_a, d_a, cw, cw, cb, cb)


def _mix_out_bwd(name, dx1, zr, y_a, y_b, o_l, lse_l, pooled, w_o, w_ya, w_yb, pool_w, pool_scale, S):
    T, D = dx1.shape
    gw = D // len(POOL_WINDOWS)
    H = POOL_HALO
    tm = _tile(S, 256, H)
    nst = S // tm
    hpt = tm // H
    last_halo = T // H - 1
    n_ext = tm + H

    def body(dx_ref, dxn_ref, ga_ref, gb_ref, gbn_ref, ya_ref, yb_ref, o0, o1, o2, l0, l1, l2, pooled_ref,
             wo_ref, wya_ref, wyb_ref, pw_ref, ps_ref,
             dz_ref, dya_ref, dyb_ref, dpm_ref, do0, do1, do2, c0, c1, c2, dps_ref, *scs):
        i = pl.program_id(0)
        it = i % nst

        @pl.when(i == 0)
        def _():
            dps_ref[...] = jnp.zeros_like(dps_ref)

        keep_next = jnp.where(it == nst - 1, 0.0, 1.0)
        dm_e = _dot(jnp.concatenate([dx_ref[...], dxn_ref[...]], axis=0), wo_ref[...], NT_DIMS)
        sgb_e = _sigmoid(jnp.concatenate([gb_ref[...], gbn_ref[...]], axis=0))
        dyb_e = dm_e * sgb_e
        dm = dm_e[:tm, :]
        sga = _sigmoid(ga_ref[...])
        sgb = sgb_e[:tm, :]
        d_ga = dm * ya_ref[...].astype(F32) * (sga * (1.0 - sga))
        d_gb = dm * yb_ref[...].astype(F32) * (sgb * (1.0 - sgb))
        dya = (dm * sga).astype(BF16)
        dya_ref[...] = dya
        dyb_ref[...] = dyb_e[:tm, :].astype(BF16)
        dmixed_e = _dot(dyb_e, wyb_ref[...], NT_DIMS)

        rows = lax.broadcasted_iota(jnp.int32, (n_ext, 1), 0)
        tseq = it * tm + rows
        live = jnp.where(rows < tm, 1.0, keep_next)
        ps = ps_ref[...]
        du_parts = []
        for gi, w in enumerate(POOL_WINDOWS):
            cs = slice(gi * gw, (gi + 1) * gw)
            pm_g = _dot(pooled_ref[:, cs], pw_ref[gi])
            dps_ref[:, cs] += jnp.sum(dmixed_e[:tm, cs] * pm_g, axis=0, keepdims=True)
            dpm_e = (dmixed_e[:, cs] * ps[:, cs]).astype(BF16)
            dpm_ref[:, cs] = dpm_e[:tm, :]
            dpooled_e = _dot(dpm_e, pw_ref[gi], NT_DIMS)
            s = dpooled_e * (_pool_inv_count(tseq, w) * live)
            step = 1
            while step < w:
                s = s + pltpu.roll(s, n_ext - step, 0)
                step *= 2
            du_parts.append(s[:tm, :] - dpooled_e[:tm, :])
        dz_ref[...] = jnp.concatenate(du_parts + [d_ga, d_gb], axis=1).astype(BF16)

        d_attn = _dot(dya, wya_ref[...], NT_DIMS)
        ov, lv = _group_values((o0, o1, o2), (l0, l1, l2), scs[:n_in_sc])
        ws = _merge_weights(*lv)
        prod = d_attn * (ws[0] * ov[0] + ws[1] * ov[1] + ws[2] * ov[2])
        rs = jnp.concatenate(
            [jnp.broadcast_to(jnp.sum(prod[:, hh * HEAD_DIM:(hh + 1) * HEAD_DIM], axis=-1, keepdims=True),
                              (tm, HEAD_DIM)) for hh in range(HEADS)], axis=1)
        out_scs = list(scs[n_in_sc:])
        for d, wg, do_ref, c_ref in zip(DILATIONS, ws, (do0, do1, do2), (c0, c1, c2)):
            if d == 1:
                do_ref[...] = (wg * d_attn).astype(BF16)
                c_ref[...] = -wg * rs
            else:
                _store_residue_major(wg * d_attn, out_scs.pop(), do_ref, d, BF16)
                _store_residue_major(-wg * rs, out_scs.pop(), c_ref, d, F32)

    row = lambda c: pl.BlockSpec((tm, D), lambda i: (i, c))
    nxt = lambda c: pl.BlockSpec((H, D), lambda i: (jnp.minimum((i + 1) * hpt, last_halo), c))
    ps2 = pool_scale.reshape(1, D)
    bf = lambda w: jax.ShapeDtypeStruct((T, w), BF16)
    grp_specs, grp_views = _group_specs(tm, nst, S)
    grp_shape = lambda dt: [jax.ShapeDtypeStruct((T, GROUP_W) if d == 1 else (T // S, d, S // d, GROUP_W), dt)
                            for d in DILATIONS]
    n_in_sc = len(_group_scratch(tm, 2))
    return pl.pallas_call(
        body, name=name, grid=(T // tm,),
        in_specs=[row(0), nxt(0), row(1), row(2), nxt(2), row(0), row(0)] + grp_specs * 2 + [row(0)]
        + [_full(w_o), _full(w_ya), _full(w_yb), _full(pool_w), _full(ps2)],
        out_specs=[pl.BlockSpec((tm, 3 * D), lambda i: (i, 0)), row(0), row(0), row(0)] + grp_specs * 2
        + [pl.BlockSpec((1, D), lambda i: (0, 0))],
        out_shape=[bf(3 * D), bf(D), bf(D), bf(D)] + grp_shape(BF16) + grp_shape(F32)
        + [jax.ShapeDtypeStruct((1, D), F32)],
        scratch_shapes=_group_scratch(tm, 4),
    )(dx1, dx1, zr, zr, zr, y_a, y_b, *grp_views(o_l), *grp_views(lse_l), pooled, w_o, w_ya, w_yb, pool_w, ps2)


def _attn_bwd(name, qkv, d_o, lse, cst, rope, g, Bl, S):
    d = DILATIONS[g]
    L = S // d
    nb = L // BLOCK
    qv = qkv.reshape(Bl * d, L, GROUP_QKV_W)
    dov = d_o.reshape(Bl * d, L, GROUP_W)
    lv = lse.reshape(Bl * d, L, GROUP_W)
    cv = cst.reshape(Bl * d, L, GROUP_W)
    tabs = [t.reshape(d, L, HEAD_DIM) for t in rope]
    scale = HEAD_DIM ** -0.5

    cb, cs = _attn_chunking(nb)
    qc = cb * BLOCK
    lead = BLOCK if nb > 1 else 0

    def body(q_ref, qn_ref, kp_ref, kc_ref, vp_ref, vc_ref, do_ref, don_ref, l_ref, ln_ref, c_ref, cn_ref,
             cos_ref, sa_ref, sb_ref, out_ref):
        n = pl.program_id(1)
        valid = _attn_mask(qc, lead, n > 0)
        qi = lax.broadcasted_iota(jnp.int32, (BLOCK, BLOCK), 0)
        ki = lax.broadcasted_iota(jnp.int32, (BLOCK, BLOCK), 1)
        valid_n = (ki >= qi) & ((n + 1) * cb < nb)
        tail = slice(qc - BLOCK, qc)
        for si in range(cs):
            cos, sa, sb = cos_ref[si], -sa_ref[si], -sb_ref[si]
            for hh in range(HEADS):
                sl = slice(hh * HEAD_DIM, (hh + 1) * HEAD_DIM)
                col = slice(hh * HEAD_DIM, hh * HEAD_DIM + 1)
                q, kc, vc, do = q_ref[si, :, sl], kc_ref[si, :, sl], vc_ref[si, :, sl], do_ref[si, :, sl]
                kk, vv = kc, vc
                if lead:
                    kk = jnp.concatenate([kp_ref[si, :, sl], kc], axis=0)
                    vv = jnp.concatenate([vp_ref[si, :, sl], vc], axis=0)
                s = jnp.where(valid, _dot(q, kk, NT_DIMS) * scale, NEG_INF)
                p = jnp.exp(s - l_ref[si, :, col])
                ds = p * (_dot(do, vv, NT_DIMS) + c_ref[si, :, col])
                dq = _dot(ds, kk) * scale
                dk = _dot(ds[:, lead:], q, TN_DIMS)
                dv = _dot(p[:, lead:], do, TN_DIMS)
                if nb > cb:
                    qn, don = qn_ref[si, :, sl], don_ref[si, :, sl]
                    s2 = jnp.where(valid_n, _dot(qn, kc[tail], NT_DIMS) * scale, NEG_INF)
                    p2 = jnp.exp(s2 - ln_ref[si, :, col])
                    ds2 = p2 * (_dot(don, vc[tail], NT_DIMS) + cn_ref[si, :, col])
                    dk = jnp.concatenate([dk[:qc - BLOCK], dk[tail] + _dot(ds2, qn, TN_DIMS)], axis=0)
                    dv = jnp.concatenate([dv[:qc - BLOCK], dv[tail] + _dot(p2, don, TN_DIMS)], axis=0)
                out_ref[si, :, sl] = _rope(dq, cos, sa, sb).astype(BF16)
                out_ref[si, :, GROUP_W + hh * HEAD_DIM:GROUP_W + (hh + 1) * HEAD_DIM] = (
                    _rope(dk * scale, cos, sa, sb).astype(BF16))
                out_ref[si, :, 2 * GROUP_W + hh * HEAD_DIM:2 * GROUP_W + (hh + 1) * HEAD_DIM] = dv.astype(BF16)

    main = (cs, cb * BLOCK, GROUP_W)
    edge = (cs, BLOCK, GROUP_W)
    cur = lambda off: pl.BlockSpec(main, lambda s, n: (s, n, off))
    prv = lambda off: pl.BlockSpec(edge, lambda s, n: (s, jnp.maximum(n * cb - 1, 0), off))
    nxt = lambda off: pl.BlockSpec(edge, lambda s, n: (s, jnp.minimum((n + 1) * cb, nb - 1), off))
    tab = pl.BlockSpec((cs, cb * BLOCK, HEAD_DIM), lambda s, n: (s % (d // cs), n, 0))
    out = pl.pallas_call(
        body, name=name, grid=(Bl * d // cs, nb // cb),
        in_specs=[cur(0), nxt(0), prv(1), cur(1), prv(2), cur(2),
                  cur(0), nxt(0), cur(0), nxt(0), cur(0), nxt(0), tab, tab, tab],
        out_specs=pl.BlockSpec((cs, cb * BLOCK, GROUP_QKV_W), lambda s, n: (s, n, 0)),
        out_shape=jax.ShapeDtypeStruct((Bl * d, L, GROUP_QKV_W), BF16),
    )(qv, qv, qv, qv, qv, qv, dov, dov, lv, lv, cv, cv, *tabs)
    return out.reshape(Bl * S, GROUP_QKV_W)


def _pool_w_grad(name, pooled, d_pm, gw):
    T = pooled.shape[0]
    ng = len(POOL_WINDOWS)
    tk = _tile(T, 1024, 8)
    return _mm(name, pooled, d_pm, grid=(ng, 1, T // tk),
               a_block=(tk, gw), a_map=lambda i, j, k: (k, i),
               b_block=(tk, gw), b_map=lambda i, j, k: (k, i),
               dims=TN_DIMS, acc_shape=(gw, gw),
               outs=[((ng, gw, gw), F32, (None, gw, gw), lambda i, j, k: (i, 0, 0))])[0]


def _up_w_grad(name, h2, du):
    T, D = h2.shape
    F = du.shape[2]
    tm, tn, tk = _tile(D, 1024), _tile(F, 1408), _tile(T, 1024, 8)
    njh = F // tn
    return _mm(name, h2, du, grid=(D // tm, 2 * njh, T // tk),
               a_block=(tk, tm), a_map=lambda i, j, k: (k, i),
               b_block=(None, tk, tn), b_map=lambda i, j, k: (j // njh, k, j % njh),
               dims=TN_DIMS, acc_shape=(tm, tn),
               outs=[((D, 2 * F), F32, (tm, tn), lambda i, j, k: (i, j))])[0]


def _adamw(name, w, m, v, pieces):
    R, C = w.shape
    nl = len(pieces)
    rl = R // nl
    if nl > 1 and rl % 8:
        per = [_adamw(f"{name}_{l}", w[l * rl:(l + 1) * rl], m[l * rl:(l + 1) * rl], v[l * rl:(l + 1) * rl],
                      [pieces[l]]) for l in range(nl)]
        return [jnp.concatenate([p[o] for p in per], axis=0) for o in range(4)]
    tr = _tile(rl, max(PACK_ROWS, (1 << 18) // C // PACK_ROWS * PACK_ROWS), PACK_ROWS)
    nbl = rl // tr
    c1 = 1.0 - ADAM_B1 ** ADAM_STEP
    c2 = 1.0 - ADAM_B2 ** ADAM_STEP

    def body(w_ref, m_ref, v_ref, *rest):
        p_refs = rest[:nl]
        g_ref, d_ref, mo_ref, vo_ref = rest[nl:]
        i = pl.program_id(0)
        for l in range(nl):
            @pl.when((i >= l * nbl) & (i < (l + 1) * nbl))
            def _():
                g = p_refs[l][0].astype(F32)
                for dev in range(1, N_DEV):
                    g = g + p_refs[l][dev].astype(F32)
                mn = ADAM_B1 * m_ref[...] + (1.0 - ADAM_B1) * g
                vn = ADAM_B2 * v_ref[...] + (1.0 - ADAM_B2) * (g * g)
                g_ref[...] = g
                mo_ref[...] = mn
                vo_ref[...] = vn
                d_ref[...] = -ADAM_LR * ((mn / c1) / (jnp.sqrt(vn / c2) + ADAM_EPS) + ADAM_WD * w_ref[...])

    row = pl.BlockSpec((tr, C), lambda i: (i, 0))
    piece = lambda l: pl.BlockSpec((N_DEV, tr, C), lambda i: (0, jnp.clip(i - l * nbl, 0, nbl - 1), 0))
    return pl.pallas_call(
        body, name=name, grid=(R // tr,),
        in_specs=[row, row, row] + [piece(l) for l in range(nl)],
        out_specs=[row] * 4,
        out_shape=[jax.ShapeDtypeStruct((R, C), F32)] * 4)(w, m, v, *pieces)


def _my_index():
    return 4 * lax.axis_index("x") + 2 * lax.axis_index("y") + lax.axis_index("c")


def _all_gather(name, mine):
    na = len(mine)

    def body(*refs):
        x_refs, out_refs, token = refs[:na], refs[na:2 * na], refs[2 * na]
        send_sems, recv_sems, local_sems = refs[2 * na + 1:]
        token[...] = jnp.zeros_like(token)
        x, y, c = lax.axis_index("x"), lax.axis_index("y"), lax.axis_index("c")
        me, sibling = (x, y, c), (x, y, 1 - c)
        chips = [(1 - x, y), (x, 1 - y), (1 - x, 1 - y)]

        def slot(a, px, py, pc):
            return out_refs[a].at[4 * px + 2 * py + pc]

        def copy(a, k, block, to, src=None):
            return pltpu.make_async_remote_copy(
                src_ref=slot(a, *block) if src is None else src, dst_ref=slot(a, *block),
                send_sem=send_sems.at[7 * a + k], recv_sem=recv_sems.at[7 * a + k],
                device_id=to, device_id_type=MESH_ID)

        own = [pltpu.make_async_copy(x_refs[a], slot(a, *me), local_sems.at[a]) for a in range(na)]
        for cp in own:
            cp.start()
        first = []
        for a in range(na):
            first.append(copy(a, 0, me, sibling, src=x_refs[a]))
            first += [copy(a, 1 + j, me, (*chip, c), src=x_refs[a]) for j, chip in enumerate(chips)]
        for cp in first:
            cp.start()
        passed = []
        for j, chip in enumerate(chips):
            for a in range(na):
                copy(a, 1 + j, (*chip, c), me).wait_recv()
                fwd = copy(a, 4 + j, (*chip, c), sibling)
                fwd.start()
                passed.append(fwd)
        for a in range(na):
            copy(a, 0, sibling, me).wait_recv()
            for j, chip in enumerate(chips):
                copy(a, 4 + j, (*chip, 1 - c), me).wait_recv()
        for cp in first + passed:
            cp.wait_send()
        for cp in own:
            cp.wait()

    res = pl.pallas_call(
        body, name=name,
        in_specs=[pl.BlockSpec(memory_space=pl.ANY)] * na,
        out_specs=[pl.BlockSpec(memory_space=pl.ANY)] * na + [pl.BlockSpec(memory_space=pltpu.VMEM)],
        out_shape=[jax.ShapeDtypeStruct((N_DEV,) + m.shape, m.dtype) for m in mine]
        + [jax.ShapeDtypeStruct((8, LANE), F32)],
        scratch_shapes=[pltpu.SemaphoreType.DMA((7 * na,)), pltpu.SemaphoreType.DMA((7 * na,)),
                        pltpu.SemaphoreType.DMA((na,))],
    )(*mine)
    return res[:na], res[na]


_HBM_SPEC = pl.BlockSpec(memory_space=pltpu.HBM)
_SEM_SPEC = pl.BlockSpec(memory_space=pltpu.SEMAPHORE)
_SIDE_EFFECT = pltpu.SideEffectType.DATAFLOW_SIDE_EFFECTING


def _peer_of(k):
    x, y, c = lax.axis_index("x"), lax.axis_index("y"), lax.axis_index("c")
    px = 1 - x if k & 4 else x
    py = 1 - y if k & 2 else y
    pc = 1 - c if k & 1 else c
    return (px, py, pc), 4 * px + 2 * py + pc


def _send_start(name, srcs, pieces):
    na = len(srcs)
    land_shapes = [s.shape if pieces else (N_DEV,) + s.shape for s in srcs]
    lands = [lax.empty(shp, s.dtype) for shp, s in zip(land_shapes, srcs)]

    def body(*refs):
        src_refs, land_refs = refs[:na], refs[na:2 * na]
        send_sems, recv_sems, token = refs[2 * na], refs[2 * na + 1], refs[4 * na + 2]
        me = 4 * lax.axis_index("x") + 2 * lax.axis_index("y") + lax.axis_index("c")
        for k in range(1, N_DEV):
            to, pidx = _peer_of(k)
            for a in range(na):
                pltpu.make_async_remote_copy(
                    src_ref=src_refs[a].at[pidx] if pieces else src_refs[a], dst_ref=land_refs[a].at[me],
                    send_sem=send_sems.at[7 * a + k - 1], recv_sem=recv_sems.at[7 * a + k - 1],
                    device_id=to, device_id_type=MESH_ID).start()
        token[...] = jnp.zeros_like(token)

    hbm = lambda arrs: [pltpu.HBM(a.shape, a.dtype) for a in arrs]
    outs = pl.pallas_call(
        body, name=name,
        out_shape=(pltpu.SemaphoreType.DMA((7 * na,)), pltpu.SemaphoreType.DMA((7 * na,)), *hbm(srcs), *hbm(lands),
                   jax.ShapeDtypeStruct((8, LANE), F32)),
        in_specs=[_HBM_SPEC] * (2 * na),
        out_specs=(_SEM_SPEC, _SEM_SPEC, *([_HBM_SPEC] * (2 * na)), pl.BlockSpec(memory_space=pltpu.VMEM)),
        input_output_aliases={i: 2 + i for i in range(2 * na)},
        compiler_params=pltpu.CompilerParams(has_side_effects=_SIDE_EFFECT),
    )(*[pltpu.with_memory_space_constraint(s, pltpu.HBM) for s in srcs],
      *[pltpu.with_memory_space_constraint(l, pltpu.HBM) for l in lands])
    return outs[0], outs[1], outs[2:2 + na], outs[2 + na:2 + 2 * na], outs[-1]


def _send_wait(name, send_sems, recv_sems, srcs, lands, pieces, after):
    na = len(srcs)

    def body(*refs):
        src_refs, land_refs = refs[:na], refs[na:2 * na]
        send_sems, recv_sems = refs[2 * na], refs[2 * na + 1]
        for k in range(1, N_DEV):
            to, pidx = _peer_of(k)
            for a in range(na):
                cp = pltpu.make_async_remote_copy(
                    src_ref=src_refs[a].at[pidx] if pieces else src_refs[a], dst_ref=land_refs[a].at[pidx],
                    send_sem=send_sems.at[7 * a + k - 1], recv_sem=recv_sems.at[7 * a + k - 1],
                    device_id=to, device_id_type=MESH_ID)
                cp.wait_send()
                cp.wait_recv()

    hbm = lambda arrs: [pltpu.HBM(a.shape, a.dtype) for a in arrs]
    outs = pl.pallas_call(
        body, name=name,
        out_shape=tuple(hbm(srcs) + hbm(lands)),
        in_specs=[_HBM_SPEC] * (2 * na) + [_SEM_SPEC, _SEM_SPEC, pl.BlockSpec(memory_space=pl.ANY)],
        out_specs=tuple([_HBM_SPEC] * (2 * na)),
        input_output_aliases={i: i for i in range(2 * na)},
        compiler_params=pltpu.CompilerParams(has_side_effects=_SIDE_EFFECT),
    )(*srcs, *lands, send_sems, recv_sems, after)
    return outs[:na], outs[na:]


def _own_slot(land, own):
    me = 4 * lax.axis_index("x") + 2 * lax.axis_index("y") + lax.axis_index("c")
    mine = lax.broadcasted_iota(jnp.int32, land.shape, 0) == me
    return jnp.where(mine, jnp.broadcast_to(own, land.shape), land)


def _exchange(name, pieces, bcast):
    n_p, n_b = len(pieces), len(bcast)
    na = n_p + n_b

    def body(*refs):
        src_refs, dst_refs = refs[:na], refs[na:2 * na]
        send_sems, recv_sems, local_sems = refs[2 * na:]
        x, y, c = lax.axis_index("x"), lax.axis_index("y"), lax.axis_index("c")
        me = 4 * x + 2 * y + c

        def src(a, slot):
            return src_refs[a].at[slot] if a < n_p else src_refs[a]

        own = [pltpu.make_async_copy(src(a, me), dst_refs[a].at[me], local_sems.at[a]) for a in range(na)]
        for cp in own:
            cp.start()

        def peer_of(k):
            px = 1 - x if k & 4 else x
            py = 1 - y if k & 2 else y
            pc = 1 - c if k & 1 else c
            return (px, py, pc), 4 * px + 2 * py + pc

        def copy(a, k, src_slot, dst_slot, to):
            return pltpu.make_async_remote_copy(
                src_ref=src(a, src_slot), dst_ref=dst_refs[a].at[dst_slot],
                send_sem=send_sems.at[7 * a + k - 1], recv_sem=recv_sems.at[7 * a + k - 1],
                device_id=to, device_id_type=MESH_ID)

        sent = []
        for k in range(1, N_DEV):
            to, pidx = peer_of(k)
            for a in range(na):
                cp = copy(a, k, pidx, me, to)
                cp.start()
                sent.append(cp)
        for k in range(1, N_DEV):
            to, pidx = peer_of(k)
            for a in range(na):
                copy(a, k, me, pidx, to).wait_recv()
        for cp in sent:
            cp.wait_send()
        for cp in own:
            cp.wait()

    arrays = list(pieces) + list(bcast)
    out_shape = [jax.ShapeDtypeStruct(p.shape, p.dtype) for p in pieces]
    out_shape += [jax.ShapeDtypeStruct((N_DEV,) + b.shape, b.dtype) for b in bcast]
    res = pl.pallas_call(
        body, name=name,
        in_specs=[pl.BlockSpec(memory_space=pl.ANY)] * na, out_specs=[pl.BlockSpec(memory_space=pl.ANY)] * na,
        out_shape=out_shape,
        scratch_shapes=[pltpu.SemaphoreType.DMA((7 * na,)), pltpu.SemaphoreType.DMA((7 * na,)),
                        pltpu.SemaphoreType.DMA((na,))],
    )(*arrays)
    return res[:n_p], res[n_p:]


def _pad_rows(flat, cols, row_mult):
    n = flat.shape[-1]
    unit = cols * row_mult
    padded = -(-n // unit) * unit
    pad = [(0, 0)] * (flat.ndim - 1) + [(0, padded - n)]
    return jnp.pad(flat, pad).reshape(flat.shape[:-1] + (padded // cols, cols))


def _perm_cols(w):
    aw = N_GROUPS * GROUP_W
    parts = [w[..., QKV_W:]]
    parts += [w[..., a * aw + g * GROUP_W:a * aw + (g + 1) * GROUP_W] for g in range(N_GROUPS) for a in range(3)]
    return jnp.concatenate(parts, axis=-1)


def _unperm_cols(wp, rest_w):
    qkv = wp[..., rest_w:]
    parts = [qkv[..., g * GROUP_QKV_W + a * GROUP_W:g * GROUP_QKV_W + (a + 1) * GROUP_W]
             for a in range(3) for g in range(N_GROUPS)]
    return jnp.concatenate(parts + [wp[..., :rest_w]], axis=-1)


SHARD_AXIS = dict(SHARDED)


def _layer_shards(wts, li, names, zero):
    out = []
    for n in names:
        w = wts[n][li] if zero is None else wts[n][li] + zero
        out.append(w if n in EXACT_F32 else w.astype(BF16))
    return out


def _assemble_weights(names, segs):
    W = {}
    for n, seg in zip(names, segs):
        ax = SHARD_AXIS[n]
        shp = seg.shape[1:]
        seg = jnp.moveaxis(seg, 0, ax)
        W[n] = seg.reshape(shp[:ax] + (N_DEV * shp[ax],) + shp[ax + 1:])
    if "w_in" in W:
        W["w_in"] = _perm_cols(W["w_in"])
    return W


def _grad_pieces(gr):
    out = []
    for n in gr:
        ax = SHARD_AXIS[n]
        shp = gr[n].shape
        g = gr[n].reshape(shp[:ax] + (N_DEV, shp[ax] // N_DEV) + shp[ax + 1:])
        out.append(jnp.moveaxis(g, ax, 0).astype(BF16))
    return out


def _pack_small(vals):
    flat = jnp.concatenate([vals[n].astype(F32).reshape(-1) for n in REPLICATED])
    return _pad_rows(flat, LANE, 8)


def _layer_fwd(li, x, p_l, w_in, other_weights, G, rope, Bl, S, F):
    T, D = x.shape
    rest_w = 3 * D
    sv = {"x0": x}
    W = {"w_in": w_in}
    hs = _rms_mix_fwd(f"rms_mix_{li}", x, G["g_mix"], S)
    h = hs[0]
    h_g = [h] + [a.reshape(T, D) for a in hs[1:]]
    sv["h_g"] = h_g
    zr = _mm_rows(f"rest_proj_{li}", h, W["w_in"], n_cols=rest_w)
    sv["zr"] = zr
    qkv_l, o_l, lse_l = [], [], []
    for g in range(N_GROUPS):
        qkv = _qkv_proj(f"qkv_proj_{li}_{g}", h_g[g], W["w_in"], rope[g], S, rest_w + g * GROUP_QKV_W)
        o, lse = _attn_fwd(f"attn_fwd_{li}_{g}", qkv, g, Bl, S)
        qkv_l.append(qkv)
        o_l.append(o)
        lse_l.append(lse)
    sv["qkv"], sv["o"], sv["lse"] = qkv_l, o_l, lse_l
    W.update(other_weights(o_l[-1]))
    x1, attn, pooled, mixed, y_a, y_b, merged = _mix_out_fwd(
        f"mix_out_fwd_{li}", x, zr, o_l, lse_l, W["w_ya"], W["pool_w"], G["pool_scale"], W["w_yb"], W["w_o"], S)
    sv.update(x1=x1, attn=attn, pooled=pooled, mixed=mixed, y_a=y_a, y_b=y_b, merged=merged)
    u, h2 = _up_proj(f"up_proj_{li}", x1, G["g_ffn"], W["w_up"], F)
    a = _ffn_act_fwd(f"ffn_act_fwd_{li}", u, W["conv_w"], G["conv_b"], S)
    x2 = _mm_rows(f"down_proj_{li}", a, W["w_down"], add=x1, chunk=512)
    sv.update(h2=h2, u=u, a=a, x2=x2)
    x3, e, pg, p_bf = _ple_fwd(f"ple_fwd_{li}", x2, p_l, G["g_ple"], W["w_ple_gate"], W["w_ple"])
    sv.update(e=e, pg=pg, p_bf=p_bf)
    return x3, sv, W


EARLY_GRADS = ("w_ple", "w_ple_gate", "w_down", "conv_w", "w_up")
LATE_GRADS = ("w_in", "w_ya", "w_yb", "pool_w", "w_o")


def _layer_bwd(li, dx3, sv, W, G, rope, Bl, S, F, send):
    T, D = dx3.shape
    rest_w = 3 * D
    gr = {}
    dx2, d_e, d_s, h3, dg = _ple_bwd(f"ple_bwd_{li}", dx3, sv["x2"], sv["e"], sv["pg"], G["g_ple"], W["w_ple_gate"])
    gr["g_ple"] = dg[0]
    gr["w_ple"] = _mm_tn(f"w_ple_grad_{li}", sv["p_bf"], d_e)
    gr["w_ple_gate"] = _mm_tn(f"w_ple_gate_grad_{li}", h3, d_s)

    d_a = _mm_rows(f"down_bwd_{li}", dx2, W["w_down"], transposed=True, chunk=1408, out_dtype=BF16)
    gr["w_down"] = _mm_tn(f"w_down_grad_{li}", sv["a"], dx2, tm=1408)
    du, d_cw, d_cb = _ffn_act_bwd(f"ffn_act_bwd_{li}", sv["u"], d_a, W["conv_w"], G["conv_b"], S)
    gr["conv_w"] = d_cw.transpose(1, 0, 2).reshape(3, 2 * F)
    gr["conv_b"] = d_cb.reshape(2 * F)
    tk_f = _tile(F, 1408)
    nkh = F // tk_f
    tm_r = _tile(T, 1024, 8)
    dx1, dg = _mm_nt_rmsbwd(f"up_bwd_{li}", du, (None, tm_r, tk_f), lambda i, j, k: (k // nkh, i, k % nkh),
                            2 * nkh, tk_f, W["w_up"], sv["x1"], G["g_ffn"], dx2)
    gr["g_ffn"] = dg[0]
    gr["w_up"] = _up_w_grad(f"w_up_grad_{li}", sv["h2"], du)
    zero = send("a", {n: gr[n] for n in EARLY_GRADS})

    (dz_rest, d_ya, d_yb, d_pm, do0, do1, do2, c0, c1, c2, dps) = _mix_out_bwd(
        f"mix_out_bwd_{li}", dx1, sv["zr"], sv["y_a"], sv["y_b"], sv["o"], sv["lse"], sv["pooled"],
        W["w_o"], W["w_ya"], W["w_yb"], W["pool_w"], G["pool_scale"] + zero, S)
    gr["pool_scale"] = dps[0]
    gr["w_o"] = _mm_tn(f"w_o_grad_{li}", sv["merged"], dx1)
    gr["w_ya"] = _mm_tn(f"w_ya_grad_{li}", sv["attn"], d_ya)
    gr["w_yb"] = _mm_tn(f"w_yb_grad_{li}", sv["mixed"], d_yb)
    gr["pool_w"] = _pool_w_grad(f"pool_w_grad_{li}", sv["pooled"], d_pm, D // len(POOL_WINDOWS))
    segs = [(dz_rest, 1)]
    for g, (do, cst) in enumerate(((do0, c0), (do1, c1), (do2, c2))):
        dqkv = _attn_bwd(f"attn_bwd_{li}_{g}", sv["qkv"][g], do, sv["lse"][g], cst, rope[g], g, Bl, S)
        segs.append((dqkv, DILATIONS[g]))

    h_rows = [sv["h_g"][0]] + sv["h_g"]
    w_in_parts = [_mm_tn(f"w_in_grad_{li}_{s}", h_rows[s], seg, tn=1536) for s, (seg, _) in enumerate(segs)]
    gr["w_in"] = _unperm_cols(jnp.concatenate(w_in_parts, axis=1), rest_w)
    zero = send("b", {n: gr[n] for n in LATE_GRADS})
    dx0, dg = _in_bwd(f"in_bwd_{li}", segs, W["w_in"], sv["x0"], G["g_mix"] + zero, dx1, S)
    gr["g_mix"] = dg[0]
    return dx0, gr


def kernel(x, p, g_mix, w_in, w_ya, w_yb, pool_w, pool_scale, w_o, g_ffn, w_up, conv_w, conv_b, w_down, g_ple, w_ple, w_ple_gate, g_final, loss_target, m_g_mix, m_w_in, m_w_ya, m_w_yb, m_pool_w, m_pool_scale, m_w_o, m_g_ffn, m_w_up, m_conv_w, m_conv_b, m_w_down, m_g_ple, m_w_ple, m_w_ple_gate, m_g_final, v_g_mix, v_w_in, v_w_ya, v_w_yb, v_pool_w, v_pool_scale, v_w_o, v_g_ffn, v_w_up, v_conv_w, v_conv_b, v_w_down, v_g_ple, v_w_ple, v_w_ple_gate, v_g_final):
    wts = dict(g_mix=g_mix, w_in=w_in, w_ya=w_ya, w_yb=w_yb, pool_w=pool_w, pool_scale=pool_scale, w_o=w_o,
               g_ffn=g_ffn, w_up=w_up, conv_w=conv_w, conv_b=conv_b, w_down=w_down, g_ple=g_ple, w_ple=w_ple,
               w_ple_gate=w_ple_gate, g_final=g_final)
    mom = dict(g_mix=m_g_mix, w_in=m_w_in, w_ya=m_w_ya, w_yb=m_w_yb, pool_w=m_pool_w, pool_scale=m_pool_scale,
               w_o=m_w_o, g_ffn=m_g_ffn, w_up=m_w_up, conv_w=m_conv_w, conv_b=m_conv_b, w_down=m_w_down,
               g_ple=m_g_ple, w_ple=m_w_ple, w_ple_gate=m_w_ple_gate, g_final=m_g_final)
    var = dict(g_mix=v_g_mix, w_in=v_w_in, w_ya=v_w_ya, w_yb=v_w_yb, pool_w=v_pool_w, pool_scale=v_pool_scale,
               w_o=v_w_o, g_ffn=v_g_ffn, w_up=v_w_up, conv_w=v_conv_w, conv_b=v_conv_b, w_down=v_w_down,
               g_ple=v_g_ple, w_ple=v_w_ple, w_ple_gate=v_w_ple_gate, g_final=v_g_final)
    Bl, S, D = x.shape
    depth = g_mix.shape[0]
    F = w_down.shape[1] * N_DEV
    T = Bl * S
    assert S % (BLOCK * DILATIONS[-1]) == 0 and D % GROUP_W == 0 and F % LANE == 0
    rope = [tuple(t if d == 1 else t.reshape(S // d, d, HEAD_DIM).transpose(1, 0, 2).reshape(S, HEAD_DIM)
                  for t in _rope_tables(S)) for d in DILATIONS]

    first, others = ("w_in",), tuple(n for n, _ in SHARDED if n != "w_in")
    got_in, tok = _all_gather("gather_w0_in", _layer_shards(wts, 0, first, None))
    gathers = {}
    for li in range(depth):
        names = others if li == 0 else first + others
        *gathers[li], tok = _send_start(f"gather_w{li}_start", _layer_shards(wts, li, names, tok[0, 0]), False)

    def gathered(li, names, after):
        shards, lands = _send_wait(f"gather_w{li}_wait", *gathers[li], False, after)
        return _assemble_weights(names, [_own_slot(l, s) for l, s in zip(lands, shards)])

    xs = x.reshape(T, D)
    saved = []
    for li in range(depth):
        G = {n: wts[n][li] for n in REPLICATED if n != "g_final"}
        if li == 0:
            G["g_mix"] = G["g_mix"] + tok[0, 0]
            w_in_full = _assemble_weights(first, got_in)["w_in"]
            rest_of = lambda after: gathered(0, others, after)
        else:
            W_all = gathered(li, first + others, xs)
            w_in_full = W_all["w_in"]
            rest_of = lambda after, W_all=W_all: W_all
        xs, sv, W = _layer_fwd(li, xs, p[li].reshape(T, -1), w_in_full, rest_of, G, rope, Bl, S, F)
        saved.append((sv, W, G))

    dx, loss_row, dg_final = _loss_bwd("loss_bwd", xs, loss_target.reshape(T, D), g_final)
    layer_grads = [None] * depth
    sends = []
    zero = [None]
    for li in reversed(range(depth)):
        sv, W, G = saved[li]
        if zero[0] is not None:
            G = dict(G, g_ple=G["g_ple"] + zero[0])

        def send(tag, group, li=li):
            *handles, tok = _send_start(f"exchange_g{li}{tag}_start", _grad_pieces(group), True)
            sends.append((li, tag, tuple(group), handles))
            zero[0] = tok[0, 0]
            return zero[0]

        dx, layer_grads[li] = _layer_bwd(li, dx, sv, W, G, rope, Bl, S, F, send)

    recv = {}
    for li, tag, names, handles in sends:
        pcs, lands = _send_wait(f"exchange_g{li}{tag}_wait", *handles, True, dx)
        for n, l, s in zip(names, lands, pcs):
            recv[(li, n)] = _own_slot(l, s)
    grads = {n: jnp.stack([layer_grads[li][n] for li in range(depth)]) for n in REPLICATED if n != "g_final"}
    grads["g_final"] = dg_final[0]
    _, (small_all,) = _exchange("exchange_small", [], [_pack_small(grads)])

    out_g, out_d, out_m, out_v = {}, {}, {}, {}
    for n, _ in SHARDED:
        shp = wts[n].shape
        two_d = (math.prod(shp[:-1]), shp[-1])
        pieces = [recv[(li, n)].reshape(N_DEV, two_d[0] // depth, two_d[1]) for li in range(depth)]
        res = _adamw(f"adamw_{n}", wts[n].reshape(two_d), mom[n].reshape(two_d), var[n].reshape(two_d), pieces)
        out_g[n], out_d[n], out_m[n], out_v[n] = [r.reshape(shp) for r in res]
    res = _adamw("adamw_replicated", _pack_small(wts), _pack_small(mom), _pack_small(var), [small_all])
    off = 0
    for n in REPLICATED:
        shp = wts[n].shape
        size = math.prod(shp)
        for dst, r in zip((out_g, out_d, out_m, out_v), res):
            dst[n] = r.reshape(-1)[off:off + size].reshape(shp)
        off += size

    loss = lax.psum(loss_row[0, 0], MESH_AXES)
    outs = [loss, dx.reshape(Bl, S, D)]
    for dct in (out_g, out_d, out_m, out_v):
        outs += [dct[n] for n in WEIGHT_ORDER]
    return tuple(outs)
```
